```python
import math
import jax, jax.numpy as jnp
from jax import lax
import numpy as np

D_MODEL = 4096
BATCH = 2
SEQ = 8192
DEPTH = 1

PLE_DIM = 256
MIX_WIDTH = D_MODEL
MLA_HEADS = 16
MLA_NOPE = 128
MLA_ROPE = 64
MLA_V = 128
Q_LORA = 1024
KV_LORA = 512
MLA_WIDTH = MLA_HEADS * MLA_V
SWA_HEADS = 32
SWA_KV_HEADS = 4
SWA_HEAD_DIM = 64
SWA_GROUP = SWA_HEADS // SWA_KV_HEADS
SWA_WIDTH = SWA_HEADS * SWA_HEAD_DIM
WINDOW = 128
BLOCK = 128
ROPE_THETA = 10000.0
NORM_EPS = 1e-6
D_FF = ((8 * D_MODEL + 3 * 256 - 1) // (3 * 256)) * 256
IN_SIZES = (Q_LORA, KV_LORA, MLA_ROPE, SWA_HEADS * SWA_HEAD_DIM,
            SWA_KV_HEADS * SWA_HEAD_DIM, SWA_KV_HEADS * SWA_HEAD_DIM)
IN_SPLITS = tuple(int(v) for v in np.cumsum(IN_SIZES)[:-1])
D_IN = sum(IN_SIZES)

kernel_name = "hymba_mla_swa_sink_alibi_sandwich_ple"


def rmsnorm(x, g):
    xf = x.astype(jnp.float32)
    xf = xf * lax.rsqrt(jnp.mean(xf * xf, axis=-1, keepdims=True) + NORM_EPS)
    return (xf * g.astype(jnp.float32)).astype(x.dtype)


def rope(x, pos):
    d = x.shape[-1]
    inv_freq = ROPE_THETA ** (-jnp.arange(0, d, 2, dtype=jnp.float32) / d)
    ang = pos.astype(jnp.float32)[..., None] * inv_freq
    cos = jnp.cos(ang)[:, :, None, :].astype(x.dtype)
    sin = jnp.sin(ang)[:, :, None, :].astype(x.dtype)
    x1, x2 = x[..., : d // 2], x[..., d // 2:]
    return jnp.concatenate([x1 * cos - x2 * sin, x1 * sin + x2 * cos], axis=-1)


def alibi_slopes(n_heads):
    return 2.0 ** (-8.0 * jnp.arange(1, n_heads + 1, dtype=jnp.float32) / n_heads)


def mla_attention(q_nope, q_rope, k_nope, k_rope, v):
    B, S, H, _ = q_nope.shape
    nblk = S // BLOCK
    scale = 1.0 / math.sqrt(MLA_NOPE + MLA_ROPE)
    qn = q_nope.reshape(B, nblk, BLOCK, H, MLA_NOPE).transpose(1, 0, 2, 3, 4)
    qr = q_rope.reshape(B, nblk, BLOCK, H, MLA_ROPE).transpose(1, 0, 2, 3, 4)
    k_idx = jnp.arange(S)

    def one_block(args):
        blk, qn_b, qr_b = args
        s = (jnp.einsum('bqhd,bkhd->bhqk', qn_b, k_nope)
             + jnp.einsum('bqhd,bkd->bhqk', qr_b, k_rope)).astype(jnp.float32) * scale
        q_idx = blk * BLOCK + jnp.arange(BLOCK)
        causal = q_idx[:, None] >= k_idx[None, :]
        s = jnp.where(causal[None, None], s, -jnp.inf)
        pr = jax.nn.softmax(s, axis=-1).astype(v.dtype)
        return jnp.einsum('bhqk,bkhd->bqhd', pr, v)

    out = lax.map(one_block, (jnp.arange(nblk), qn, qr))
    return out.transpose(1, 0, 2, 3, 4).reshape(B, S, H * MLA_V)


def band(t, nblk):
    B = t.shape[0]
    tb = t.reshape(B, nblk, BLOCK, *t.shape[2:])
    prev = jnp.concatenate([jnp.zeros_like(tb[:, :1]), tb[:, :-1]], axis=1)
    return jnp.concatenate([prev, tb], axis=2)


def swa_attention(q, k, v, sinks, pos):
    B, S = q.shape[:2]
    nblk = S // BLOCK
    scale = 1.0 / math.sqrt(SWA_HEAD_DIM)
    qb = q.reshape(B, nblk, BLOCK, SWA_KV_HEADS, SWA_GROUP, SWA_HEAD_DIM)
    kb, vb = band(k, nblk), band(v, nblk)
    pq = pos.reshape(B, nblk, BLOCK).astype(jnp.float32)
    pk = band(pos, nblk).astype(jnp.float32)
    q_loc = BLOCK + jnp.arange(BLOCK)
    k_loc = jnp.arange(2 * BLOCK)
    rel = q_loc[:, None] - k_loc[None, :]
    in_window = (rel >= 0) & (rel < WINDOW)
    not_pad = (jnp.arange(nblk)[:, None, None] > 0) | (k_loc[None, None, :] >= BLOCK)
    valid = in_window[None] & not_pad
    s = jnp.einsum('bnqhgd,bnkhd->bnhgqk', qb, kb).astype(jnp.float32) * scale
    dist = jnp.abs(pq[..., :, None] - pk[..., None, :])
    slopes = alibi_slopes(SWA_HEADS).reshape(SWA_KV_HEADS, SWA_GROUP)
    s = s - slopes[None, None, :, :, None, None] * dist[:, :, None, None]
    s = jnp.where(valid[None, :, None, None], s, -jnp.inf)
    sink = jnp.broadcast_to(sinks.astype(jnp.float32).reshape(1, 1, SWA_KV_HEADS, SWA_GROUP, 1, 1),
                            s.shape[:-1] + (1,))
    pr = jax.nn.softmax(jnp.concatenate([s, sink], axis=-1), axis=-1)[..., :-1]
    out = jnp.einsum('bnhgqk,bnkhd->bnqhgd', pr.astype(v.dtype), vb)
    return out.reshape(B, S, SWA_WIDTH)


def setup_inputs(seed: int = 0) -> dict:
    key = jax.random.key(seed)
    ks = jax.random.split(key, 21)
    f32 = jnp.float32

    def w(k, shape, fan_in):
        return jax.random.normal(k, shape, f32) * fan_in ** -0.5

    def gain(k, n):
        return 1.0 + 0.05 * jax.random.normal(k, (DEPTH, n), f32)

    return {
        "x": jax.random.normal(ks[0], (BATCH, SEQ, D_MODEL), f32),
        "p": jax.random.normal(ks[1], (DEPTH, BATCH, SEQ, PLE_DIM), f32),
        "positions": jnp.broadcast_to(jnp.arange(SEQ, dtype=jnp.int32), (BATCH, SEQ)),
        "attn_pre_norm": gain(ks[2], D_MODEL),
        "w_in": w(ks[3], (DEPTH, D_MODEL, D_IN), D_MODEL),
        "q_a_norm": gain(ks[4], Q_LORA),
        "w_q_up": w(ks[5], (DEPTH, Q_LORA, MLA_HEADS * (MLA_NOPE + MLA_ROPE)), Q_LORA),
        "kv_a_norm": gain(ks[6], KV_LORA),
        "w_kv_up": w(ks[7], (DEPTH, KV_LORA, MLA_HEADS * (MLA_NOPE + MLA_V)), KV_LORA),
        "sinks": 0.5 * jax.random.normal(ks[8], (DEPTH, SWA_HEADS), f32),
        "mla_out_norm": gain(ks[9], MLA_WIDTH),
        "swa_out_norm": gain(ks[10], SWA_WIDTH),
        "w_o": w(ks[11], (DEPTH, MIX_WIDTH, D_MODEL), MIX_WIDTH),
        "attn_post_norm": gain(ks[12], D_MODEL),
        "ffn_pre_norm": gain(ks[13], D_MODEL),
        "w_gate": w(ks[14], (DEPTH, D_MODEL, D_FF), D_MODEL),
        "w_up": w(ks[15], (DEPTH, D_MODEL, D_FF), D_MODEL),
        "w_down": w(ks[16], (DEPTH, D_FF, D_MODEL), D_FF),
        "ffn_post_norm": gain(ks[17], D_MODEL),
        "w_ple_gate": w(ks[18], (DEPTH, D_MODEL, D_MODEL), D_MODEL),
        "w_ple_proj": w(ks[19], (DEPTH, PLE_DIM, D_MODEL), PLE_DIM),
    }


def reference(x, p, positions, attn_pre_norm, w_in, q_a_norm, w_q_up, kv_a_norm, w_kv_up,
              sinks, mla_out_norm, swa_out_norm, w_o, attn_post_norm, ffn_pre_norm,
              w_gate, w_up, w_down, ffn_post_norm, w_ple_gate, w_ple_proj):
    B, S, _ = x.shape
    for i in range(DEPTH):
        h = rmsnorm(x, attn_pre_norm[i])
        proj = h @ w_in[i]
        c_q, c_kv, k_rope, q_swa, k_swa, v_swa = jnp.split(proj, IN_SPLITS, axis=-1)

        q = (rmsnorm(c_q, q_a_norm[i]) @ w_q_up[i]).reshape(B, S, MLA_HEADS, MLA_NOPE + MLA_ROPE)
        kv = (rmsnorm(c_kv, kv_a_norm[i]) @ w_kv_up[i]).reshape(B, S, MLA_HEADS, MLA_NOPE + MLA_V)
        q_nope, q_rope = q[..., :MLA_NOPE], rope(q[..., MLA_NOPE:], positions)
        k_nope, v_mla = kv[..., :MLA_NOPE], kv[..., MLA_NOPE:]
        k_rope = rope(k_rope[:, :, None, :], positions)[:, :, 0]
        o_mla = mla_attention(q_nope, q_rope, k_nope, k_rope, v_mla)

        o_swa = swa_attention(q_swa.reshape(B, S, SWA_HEADS, SWA_HEAD_DIM),
                              k_swa.reshape(B, S, SWA_KV_HEADS, SWA_HEAD_DIM),
                              v_swa.reshape(B, S, SWA_KV_HEADS, SWA_HEAD_DIM),
                              sinks[i], positions)

        mixed = jnp.concatenate([rmsnorm(o_mla, mla_out_norm[i]),
                                 rmsnorm(o_swa, swa_out_norm[i])], axis=-1)
        x = x + rmsnorm(mixed @ w_o[i], attn_post_norm[i])

        h = rmsnorm(x, ffn_pre_norm[i])
        f = (jax.nn.silu(h @ w_gate[i]) * (h @ w_up[i])) @ w_down[i]
        x = x + rmsnorm(f, ffn_post_norm[i])

        gate = jax.nn.sigmoid(x @ w_ple_gate[i])
        x = x + gate * (p[i] @ w_ple_proj[i])
    return x
```

```python
import functools
import math

import numpy as np
import jax
import jax.numpy as jnp
from jax import lax
from jax.experimental import pallas as pl
from jax.experimental.pallas import tpu as pltpu

D_MODEL = 4096
PLE_DIM = 256
MLA_HEADS = 16
MLA_NOPE = 128
MLA_ROPE = 64
MLA_V = 128
Q_LORA = 1024
KV_LORA = 512
MLA_WIDTH = MLA_HEADS * MLA_V
SWA_HEADS = 32
SWA_KV_HEADS = 4
SWA_HEAD_DIM = 64
SWA_GROUP = SWA_HEADS // SWA_KV_HEADS
SWA_WIDTH = SWA_HEADS * SWA_HEAD_DIM
WINDOW = 128
BLOCK = 128
ROPE_THETA = 10000.0
NORM_EPS = 1e-6
D_FF = 11008

LANES = 128
VMEM_LIMIT = 56 * 1024 * 1024

COL_CQ = 0
COL_CKV = COL_CQ + Q_LORA
COL_QSWA = COL_CKV + KV_LORA
COL_KSWA = COL_QSWA + SWA_WIDTH
COL_VSWA = COL_KSWA + SWA_KV_HEADS * SWA_HEAD_DIM
COL_KR = COL_VSWA + SWA_KV_HEADS * SWA_HEAD_DIM
D_IN2 = COL_KR + 2 * MLA_ROPE

F32 = jnp.float32
BF16 = jnp.bfloat16


def _cparams(sem):
    return pltpu.CompilerParams(dimension_semantics=sem, vmem_limit_bytes=VMEM_LIMIT)


def _rms(xf, g):
    ms = jnp.mean(xf * xf, axis=-1, keepdims=True)
    return xf * lax.rsqrt(ms + NORM_EPS) * g


def _swap_half(w):
    d = w.shape[-1]
    return jnp.concatenate([-w[..., d // 2:], w[..., : d // 2]], axis=-1)


def _rope_tables(pos_col, invf_row):
    ang = pos_col.astype(F32) * invf_row
    return jnp.cos(ang), jnp.sin(ang)


def _norm_cast_kernel(x_ref, g_ref, o_ref):
    o_ref[...] = _rms(x_ref[...], g_ref[...]).astype(o_ref.dtype)


def norm_cast(x, g, tm=256):
    m, d = x.shape
    return pl.pallas_call(
        _norm_cast_kernel,
        grid=(m // tm,),
        in_specs=[pl.BlockSpec((tm, d), lambda i: (i, 0)),
                  pl.BlockSpec((1, d), lambda i: (0, 0))],
        out_specs=pl.BlockSpec((tm, d), lambda i: (i, 0)),
        out_shape=jax.ShapeDtypeStruct((m, d), BF16),
        compiler_params=_cparams(("parallel",)),
        name="norm_cast",
    )(x, g)


def _matmul_kernel(a_ref, w_ref, o_ref):
    o_ref[...] = jnp.dot(a_ref[...], w_ref[...], preferred_element_type=F32).astype(o_ref.dtype)


def matmul(a, w, out_dtype, tm, tn, name):
    m, k = a.shape
    n = w.shape[1]
    return pl.pallas_call(
        _matmul_kernel,
        grid=(n // tn, m // tm),
        in_specs=[pl.BlockSpec((tm, k), lambda j, i: (i, 0)),
                  pl.BlockSpec((k, tn), lambda j, i: (0, j))],
        out_specs=pl.BlockSpec((tm, tn), lambda j, i: (i, j)),
        out_shape=jax.ShapeDtypeStruct((m, n), out_dtype),
        compiler_params=_cparams(("parallel", "parallel")),
        name=name,
    )(a, w)


def _q_up_kernel(cq_ref, g_ref, w_ref, pos_ref, invf_ref, qn_ref, qr_ref):
    h = _rms(cq_ref[...].astype(F32), g_ref[...]).astype(BF16)
    r = jnp.dot(h, w_ref[...], preferred_element_type=F32)
    qn_ref[...] = r[:, :MLA_WIDTH].astype(BF16)
    cos, sin = _rope_tables(pos_ref[...], invf_ref[...])
    n_rope = MLA_HEADS * MLA_ROPE
    for c in range(n_rope // LANES):
        a = r[:, MLA_WIDTH + c * LANES: MLA_WIDTH + (c + 1) * LANES]
        b = r[:, MLA_WIDTH + n_rope + c * LANES: MLA_WIDTH + n_rope + (c + 1) * LANES]
        qr_ref[:, c * LANES:(c + 1) * LANES] = (a * cos + b * sin).astype(BF16)


def q_up(proj, g, w, pos_col, invf, tm=512):
    m = proj.shape[0]
    n_rope = MLA_HEADS * MLA_ROPE
    return pl.pallas_call(
        _q_up_kernel,
        grid=(m // tm,),
        in_specs=[pl.BlockSpec((tm, Q_LORA), lambda i: (i, COL_CQ // Q_LORA)),
                  pl.BlockSpec((1, Q_LORA), lambda i: (0, 0)),
                  pl.BlockSpec(w.shape, lambda i: (0, 0)),
                  pl.BlockSpec((tm, 1), lambda i: (i, 0)),
                  pl.BlockSpec((1, LANES), lambda i: (0, 0))],
        out_specs=[pl.BlockSpec((tm, MLA_WIDTH), lambda i: (i, 0)),
                   pl.BlockSpec((tm, n_rope), lambda i: (i, 0))],
        out_shape=[jax.ShapeDtypeStruct((m, MLA_WIDTH), BF16),
                   jax.ShapeDtypeStruct((m, n_rope), BF16)],
        compiler_params=_cparams(("parallel",)),
        name="q_up",
    )(proj, g, w, pos_col, invf)


def _kv_up_kernel(ckv_ref, g_ref, w_ref, krab_ref, pos_ref, invf_ref, kn_ref, v_ref, kr_ref):
    h = _rms(ckv_ref[...].astype(F32), g_ref[...]).astype(BF16)
    r = jnp.dot(h, w_ref[...], preferred_element_type=F32)
    kn_ref[...] = r[:, :MLA_WIDTH].astype(BF16)
    v_ref[...] = r[:, MLA_WIDTH:].astype(BF16)
    cos, sin = _rope_tables(pos_ref[...], invf_ref[...])
    lane = lax.broadcasted_iota(jnp.int32, cos.shape, 1)
    lo = lane < MLA_ROPE
    prod = krab_ref[...].astype(F32) * jnp.where(lo, cos, sin)
    kk = prod + pltpu.roll(prod, MLA_ROPE, 1)
    zero = jnp.zeros_like(kk)
    kr_ref[:, :LANES] = jnp.where(lo, kk, zero).astype(BF16)
    kr_ref[:, LANES:] = jnp.where(lo, zero, kk).astype(BF16)


def kv_up(proj, g, w, pos_col, invf, tm=512):
    m = proj.shape[0]
    return pl.pallas_call(
        _kv_up_kernel,
        grid=(m // tm,),
        in_specs=[pl.BlockSpec((tm, KV_LORA), lambda i: (i, COL_CKV // KV_LORA)),
                  pl.BlockSpec((1, KV_LORA), lambda i: (0, 0)),
                  pl.BlockSpec(w.shape, lambda i: (0, 0)),
                  pl.BlockSpec((tm, LANES), lambda i: (i, COL_KR // LANES)),
                  pl.BlockSpec((tm, 1), lambda i: (i, 0)),
                  pl.BlockSpec((1, LANES), lambda i: (0, 0))],
        out_specs=[pl.BlockSpec((tm, MLA_WIDTH), lambda i: (i, 0)),
                   pl.BlockSpec((tm, MLA_WIDTH), lambda i: (i, 0)),
                   pl.BlockSpec((tm, 2 * LANES), lambda i: (i, 0))],
        out_shape=[jax.ShapeDtypeStruct((m, MLA_WIDTH), BF16),
                   jax.ShapeDtypeStruct((m, MLA_WIDTH), BF16),
                   jax.ShapeDtypeStruct((m, 2 * LANES), BF16)],
        compiler_params=_cparams(("parallel",)),
        name="kv_up",
    )(proj, g, w, proj, pos_col, invf)


def _mla_kernel(qn_ref, qr_ref, kn_ref, kr_ref, v_ref, o_ref, m_sc, l_sc, acc_sc, *, tq):
    qi = pl.program_id(2)
    q = jnp.concatenate([qn_ref[...], qr_ref[...]], axis=-1)
    m_sc[...] = jnp.full(m_sc.shape, -jnp.inf, F32)
    l_sc[...] = jnp.zeros(l_sc.shape, F32)
    acc_sc[...] = jnp.zeros(acc_sc.shape, F32)

    def step(kb, masked):
        off = pl.multiple_of(kb * tq, tq)
        k = jnp.concatenate([kn_ref[pl.ds(off, tq), :], kr_ref[pl.ds(off, tq), :]], axis=-1)
        s = lax.dot_general(q, k, (((1,), (1,)), ((), ())), preferred_element_type=F32)
        if masked:
            row = lax.broadcasted_iota(jnp.int32, s.shape, 0)
            col = lax.broadcasted_iota(jnp.int32, s.shape, 1)
            s = jnp.where(row >= col, s, -jnp.inf)
        m_old = m_sc[...]
        m_new = jnp.maximum(m_old, jnp.max(s, axis=-1, keepdims=True))
        alpha = jnp.exp(m_old - m_new)
        p = jnp.exp(s - m_new)
        l_sc[...] = alpha * l_sc[...] + jnp.sum(p, axis=-1, keepdims=True)
        acc_sc[...] = alpha * acc_sc[...] + jnp.dot(p.astype(BF16), v_ref[pl.ds(off, tq), :],
                                                    preferred_element_type=F32)
        m_sc[...] = m_new

    def body(kb, carry):
        step(kb, False)
        return carry

    lax.fori_loop(0, qi, body, 0)
    step(qi, True)
    o_ref[...] = (acc_sc[...] / l_sc[...]).astype(o_ref.dtype)


def mla_attention(qn, qr, kn, kr, v, batch, seq, tq=512):
    nq = seq // tq
    return pl.pallas_call(
        functools.partial(_mla_kernel, tq=tq),
        grid=(batch, MLA_HEADS, nq),
        in_specs=[pl.BlockSpec((tq, MLA_NOPE), lambda b, h, i: (b * nq + i, h)),
                  pl.BlockSpec((tq, LANES), lambda b, h, i: (b * nq + i, h // 2)),
                  pl.BlockSpec((seq, MLA_NOPE), lambda b, h, i: (b, h)),
                  pl.BlockSpec((seq, LANES), lambda b, h, i: (b, h % 2)),
                  pl.BlockSpec((seq, MLA_V), lambda b, h, i: (b, h))],
        out_specs=pl.BlockSpec((tq, MLA_V), lambda b, h, i: (b * nq + i, h)),
        out_shape=jax.ShapeDtypeStruct((batch * seq, MLA_WIDTH), BF16),
        scratch_shapes=[pltpu.VMEM((tq, 1), F32), pltpu.VMEM((tq, 1), F32),
                        pltpu.VMEM((tq, MLA_V), F32)],
        compiler_params=_cparams(("parallel", "parallel", "parallel")),
        name="mla_attention",
    )(qn, qr, kn, kr, v)


def _swa_kernel(sinks_ref, q0_ref, q1_ref, q2_ref, q3_ref, kc_ref, kp_ref, vc_ref, vp_ref,
                pq_ref, pkc_ref, pkp_ref, o_ref):
    blk = pl.program_id(1)
    q_refs = (q0_ref, q1_ref, q2_ref, q3_ref)
    k_all = jnp.concatenate([kp_ref[...], kc_ref[...]], axis=0).astype(F32)
    v_all = jnp.concatenate([vp_ref[...], vc_ref[...]], axis=0).astype(F32)
    pk = jnp.concatenate([pkp_ref[0], pkc_ref[0]], axis=-1).astype(F32)
    dist = jnp.abs(pq_ref[...].astype(F32) - pk)
    row = lax.broadcasted_iota(jnp.int32, dist.shape, 0)
    col = lax.broadcasted_iota(jnp.int32, dist.shape, 1)
    rel = BLOCK + row - col
    first_key = jnp.where(blk > 0, 0, BLOCK)
    valid = (rel >= 0) & (rel < WINDOW) & (col >= first_key)
    lane = lax.broadcasted_iota(jnp.int32, (2 * BLOCK, LANES), 1)
    lo = lane < SWA_HEAD_DIM
    lane_o = lax.broadcasted_iota(jnp.int32, (BLOCK, LANES), 1)
    lo_o = lane_o < SWA_HEAD_DIM

    def dup(x_all, g):
        pair = x_all[:, (g // 2) * LANES:(g // 2 + 1) * LANES]
        rolled = pltpu.roll(pair, SWA_HEAD_DIM, 1)
        return jnp.where(lo, pair, rolled) if g % 2 == 0 else jnp.where(lo, rolled, pair)

    for g in range(SWA_KV_HEADS):
        kk = dup(k_all, g)
        vv = dup(v_all, g).astype(BF16)
        zero = jnp.zeros_like(kk)
        k_half = (jnp.where(lo, kk, zero).astype(BF16), jnp.where(lo, zero, kk).astype(BF16))
        for i in range(SWA_GROUP // 2):
            q_pair = q_refs[g][:, i * LANES:(i + 1) * LANES]
            outs = []
            for j in range(2):
                head = g * SWA_GROUP + 2 * i + j
                slope = 2.0 ** (-8.0 * (head + 1) / SWA_HEADS)
                s = lax.dot_general(q_pair, k_half[j], (((1,), (1,)), ((), ())),
                                    preferred_element_type=F32)
                s = jnp.where(valid, s - slope * dist, -jnp.inf)
                sink = sinks_ref[head]
                m = jnp.maximum(jnp.max(s, axis=-1, keepdims=True), sink)
                e = jnp.exp(s - m)
                denom = jnp.sum(e, axis=-1, keepdims=True) + jnp.exp(sink - m)
                p = (e / denom).astype(BF16)
                outs.append(jnp.dot(p, vv, preferred_element_type=F32))
            col0 = (g * SWA_GROUP // 2 + i) * LANES
            o_ref[:, col0:col0 + LANES] = jnp.where(lo_o, outs[0], outs[1]).astype(o_ref.dtype)


def swa_attention(proj, sinks, pos_col, pos_row, batch, seq):
    nblk = seq // BLOCK
    kvw = SWA_KV_HEADS * SWA_HEAD_DIM
    gw = SWA_GROUP * SWA_HEAD_DIM
    cur = lambda b, n: b * nblk + n
    prev = lambda b, n: b * nblk + jnp.maximum(n - 1, 0)
    q_specs = [pl.BlockSpec((BLOCK, gw), functools.partial(lambda b, n, g: (cur(b, n), COL_QSWA // gw + g), g=g))
               for g in range(SWA_KV_HEADS)]
    return pl.pallas_call(
        _swa_kernel,
        grid=(batch, nblk),
        in_specs=[pl.BlockSpec(memory_space=pltpu.SMEM)] + q_specs + [
            pl.BlockSpec((BLOCK, kvw), lambda b, n: (cur(b, n), COL_KSWA // kvw)),
            pl.BlockSpec((BLOCK, kvw), lambda b, n: (prev(b, n), COL_KSWA // kvw)),
            pl.BlockSpec((BLOCK, kvw), lambda b, n: (cur(b, n), COL_VSWA // kvw)),
            pl.BlockSpec((BLOCK, kvw), lambda b, n: (prev(b, n), COL_VSWA // kvw)),
            pl.BlockSpec((BLOCK, 1), lambda b, n: (cur(b, n), 0)),
            pl.BlockSpec((1, 1, BLOCK), lambda b, n: (cur(b, n), 0, 0)),
            pl.BlockSpec((1, 1, BLOCK), lambda b, n: (prev(b, n), 0, 0))],
        out_specs=pl.BlockSpec((BLOCK, SWA_WIDTH), lambda b, n: (cur(b, n), 0)),
        out_shape=jax.ShapeDtypeStruct((batch * seq, SWA_WIDTH), BF16),
        compiler_params=_cparams(("parallel", "parallel")),
        name="swa_attention",
    )(sinks, proj, proj, proj, proj, proj, proj, proj, proj, pos_col, pos_row, pos_row)


def _wo_kernel(oa_ref, ob_ref, ga_ref, gb_ref, wa_ref, wb_ref, y_ref, na_sc, nb_sc):
    @pl.when(pl.program_id(1) == 0)
    def _():
        na_sc[...] = _rms(oa_ref[...].astype(F32), ga_ref[...]).astype(BF16)
        nb_sc[...] = _rms(ob_ref[...].astype(F32), gb_ref[...]).astype(BF16)

    y_ref[...] = (jnp.dot(na_sc[...], wa_ref[...], preferred_element_type=F32)
                  + jnp.dot(nb_sc[...], wb_ref[...], preferred_element_type=F32))


def out_proj(oa, ob, ga, gb, w, tm=512, tn=1024):
    m, ka = oa.shape
    kb = ob.shape[1]
    n = w.shape[1]
    return pl.pallas_call(
        _wo_kernel,
        grid=(m // tm, n // tn),
        in_specs=[pl.BlockSpec((tm, ka), lambda i, j: (i, 0)),
                  pl.BlockSpec((tm, kb), lambda i, j: (i, 0)),
                  pl.BlockSpec((1, ka), lambda i, j: (0, 0)),
                  pl.BlockSpec((1, kb), lambda i, j: (0, 0)),
                  pl.BlockSpec((ka, tn), lambda i, j: (0, j)),
                  pl.BlockSpec((kb, tn), lambda i, j: (1, j))],
        out_specs=pl.BlockSpec((tm, tn), lambda i, j: (i, j)),
        out_shape=jax.ShapeDtypeStruct((m, n), F32),
        scratch_shapes=[pltpu.VMEM((tm, ka), BF16), pltpu.VMEM((tm, kb), BF16)],
        compiler_params=_cparams(("parallel", "arbitrary")),
        name="out_proj",
    )(oa, ob, ga, gb, w, w)


def _resid_kernel(y_ref, x_ref, gy_ref, gn_ref, xo_ref, ho_ref, *, norm_next):
    xn = x_ref[...] + _rms(y_ref[...], gy_ref[...])
    xo_ref[...] = xn
    ho_ref[...] = (_rms(xn, gn_ref[...]) if norm_next else xn).astype(ho_ref.dtype)


def resid_norm(y, x, gy, gn, norm_next, tm=256):
    m, d = x.shape
    row = pl.BlockSpec((tm, d), lambda i: (i, 0))
    vec = pl.BlockSpec((1, d), lambda i: (0, 0))
    return pl.pallas_call(
        functools.partial(_resid_kernel, norm_next=norm_next),
        grid=(m // tm,),
        in_specs=[row, row, vec, vec],
        out_specs=[row, row],
        out_shape=[jax.ShapeDtypeStruct((m, d), F32), jax.ShapeDtypeStruct((m, d), BF16)],
        compiler_params=_cparams(("parallel",)),
        name="resid_norm",
    )(y, x, gy, gn)


def _ffn_kernel(h_ref, wg_ref, wu_ref, wd_ref, o_ref):
    @pl.when(pl.program_id(1) == 0)
    def _():
        o_ref[...] = jnp.zeros(o_ref.shape, o_ref.dtype)

    h = h_ref[...]
    g = jnp.dot(h, wg_ref[...], preferred_element_type=F32)
    u = jnp.dot(h, wu_ref[...], preferred_element_type=F32)
    a = (g * jax.nn.sigmoid(g) * u).astype(BF16)
    o_ref[...] += jnp.dot(a, wd_ref[...], preferred_element_type=F32)


def ffn(h, wg, wu, wd, tm=512, tf=256):
    m, d = h.shape
    f = wg.shape[1]
    return pl.pallas_call(
        _ffn_kernel,
        grid=(m // tm, f // tf),
        in_specs=[pl.BlockSpec((tm, d), lambda i, j: (i, 0)),
                  pl.BlockSpec((d, tf), lambda i, j: (0, j)),
                  pl.BlockSpec((d, tf), lambda i, j: (0, j)),
                  pl.BlockSpec((tf, d), lambda i, j: (j, 0))],
        out_specs=pl.BlockSpec((tm, d), lambda i, j: (i, 0)),
        out_shape=jax.ShapeDtypeStruct((m, d), F32),
        compiler_params=_cparams(("parallel", "arbitrary")),
        name="ffn",
    )(h, wg, wu, wd)


def _ple_kernel(xb_ref, wg_ref, p_ref, wp_ref, x_ref, o_ref):
    gate = jax.nn.sigmoid(jnp.dot(xb_ref[...], wg_ref[...], preferred_element_type=F32))
    e = jnp.dot(p_ref[...].astype(BF16), wp_ref[...], preferred_element_type=F32)
    o_ref[...] = x_ref[...] + gate * e


def ple(xb, wg, p, wp, x, tm=512, tn=1024):
    m, d = xb.shape
    n = wg.shape[1]
    pd = p.shape[1]
    return pl.pallas_call(
        _ple_kernel,
        grid=(m // tm, n // tn),
        in_specs=[pl.BlockSpec((tm, d), lambda i, j: (i, 0)),
                  pl.BlockSpec((d, tn), lambda i, j: (0, j)),
                  pl.BlockSpec((tm, pd), lambda i, j: (i, 0)),
                  pl.BlockSpec((pd, tn), lambda i, j: (0, j)),
                  pl.BlockSpec((tm, tn), lambda i, j: (i, j))],
        out_specs=pl.BlockSpec((tm, tn), lambda i, j: (i, j)),
        out_shape=jax.ShapeDtypeStruct((m, n), F32),
        compiler_params=_cparams(("parallel", "parallel")),
        name="ple",
    )(xb, wg, p, wp, x)


def _prep_w_in(w):
    c_q, c_kv, k_r, q_s, k_s, v_s = jnp.split(
        w, [Q_LORA, Q_LORA + KV_LORA, Q_LORA + KV_LORA + MLA_ROPE,
            Q_LORA + KV_LORA + MLA_ROPE + SWA_WIDTH,
            Q_LORA + KV_LORA + MLA_ROPE + SWA_WIDTH + SWA_KV_HEADS * SWA_HEAD_DIM], axis=1)
    q_s = q_s * (1.0 / math.sqrt(SWA_HEAD_DIM))
    return jnp.concatenate([c_q, c_kv, q_s, k_s, v_s, k_r, _swap_half(k_r)], axis=1).astype(BF16)


def _prep_w_q_up(w):
    w = (w * (1.0 / math.sqrt(MLA_NOPE + MLA_ROPE))).reshape(Q_LORA, MLA_HEADS, MLA_NOPE + MLA_ROPE)
    nope = w[:, :, :MLA_NOPE].reshape(Q_LORA, MLA_WIDTH)
    rope = w[:, :, MLA_NOPE:]
    n_rope = MLA_HEADS * MLA_ROPE
    return jnp.concatenate([nope, rope.reshape(Q_LORA, n_rope),
                            _swap_half(rope).reshape(Q_LORA, n_rope)], axis=1).astype(BF16)


def _prep_w_kv_up(w):
    w = w.reshape(KV_LORA, MLA_HEADS, MLA_NOPE + MLA_V)
    return jnp.concatenate([w[:, :, :MLA_NOPE].reshape(KV_LORA, MLA_WIDTH),
                            w[:, :, MLA_NOPE:].reshape(KV_LORA, MLA_WIDTH)], axis=1).astype(BF16)


def _layer(x, p, pos_col, pos_row, invf, batch, seq, attn_pre_norm, w_in, q_a_norm, w_q_up,
           kv_a_norm, w_kv_up, sinks, mla_out_norm, swa_out_norm, w_o, attn_post_norm,
           ffn_pre_norm, w_gate, w_up, w_down, ffn_post_norm, w_ple_gate, w_ple_proj):
    row = lambda g: g.reshape(1, -1)
    h = norm_cast(x, row(attn_pre_norm))
    proj = matmul(h, _prep_w_in(w_in), BF16, tm=512, tn=D_IN2 // 3, name="in_proj")
    qn, qr = q_up(proj, row(q_a_norm), _prep_w_q_up(w_q_up), pos_col, invf)
    kn, v, kr = kv_up(proj, row(kv_a_norm), _prep_w_kv_up(w_kv_up), pos_col, invf)
    o_mla = mla_attention(qn, qr, kn, kr, v, batch, seq)
    o_swa = swa_attention(proj, sinks, pos_col, pos_row, batch, seq)
    y = out_proj(o_mla, o_swa, row(mla_out_norm), row(swa_out_norm), w_o.astype(BF16))
    x1, h2 = resid_norm(y, x, row(attn_post_norm), row(ffn_pre_norm), True)
    f = ffn(h2, w_gate.astype(BF16), w_up.astype(BF16), w_down.astype(BF16))
    x2, x2b = resid_norm(f, x1, row(ffn_post_norm), row(ffn_post_norm), False)
    return ple(x2b, w_ple_gate.astype(BF16), p, w_ple_proj.astype(BF16), x2)


def kernel(x, p, positions, attn_pre_norm, w_in, q_a_norm, w_q_up, kv_a_norm, w_kv_up, sinks,
           mla_out_norm, swa_out_norm, w_o, attn_post_norm, ffn_pre_norm, w_gate, w_up, w_down,
           ffn_post_norm, w_ple_gate, w_ple_proj):
    batch, seq, d = x.shape
    depth = w_in.shape[0]
    t = batch * seq
    pos_col = positions.reshape(t, 1)
    pos_row = positions.reshape(t // BLOCK, 1, BLOCK)
    half = MLA_ROPE // 2
    invf = np.asarray(ROPE_THETA ** (-(np.arange(LANES) % half) * 2.0 / MLA_ROPE), np.float32).reshape(1, LANES)
    invf = jnp.asarray(invf)
    xf = x.reshape(t, d)
    for i in range(depth):
        xf = _layer(xf, p[i].reshape(t, PLE_DIM), pos_col, pos_row, invf, batch, seq,
                    attn_pre_norm[i], w_in[i], q_a_norm[i], w_q_up[i], kv_a_norm[i], w_kv_up[i],
                    sinks[i], mla_out_norm[i], swa_out_norm[i], w_o[i], attn_post_norm[i],
                    ffn_pre_norm[i], w_gate[i], w_up[i], w_down[i], ffn_post_norm[i],
                    w_ple_gate[i], w_ple_proj[i])
    return xf.reshape(batch, seq, d)
```

```python
import functools
import math

import numpy as np
import jax
import jax.numpy as jnp
from jax import lax
from jax.experimental import pallas as pl
from jax.experimental.pallas import tpu as pltpu

D_MODEL = 4096
PLE_DIM = 256
MLA_HEADS = 16
MLA_NOPE = 128
MLA_ROPE = 64
MLA_V = 128
Q_LORA = 1024
KV_LORA = 512
MLA_WIDTH = MLA_HEADS * MLA_V
SWA_HEADS = 32
SWA_KV_HEADS = 4
SWA_HEAD_DIM = 64
SWA_GROUP = SWA_HEADS // SWA_KV_HEADS
SWA_WIDTH = SWA_HEADS * SWA_HEAD_DIM
WINDOW = 128
BLOCK = 128
ROPE_THETA = 10000.0
NORM_EPS = 1e-6
D_FF = 11008

LANES = 128
MXU_N = 256
MLA_TILE = 512
VMEM_LIMIT = 56 * 1024 * 1024

COL_CQ = 0
COL_CKV = COL_CQ + Q_LORA
COL_QSWA = COL_CKV + KV_LORA
COL_KSWA = COL_QSWA + SWA_WIDTH
COL_VSWA = COL_KSWA + SWA_KV_HEADS * SWA_HEAD_DIM
COL_KR = COL_VSWA + SWA_KV_HEADS * SWA_HEAD_DIM
D_IN2 = COL_KR + 2 * MLA_ROPE

F32 = jnp.float32
BF16 = jnp.bfloat16


def _cparams(sem):
    return pltpu.CompilerParams(dimension_semantics=sem, vmem_limit_bytes=VMEM_LIMIT)


def _rms(xf, g):
    ms = jnp.mean(xf * xf, axis=-1, keepdims=True)
    return xf * lax.rsqrt(ms + NORM_EPS) * g


def _swap_half(w):
    d = w.shape[-1]
    return jnp.concatenate([-w[..., d // 2:], w[..., : d // 2]], axis=-1)


def _rope_tables(pos_col, invf_row):
    ang = pos_col.astype(F32) * invf_row
    return jnp.cos(ang), jnp.sin(ang)


def _norm_cast_kernel(x_ref, g_ref, o_ref):
    o_ref[...] = _rms(x_ref[...], g_ref[...]).astype(o_ref.dtype)


def norm_cast(x, g, tm=256):
    m, d = x.shape
    return pl.pallas_call(
        _norm_cast_kernel,
        grid=(m // tm,),
        in_specs=[pl.BlockSpec((tm, d), lambda i: (i, 0)),
                  pl.BlockSpec((1, d), lambda i: (0, 0))],
        out_specs=pl.BlockSpec((tm, d), lambda i: (i, 0)),
        out_shape=jax.ShapeDtypeStruct((m, d), BF16),
        compiler_params=_cparams(("parallel",)),
        name="norm_cast",
    )(x, g)


def _matmul_kernel(a_ref, w_ref, o_ref):
    o_ref[...] = jnp.dot(a_ref[...], w_ref[...], preferred_element_type=F32).astype(o_ref.dtype)


def matmul(a, w, out_dtype, tm, tn, name):
    m, k = a.shape
    n = w.shape[1]
    return pl.pallas_call(
        _matmul_kernel,
        grid=(n // tn, m // tm),
        in_specs=[pl.BlockSpec((tm, k), lambda j, i: (i, 0)),
                  pl.BlockSpec((k, tn), lambda j, i: (0, j))],
        out_specs=pl.BlockSpec((tm, tn), lambda j, i: (i, j)),
        out_shape=jax.ShapeDtypeStruct((m, n), out_dtype),
        compiler_params=_cparams(("parallel", "parallel")),
        name=name,
    )(a, w)


def _q_up_kernel(cq_ref, g_ref, w_ref, pos_ref, invf_ref, qn_ref, qr_ref):
    h = _rms(cq_ref[...].astype(F32), g_ref[...]).astype(BF16)
    r = jnp.dot(h, w_ref[...], preferred_element_type=F32)
    qn_ref[...] = r[:, :MLA_WIDTH].astype(BF16)
    cos, sin = _rope_tables(pos_ref[...], invf_ref[...])
    n_rope = MLA_HEADS * MLA_ROPE
    for c in range(n_rope // LANES):
        a = r[:, MLA_WIDTH + c * LANES: MLA_WIDTH + (c + 1) * LANES]
        b = r[:, MLA_WIDTH + n_rope + c * LANES: MLA_WIDTH + n_rope + (c + 1) * LANES]
        qr_ref[:, c * LANES:(c + 1) * LANES] = (a * cos + b * sin).astype(BF16)


def q_up(proj, g, w, pos_col, invf, tm=512):
    m = proj.shape[0]
    n_rope = MLA_HEADS * MLA_ROPE
    return pl.pallas_call(
        _q_up_kernel,
        grid=(m // tm,),
        in_specs=[pl.BlockSpec((tm, Q_LORA), lambda i: (i, COL_CQ // Q_LORA)),
                  pl.BlockSpec((1, Q_LORA), lambda i: (0, 0)),
                  pl.BlockSpec(w.shape, lambda i: (0, 0)),
                  pl.BlockSpec((tm, 1), lambda i: (i, 0)),
                  pl.BlockSpec((1, LANES), lambda i: (0, 0))],
        out_specs=[pl.BlockSpec((tm, MLA_WIDTH), lambda i: (i, 0)),
                   pl.BlockSpec((tm, n_rope), lambda i: (i, 0))],
        out_shape=[jax.ShapeDtypeStruct((m, MLA_WIDTH), BF16),
                   jax.ShapeDtypeStruct((m, n_rope), BF16)],
        compiler_params=_cparams(("parallel",)),
        name="q_up",
    )(proj, g, w, pos_col, invf)


def _kv_up_kernel(ckv_ref, g_ref, wk_ref, wvt_ref, krab_ref, pos_ref, invf_ref, kn_ref, vt_ref, kr_ref):
    h = _rms(ckv_ref[...].astype(F32), g_ref[...]).astype(BF16)
    kn_ref[...] = jnp.dot(h, wk_ref[...], preferred_element_type=F32).astype(BF16)
    vt = lax.dot_general(wvt_ref[...], h, (((1,), (1,)), ((), ())), preferred_element_type=F32)
    for hh in range(MLA_HEADS):
        vt_ref[0, hh, 0] = vt[hh * MLA_V:(hh + 1) * MLA_V, :].astype(BF16)
    cos, sin = _rope_tables(pos_ref[...], invf_ref[...])
    lane = lax.broadcasted_iota(jnp.int32, cos.shape, 1)
    lo = lane < MLA_ROPE
    prod = krab_ref[...].astype(F32) * jnp.where(lo, cos, sin)
    kk = prod + pltpu.roll(prod, MLA_ROPE, 1)
    zero = jnp.zeros_like(kk)
    kr_ref[:, :LANES] = jnp.where(lo, kk, zero).astype(BF16)
    kr_ref[:, LANES:] = jnp.where(lo, zero, kk).astype(BF16)


def kv_up(proj, g, wk, wvt, pos_col, invf, batch, seq, tm=MLA_TILE):
    m = proj.shape[0]
    nk = seq // tm
    return pl.pallas_call(
        _kv_up_kernel,
        grid=(m // tm,),
        in_specs=[pl.BlockSpec((tm, KV_LORA), lambda i: (i, COL_CKV // KV_LORA)),
                  pl.BlockSpec((1, KV_LORA), lambda i: (0, 0)),
                  pl.BlockSpec(wk.shape, lambda i: (0, 0)),
                  pl.BlockSpec(wvt.shape, lambda i: (0, 0)),
                  pl.BlockSpec((tm, LANES), lambda i: (i, COL_KR // LANES)),
                  pl.BlockSpec((tm, 1), lambda i: (i, 0)),
                  pl.BlockSpec((1, LANES), lambda i: (0, 0))],
        out_specs=[pl.BlockSpec((tm, MLA_WIDTH), lambda i: (i, 0)),
                   pl.BlockSpec((1, MLA_HEADS, 1, MLA_V, tm), lambda i: (i // nk, 0, i % nk, 0, 0)),
                   pl.BlockSpec((tm, 2 * LANES), lambda i: (i, 0))],
        out_shape=[jax.ShapeDtypeStruct((m, MLA_WIDTH), BF16),
                   jax.ShapeDtypeStruct((batch, MLA_HEADS, nk, MLA_V, tm), BF16),
                   jax.ShapeDtypeStruct((m, 2 * LANES), BF16)],
        compiler_params=_cparams(("parallel",)),
        name="kv_up",
    )(proj, g, wk, wvt, proj, pos_col, invf)


def _mla_kernel(qn_ref, qr_ref, kn_ref, kr_ref, vt_ref, o_ref, s_a, s_b, m_sc, l_sc, acc_sc, *, tq):
    qi = pl.program_id(2)
    q = jnp.concatenate([qn_ref[...], qr_ref[...]], axis=-1)
    m_sc[...] = jnp.full(m_sc.shape, -jnp.inf, F32)
    l_sc[...] = jnp.zeros(l_sc.shape, F32)
    acc_sc[...] = jnp.zeros(acc_sc.shape, F32)

    def scores(kb, s_ref):
        off = pl.multiple_of(kb * tq, tq)
        k = jnp.concatenate([kn_ref[pl.ds(off, tq), :], kr_ref[pl.ds(off, tq), :]], axis=-1)
        s_ref[...] = lax.dot_general(k, q, (((1,), (1,)), ((), ())), preferred_element_type=F32)

    def softmax_pv(s_ref, kb, masked):
        vt = vt_ref[0, 0, kb]
        for c in range(tq // MXU_N):
            sl = slice(c * MXU_N, (c + 1) * MXU_N)

            def s_chunk():
                s = s_ref[:, sl]
                if masked:
                    key = lax.broadcasted_iota(jnp.int32, s.shape, 0)
                    qry = lax.broadcasted_iota(jnp.int32, s.shape, 1) + c * MXU_N
                    s = jnp.where(qry >= key, s, -jnp.inf)
                return s

            m_old = m_sc[:, sl]
            m_new = jnp.maximum(m_old, jnp.max(s_chunk(), axis=0, keepdims=True))
            alpha = jnp.exp(m_old - m_new)
            p = jnp.exp(s_chunk() - m_new)
            l_sc[:, sl] = alpha * l_sc[:, sl] + jnp.sum(p, axis=0, keepdims=True)
            acc_sc[:, sl] = alpha * acc_sc[:, sl] + jnp.dot(vt, p.astype(BF16),
                                                            preferred_element_type=F32)
            m_sc[:, sl] = m_new

    scores(0, s_a)

    def body(i, carry):
        scores(2 * i + 1, s_b)
        softmax_pv(s_a, 2 * i, False)
        scores(2 * i + 2, s_a)
        softmax_pv(s_b, 2 * i + 1, False)
        return carry

    lax.fori_loop(0, qi // 2, body, 0)

    @pl.when(qi % 2 == 0)
    def _():
        softmax_pv(s_a, qi, True)

    @pl.when(qi % 2 == 1)
    def _():
        scores(qi, s_b)
        softmax_pv(s_a, qi - 1, False)
        softmax_pv(s_b, qi, True)

    o_ref[...] = (acc_sc[...] / l_sc[...]).T.astype(o_ref.dtype)


def mla_attention(qn, qr, kn, kr, vt, batch, seq, tq=MLA_TILE):
    nq = seq // tq
    return pl.pallas_call(
        functools.partial(_mla_kernel, tq=tq),
        grid=(batch, MLA_HEADS, nq),
        in_specs=[pl.BlockSpec((tq, MLA_NOPE), lambda b, h, i: (b * nq + i, h)),
                  pl.BlockSpec((tq, LANES), lambda b, h, i: (b * nq + i, h // 2)),
                  pl.BlockSpec((seq, MLA_NOPE), lambda b, h, i: (b, h)),
                  pl.BlockSpec((seq, LANES), lambda b, h, i: (b, h % 2)),
                  pl.BlockSpec((1, 1, nq, MLA_V, tq), lambda b, h, i: (b, h, 0, 0, 0))],
        out_specs=pl.BlockSpec((tq, MLA_V), lambda b, h, i: (b * nq + i, h)),
        out_shape=jax.ShapeDtypeStruct((batch * seq, MLA_WIDTH), BF16),
        scratch_shapes=[pltpu.VMEM((tq, tq), F32), pltpu.VMEM((tq, tq), F32),
                        pltpu.VMEM((1, tq), F32), pltpu.VMEM((1, tq), F32),
                        pltpu.VMEM((MLA_V, tq), F32)],
        compiler_params=_cparams(("parallel", "parallel", "parallel")),
        name="mla_attention",
    )(qn, qr, kn, kr, vt)


def _swa_kernel(sinks_ref, q0_ref, q1_ref, q2_ref, q3_ref, kc_ref, kp_ref, vc_ref, vp_ref,
                pq_ref, pkc_ref, pkp_ref, o_ref):
    blk = pl.program_id(1)
    q_refs = (q0_ref, q1_ref, q2_ref, q3_ref)
    k_all = jnp.concatenate([kp_ref[...], kc_ref[...]], axis=0).astype(F32)
    v_all = jnp.concatenate([vp_ref[...], vc_ref[...]], axis=0).astype(F32)
    pk = jnp.concatenate([pkp_ref[0], pkc_ref[0]], axis=-1).astype(F32)
    dist = jnp.abs(pq_ref[...].astype(F32) - pk)
    row = lax.broadcasted_iota(jnp.int32, dist.shape, 0)
    col = lax.broadcasted_iota(jnp.int32, dist.shape, 1)
    rel = BLOCK + row - col
    first_key = jnp.where(blk > 0, 0, BLOCK)
    valid = (rel >= 0) & (rel < WINDOW) & (col >= first_key)
    lane = lax.broadcasted_iota(jnp.int32, (2 * BLOCK, LANES), 1)
    lo = lane < SWA_HEAD_DIM
    lane_o = lax.broadcasted_iota(jnp.int32, (BLOCK, LANES), 1)
    lo_o = lane_o < SWA_HEAD_DIM

    def dup(x_all, g):
        pair = x_all[:, (g // 2) * LANES:(g // 2 + 1) * LANES]
        rolled = pltpu.roll(pair, SWA_HEAD_DIM, 1)
        return jnp.where(lo, pair, rolled) if g % 2 == 0 else jnp.where(lo, rolled, pair)

    for g in range(SWA_KV_HEADS):
        kk = dup(k_all, g)
        vv = dup(v_all, g).astype(BF16)
        zero = jnp.zeros_like(kk)
        k_half = (jnp.where(lo, kk, zero).astype(BF16), jnp.where(lo, zero, kk).astype(BF16))
        for i in range(SWA_GROUP // 2):
            q_pair = q_refs[g][:, i * LANES:(i + 1) * LANES]
            outs = []
            for j in range(2):
                head = g * SWA_GROUP + 2 * i + j
                slope = 2.0 ** (-8.0 * (head + 1) / SWA_HEADS)
                s = lax.dot_general(q_pair, k_half[j], (((1,), (1,)), ((), ())),
                                    preferred_element_type=F32)
                s = jnp.where(valid, s - slope * dist, -jnp.inf)
                sink = sinks_ref[head]
                m = jnp.maximum(jnp.max(s, axis=-1, keepdims=True), sink)
                e = jnp.exp(s - m)
                denom = jnp.sum(e, axis=-1, keepdims=True) + jnp.exp(sink - m)
                p = (e / denom).astype(BF16)
                outs.append(jnp.dot(p, vv, preferred_element_type=F32))
            col0 = (g * SWA_GROUP // 2 + i) * LANES
            o_ref[:, col0:col0 + LANES] = jnp.where(lo_o, outs[0], outs[1]).astype(o_ref.dtype)


def swa_attention(proj, sinks, pos_col, pos_row, batch, seq):
    nblk = seq // BLOCK
    kvw = SWA_KV_HEADS * SWA_HEAD_DIM
    gw = SWA_GROUP * SWA_HEAD_DIM
    cur = lambda b, n: b * nblk + n
    prev = lambda b, n: b * nblk + jnp.maximum(n - 1, 0)
    q_specs = [pl.BlockSpec((BLOCK, gw), functools.partial(lambda b, n, g: (cur(b, n), COL_QSWA // gw + g), g=g))
               for g in range(SWA_KV_HEADS)]
    return pl.pallas_call(
        _swa_kernel,
        grid=(batch, nblk),
        in_specs=[pl.BlockSpec(memory_space=pltpu.SMEM)] + q_specs + [
            pl.BlockSpec((BLOCK, kvw), lambda b, n: (cur(b, n), COL_KSWA // kvw)),
            pl.BlockSpec((BLOCK, kvw), lambda b, n: (prev(b, n), COL_KSWA // kvw)),
            pl.BlockSpec((BLOCK, kvw), lambda b, n: (cur(b, n), COL_VSWA // kvw)),
            pl.BlockSpec((BLOCK, kvw), lambda b, n: (prev(b, n), COL_VSWA // kvw)),
            pl.BlockSpec((BLOCK, 1), lambda b, n: (cur(b, n), 0)),
            pl.BlockSpec((1, 1, BLOCK), lambda b, n: (cur(b, n), 0, 0)),
            pl.BlockSpec((1, 1, BLOCK), lambda b, n: (prev(b, n), 0, 0))],
        out_specs=pl.BlockSpec((BLOCK, SWA_WIDTH), lambda b, n: (cur(b, n), 0)),
        out_shape=jax.ShapeDtypeStruct((batch * seq, SWA_WIDTH), BF16),
        compiler_params=_cparams(("parallel", "parallel")),
        name="swa_attention",
    )(sinks, proj, proj, proj, proj, proj, proj, proj, proj, pos_col, pos_row, pos_row)


def _wo_kernel(oa_ref, ob_ref, ga_ref, gb_ref, wa_ref, wb_ref, y_ref, na_sc, nb_sc):
    @pl.when(pl.program_id(1) == 0)
    def _():
        na_sc[...] = _rms(oa_ref[...].astype(F32), ga_ref[...]).astype(BF16)
        nb_sc[...] = _rms(ob_ref[...].astype(F32), gb_ref[...]).astype(BF16)

    y_ref[...] = (jnp.dot(na_sc[...], wa_ref[...], preferred_element_type=F32)
                  + jnp.dot(nb_sc[...], wb_ref[...], preferred_element_type=F32))


def out_proj(oa, ob, ga, gb, w, tm=512, tn=1024):
    m, ka = oa.shape
    kb = ob.shape[1]
    n = w.shape[1]
    return pl.pallas_call(
        _wo_kernel,
        grid=(m // tm, n // tn),
        in_specs=[pl.BlockSpec((tm, ka), lambda i, j: (i, 0)),
                  pl.BlockSpec((tm, kb), lambda i, j: (i, 0)),
                  pl.BlockSpec((1, ka), lambda i, j: (0, 0)),
                  pl.BlockSpec((1, kb), lambda i, j: (0, 0)),
                  pl.BlockSpec((ka, tn), lambda i, j: (0, j)),
                  pl.BlockSpec((kb, tn), lambda i, j: (1, j))],
        out_specs=pl.BlockSpec((tm, tn), lambda i, j: (i, j)),
        out_shape=jax.ShapeDtypeStruct((m, n), F32),
        scratch_shapes=[pltpu.VMEM((tm, ka), BF16), pltpu.VMEM((tm, kb), BF16)],
        compiler_params=_cparams(("parallel", "arbitrary")),
        name="out_proj",
    )(oa, ob, ga, gb, w, w)


def _resid_kernel(y_ref, x_ref, gy_ref, gn_ref, xo_ref, ho_ref, *, norm_next):
    xn = x_ref[...] + _rms(y_ref[...], gy_ref[...])
    xo_ref[...] = xn
    ho_ref[...] = (_rms(xn, gn_ref[...]) if norm_next else xn).astype(ho_ref.dtype)


def resid_norm(y, x, gy, gn, norm_next, tm=256):
    m, d = x.shape
    row = pl.BlockSpec((tm, d), lambda i: (i, 0))
    vec = pl.BlockSpec((1, d), lambda i: (0, 0))
    return pl.pallas_call(
        functools.partial(_resid_kernel, norm_next=norm_next),
        grid=(m // tm,),
        in_specs=[row, row, vec, vec],
        out_specs=[row, row],
        out_shape=[jax.ShapeDtypeStruct((m, d), F32), jax.ShapeDtypeStruct((m, d), BF16)],
        compiler_params=_cparams(("parallel",)),
        name="resid_norm",
    )(y, x, gy, gn)


def _ffn_kernel(h_ref, wg_ref, wu_ref, wd_ref, o_ref):
    @pl.when(pl.program_id(1) == 0)
    def _():
        o_ref[...] = jnp.zeros(o_ref.shape, o_ref.dtype)

    h = h_ref[...]
    g = jnp.dot(h, wg_ref[...], preferred_element_type=F32)
    u = jnp.dot(h, wu_ref[...], preferred_element_type=F32)
    a = (g * jax.nn.sigmoid(g) * u).astype(BF16)
    o_ref[...] += jnp.dot(a, wd_ref[...], preferred_element_type=F32)


def ffn(h, wg, wu, wd, tm=512, tf=256):
    m, d = h.shape
    f = wg.shape[1]
    return pl.pallas_call(
        _ffn_kernel,
        grid=(m // tm, f // tf),
        in_specs=[pl.BlockSpec((tm, d), lambda i, j: (i, 0)),
                  pl.BlockSpec((d, tf), lambda i, j: (0, j)),
                  pl.BlockSpec((d, tf), lambda i, j: (0, j)),
                  pl.BlockSpec((tf, d), lambda i, j: (j, 0))],
        out_specs=pl.BlockSpec((tm, d), lambda i, j: (i, 0)),
        out_shape=jax.ShapeDtypeStruct((m, d), F32),
        compiler_params=_cparams(("parallel", "arbitrary")),
        name="ffn",
    )(h, wg, wu, wd)


def _ple_kernel(xb_ref, wg_ref, p_ref, wp_ref, x_ref, o_ref):
    gate = jax.nn.sigmoid(jnp.dot(xb_ref[...], wg_ref[...], preferred_element_type=F32))
    e = jnp.dot(p_ref[...].astype(BF16), wp_ref[...], preferred_element_type=F32)
    o_ref[...] = x_ref[...] + gate * e


def ple(xb, wg, p, wp, x, tm=512, tn=1024):
    m, d = xb.shape
    n = wg.shape[1]
    pd = p.shape[1]
    return pl.pallas_call(
        _ple_kernel,
        grid=(m // tm, n // tn),
        in_specs=[pl.BlockSpec((tm, d), lambda i, j: (i, 0)),
                  pl.BlockSpec((d, tn), lambda i, j: (0, j)),
                  pl.BlockSpec((tm, pd), lambda i, j: (i, 0)),
                  pl.BlockSpec((pd, tn), lambda i, j: (0, j)),
                  pl.BlockSpec((tm, tn), lambda i, j: (i, j))],
        out_specs=pl.BlockSpec((tm, tn), lambda i, j: (i, j)),
        out_shape=jax.ShapeDtypeStruct((m, n), F32),
        compiler_params=_cparams(("parallel", "parallel")),
        name="ple",
    )(xb, wg, p, wp, x)


def _prep_w_in(w):
    c_q, c_kv, k_r, q_s, k_s, v_s = jnp.split(
        w, [Q_LORA, Q_LORA + KV_LORA, Q_LORA + KV_LORA + MLA_ROPE,
            Q_LORA + KV_LORA + MLA_ROPE + SWA_WIDTH,
            Q_LORA + KV_LORA + MLA_ROPE + SWA_WIDTH + SWA_KV_HEADS * SWA_HEAD_DIM], axis=1)
    q_s = q_s * (1.0 / math.sqrt(SWA_HEAD_DIM))
    return jnp.concatenate([c_q, c_kv, q_s, k_s, v_s, k_r, _swap_half(k_r)], axis=1).astype(BF16)


def _prep_w_q_up(w):
    w = (w * (1.0 / math.sqrt(MLA_NOPE + MLA_ROPE))).reshape(Q_LORA, MLA_HEADS, MLA_NOPE + MLA_ROPE)
    nope = w[:, :, :MLA_NOPE].reshape(Q_LORA, MLA_WIDTH)
    rope = w[:, :, MLA_NOPE:]
    n_rope = MLA_HEADS * MLA_ROPE
    return jnp.concatenate([nope, rope.reshape(Q_LORA, n_rope),
                            _swap_half(rope).reshape(Q_LORA, n_rope)], axis=1).astype(BF16)


def _prep_w_kv_up(w):
    w = w.reshape(KV_LORA, MLA_HEADS, MLA_NOPE + MLA_V)
    wk = w[:, :, :MLA_NOPE].reshape(KV_LORA, MLA_WIDTH).astype(BF16)
    wvt = w[:, :, MLA_NOPE:].reshape(KV_LORA, MLA_WIDTH).T.astype(BF16)
    return wk, wvt


def _layer(x, p, pos_col, pos_row, invf, batch, seq, attn_pre_norm, w_in, q_a_norm, w_q_up,
           kv_a_norm, w_kv_up, sinks, mla_out_norm, swa_out_norm, w_o, attn_post_norm,
           ffn_pre_norm, w_gate, w_up, w_down, ffn_post_norm, w_ple_gate, w_ple_proj):
    row = lambda g: g.reshape(1, -1)
    h = norm_cast(x, row(attn_pre_norm))
    proj = matmul(h, _prep_w_in(w_in), BF16, tm=512, tn=D_IN2 // 3, name="in_proj")
    qn, qr = q_up(proj, row(q_a_norm), _prep_w_q_up(w_q_up), pos_col, invf)
    wk, wvt = _prep_w_kv_up(w_kv_up)
    kn, vt, kr = kv_up(proj, row(kv_a_norm), wk, wvt, pos_col, invf, batch, seq)
    o_mla = mla_attention(qn, qr, kn, kr, vt, batch, seq)
    o_swa = swa_attention(proj, sinks, pos_col, pos_row, batch, seq)
    y = out_proj(o_mla, o_swa, row(mla_out_norm), row(swa_out_norm), w_o.astype(BF16))
    x1, h2 = resid_norm(y, x, row(attn_post_norm), row(ffn_pre_norm), True)
    f = ffn(h2, w_gate.astype(BF16), w_up.astype(BF16), w_down.astype(BF16))
    x2, x2b = resid_norm(f, x1, row(ffn_post_norm), row(ffn_post_norm), False)
    return ple(x2b, w_ple_gate.astype(BF16), p, w_ple_proj.astype(BF16), x2)


def kernel(x, p, positions, attn_pre_norm, w_in, q_a_norm, w_q_up, kv_a_norm, w_kv_up, sinks,
           mla_out_norm, swa_out_norm, w_o, attn_post_norm, ffn_pre_norm, w_gate, w_up, w_down,
           ffn_post_norm, w_ple_gate, w_ple_proj):
    batch, seq, d = x.shape
    depth = w_in.shape[0]
    t = batch * seq
    pos_col = positions.reshape(t, 1)
    pos_row = positions.reshape(t // BLOCK, 1, BLOCK)
    half = MLA_ROPE // 2
    invf = np.asarray(ROPE_THETA ** (-(np.arange(LANES) % half) * 2.0 / MLA_ROPE), np.float32).reshape(1, LANES)
    invf = jnp.asarray(invf)
    xf = x.reshape(t, d)
    for i in range(depth):
        xf = _layer(xf, p[i].reshape(t, PLE_DIM), pos_col, pos_row, invf, batch, seq,
                    attn_pre_norm[i], w_in[i], q_a_norm[i], w_q_up[i], kv_a_norm[i], w_kv_up[i],
                    sinks[i], mla_out_norm[i], swa_out_norm[i], w_o[i], attn_post_norm[i],
                    ffn_pre_norm[i], w_gate[i], w_up[i], w_down[i], ffn_post_norm[i],
                    w_ple_gate[i], w_ple_proj[i])
    return xf.reshape(batch, seq, d)
```

```python
import functools
import math

import numpy as np
import jax
import jax.numpy as jnp
from jax import lax
from jax.experimental import pallas as pl
from jax.experimental.pallas import tpu as pltpu

D_MODEL = 4096
PLE_DIM = 256
MLA_HEADS = 16
MLA_NOPE = 128
MLA_ROPE = 64
MLA_V = 128
Q_LORA = 1024
KV_LORA = 512
MLA_WIDTH = MLA_HEADS * MLA_V
SWA_HEADS = 32
SWA_KV_HEADS = 4
SWA_HEAD_DIM = 64
SWA_GROUP = SWA_HEADS // SWA_KV_HEADS
SWA_WIDTH = SWA_HEADS * SWA_HEAD_DIM
WINDOW = 128
BLOCK = 128
ROPE_THETA = 10000.0
NORM_EPS = 1e-6
D_FF = 11008

LANES = 128
MXU_N = 256
MLA_TILE = 512
VT_ROWS = MLA_V + 16
VMEM_LIMIT = 60 * 1024 * 1024

COL_CQ = 0
COL_CKV = COL_CQ + Q_LORA
COL_QSWA = COL_CKV + KV_LORA
COL_KSWA = COL_QSWA + SWA_WIDTH
COL_VSWA = COL_KSWA + SWA_KV_HEADS * SWA_HEAD_DIM
COL_KR = COL_VSWA + SWA_KV_HEADS * SWA_HEAD_DIM
D_IN2 = COL_KR + 2 * MLA_ROPE

F32 = jnp.float32
BF16 = jnp.bfloat16


def _cparams(sem):
    return pltpu.CompilerParams(dimension_semantics=sem, vmem_limit_bytes=VMEM_LIMIT)


def _rms(xf, g):
    ms = jnp.mean(xf * xf, axis=-1, keepdims=True)
    return xf * lax.rsqrt(ms + NORM_EPS) * g


def _swap_half(w):
    d = w.shape[-1]
    return jnp.concatenate([-w[..., d // 2:], w[..., : d // 2]], axis=-1)


def _rope_tables(pos_col, invf_row):
    ang = pos_col.astype(F32) * invf_row
    return jnp.cos(ang), jnp.sin(ang)


def _in_proj_kernel(x_ref, g_ref, w_ref, o_ref, h_sc):
    @pl.when(pl.program_id(1) == 0)
    def _():
        h_sc[...] = _rms(x_ref[...], g_ref[...]).astype(BF16)

    o_ref[...] = jnp.dot(h_sc[...], w_ref[...], preferred_element_type=F32).astype(o_ref.dtype)


def in_proj(x, g, w, tm=512, tn=D_IN2 // 3):
    m, k = x.shape
    n = w.shape[1]
    return pl.pallas_call(
        _in_proj_kernel,
        grid=(m // tm, n // tn),
        in_specs=[pl.BlockSpec((tm, k), lambda i, j: (i, 0)),
                  pl.BlockSpec((1, k), lambda i, j: (0, 0)),
                  pl.BlockSpec((k, tn), lambda i, j: (0, j))],
        out_specs=pl.BlockSpec((tm, tn), lambda i, j: (i, j)),
        out_shape=jax.ShapeDtypeStruct((m, n), BF16),
        scratch_shapes=[pltpu.VMEM((tm, k), BF16)],
        compiler_params=_cparams(("parallel", "arbitrary")),
        name="in_proj",
    )(x, g, w)


def _q_up_kernel(cq_ref, g_ref, w_ref, pos_ref, invf_ref, qn_ref, qr_ref):
    h = _rms(cq_ref[...].astype(F32), g_ref[...]).astype(BF16)
    r = jnp.dot(h, w_ref[...], preferred_element_type=F32)
    qn_ref[...] = r[:, :MLA_WIDTH].astype(BF16)
    cos, sin = _rope_tables(pos_ref[...], invf_ref[...])
    n_rope = MLA_HEADS * MLA_ROPE
    for c in range(n_rope // LANES):
        a = r[:, MLA_WIDTH + c * LANES: MLA_WIDTH + (c + 1) * LANES]
        b = r[:, MLA_WIDTH + n_rope + c * LANES: MLA_WIDTH + n_rope + (c + 1) * LANES]
        qr_ref[:, c * LANES:(c + 1) * LANES] = (a * cos + b * sin).astype(BF16)


def q_up(proj, g, w, pos_col, invf, tm=512):
    m = proj.shape[0]
    n_rope = MLA_HEADS * MLA_ROPE
    return pl.pallas_call(
        _q_up_kernel,
        grid=(m // tm,),
        in_specs=[pl.BlockSpec((tm, Q_LORA), lambda i: (i, COL_CQ // Q_LORA)),
                  pl.BlockSpec((1, Q_LORA), lambda i: (0, 0)),
                  pl.BlockSpec(w.shape, lambda i: (0, 0)),
                  pl.BlockSpec((tm, 1), lambda i: (i, 0)),
                  pl.BlockSpec((1, LANES), lambda i: (0, 0))],
        out_specs=[pl.BlockSpec((tm, MLA_WIDTH), lambda i: (i, 0)),
                   pl.BlockSpec((tm, n_rope), lambda i: (i, 0))],
        out_shape=[jax.ShapeDtypeStruct((m, MLA_WIDTH), BF16),
                   jax.ShapeDtypeStruct((m, n_rope), BF16)],
        compiler_params=_cparams(("parallel",)),
        name="q_up",
    )(proj, g, w, pos_col, invf)


def _kv_up_kernel(ckv_ref, g_ref, wk_ref, wvt_ref, krab_ref, pos_ref, invf_ref, kn_ref, vt_ref, kr_ref):
    h = _rms(ckv_ref[...].astype(F32), g_ref[...]).astype(BF16)
    kn_ref[...] = jnp.dot(h, wk_ref[...], preferred_element_type=F32).astype(BF16)
    vt = lax.dot_general(wvt_ref[...], h, (((1,), (1,)), ((), ())), preferred_element_type=F32)
    ones = jnp.ones((VT_ROWS - MLA_V, vt.shape[1]), BF16)
    for hh in range(MLA_HEADS):
        vt_ref[0, hh, 0, :MLA_V, :] = vt[hh * MLA_V:(hh + 1) * MLA_V, :].astype(BF16)
        vt_ref[0, hh, 0, MLA_V:, :] = ones
    cos, sin = _rope_tables(pos_ref[...], invf_ref[...])
    lane = lax.broadcasted_iota(jnp.int32, cos.shape, 1)
    lo = lane < MLA_ROPE
    prod = krab_ref[...].astype(F32) * jnp.where(lo, cos, sin)
    kk = prod + pltpu.roll(prod, MLA_ROPE, 1)
    zero = jnp.zeros_like(kk)
    kr_ref[:, :LANES] = jnp.where(lo, kk, zero).astype(BF16)
    kr_ref[:, LANES:] = jnp.where(lo, zero, kk).astype(BF16)


def kv_up(proj, g, wk, wvt, pos_col, invf, batch, seq, tm=MLA_TILE):
    m = proj.shape[0]
    nk = seq // tm
    return pl.pallas_call(
        _kv_up_kernel,
        grid=(m // tm,),
        in_specs=[pl.BlockSpec((tm, KV_LORA), lambda i: (i, COL_CKV // KV_LORA)),
                  pl.BlockSpec((1, KV_LORA), lambda i: (0, 0)),
                  pl.BlockSpec(wk.shape, lambda i: (0, 0)),
                  pl.BlockSpec(wvt.shape, lambda i: (0, 0)),
                  pl.BlockSpec((tm, LANES), lambda i: (i, COL_KR // LANES)),
                  pl.BlockSpec((tm, 1), lambda i: (i, 0)),
                  pl.BlockSpec((1, LANES), lambda i: (0, 0))],
        out_specs=[pl.BlockSpec((tm, MLA_WIDTH), lambda i: (i, 0)),
                   pl.BlockSpec((1, MLA_HEADS, 1, VT_ROWS, tm), lambda i: (i // nk, 0, i % nk, 0, 0)),
                   pl.BlockSpec((tm, 2 * LANES), lambda i: (i, 0))],
        out_shape=[jax.ShapeDtypeStruct((m, MLA_WIDTH), BF16),
                   jax.ShapeDtypeStruct((batch, MLA_HEADS, nk, VT_ROWS, tm), BF16),
                   jax.ShapeDtypeStruct((m, 2 * LANES), BF16)],
        compiler_params=_cparams(("parallel",)),
        name="kv_up",
    )(proj, g, wk, wvt, proj, pos_col, invf)


def _mla_kernel(qn_ref, qr_ref, kn_ref, kr_ref, vt_ref, o_ref, s_a, s_b, m_sc, acc_sc, *, tq, tk):
    qi = pl.program_id(2)
    q = jnp.concatenate([qn_ref[...], qr_ref[...]], axis=-1)
    m_sc[...] = jnp.full(m_sc.shape, -jnp.inf, F32)
    acc_sc[...] = jnp.zeros(acc_sc.shape, F32)

    n_chunk = tq // MXU_N
    blk_chunks = tk // MXU_N

    def scores(kb, s_ref, c0=0):
        off = pl.multiple_of(kb * tk, tk)
        k = jnp.concatenate([kn_ref[pl.ds(off, tk), :], kr_ref[pl.ds(off, tk), :]], axis=-1)
        s_ref[:, c0 * MXU_N:] = lax.dot_general(k, q[c0 * MXU_N:, :], (((1,), (1,)), ((), ())),
                                                preferred_element_type=F32)

    def softmax_pv(s_ref, kb, diag=None):
        vt = vt_ref[0, 0, kb]
        for c in range(n_chunk):
            if diag is not None and c < diag * blk_chunks:
                continue
            masked = diag is not None and c < (diag + 1) * blk_chunks
            sl = slice(c * MXU_N, (c + 1) * MXU_N)

            def s_chunk():
                s = s_ref[:, sl]
                if masked:
                    key = lax.broadcasted_iota(jnp.int32, s.shape, 0) + diag * tk
                    qry = lax.broadcasted_iota(jnp.int32, s.shape, 1) + c * MXU_N
                    s = jnp.where(qry >= key, s, -jnp.inf)
                return s

            m_old = m_sc[:, sl]
            m_new = jnp.maximum(m_old, jnp.max(s_chunk(), axis=0, keepdims=True))
            alpha = jnp.exp2(m_old - m_new)
            p = jnp.exp2(s_chunk() - m_new).astype(BF16)
            acc_sc[:, sl] = alpha * acc_sc[:, sl] + jnp.dot(vt, p, preferred_element_type=F32)
            m_sc[:, sl] = m_new

    scores(0, s_a)

    def body(j, carry):
        scores(2 * j + 1, s_b)
        softmax_pv(s_a, 2 * j)
        scores(2 * j + 2, s_a)
        softmax_pv(s_b, 2 * j + 1)
        return carry

    lax.fori_loop(0, qi, body, 0)

    scores(2 * qi + 1, s_b, c0=blk_chunks)
    softmax_pv(s_a, 2 * qi, diag=0)
    softmax_pv(s_b, 2 * qi + 1, diag=1)

    o_ref[...] = (acc_sc[:MLA_V, :] / acc_sc[MLA_V:MLA_V + 1, :]).T.astype(o_ref.dtype)


def mla_attention(qn, qr, kn, kr, vt, batch, seq, tq=2 * MLA_TILE, tk=MLA_TILE):
    assert tq == 2 * tk
    nq = seq // tq
    return pl.pallas_call(
        functools.partial(_mla_kernel, tq=tq, tk=tk),
        grid=(batch, MLA_HEADS, nq),
        in_specs=[pl.BlockSpec((tq, MLA_NOPE), lambda b, h, i: (b * nq + i, h)),
                  pl.BlockSpec((tq, LANES), lambda b, h, i: (b * nq + i, h // 2)),
                  pl.BlockSpec((seq, MLA_NOPE), lambda b, h, i: (b, h)),
                  pl.BlockSpec((seq, LANES), lambda b, h, i: (b, h % 2)),
                  pl.BlockSpec((1, 1, seq // tk, VT_ROWS, tk), lambda b, h, i: (b, h, 0, 0, 0))],
        out_specs=pl.BlockSpec((tq, MLA_V), lambda b, h, i: (b * nq + i, h)),
        out_shape=jax.ShapeDtypeStruct((batch * seq, MLA_WIDTH), BF16),
        scratch_shapes=[pltpu.VMEM((tk, tq), F32), pltpu.VMEM((tk, tq), F32),
                        pltpu.VMEM((1, tq), F32), pltpu.VMEM((VT_ROWS, tq), F32)],
        compiler_params=_cparams(("parallel", "parallel", "parallel")),
        name="mla_attention",
    )(qn, qr, kn, kr, vt)


def _swa_kernel(sinks_ref, q0_ref, q1_ref, q2_ref, q3_ref, kc_ref, kp_ref, vc_ref, vp_ref,
                pq_ref, pkc_ref, pkp_ref, o_ref):
    blk = pl.program_id(1)
    q_refs = (q0_ref, q1_ref, q2_ref, q3_ref)
    k_all = jnp.concatenate([kp_ref[...], kc_ref[...]], axis=0).astype(F32)
    v_all = jnp.concatenate([vp_ref[...], vc_ref[...]], axis=0).astype(F32)
    pk = jnp.concatenate([pkp_ref[0], pkc_ref[0]], axis=-1).astype(F32)
    dist = jnp.abs(pq_ref[...].astype(F32) - pk)
    row = lax.broadcasted_iota(jnp.int32, dist.shape, 0)
    col = lax.broadcasted_iota(jnp.int32, dist.shape, 1)
    rel = BLOCK + row - col
    first_key = jnp.where(blk > 0, 0, BLOCK)
    valid = (rel >= 0) & (rel < WINDOW) & (col >= first_key)
    lane = lax.broadcasted_iota(jnp.int32, (2 * BLOCK, LANES), 1)
    lo = lane < SWA_HEAD_DIM
    lane_o = lax.broadcasted_iota(jnp.int32, (BLOCK, LANES), 1)
    lo_o = lane_o < SWA_HEAD_DIM

    def dup(x_all, g):
        pair = x_all[:, (g // 2) * LANES:(g // 2 + 1) * LANES]
        rolled = pltpu.roll(pair, SWA_HEAD_DIM, 1)
        return jnp.where(lo, pair, rolled) if g % 2 == 0 else jnp.where(lo, rolled, pair)

    for g in range(SWA_KV_HEADS):
        kk = dup(k_all, g)
        vv = dup(v_all, g).astype(BF16)
        zero = jnp.zeros_like(kk)
        k_half = (jnp.where(lo, kk, zero).astype(BF16), jnp.where(lo, zero, kk).astype(BF16))
        for i in range(SWA_GROUP // 2):
            q_pair = q_refs[g][:, i * LANES:(i + 1) * LANES]
            outs = []
            for j in range(2):
                head = g * SWA_GROUP + 2 * i + j
                slope = 2.0 ** (-8.0 * (head + 1) / SWA_HEADS)
                s = lax.dot_general(q_pair, k_half[j], (((1,), (1,)), ((), ())),
                                    preferred_element_type=F32)
                s = jnp.where(valid, s - slope * dist, -jnp.inf)
                sink = sinks_ref[head]
                m = jnp.maximum(jnp.max(s, axis=-1, keepdims=True), sink)
                e = jnp.exp(s - m)
                denom = jnp.sum(e, axis=-1, keepdims=True) + jnp.exp(sink - m)
                p = (e / denom).astype(BF16)
                outs.append(jnp.dot(p, vv, preferred_element_type=F32))
            col0 = (g * SWA_GROUP // 2 + i) * LANES
            o_ref[:, col0:col0 + LANES] = jnp.where(lo_o, outs[0], outs[1]).astype(o_ref.dtype)


def swa_attention(proj, sinks, pos_col, pos_row, batch, seq):
    nblk = seq // BLOCK
    kvw = SWA_KV_HEADS * SWA_HEAD_DIM
    gw = SWA_GROUP * SWA_HEAD_DIM
    cur = lambda b, n: b * nblk + n
    prev = lambda b, n: b * nblk + jnp.maximum(n - 1, 0)
    q_specs = [pl.BlockSpec((BLOCK, gw), functools.partial(lambda b, n, g: (cur(b, n), COL_QSWA // gw + g), g=g))
               for g in range(SWA_KV_HEADS)]
    return pl.pallas_call(
        _swa_kernel,
        grid=(batch, nblk),
        in_specs=[pl.BlockSpec(memory_space=pltpu.SMEM)] + q_specs + [
            pl.BlockSpec((BLOCK, kvw), lambda b, n: (cur(b, n), COL_KSWA // kvw)),
            pl.BlockSpec((BLOCK, kvw), lambda b, n: (prev(b, n), COL_KSWA // kvw)),
            pl.BlockSpec((BLOCK, kvw), lambda b, n: (cur(b, n), COL_VSWA // kvw)),
            pl.BlockSpec((BLOCK, kvw), lambda b, n: (prev(b, n), COL_VSWA // kvw)),
            pl.BlockSpec((BLOCK, 1), lambda b, n: (cur(b, n), 0)),
            pl.BlockSpec((1, 1, BLOCK), lambda b, n: (cur(b, n), 0, 0)),
            pl.BlockSpec((1, 1, BLOCK), lambda b, n: (prev(b, n), 0, 0))],
        out_specs=pl.BlockSpec((BLOCK, SWA_WIDTH), lambda b, n: (cur(b, n), 0)),
        out_shape=jax.ShapeDtypeStruct((batch * seq, SWA_WIDTH), BF16),
        compiler_params=_cparams(("parallel", "parallel")),
        name="swa_attention",
    )(sinks, proj, proj, proj, proj, proj, proj, proj, proj, pos_col, pos_row, pos_row)


def _wo_kernel(oa_ref, ob_ref, ga_ref, gb_ref, wa_ref, wb_ref, y_ref, na_sc, nb_sc):
    @pl.when(pl.program_id(1) == 0)
    def _():
        na_sc[...] = _rms(oa_ref[...].astype(F32), ga_ref[...]).astype(BF16)
        nb_sc[...] = _rms(ob_ref[...].astype(F32), gb_ref[...]).astype(BF16)

    y_ref[...] = (jnp.dot(na_sc[...], wa_ref[...], preferred_element_type=F32)
                  + jnp.dot(nb_sc[...], wb_ref[...], preferred_element_type=F32)).astype(y_ref.dtype)


def out_proj(oa, ob, ga, gb, w, tm=512, tn=1024):
    m, ka = oa.shape
    kb = ob.shape[1]
    n = w.shape[1]
    return pl.pallas_call(
        _wo_kernel,
        grid=(m // tm, n // tn),
        in_specs=[pl.BlockSpec((tm, ka), lambda i, j: (i, 0)),
                  pl.BlockSpec((tm, kb), lambda i, j: (i, 0)),
                  pl.BlockSpec((1, ka), lambda i, j: (0, 0)),
                  pl.BlockSpec((1, kb), lambda i, j: (0, 0)),
                  pl.BlockSpec((ka, tn), lambda i, j: (0, j)),
                  pl.BlockSpec((kb, tn), lambda i, j: (1, j))],
        out_specs=pl.BlockSpec((tm, tn), lambda i, j: (i, j)),
        out_shape=jax.ShapeDtypeStruct((m, n), BF16),
        scratch_shapes=[pltpu.VMEM((tm, ka), BF16), pltpu.VMEM((tm, kb), BF16)],
        compiler_params=_cparams(("parallel", "arbitrary")),
        name="out_proj",
    )(oa, ob, ga, gb, w, w)


def _resid_kernel(y_ref, x_ref, gy_ref, gn_ref, xo_ref, ho_ref):
    xn = x_ref[...] + _rms(y_ref[...].astype(F32), gy_ref[...])
    xo_ref[...] = xn
    ho_ref[...] = _rms(xn, gn_ref[...]).astype(ho_ref.dtype)


def resid_norm(y, x, gy, gn, tm=256):
    m, d = x.shape
    row = pl.BlockSpec((tm, d), lambda i: (i, 0))
    vec = pl.BlockSpec((1, d), lambda i: (0, 0))
    return pl.pallas_call(
        _resid_kernel,
        grid=(m // tm,),
        in_specs=[row, row, vec, vec],
        out_specs=[row, row],
        out_shape=[jax.ShapeDtypeStruct((m, d), F32), jax.ShapeDtypeStruct((m, d), BF16)],
        compiler_params=_cparams(("parallel",)),
        name="resid_norm",
    )(y, x, gy, gn)


def _ffn_kernel(h_ref, wg_ref, wu_ref, wd_ref, x_ref, gp_ref, o_ref):
    j = pl.program_id(1)

    @pl.when(j == 0)
    def _():
        o_ref[...] = jnp.zeros(o_ref.shape, o_ref.dtype)

    h = h_ref[...]
    g = jnp.dot(h, wg_ref[...], preferred_element_type=F32)
    u = jnp.dot(h, wu_ref[...], preferred_element_type=F32)
    a = (g * jax.nn.sigmoid(g) * u).astype(BF16)
    o_ref[...] += jnp.dot(a, wd_ref[...], preferred_element_type=F32)

    @pl.when(j == pl.num_programs(1) - 1)
    def _():
        o_ref[...] = x_ref[...] + _rms(o_ref[...], gp_ref[...])


def ffn(h, wg, wu, wd, x, g_post, tm=512, tf=256):
    m, d = h.shape
    f = wg.shape[1]
    return pl.pallas_call(
        _ffn_kernel,
        grid=(m // tm, f // tf),
        in_specs=[pl.BlockSpec((tm, d), lambda i, j: (i, 0)),
                  pl.BlockSpec((d, tf), lambda i, j: (0, j)),
                  pl.BlockSpec((d, tf), lambda i, j: (0, j)),
                  pl.BlockSpec((tf, d), lambda i, j: (j, 0)),
                  pl.BlockSpec((tm, d), lambda i, j: (i, 0), pipeline_mode=pl.Buffered(1)),
                  pl.BlockSpec((1, d), lambda i, j: (0, 0))],
        out_specs=pl.BlockSpec((tm, d), lambda i, j: (i, 0)),
        out_shape=jax.ShapeDtypeStruct((m, d), F32),
        compiler_params=_cparams(("parallel", "arbitrary")),
        name="ffn",
    )(h, wg, wu, wd, x, g_post)


def _ple_kernel(xrow_ref, wg_ref, p_ref, wp_ref, x_ref, o_ref, xb_sc):
    @pl.when(pl.program_id(1) == 0)
    def _():
        xb_sc[...] = xrow_ref[...].astype(BF16)

    gate = jax.nn.sigmoid(jnp.dot(xb_sc[...], wg_ref[...], preferred_element_type=F32))
    e = jnp.dot(p_ref[...].astype(BF16), wp_ref[...], preferred_element_type=F32)
    o_ref[...] = x_ref[...] + gate * e


def ple(x, wg, p, wp, tm=512, tn=1024):
    m, d = x.shape
    n = wg.shape[1]
    pd = p.shape[1]
    return pl.pallas_call(
        _ple_kernel,
        grid=(m // tm, n // tn),
        in_specs=[pl.BlockSpec((tm, d), lambda i, j: (i, 0)),
                  pl.BlockSpec((d, tn), lambda i, j: (0, j)),
                  pl.BlockSpec((tm, pd), lambda i, j: (i, 0)),
                  pl.BlockSpec((pd, tn), lambda i, j: (0, j)),
                  pl.BlockSpec((tm, tn), lambda i, j: (i, j))],
        out_specs=pl.BlockSpec((tm, tn), lambda i, j: (i, j)),
        out_shape=jax.ShapeDtypeStruct((m, n), F32),
        scratch_shapes=[pltpu.VMEM((tm, d), BF16)],
        compiler_params=_cparams(("parallel", "arbitrary")),
        name="ple",
    )(x, wg, p, wp, x)


def _prep_w_in(w):
    c_q, c_kv, k_r, q_s, k_s, v_s = jnp.split(
        w, [Q_LORA, Q_LORA + KV_LORA, Q_LORA + KV_LORA + MLA_ROPE,
            Q_LORA + KV_LORA + MLA_ROPE + SWA_WIDTH,
            Q_LORA + KV_LORA + MLA_ROPE + SWA_WIDTH + SWA_KV_HEADS * SWA_HEAD_DIM], axis=1)
    q_s = q_s * (1.0 / math.sqrt(SWA_HEAD_DIM))
    parts = [c_q, c_kv, q_s, k_s, v_s, k_r, _swap_half(k_r)]
    return jnp.concatenate([part.astype(BF16) for part in parts], axis=1)


def _prep_w_q_up(w):
    w = (w * (math.log2(math.e) / math.sqrt(MLA_NOPE + MLA_ROPE))).reshape(Q_LORA, MLA_HEADS, MLA_NOPE + MLA_ROPE)
    nope = w[:, :, :MLA_NOPE].reshape(Q_LORA, MLA_WIDTH)
    rope = w[:, :, MLA_NOPE:]
    n_rope = MLA_HEADS * MLA_ROPE
    return jnp.concatenate([nope, rope.reshape(Q_LORA, n_rope),
                            _swap_half(rope).reshape(Q_LORA, n_rope)], axis=1).astype(BF16)


def _prep_w_kv_up(w):
    w = w.reshape(KV_LORA, MLA_HEADS, MLA_NOPE + MLA_V)
    wk = w[:, :, :MLA_NOPE].reshape(KV_LORA, MLA_WIDTH).astype(BF16)
    wvt = w[:, :, MLA_NOPE:].reshape(KV_LORA, MLA_WIDTH).T.astype(BF16)
    return wk, wvt


def _layer(x, p, pos_col, pos_row, invf, batch, seq, attn_pre_norm, w_in, q_a_norm, w_q_up,
           kv_a_norm, w_kv_up, sinks, mla_out_norm, swa_out_norm, w_o, attn_post_norm,
           ffn_pre_norm, w_gate, w_up, w_down, ffn_post_norm, w_ple_gate, w_ple_proj):
    row = lambda g: g.reshape(1, -1)
    proj = in_proj(x, row(attn_pre_norm), _prep_w_in(w_in))
    qn, qr = q_up(proj, row(q_a_norm), _prep_w_q_up(w_q_up), pos_col, invf)
    wk, wvt = _prep_w_kv_up(w_kv_up)
    kn, vt, kr = kv_up(proj, row(kv_a_norm), wk, wvt, pos_col, invf, batch, seq)
    o_mla = mla_attention(qn, qr, kn, kr, vt, batch, seq)
    o_swa = swa_attention(proj, sinks, pos_col, pos_row, batch, seq)
    y = out_proj(o_mla, o_swa, row(mla_out_norm), row(swa_out_norm), w_o.astype(BF16))
    x1, h2 = resid_norm(y, x, row(attn_post_norm), row(ffn_pre_norm))
    x2 = ffn(h2, w_gate.astype(BF16), w_up.astype(BF16), w_down.astype(BF16), x1, row(ffn_post_norm))
    return ple(x2, w_ple_gate.astype(BF16), p, w_ple_proj.astype(BF16))


def kernel(x, p, positions, attn_pre_norm, w_in, q_a_norm, w_q_up, kv_a_norm, w_kv_up, sinks,
           mla_out_norm, swa_out_norm, w_o, attn_post_norm, ffn_pre_norm, w_gate, w_up, w_down,
           ffn_post_norm, w_ple_gate, w_ple_proj):
    batch, seq, d = x.shape
    depth = w_in.shape[0]
    t = batch * seq
    pos_col = positions.reshape(t, 1)
    pos_row = positions.reshape(t // BLOCK, 1, BLOCK)
    half = MLA_ROPE // 2
    invf = np.asarray(ROPE_THETA ** (-(np.arange(LANES) % half) * 2.0 / MLA_ROPE), np.float32).reshape(1, LANES)
    invf = jnp.asarray(invf)
    xf = x.reshape(t, d)
    for i in range(depth):
        xf = _layer(xf, p[i].reshape(t, PLE_DIM), pos_col, pos_row, invf, batch, seq,
                    attn_pre_norm[i], w_in[i], q_a_norm[i], w_q_up[i], kv_a_norm[i], w_kv_up[i],
                    sinks[i], mla_out_norm[i], swa_out_norm[i], w_o[i], attn_post_norm[i],
                    ffn_pre_norm[i], w_gate[i], w_up[i], w_down[i], ffn_post_norm[i],
                    w_ple_gate[i], w_ple_proj[i])
    return xf.reshape(batch, seq, d)
```

```python
import functools
import math

import numpy as np
import jax
import jax.numpy as jnp
from jax import lax
from jax.experimental import pallas as pl
from jax.experimental.pallas import tpu as pltpu

D_MODEL = 4096
PLE_DIM = 256
MLA_HEADS = 16
MLA_NOPE = 128
MLA_ROPE = 64
MLA_V = 128
Q_LORA = 1024
KV_LORA = 512
MLA_WIDTH = MLA_HEADS * MLA_V
SWA_HEADS = 32
SWA_KV_HEADS = 4
SWA_HEAD_DIM = 64
SWA_GROUP = SWA_HEADS // SWA_KV_HEADS
SWA_WIDTH = SWA_HEADS * SWA_HEAD_DIM
WINDOW = 128
BLOCK = 128
ROPE_THETA = 10000.0
NORM_EPS = 1e-6
D_FF = 11008

LANES = 128
MXU_N = 256
MLA_TILE = 512
VT_ROWS = MLA_V + 16
VMEM_LIMIT = 60 * 1024 * 1024

COL_CQ = 0
COL_CKV = COL_CQ + Q_LORA
COL_QSWA = COL_CKV + KV_LORA
COL_KSWA = COL_QSWA + SWA_WIDTH
COL_VSWA = COL_KSWA + SWA_KV_HEADS * SWA_HEAD_DIM
COL_KR = COL_VSWA + SWA_KV_HEADS * SWA_HEAD_DIM
D_IN2 = COL_KR + 2 * MLA_ROPE

F32 = jnp.float32
BF16 = jnp.bfloat16


def _cparams(sem):
    return pltpu.CompilerParams(dimension_semantics=sem, vmem_limit_bytes=VMEM_LIMIT)


def _rms(xf, g):
    ms = jnp.mean(xf * xf, axis=-1, keepdims=True)
    return xf * lax.rsqrt(ms + NORM_EPS) * g


def _swap_half(w):
    d = w.shape[-1]
    return jnp.concatenate([-w[..., d // 2:], w[..., : d // 2]], axis=-1)


def _rope_tables(pos_col, invf_row):
    ang = pos_col.astype(F32) * invf_row
    return jnp.cos(ang), jnp.sin(ang)


def _in_proj_kernel(x_ref, g_ref, w_ref, o_ref, h_sc):
    @pl.when(pl.program_id(1) == 0)
    def _():
        h_sc[...] = _rms(x_ref[...], g_ref[...]).astype(BF16)

    o_ref[...] = jnp.dot(h_sc[...], w_ref[...], preferred_element_type=F32).astype(o_ref.dtype)


def in_proj(x, g, w, tm=512, tn=D_IN2 // 3):
    m, k = x.shape
    n = w.shape[1]
    return pl.pallas_call(
        _in_proj_kernel,
        grid=(m // tm, n // tn),
        in_specs=[pl.BlockSpec((tm, k), lambda i, j: (i, 0)),
                  pl.BlockSpec((1, k), lambda i, j: (0, 0)),
                  pl.BlockSpec((k, tn), lambda i, j: (0, j))],
        out_specs=pl.BlockSpec((tm, tn), lambda i, j: (i, j)),
        out_shape=jax.ShapeDtypeStruct((m, n), BF16),
        scratch_shapes=[pltpu.VMEM((tm, k), BF16)],
        compiler_params=_cparams(("parallel", "arbitrary")),
        name="in_proj",
    )(x, g, w)


def _q_up_kernel(cq_ref, g_ref, w_ref, pos_ref, invf_ref, qn_ref, qr_ref):
    h = _rms(cq_ref[...].astype(F32), g_ref[...]).astype(BF16)
    r = jnp.dot(h, w_ref[...], preferred_element_type=F32)
    qn_ref[...] = r[:, :MLA_WIDTH].astype(BF16)
    cos, sin = _rope_tables(pos_ref[...], invf_ref[...])
    n_rope = MLA_HEADS * MLA_ROPE
    for c in range(n_rope // LANES):
        a = r[:, MLA_WIDTH + c * LANES: MLA_WIDTH + (c + 1) * LANES]
        b = r[:, MLA_WIDTH + n_rope + c * LANES: MLA_WIDTH + n_rope + (c + 1) * LANES]
        qr_ref[:, c * LANES:(c + 1) * LANES] = (a * cos + b * sin).astype(BF16)


def q_up(proj, g, w, pos_col, invf, tm=512):
    m = proj.shape[0]
    n_rope = MLA_HEADS * MLA_ROPE
    return pl.pallas_call(
        _q_up_kernel,
        grid=(m // tm,),
        in_specs=[pl.BlockSpec((tm, Q_LORA), lambda i: (i, COL_CQ // Q_LORA)),
                  pl.BlockSpec((1, Q_LORA), lambda i: (0, 0)),
                  pl.BlockSpec(w.shape, lambda i: (0, 0)),
                  pl.BlockSpec((tm, 1), lambda i: (i, 0)),
                  pl.BlockSpec((1, LANES), lambda i: (0, 0))],
        out_specs=[pl.BlockSpec((tm, MLA_WIDTH), lambda i: (i, 0)),
                   pl.BlockSpec((tm, n_rope), lambda i: (i, 0))],
        out_shape=[jax.ShapeDtypeStruct((m, MLA_WIDTH), BF16),
                   jax.ShapeDtypeStruct((m, n_rope), BF16)],
        compiler_params=_cparams(("parallel",)),
        name="q_up",
    )(proj, g, w, pos_col, invf)


def _kv_up_kernel(ckv_ref, g_ref, wk_ref, wvt_ref, krab_ref, pos_ref, invf_ref, kn_ref, vt_ref, kr_ref):
    h = _rms(ckv_ref[...].astype(F32), g_ref[...]).astype(BF16)
    kn_ref[...] = jnp.dot(h, wk_ref[...], preferred_element_type=F32).astype(BF16)
    vt = lax.dot_general(wvt_ref[...], h, (((1,), (1,)), ((), ())), preferred_element_type=F32)
    ones = jnp.ones((VT_ROWS - MLA_V, vt.shape[1]), BF16)
    for hh in range(MLA_HEADS):
        vt_ref[0, hh, 0, :MLA_V, :] = vt[hh * MLA_V:(hh + 1) * MLA_V, :].astype(BF16)
        vt_ref[0, hh, 0, MLA_V:, :] = ones
    cos, sin = _rope_tables(pos_ref[...], invf_ref[...])
    lane = lax.broadcasted_iota(jnp.int32, cos.shape, 1)
    lo = lane < MLA_ROPE
    prod = krab_ref[...].astype(F32) * jnp.where(lo, cos, sin)
    kk = prod + pltpu.roll(prod, MLA_ROPE, 1)
    zero = jnp.zeros_like(kk)
    kr_ref[:, :LANES] = jnp.where(lo, kk, zero).astype(BF16)
    kr_ref[:, LANES:] = jnp.where(lo, zero, kk).astype(BF16)


def kv_up(proj, g, wk, wvt, pos_col, invf, batch, seq, tm=MLA_TILE):
    m = proj.shape[0]
    nk = seq // tm
    return pl.pallas_call(
        _kv_up_kernel,
        grid=(m // tm,),
        in_specs=[pl.BlockSpec((tm, KV_LORA), lambda i: (i, COL_CKV // KV_LORA)),
                  pl.BlockSpec((1, KV_LORA), lambda i: (0, 0)),
                  pl.BlockSpec(wk.shape, lambda i: (0, 0)),
                  pl.BlockSpec(wvt.shape, lambda i: (0, 0)),
                  pl.BlockSpec((tm, LANES), lambda i: (i, COL_KR // LANES)),
                  pl.BlockSpec((tm, 1), lambda i: (i, 0)),
                  pl.BlockSpec((1, LANES), lambda i: (0, 0))],
        out_specs=[pl.BlockSpec((tm, MLA_WIDTH), lambda i: (i, 0)),
                   pl.BlockSpec((1, MLA_HEADS, 1, VT_ROWS, tm), lambda i: (i // nk, 0, i % nk, 0, 0)),
                   pl.BlockSpec((tm, 2 * LANES), lambda i: (i, 0))],
        out_shape=[jax.ShapeDtypeStruct((m, MLA_WIDTH), BF16),
                   jax.ShapeDtypeStruct((batch, MLA_HEADS, nk, VT_ROWS, tm), BF16),
                   jax.ShapeDtypeStruct((m, 2 * LANES), BF16)],
        compiler_params=_cparams(("parallel",)),
        name="kv_up",
    )(proj, g, wk, wvt, proj, pos_col, invf)


def _mla_kernel(qn_ref, qr_ref, kn_ref, kr_ref, vt_ref, o_ref, s_a, s_b, mx_a, mx_b, m_sc, acc_sc, *, tq, tk):
    qi = pl.program_id(2)
    q = jnp.concatenate([qn_ref[...], qr_ref[...]], axis=-1)
    m_sc[...] = jnp.full(m_sc.shape, -jnp.inf, F32)
    acc_sc[...] = jnp.zeros(acc_sc.shape, F32)

    n_chunk = tq // MXU_N
    blk_chunks = tk // MXU_N
    n_diag = tq // tk
    bufs = ((s_a, mx_a), (s_b, mx_b))

    def live_chunks(diag):
        return range(0 if diag is None else diag * blk_chunks, n_chunk)

    def scores(kb, buf, diag=None):
        s_ref, mx_ref = buf
        off = pl.multiple_of(kb * tk, tk)
        k = jnp.concatenate([kn_ref[pl.ds(off, tk), :], kr_ref[pl.ds(off, tk), :]], axis=-1)
        for c in live_chunks(diag):
            s = lax.dot_general(k, q[c * MXU_N:(c + 1) * MXU_N, :], (((1,), (1,)), ((), ())),
                                preferred_element_type=F32)
            if diag is not None and c < (diag + 1) * blk_chunks:
                key = lax.broadcasted_iota(jnp.int32, s.shape, 0) + diag * tk
                qry = lax.broadcasted_iota(jnp.int32, s.shape, 1) + c * MXU_N
                s = jnp.where(qry >= key, s, -jnp.inf)
            s_ref[c] = s
            mx_ref[:, c * MXU_N:(c + 1) * MXU_N] = jnp.max(s, axis=0, keepdims=True)

    def softmax_pv(kb, buf, diag=None, mask_here=False):
        s_ref, mx_ref = buf
        vt = vt_ref[0, 0, kb]
        for c in live_chunks(diag):
            sl = slice(c * MXU_N, (c + 1) * MXU_N)
            remask = mask_here and c < blk_chunks

            def s_chunk():
                s = s_ref[c]
                if remask:
                    key = lax.broadcasted_iota(jnp.int32, s.shape, 0)
                    qry = lax.broadcasted_iota(jnp.int32, s.shape, 1) + c * MXU_N
                    s = jnp.where(qry >= key, s, -jnp.inf)
                return s

            mx = jnp.max(s_chunk(), axis=0, keepdims=True) if remask else mx_ref[:, sl]
            m_old = m_sc[:, sl]
            m_new = jnp.maximum(m_old, mx)
            alpha = jnp.exp2(m_old - m_new)
            p = jnp.exp2(s_chunk() - m_new).astype(BF16)
            acc_sc[:, sl] = alpha * acc_sc[:, sl] + jnp.dot(vt, p, preferred_element_type=F32)
            m_sc[:, sl] = m_new

    n_full = n_diag * qi
    scores(0, bufs[0])

    def body(j, carry):
        scores(2 * j + 1, bufs[1])
        softmax_pv(2 * j, bufs[0])
        scores(2 * j + 2, bufs[0])
        softmax_pv(2 * j + 1, bufs[1])
        return carry

    lax.fori_loop(0, n_full // 2, body, 0)

    for d in range(n_diag):
        if d + 1 < n_diag:
            scores(n_full + d + 1, bufs[(d + 1) % 2], diag=d + 1)
        softmax_pv(n_full + d, bufs[d % 2], diag=d, mask_here=(d == 0))

    o_ref[...] = (acc_sc[:MLA_V, :] / acc_sc[MLA_V:MLA_V + 1, :]).T.astype(o_ref.dtype)


def mla_attention(qn, qr, kn, kr, vt, batch, seq, tq=4 * MLA_TILE, tk=MLA_TILE):
    assert tq % (2 * tk) == 0
    nq = seq // tq
    return pl.pallas_call(
        functools.partial(_mla_kernel, tq=tq, tk=tk),
        grid=(batch, MLA_HEADS, nq),
        in_specs=[pl.BlockSpec((tq, MLA_NOPE), lambda b, h, i: (b * nq + i, h)),
                  pl.BlockSpec((tq, LANES), lambda b, h, i: (b * nq + i, h // 2)),
                  pl.BlockSpec((seq, MLA_NOPE), lambda b, h, i: (b, h)),
                  pl.BlockSpec((seq, LANES), lambda b, h, i: (b, h % 2)),
                  pl.BlockSpec((1, 1, seq // tk, VT_ROWS, tk), lambda b, h, i: (b, h, 0, 0, 0))],
        out_specs=pl.BlockSpec((tq, MLA_V), lambda b, h, i: (b * nq + i, h)),
        out_shape=jax.ShapeDtypeStruct((batch * seq, MLA_WIDTH), BF16),
        scratch_shapes=[pltpu.VMEM((tq // MXU_N, tk, MXU_N), F32), pltpu.VMEM((tq // MXU_N, tk, MXU_N), F32),
                        pltpu.VMEM((1, tq), F32), pltpu.VMEM((1, tq), F32),
                        pltpu.VMEM((1, tq), F32), pltpu.VMEM((VT_ROWS, tq), F32)],
        compiler_params=_cparams(("parallel", "parallel", "parallel")),
        name="mla_attention",
    )(qn, qr, kn, kr, vt)


def _swa_kernel(sinks_ref, q0_ref, q1_ref, q2_ref, q3_ref, kc_ref, kp_ref, vc_ref, vp_ref,
                pq_ref, pkc_ref, pkp_ref, o_ref):
    blk = pl.program_id(1)
    q_refs = (q0_ref, q1_ref, q2_ref, q3_ref)
    k_all = jnp.concatenate([kp_ref[...], kc_ref[...]], axis=0).astype(F32)
    v_all = jnp.concatenate([vp_ref[...], vc_ref[...]], axis=0).astype(F32)
    pk = jnp.concatenate([pkp_ref[0], pkc_ref[0]], axis=-1).astype(F32)
    dist = jnp.abs(pq_ref[...].astype(F32) - pk)
    row = lax.broadcasted_iota(jnp.int32, dist.shape, 0)
    col = lax.broadcasted_iota(jnp.int32, dist.shape, 1)
    rel = BLOCK + row - col
    first_key = jnp.where(blk > 0, 0, BLOCK)
    valid = (rel >= 0) & (rel < WINDOW) & (col >= first_key)
    lane = lax.broadcasted_iota(jnp.int32, (2 * BLOCK, LANES), 1)
    lo = lane < SWA_HEAD_DIM
    lane_o = lax.broadcasted_iota(jnp.int32, (BLOCK, LANES), 1)
    lo_o = lane_o < SWA_HEAD_DIM

    def dup(x_all, g):
        pair = x_all[:, (g // 2) * LANES:(g // 2 + 1) * LANES]
        rolled = pltpu.roll(pair, SWA_HEAD_DIM, 1)
        return jnp.where(lo, pair, rolled) if g % 2 == 0 else jnp.where(lo, rolled, pair)

    for g in range(SWA_KV_HEADS):
        kk = dup(k_all, g)
        vv = dup(v_all, g).astype(BF16)
        zero = jnp.zeros_like(kk)
        k_half = (jnp.where(lo, kk, zero).astype(BF16), jnp.where(lo, zero, kk).astype(BF16))
        for i in range(SWA_GROUP // 2):
            q_pair = q_refs[g][:, i * LANES:(i + 1) * LANES]
            outs = []
            for j in range(2):
                head = g * SWA_GROUP + 2 * i + j
                slope = 2.0 ** (-8.0 * (head + 1) / SWA_HEADS)
                s = lax.dot_general(q_pair, k_half[j], (((1,), (1,)), ((), ())),
                                    preferred_element_type=F32)
                s = jnp.where(valid, s - slope * dist, -jnp.inf)
                sink = sinks_ref[head]
                m = jnp.maximum(jnp.max(s, axis=-1, keepdims=True), sink)
                e = jnp.exp(s - m)
                denom = jnp.sum(e, axis=-1, keepdims=True) + jnp.exp(sink - m)
                p = (e / denom).astype(BF16)
                outs.append(jnp.dot(p, vv, preferred_element_type=F32))
            col0 = (g * SWA_GROUP // 2 + i) * LANES
            o_ref[:, col0:col0 + LANES] = jnp.where(lo_o, outs[0], outs[1]).astype(o_ref.dtype)


def swa_attention(proj, sinks, pos_col, pos_row, batch, seq):
    nblk = seq // BLOCK
    kvw = SWA_KV_HEADS * SWA_HEAD_DIM
    gw = SWA_GROUP * SWA_HEAD_DIM
    cur = lambda b, n: b * nblk + n
    prev = lambda b, n: b * nblk + jnp.maximum(n - 1, 0)
    q_specs = [pl.BlockSpec((BLOCK, gw), functools.partial(lambda b, n, g: (cur(b, n), COL_QSWA // gw + g), g=g))
               for g in range(SWA_KV_HEADS)]
    return pl.pallas_call(
        _swa_kernel,
        grid=(batch, nblk),
        in_specs=[pl.BlockSpec(memory_space=pltpu.SMEM)] + q_specs + [
            pl.BlockSpec((BLOCK, kvw), lambda b, n: (cur(b, n), COL_KSWA // kvw)),
            pl.BlockSpec((BLOCK, kvw), lambda b, n: (prev(b, n), COL_KSWA // kvw)),
            pl.BlockSpec((BLOCK, kvw), lambda b, n: (cur(b, n), COL_VSWA // kvw)),
            pl.BlockSpec((BLOCK, kvw), lambda b, n: (prev(b, n), COL_VSWA // kvw)),
            pl.BlockSpec((BLOCK, 1), lambda b, n: (cur(b, n), 0)),
            pl.BlockSpec((1, 1, BLOCK), lambda b, n: (cur(b, n), 0, 0)),
            pl.BlockSpec((1, 1, BLOCK), lambda b, n: (prev(b, n), 0, 0))],
        out_specs=pl.BlockSpec((BLOCK, SWA_WIDTH), lambda b, n: (cur(b, n), 0)),
        out_shape=jax.ShapeDtypeStruct((batch * seq, SWA_WIDTH), BF16),
        compiler_params=_cparams(("parallel", "parallel")),
        name="swa_attention",
    )(sinks, proj, proj, proj, proj, proj, proj, proj, proj, pos_col, pos_row, pos_row)


def _wo_kernel(oa_ref, ob_ref, ga_ref, gb_ref, wa_ref, wb_ref, y_ref, na_sc, nb_sc):
    @pl.when(pl.program_id(1) == 0)
    def _():
        na_sc[...] = _rms(oa_ref[...].astype(F32), ga_ref[...]).astype(BF16)
        nb_sc[...] = _rms(ob_ref[...].astype(F32), gb_ref[...]).astype(BF16)

    y_ref[...] = (jnp.dot(na_sc[...], wa_ref[...], preferred_element_type=F32)
                  + jnp.dot(nb_sc[...], wb_ref[...], preferred_element_type=F32)).astype(y_ref.dtype)


def out_proj(oa, ob, ga, gb, w, tm=512, tn=1024):
    m, ka = oa.shape
    kb = ob.shape[1]
    n = w.shape[1]
    return pl.pallas_call(
        _wo_kernel,
        grid=(m // tm, n // tn),
        in_specs=[pl.BlockSpec((tm, ka), lambda i, j: (i, 0)),
                  pl.BlockSpec((tm, kb), lambda i, j: (i, 0)),
                  pl.BlockSpec((1, ka), lambda i, j: (0, 0)),
                  pl.BlockSpec((1, kb), lambda i, j: (0, 0)),
                  pl.BlockSpec((ka, tn), lambda i, j: (0, j)),
                  pl.BlockSpec((kb, tn), lambda i, j: (1, j))],
        out_specs=pl.BlockSpec((tm, tn), lambda i, j: (i, j)),
        out_shape=jax.ShapeDtypeStruct((m, n), BF16),
        scratch_shapes=[pltpu.VMEM((tm, ka), BF16), pltpu.VMEM((tm, kb), BF16)],
        compiler_params=_cparams(("parallel", "arbitrary")),
        name="out_proj",
    )(oa, ob, ga, gb, w, w)


def _resid_kernel(y_ref, x_ref, gy_ref, gn_ref, xo_ref, ho_ref):
    xn = x_ref[...] + _rms(y_ref[...].astype(F32), gy_ref[...])
    xo_ref[...] = xn
    ho_ref[...] = _rms(xn, gn_ref[...]).astype(ho_ref.dtype)


def resid_norm(y, x, gy, gn, tm=256):
    m, d = x.shape
    row = pl.BlockSpec((tm, d), lambda i: (i, 0))
    vec = pl.BlockSpec((1, d), lambda i: (0, 0))
    return pl.pallas_call(
        _resid_kernel,
        grid=(m // tm,),
        in_specs=[row, row, vec, vec],
        out_specs=[row, row],
        out_shape=[jax.ShapeDtypeStruct((m, d), F32), jax.ShapeDtypeStruct((m, d), BF16)],
        compiler_params=_cparams(("parallel",)),
        name="resid_norm",
    )(y, x, gy, gn)


def _ffn_kernel(h_ref, wg_ref, wu_ref, wd_ref, x_ref, gp_ref, o_ref):
    j = pl.program_id(1)

    @pl.when(j == 0)
    def _():
        o_ref[...] = jnp.zeros(o_ref.shape, o_ref.dtype)

    h = h_ref[...]
    g = jnp.dot(h, wg_ref[...], preferred_element_type=F32)
    u = jnp.dot(h, wu_ref[...], preferred_element_type=F32)
    a = (g * jax.nn.sigmoid(g) * u).astype(BF16)
    o_ref[...] += jnp.dot(a, wd_ref[...], preferred_element_type=F32)

    @pl.when(j == pl.num_programs(1) - 1)
    def _():
        o_ref[...] = x_ref[...] + _rms(o_ref[...], gp_ref[...])


def ffn(h, wg, wu, wd, x, g_post, tm=512, tf=256):
    m, d = h.shape
    f = wg.shape[1]
    return pl.pallas_call(
        _ffn_kernel,
        grid=(m // tm, f // tf),
        in_specs=[pl.BlockSpec((tm, d), lambda i, j: (i, 0)),
                  pl.BlockSpec((d, tf), lambda i, j: (0, j)),
                  pl.BlockSpec((d, tf), lambda i, j: (0, j)),
                  pl.BlockSpec((tf, d), lambda i, j: (j, 0)),
                  pl.BlockSpec((tm, d), lambda i, j: (i, 0), pipeline_mode=pl.Buffered(1)),
                  pl.BlockSpec((1, d), lambda i, j: (0, 0))],
        out_specs=pl.BlockSpec((tm, d), lambda i, j: (i, 0)),
        out_shape=jax.ShapeDtypeStruct((m, d), F32),
        compiler_params=_cparams(("parallel", "arbitrary")),
        name="ffn",
    )(h, wg, wu, wd, x, g_post)


def _ple_kernel(xrow_ref, wg_ref, p_ref, wp_ref, x_ref, o_ref, xb_sc):
    @pl.when(pl.program_id(1) == 0)
    def _():
        xb_sc[...] = xrow_ref[...].astype(BF16)

    gate = jax.nn.sigmoid(jnp.dot(xb_sc[...], wg_ref[...], preferred_element_type=F32))
    e = jnp.dot(p_ref[...].astype(BF16), wp_ref[...], preferred_element_type=F32)
    o_ref[...] = x_ref[...] + gate * e


def ple(x, wg, p, wp, tm=512, tn=1024):
    m, d = x.shape
    n = wg.shape[1]
    pd = p.shape[1]
    return pl.pallas_call(
        _ple_kernel,
        grid=(m // tm, n // tn),
        in_specs=[pl.BlockSpec((tm, d), lambda i, j: (i, 0)),
                  pl.BlockSpec((d, tn), lambda i, j: (0, j)),
                  pl.BlockSpec((tm, pd), lambda i, j: (i, 0)),
                  pl.BlockSpec((pd, tn), lambda i, j: (0, j)),
                  pl.BlockSpec((tm, tn), lambda i, j: (i, j))],
        out_specs=pl.BlockSpec((tm, tn), lambda i, j: (i, j)),
        out_shape=jax.ShapeDtypeStruct((m, n), F32),
        scratch_shapes=[pltpu.VMEM((tm, d), BF16)],
        compiler_params=_cparams(("parallel", "arbitrary")),
        name="ple",
    )(x, wg, p, wp, x)


def _prep_w_in(w):
    c_q, c_kv, k_r, q_s, k_s, v_s = jnp.split(
        w, [Q_LORA, Q_LORA + KV_LORA, Q_LORA + KV_LORA + MLA_ROPE,
            Q_LORA + KV_LORA + MLA_ROPE + SWA_WIDTH,
            Q_LORA + KV_LORA + MLA_ROPE + SWA_WIDTH + SWA_KV_HEADS * SWA_HEAD_DIM], axis=1)
    q_s = q_s * (1.0 / math.sqrt(SWA_HEAD_DIM))
    parts = [c_q, c_kv, q_s, k_s, v_s, k_r, _swap_half(k_r)]
    return jnp.concatenate([part.astype(BF16) for part in parts], axis=1)


def _prep_w_q_up(w):
    w = (w * (math.log2(math.e) / math.sqrt(MLA_NOPE + MLA_ROPE))).reshape(Q_LORA, MLA_HEADS, MLA_NOPE + MLA_ROPE)
    nope = w[:, :, :MLA_NOPE].reshape(Q_LORA, MLA_WIDTH)
    rope = w[:, :, MLA_NOPE:]
    n_rope = MLA_HEADS * MLA_ROPE
    return jnp.concatenate([nope, rope.reshape(Q_LORA, n_rope),
                            _swap_half(rope).reshape(Q_LORA, n_rope)], axis=1).astype(BF16)


def _prep_w_kv_up(w):
    w = w.reshape(KV_LORA, MLA_HEADS, MLA_NOPE + MLA_V)
    wk = w[:, :, :MLA_NOPE].reshape(KV_LORA, MLA_WIDTH).astype(BF16)
    wvt = w[:, :, MLA_NOPE:].reshape(KV_LORA, MLA_WIDTH).T.astype(BF16)
    return wk, wvt


def _layer(x, p, pos_col, pos_row, invf, batch, seq, attn_pre_norm, w_in, q_a_norm, w_q_up,
           kv_a_norm, w_kv_up, sinks, mla_out_norm, swa_out_norm, w_o, attn_post_norm,
           ffn_pre_norm, w_gate, w_up, w_down, ffn_post_norm, w_ple_gate, w_ple_proj):
    row = lambda g: g.reshape(1, -1)
    proj = in_proj(x, row(attn_pre_norm), _prep_w_in(w_in))
    qn, qr = q_up(proj, row(q_a_norm), _prep_w_q_up(w_q_up), pos_col, invf)
    wk, wvt = _prep_w_kv_up(w_kv_up)
    kn, vt, kr = kv_up(proj, row(kv_a_norm), wk, wvt, pos_col, invf, batch, seq)
    o_mla = mla_attention(qn, qr, kn, kr, vt, batch, seq)
    o_swa = swa_attention(proj, sinks, pos_col, pos_row, batch, seq)
    y = out_proj(o_mla, o_swa, row(mla_out_norm), row(swa_out_norm), w_o.astype(BF16))
    x1, h2 = resid_norm(y, x, row(attn_post_norm), row(ffn_pre_norm))
    x2 = ffn(h2, w_gate.astype(BF16), w_up.astype(BF16), w_down.astype(BF16), x1, row(ffn_post_norm))
    return ple(x2, w_ple_gate.astype(BF16), p, w_ple_proj.astype(BF16))


def kernel(x, p, positions, attn_pre_norm, w_in, q_a_norm, w_q_up, kv_a_norm, w_kv_up, sinks,
           mla_out_norm, swa_out_norm, w_o, attn_post_norm, ffn_pre_norm, w_gate, w_up, w_down,
           ffn_post_norm, w_ple_gate, w_ple_proj):
    batch, seq, d = x.shape
    depth = w_in.shape[0]
    t = batch * seq
    pos_col = positions.reshape(t, 1)
    pos_row = positions.reshape(t // BLOCK, 1, BLOCK)
    half = MLA_ROPE // 2
    invf = np.asarray(ROPE_THETA ** (-(np.arange(LANES) % half) * 2.0 / MLA_ROPE), np.float32).reshape(1, LANES)
    invf = jnp.asarray(invf)
    xf = x.reshape(t, d)
    for i in range(depth):
        xf = _layer(xf, p[i].reshape(t, PLE_DIM), pos_col, pos_row, invf, batch, seq,
                    attn_pre_norm[i], w_in[i], q_a_norm[i], w_q_up[i], kv_a_norm[i], w_kv_up[i],
                    sinks[i], mla_out_norm[i], swa_out_norm[i], w_o[i], attn_post_norm[i],
                    ffn_pre_norm[i], w_gate[i], w_up[i], w_down[i], ffn_post_norm[i],
                    w_ple_gate[i], w_ple_proj[i])
    return xf.reshape(batch, seq, d)
```

```python
import functools
import math

import numpy as np
import jax
import jax.numpy as jnp
from jax import lax
from jax.experimental import pallas as pl
from jax.experimental.pallas import tpu as pltpu

D_MODEL = 4096
PLE_DIM = 256
MLA_HEADS = 16
MLA_NOPE = 128
MLA_ROPE = 64
MLA_V = 128
Q_LORA = 1024
KV_LORA = 512
MLA_WIDTH = MLA_HEADS * MLA_V
SWA_HEADS = 32
SWA_KV_HEADS = 4
SWA_HEAD_DIM = 64
SWA_GROUP = SWA_HEADS // SWA_KV_HEADS
SWA_WIDTH = SWA_HEADS * SWA_HEAD_DIM
WINDOW = 128
BLOCK = 128
ROPE_THETA = 10000.0
NORM_EPS = 1e-6
LOG2E = math.log2(math.e)
D_FF = 11008

LANES = 128
MXU_N = 256
MLA_TILE = 512
FFN_TILE = 256
VT_ROWS = MLA_V + 16
VMEM_LIMIT = 60 * 1024 * 1024

COL_CQ = 0
COL_CKV = COL_CQ + Q_LORA
COL_QSWA = COL_CKV + KV_LORA
COL_KSWA = COL_QSWA + SWA_WIDTH
COL_VSWA = COL_KSWA + SWA_KV_HEADS * SWA_HEAD_DIM
COL_KR = COL_VSWA + SWA_KV_HEADS * SWA_HEAD_DIM
D_IN2 = COL_KR + 2 * MLA_ROPE

F32 = jnp.float32
BF16 = jnp.bfloat16


def _cparams(sem):
    return pltpu.CompilerParams(dimension_semantics=sem, vmem_limit_bytes=VMEM_LIMIT)


def _rms(xf, g):
    ms = jnp.mean(xf * xf, axis=-1, keepdims=True)
    return xf * lax.rsqrt(ms + NORM_EPS) * g


def _swap_half(w):
    d = w.shape[-1]
    return jnp.concatenate([-w[..., d // 2:], w[..., : d // 2]], axis=-1)


def _rope_tables(pos_col, invf_row):
    ang = pos_col.astype(F32) * invf_row
    return jnp.cos(ang), jnp.sin(ang)


def _in_proj_kernel(x_ref, g_ref, w_ref, o_ref, h_sc):
    @pl.when(pl.program_id(1) == 0)
    def _():
        h_sc[...] = _rms(x_ref[...], g_ref[...]).astype(BF16)

    o_ref[...] = jnp.dot(h_sc[...], w_ref[...], preferred_element_type=F32).astype(o_ref.dtype)


def in_proj(x, g, w, tm=512, tn=D_IN2 // 3):
    m, k = x.shape
    n = w.shape[1]
    return pl.pallas_call(
        _in_proj_kernel,
        grid=(m // tm, n // tn),
        in_specs=[pl.BlockSpec((tm, k), lambda i, j: (i, 0)),
                  pl.BlockSpec((1, k), lambda i, j: (0, 0)),
                  pl.BlockSpec((k, tn), lambda i, j: (0, j))],
        out_specs=pl.BlockSpec((tm, tn), lambda i, j: (i, j)),
        out_shape=jax.ShapeDtypeStruct((m, n), BF16),
        scratch_shapes=[pltpu.VMEM((tm, k), BF16)],
        compiler_params=_cparams(("parallel", "arbitrary")),
        name="in_proj",
    )(x, g, w)


def _q_up_kernel(cq_ref, g_ref, w_ref, pos_ref, invf_ref, qn_ref, qr_ref):
    h = _rms(cq_ref[...].astype(F32), g_ref[...]).astype(BF16)
    r = jnp.dot(h, w_ref[...], preferred_element_type=F32)
    qn_ref[...] = r[:, :MLA_WIDTH].astype(BF16)
    cos, sin = _rope_tables(pos_ref[...], invf_ref[...])
    n_rope = MLA_HEADS * MLA_ROPE
    for c in range(n_rope // LANES):
        a = r[:, MLA_WIDTH + c * LANES: MLA_WIDTH + (c + 1) * LANES]
        b = r[:, MLA_WIDTH + n_rope + c * LANES: MLA_WIDTH + n_rope + (c + 1) * LANES]
        qr_ref[:, c * LANES:(c + 1) * LANES] = (a * cos + b * sin).astype(BF16)


def q_up(proj, g, w, pos_col, invf, tm=512):
    m = proj.shape[0]
    n_rope = MLA_HEADS * MLA_ROPE
    return pl.pallas_call(
        _q_up_kernel,
        grid=(m // tm,),
        in_specs=[pl.BlockSpec((tm, Q_LORA), lambda i: (i, COL_CQ // Q_LORA)),
                  pl.BlockSpec((1, Q_LORA), lambda i: (0, 0)),
                  pl.BlockSpec(w.shape, lambda i: (0, 0)),
                  pl.BlockSpec((tm, 1), lambda i: (i, 0)),
                  pl.BlockSpec((1, LANES), lambda i: (0, 0))],
        out_specs=[pl.BlockSpec((tm, MLA_WIDTH), lambda i: (i, 0)),
                   pl.BlockSpec((tm, n_rope), lambda i: (i, 0))],
        out_shape=[jax.ShapeDtypeStruct((m, MLA_WIDTH), BF16),
                   jax.ShapeDtypeStruct((m, n_rope), BF16)],
        compiler_params=_cparams(("parallel",)),
        name="q_up",
    )(proj, g, w, pos_col, invf)


def _kv_up_kernel(ckv_ref, g_ref, wk_ref, wvt_ref, krab_ref, pos_ref, invf_ref, kn_ref, vt_ref, kr_ref):
    h = _rms(ckv_ref[...].astype(F32), g_ref[...]).astype(BF16)
    kn_ref[...] = jnp.dot(h, wk_ref[...], preferred_element_type=F32).astype(BF16)
    vt = lax.dot_general(wvt_ref[...], h, (((1,), (1,)), ((), ())), preferred_element_type=F32)
    ones = jnp.ones((VT_ROWS - MLA_V, vt.shape[1]), BF16)
    for hh in range(MLA_HEADS):
        vt_ref[0, hh, 0, :MLA_V, :] = vt[hh * MLA_V:(hh + 1) * MLA_V, :].astype(BF16)
        vt_ref[0, hh, 0, MLA_V:, :] = ones
    cos, sin = _rope_tables(pos_ref[...], invf_ref[...])
    lane = lax.broadcasted_iota(jnp.int32, cos.shape, 1)
    lo = lane < MLA_ROPE
    prod = krab_ref[...].astype(F32) * jnp.where(lo, cos, sin)
    kk = prod + pltpu.roll(prod, MLA_ROPE, 1)
    zero = jnp.zeros_like(kk)
    kr_ref[:, :LANES] = jnp.where(lo, kk, zero).astype(BF16)
    kr_ref[:, LANES:] = jnp.where(lo, zero, kk).astype(BF16)


def kv_up(proj, g, wk, wvt, pos_col, invf, batch, seq, tm=MLA_TILE):
    m = proj.shape[0]
    nk = seq // tm
    return pl.pallas_call(
        _kv_up_kernel,
        grid=(m // tm,),
        in_specs=[pl.BlockSpec((tm, KV_LORA), lambda i: (i, COL_CKV // KV_LORA)),
                  pl.BlockSpec((1, KV_LORA), lambda i: (0, 0)),
                  pl.BlockSpec(wk.shape, lambda i: (0, 0)),
                  pl.BlockSpec(wvt.shape, lambda i: (0, 0)),
                  pl.BlockSpec((tm, LANES), lambda i: (i, COL_KR // LANES)),
                  pl.BlockSpec((tm, 1), lambda i: (i, 0)),
                  pl.BlockSpec((1, LANES), lambda i: (0, 0))],
        out_specs=[pl.BlockSpec((tm, MLA_WIDTH), lambda i: (i, 0)),
                   pl.BlockSpec((1, MLA_HEADS, 1, VT_ROWS, tm), lambda i: (i // nk, 0, i % nk, 0, 0)),
                   pl.BlockSpec((tm, 2 * LANES), lambda i: (i, 0))],
        out_shape=[jax.ShapeDtypeStruct((m, MLA_WIDTH), BF16),
                   jax.ShapeDtypeStruct((batch, MLA_HEADS, nk, VT_ROWS, tm), BF16),
                   jax.ShapeDtypeStruct((m, 2 * LANES), BF16)],
        compiler_params=_cparams(("parallel",)),
        name="kv_up",
    )(proj, g, wk, wvt, proj, pos_col, invf)


def _mla_kernel(qn_ref, qr_ref, kn_ref, kr_ref, vt_ref, o_ref, s_a, s_b, mx_a, mx_b, m_sc, acc_sc, *, tq, tk):
    qi = pl.program_id(2)
    q = jnp.concatenate([qn_ref[...], qr_ref[...]], axis=-1)
    m_sc[...] = jnp.full(m_sc.shape, -jnp.inf, F32)
    acc_sc[...] = jnp.zeros(acc_sc.shape, F32)

    n_chunk = tq // MXU_N
    blk_chunks = tk // MXU_N
    n_diag = tq // tk
    bufs = ((s_a, mx_a), (s_b, mx_b))

    def live_chunks(diag):
        return range(0 if diag is None else diag * blk_chunks, n_chunk)

    def scores(kb, buf, diag=None):
        s_ref, mx_ref = buf
        off = pl.multiple_of(kb * tk, tk)
        k = jnp.concatenate([kn_ref[pl.ds(off, tk), :], kr_ref[pl.ds(off, tk), :]], axis=-1)
        for c in live_chunks(diag):
            s = lax.dot_general(k, q[c * MXU_N:(c + 1) * MXU_N, :], (((1,), (1,)), ((), ())),
                                preferred_element_type=F32)
            if diag is not None and c < (diag + 1) * blk_chunks:
                key = lax.broadcasted_iota(jnp.int32, s.shape, 0) + diag * tk
                qry = lax.broadcasted_iota(jnp.int32, s.shape, 1) + c * MXU_N
                s = jnp.where(qry >= key, s, -jnp.inf)
            s_ref[c] = s
            mx_ref[:, c * MXU_N:(c + 1) * MXU_N] = jnp.max(s, axis=0, keepdims=True)

    def softmax_pv(kb, buf, diag=None, mask_here=False):
        s_ref, mx_ref = buf
        vt = vt_ref[0, 0, kb]
        for c in live_chunks(diag):
            sl = slice(c * MXU_N, (c + 1) * MXU_N)
            remask = mask_here and c < blk_chunks

            def s_chunk():
                s = s_ref[c]
                if remask:
                    key = lax.broadcasted_iota(jnp.int32, s.shape, 0)
                    qry = lax.broadcasted_iota(jnp.int32, s.shape, 1) + c * MXU_N
                    s = jnp.where(qry >= key, s, -jnp.inf)
                return s

            mx = jnp.max(s_chunk(), axis=0, keepdims=True) if remask else mx_ref[:, sl]
            m_old = m_sc[:, sl]
            m_new = jnp.maximum(m_old, mx)
            alpha = jnp.exp2(m_old - m_new)
            p = jnp.exp2(s_chunk() - m_new).astype(BF16)
            acc_sc[:, sl] = alpha * acc_sc[:, sl] + jnp.dot(vt, p, preferred_element_type=F32)
            m_sc[:, sl] = m_new

    n_full = n_diag * qi
    scores(0, bufs[0])

    def body(j, carry):
        scores(2 * j + 1, bufs[1])
        softmax_pv(2 * j, bufs[0])
        scores(2 * j + 2, bufs[0])
        softmax_pv(2 * j + 1, bufs[1])
        return carry

    lax.fori_loop(0, n_full // 2, body, 0)

    for d in range(n_diag):
        if d + 1 < n_diag:
            scores(n_full + d + 1, bufs[(d + 1) % 2], diag=d + 1)
        softmax_pv(n_full + d, bufs[d % 2], diag=d, mask_here=(d == 0))

    o_ref[...] = (acc_sc[:MLA_V, :] / acc_sc[MLA_V:MLA_V + 1, :]).T.astype(o_ref.dtype)


def mla_attention(qn, qr, kn, kr, vt, batch, seq, tq=4 * MLA_TILE, tk=MLA_TILE):
    assert tq % (2 * tk) == 0
    nq = seq // tq
    return pl.pallas_call(
        functools.partial(_mla_kernel, tq=tq, tk=tk),
        grid=(batch, MLA_HEADS, nq),
        in_specs=[pl.BlockSpec((tq, MLA_NOPE), lambda b, h, i: (b * nq + i, h)),
                  pl.BlockSpec((tq, LANES), lambda b, h, i: (b * nq + i, h // 2)),
                  pl.BlockSpec((seq, MLA_NOPE), lambda b, h, i: (b, h)),
                  pl.BlockSpec((seq, LANES), lambda b, h, i: (b, h % 2)),
                  pl.BlockSpec((1, 1, seq // tk, VT_ROWS, tk), lambda b, h, i: (b, h, 0, 0, 0))],
        out_specs=pl.BlockSpec((tq, MLA_V), lambda b, h, i: (b * nq + i, h)),
        out_shape=jax.ShapeDtypeStruct((batch * seq, MLA_WIDTH), BF16),
        scratch_shapes=[pltpu.VMEM((tq // MXU_N, tk, MXU_N), F32), pltpu.VMEM((tq // MXU_N, tk, MXU_N), F32),
                        pltpu.VMEM((1, tq), F32), pltpu.VMEM((1, tq), F32),
                        pltpu.VMEM((1, tq), F32), pltpu.VMEM((VT_ROWS, tq), F32)],
        compiler_params=_cparams(("parallel", "parallel", "parallel")),
        name="mla_attention",
    )(qn, qr, kn, kr, vt)


def _swa_kernel(sinks_ref, q0_ref, q1_ref, q2_ref, q3_ref, kc_ref, kp_ref, vc_ref, vp_ref,
                pq_ref, pkc_ref, pkp_ref, o_ref):
    blk = pl.program_id(1)
    q_refs = (q0_ref, q1_ref, q2_ref, q3_ref)
    k_all = jnp.concatenate([kp_ref[...], kc_ref[...]], axis=0).astype(F32)
    v_all = jnp.concatenate([vp_ref[...], vc_ref[...]], axis=0).astype(F32)
    pk = jnp.concatenate([pkp_ref[0], pkc_ref[0]], axis=-1).astype(F32)
    dist = jnp.abs(pq_ref[...].astype(F32) - pk)
    row = lax.broadcasted_iota(jnp.int32, dist.shape, 0)
    col = lax.broadcasted_iota(jnp.int32, dist.shape, 1)
    rel = BLOCK + row - col
    first_key = jnp.where(blk > 0, 0, BLOCK)
    valid = (rel >= 0) & (rel < WINDOW) & (col >= first_key)
    dist_w = jnp.where(valid, dist, jnp.inf)
    lane =lax.broadcasted_iota(jnp.int32, (2 * BLOCK, LANES), 1)
    lo = lane < SWA_HEAD_DIM
    lane_o = lax.broadcasted_iota(jnp.int32, (BLOCK, LANES), 1)
    lo_o = lane_o < SWA_HEAD_DIM

    def dup(x_all, g):
        pair = x_all[:, (g // 2) * LANES:(g // 2 + 1) * LANES]
        rolled = pltpu.roll(pair, SWA_HEAD_DIM, 1)
        return jnp.where(lo, pair, rolled) if g % 2 == 0 else jnp.where(lo, rolled, pair)

    for g in range(SWA_KV_HEADS):
        kk = dup(k_all, g)
        vv = dup(v_all, g).astype(BF16)
        zero = jnp.zeros_like(kk)
        k_half = (jnp.where(lo, kk, zero).astype(BF16), jnp.where(lo, zero, kk).astype(BF16))
        for i in range(SWA_GROUP // 2):
            q_pair = q_refs[g][:, i * LANES:(i + 1) * LANES]
            outs = []
            for j in range(2):
                head = g * SWA_GROUP + 2 * i + j
                slope2 = LOG2E * 2.0 ** (-8.0 * (head + 1) / SWA_HEADS)
                s = lax.dot_general(q_pair, k_half[j], (((1,), (1,)), ((), ())),
                                    preferred_element_type=F32)
                s = s - slope2 * dist_w
                sink2 = LOG2E * sinks_ref[head]
                m = jnp.maximum(jnp.max(s, axis=-1, keepdims=True), sink2)
                e = jnp.exp2(s - m)
                denom = jnp.sum(e, axis=-1, keepdims=True) + jnp.exp2(sink2 - m)
                pv = jnp.dot(e.astype(BF16), vv, preferred_element_type=F32)
                outs.append(pv / denom)
            col0 = (g * SWA_GROUP // 2 + i) * LANES
            o_ref[:, col0:col0 + LANES] = jnp.where(lo_o, outs[0], outs[1]).astype(o_ref.dtype)


def swa_attention(proj, sinks, pos_col, pos_row, batch, seq):
    nblk = seq // BLOCK
    kvw = SWA_KV_HEADS * SWA_HEAD_DIM
    gw = SWA_GROUP * SWA_HEAD_DIM
    cur = lambda b, n: b * nblk + n
    prev = lambda b, n: b * nblk + jnp.maximum(n - 1, 0)
    q_specs = [pl.BlockSpec((BLOCK, gw), functools.partial(lambda b, n, g: (cur(b, n), COL_QSWA // gw + g), g=g))
               for g in range(SWA_KV_HEADS)]
    return pl.pallas_call(
        _swa_kernel,
        grid=(batch, nblk),
        in_specs=[pl.BlockSpec(memory_space=pltpu.SMEM)] + q_specs + [
            pl.BlockSpec((BLOCK, kvw), lambda b, n: (cur(b, n), COL_KSWA // kvw)),
            pl.BlockSpec((BLOCK, kvw), lambda b, n: (prev(b, n), COL_KSWA // kvw)),
            pl.BlockSpec((BLOCK, kvw), lambda b, n: (cur(b, n), COL_VSWA // kvw)),
            pl.BlockSpec((BLOCK, kvw), lambda b, n: (prev(b, n), COL_VSWA // kvw)),
            pl.BlockSpec((BLOCK, 1), lambda b, n: (cur(b, n), 0)),
            pl.BlockSpec((1, 1, BLOCK), lambda b, n: (cur(b, n), 0, 0)),
            pl.BlockSpec((1, 1, BLOCK), lambda b, n: (prev(b, n), 0, 0))],
        out_specs=pl.BlockSpec((BLOCK, SWA_WIDTH), lambda b, n: (cur(b, n), 0)),
        out_shape=jax.ShapeDtypeStruct((batch * seq, SWA_WIDTH), BF16),
        compiler_params=_cparams(("parallel", "parallel")),
        name="swa_attention",
    )(sinks, proj, proj, proj, proj, proj, proj, proj, proj, pos_col, pos_row, pos_row)


def _wo_kernel(oa_ref, ob_ref, ga_ref, gb_ref, wa_ref, wb_ref, y_ref, na_sc, nb_sc):
    @pl.when(pl.program_id(1) == 0)
    def _():
        na_sc[...] = _rms(oa_ref[...].astype(F32), ga_ref[...]).astype(BF16)
        nb_sc[...] = _rms(ob_ref[...].astype(F32), gb_ref[...]).astype(BF16)

    y_ref[...] = (jnp.dot(na_sc[...], wa_ref[...], preferred_element_type=F32)
                  + jnp.dot(nb_sc[...], wb_ref[...], preferred_element_type=F32)).astype(y_ref.dtype)


def out_proj(oa, ob, ga, gb, w, tm=512, tn=1024):
    m, ka = oa.shape
    kb = ob.shape[1]
    n = w.shape[1]
    return pl.pallas_call(
        _wo_kernel,
        grid=(m // tm, n // tn),
        in_specs=[pl.BlockSpec((tm, ka), lambda i, j: (i, 0)),
                  pl.BlockSpec((tm, kb), lambda i, j: (i, 0)),
                  pl.BlockSpec((1, ka), lambda i, j: (0, 0)),
                  pl.BlockSpec((1, kb), lambda i, j: (0, 0)),
                  pl.BlockSpec((ka, tn), lambda i, j: (0, j)),
                  pl.BlockSpec((kb, tn), lambda i, j: (1, j))],
        out_specs=pl.BlockSpec((tm, tn), lambda i, j: (i, j)),
        out_shape=jax.ShapeDtypeStruct((m, n), BF16),
        scratch_shapes=[pltpu.VMEM((tm, ka), BF16), pltpu.VMEM((tm, kb), BF16)],
        compiler_params=_cparams(("parallel", "arbitrary")),
        name="out_proj",
    )(oa, ob, ga, gb, w, w)


def _resid_kernel(y_ref, x_ref, gy_ref, gn_ref, xo_ref, ho_ref):
    xn = x_ref[...] + _rms(y_ref[...].astype(F32), gy_ref[...])
    xo_ref[...] = xn
    ho_ref[...] = _rms(xn, gn_ref[...]).astype(ho_ref.dtype)


def resid_norm(y, x, gy, gn, tm=256):
    m, d = x.shape
    row = pl.BlockSpec((tm, d), lambda i: (i, 0))
    vec = pl.BlockSpec((1, d), lambda i: (0, 0))
    return pl.pallas_call(
        _resid_kernel,
        grid=(m // tm,),
        in_specs=[row, row, vec, vec],
        out_specs=[row, row],
        out_shape=[jax.ShapeDtypeStruct((m, d), F32), jax.ShapeDtypeStruct((m, d), BF16)],
        compiler_params=_cparams(("parallel",)),
        name="resid_norm",
    )(y, x, gy, gn)


def _ffn_kernel(h_ref, wgu_ref, wd_ref, x_ref, gp_ref, o_ref):
    j = pl.program_id(1)

    @pl.when(j == 0)
    def _():
        o_ref[...] = jnp.zeros(o_ref.shape, o_ref.dtype)

    gu = jnp.dot(h_ref[...], wgu_ref[0], preferred_element_type=F32)
    tf = gu.shape[1] // 2
    g, u = gu[:, :tf], gu[:, tf:]
    a = (g * jax.nn.sigmoid(g) * u).astype(BF16)
    o_ref[...] += jnp.dot(a, wd_ref[...], preferred_element_type=F32)

    @pl.when(j == pl.num_programs(1) - 1)
    def _():
        o_ref[...] = x_ref[...] + _rms(o_ref[...], gp_ref[...])


def ffn(h, wgu, wd, x, g_post, tm=512):
    m, d = h.shape
    nf, _, tf2 = wgu.shape
    tf = tf2 // 2
    return pl.pallas_call(
        _ffn_kernel,
        grid=(m // tm, nf),
        in_specs=[pl.BlockSpec((tm, d), lambda i, j: (i, 0)),
                  pl.BlockSpec((1, d, tf2), lambda i, j: (j, 0, 0)),
                  pl.BlockSpec((tf, d), lambda i, j: (j, 0)),
                  pl.BlockSpec((tm, d), lambda i, j: (i, 0), pipeline_mode=pl.Buffered(1)),
                  pl.BlockSpec((1, d), lambda i, j: (0, 0))],
        out_specs=pl.BlockSpec((tm, d), lambda i, j: (i, 0)),
        out_shape=jax.ShapeDtypeStruct((m, d), F32),
        compiler_params=_cparams(("parallel", "arbitrary")),
        name="ffn",
    )(h, wgu, wd, x, g_post)


def _ple_kernel(xrow_ref, wg_ref, p_ref, wp_ref, x_ref, o_ref, xb_sc):
    @pl.when(pl.program_id(1) == 0)
    def _():
        xb_sc[...] = xrow_ref[...].astype(BF16)

    gate = jax.nn.sigmoid(jnp.dot(xb_sc[...], wg_ref[...], preferred_element_type=F32))
    e = jnp.dot(p_ref[...].astype(BF16), wp_ref[...], preferred_element_type=F32)
    o_ref[...] = x_ref[...] + gate * e


def ple(x, wg, p, wp, tm=512, tn=1024):
    m, d = x.shape
    n = wg.shape[1]
    pd = p.shape[1]
    return pl.pallas_call(
        _ple_kernel,
        grid=(m // tm, n // tn),
        in_specs=[pl.BlockSpec((tm, d), lambda i, j: (i, 0)),
                  pl.BlockSpec((d, tn), lambda i, j: (0, j)),
                  pl.BlockSpec((tm, pd), lambda i, j: (i, 0)),
                  pl.BlockSpec((pd, tn), lambda i, j: (0, j)),
                  pl.BlockSpec((tm, tn), lambda i, j: (i, j))],
        out_specs=pl.BlockSpec((tm, tn), lambda i, j: (i, j)),
        out_shape=jax.ShapeDtypeStruct((m, n), F32),
        scratch_shapes=[pltpu.VMEM((tm, d), BF16)],
        compiler_params=_cparams(("parallel", "arbitrary")),
        name="ple",
    )(x, wg, p, wp, x)


def _prep_w_in(w):
    c_q, c_kv, k_r, q_s, k_s, v_s = jnp.split(
        w, [Q_LORA, Q_LORA + KV_LORA, Q_LORA + KV_LORA + MLA_ROPE,
            Q_LORA + KV_LORA + MLA_ROPE + SWA_WIDTH,
            Q_LORA + KV_LORA + MLA_ROPE + SWA_WIDTH + SWA_KV_HEADS * SWA_HEAD_DIM], axis=1)
    q_s = q_s * (LOG2E / math.sqrt(SWA_HEAD_DIM))
    parts = [c_q, c_kv, q_s, k_s, v_s, k_r, _swap_half(k_r)]
    return jnp.concatenate([part.astype(BF16) for part in parts], axis=1)


def _prep_w_q_up(w):
    w = (w * (LOG2E / math.sqrt(MLA_NOPE + MLA_ROPE))).reshape(Q_LORA, MLA_HEADS, MLA_NOPE + MLA_ROPE)
    nope = w[:, :, :MLA_NOPE].reshape(Q_LORA, MLA_WIDTH)
    rope = w[:, :, MLA_NOPE:]
    n_rope = MLA_HEADS * MLA_ROPE
    return jnp.concatenate([nope, rope.reshape(Q_LORA, n_rope),
                            _swap_half(rope).reshape(Q_LORA, n_rope)], axis=1).astype(BF16)


def _prep_w_kv_up(w):
    w = w.reshape(KV_LORA, MLA_HEADS, MLA_NOPE + MLA_V)
    wk = w[:, :, :MLA_NOPE].reshape(KV_LORA, MLA_WIDTH).astype(BF16)
    wvt = w[:, :, MLA_NOPE:].reshape(KV_LORA, MLA_WIDTH).T.astype(BF16)
    return wk, wvt


def _prep_w_gate_up(w_gate, w_up, tf=FFN_TILE):
    d, f = w_gate.shape
    both = jnp.concatenate([w_gate.astype(BF16).reshape(d, f // tf, tf),
                            w_up.astype(BF16).reshape(d, f // tf, tf)], axis=2)
    return both.transpose(1, 0, 2)


def _layer(x, p, pos_col, pos_row, invf, batch, seq, attn_pre_norm, w_in, q_a_norm, w_q_up,
           kv_a_norm, w_kv_up, sinks, mla_out_norm, swa_out_norm, w_o, attn_post_norm,
           ffn_pre_norm, w_gate, w_up, w_down, ffn_post_norm, w_ple_gate, w_ple_proj):
    row = lambda g: g.reshape(1, -1)
    proj = in_proj(x, row(attn_pre_norm), _prep_w_in(w_in))
    qn, qr = q_up(proj, row(q_a_norm), _prep_w_q_up(w_q_up), pos_col, invf)
    wk, wvt = _prep_w_kv_up(w_kv_up)
    kn, vt, kr = kv_up(proj, row(kv_a_norm), wk, wvt, pos_col, invf, batch, seq)
    o_mla = mla_attention(qn, qr, kn, kr, vt, batch, seq)
    o_swa = swa_attention(proj, sinks, pos_col, pos_row, batch, seq)
    y = out_proj(o_mla, o_swa, row(mla_out_norm), row(swa_out_norm), w_o.astype(BF16))
    x1, h2 = resid_norm(y, x, row(attn_post_norm), row(ffn_pre_norm))
    x2 = ffn(h2, _prep_w_gate_up(w_gate, w_up), w_down.astype(BF16), x1, row(ffn_post_norm))
    return ple(x2, w_ple_gate.astype(BF16), p, w_ple_proj.astype(BF16))


def kernel(x, p, positions, attn_pre_norm, w_in, q_a_norm, w_q_up, kv_a_norm, w_kv_up, sinks,
           mla_out_norm, swa_out_norm, w_o, attn_post_norm, ffn_pre_norm, w_gate, w_up, w_down,
           ffn_post_norm, w_ple_gate, w_ple_proj):
    batch, seq, d = x.shape
    depth = w_in.shape[0]
    t = batch * seq
    pos_col = positions.reshape(t, 1)
    pos_row = positions.reshape(t // BLOCK, 1, BLOCK)
    half = MLA_ROPE // 2
    invf = np.asarray(ROPE_THETA ** (-(np.arange(LANES) % half) * 2.0 / MLA_ROPE), np.float32).reshape(1, LANES)
    invf = jnp.asarray(invf)
    xf = x.reshape(t, d)
    for i in range(depth):
        xf = _layer(xf, p[i].reshape(t, PLE_DIM), pos_col, pos_row, invf, batch, seq,
                    attn_pre_norm[i], w_in[i], q_a_norm[i], w_q_up[i], kv_a_norm[i], w_kv_up[i],
                    sinks[i], mla_out_norm[i], swa_out_norm[i], w_o[i], attn_post_norm[i],
                    ffn_pre_norm[i], w_gate[i], w_up[i], w_down[i], ffn_post_norm[i],
                    w_ple_gate[i], w_ple_proj[i])
    return xf.reshape(batch, seq, d)
```

```python
import functools
import math

import numpy as np
import jax
import jax.numpy as jnp
from jax import lax
from jax.experimental import pallas as pl
from jax.experimental.pallas import tpu as pltpu

D_MODEL = 4096
PLE_DIM = 256
MLA_HEADS = 16
MLA_NOPE = 128
MLA_ROPE = 64
MLA_V = 128
Q_LORA = 1024
KV_LORA = 512
MLA_WIDTH = MLA_HEADS * MLA_V
SWA_HEADS = 32
SWA_KV_HEADS = 4
SWA_HEAD_DIM = 64
SWA_GROUP = SWA_HEADS // SWA_KV_HEADS
SWA_WIDTH = SWA_HEADS * SWA_HEAD_DIM
WINDOW = 128
BLOCK = 128
ROPE_THETA = 10000.0
NORM_EPS = 1e-6
LOG2E = math.log2(math.e)
D_FF = 11008

LANES = 128
MXU_N = 256
MLA_TILE = 512
FFN_TILE = 256
VT_ROWS = MLA_V + 16
VMEM_LIMIT = 60 * 1024 * 1024

COL_CQ = 0
COL_CKV = COL_CQ + Q_LORA
COL_QSWA = COL_CKV + KV_LORA
COL_KSWA = COL_QSWA + SWA_WIDTH
COL_VSWA = COL_KSWA + SWA_KV_HEADS * SWA_HEAD_DIM
COL_KR = COL_VSWA + SWA_KV_HEADS * SWA_HEAD_DIM
D_IN2 = COL_KR + 2 * MLA_ROPE

F32 = jnp.float32
BF16 = jnp.bfloat16


def _cparams(sem):
    return pltpu.CompilerParams(dimension_semantics=sem, vmem_limit_bytes=VMEM_LIMIT)


def _rms(xf, g):
    ms = jnp.mean(xf * xf, axis=-1, keepdims=True)
    return xf * lax.rsqrt(ms + NORM_EPS) * g


def _swap_half(w):
    d = w.shape[-1]
    return jnp.concatenate([-w[..., d // 2:], w[..., : d // 2]], axis=-1)


def _rope_tables(pos_col, invf_row):
    ang = pos_col.astype(F32) * invf_row
    return jnp.cos(ang), jnp.sin(ang)


def _in_proj_kernel(x_ref, g_ref, w_ref, o_ref, h_sc):
    @pl.when(pl.program_id(1) == 0)
    def _():
        h_sc[...] = _rms(x_ref[...], g_ref[...]).astype(BF16)

    o_ref[...] = jnp.dot(h_sc[...], w_ref[...], preferred_element_type=F32).astype(o_ref.dtype)


def in_proj(x, g, w, tm=512, tn=D_IN2 // 3):
    m, k = x.shape
    n = w.shape[1]
    return pl.pallas_call(
        _in_proj_kernel,
        grid=(m // tm, n // tn),
        in_specs=[pl.BlockSpec((tm, k), lambda i, j: (i, 0)),
                  pl.BlockSpec((1, k), lambda i, j: (0, 0)),
                  pl.BlockSpec((k, tn), lambda i, j: (0, j))],
        out_specs=pl.BlockSpec((tm, tn), lambda i, j: (i, j)),
        out_shape=jax.ShapeDtypeStruct((m, n), BF16),
        scratch_shapes=[pltpu.VMEM((tm, k), BF16)],
        compiler_params=_cparams(("parallel", "arbitrary")),
        name="in_proj",
    )(x, g, w)


def _q_up_kernel(cq_ref, g_ref, w_ref, pos_ref, invf_ref, qn_ref, qr_ref):
    h = _rms(cq_ref[...].astype(F32), g_ref[...]).astype(BF16)
    r = jnp.dot(h, w_ref[...], preferred_element_type=F32)
    qn_ref[...] = r[:, :MLA_WIDTH].astype(BF16)
    cos, sin = _rope_tables(pos_ref[...], invf_ref[...])
    n_rope = MLA_HEADS * MLA_ROPE
    for c in range(n_rope // LANES):
        a = r[:, MLA_WIDTH + c * LANES: MLA_WIDTH + (c + 1) * LANES]
        b = r[:, MLA_WIDTH + n_rope + c * LANES: MLA_WIDTH + n_rope + (c + 1) * LANES]
        qr_ref[:, c * LANES:(c + 1) * LANES] = (a * cos + b * sin).astype(BF16)


def q_up(proj, g, w, pos_col, invf, tm=512):
    m = proj.shape[0]
    n_rope = MLA_HEADS * MLA_ROPE
    return pl.pallas_call(
        _q_up_kernel,
        grid=(m // tm,),
        in_specs=[pl.BlockSpec((tm, Q_LORA), lambda i: (i, COL_CQ // Q_LORA)),
                  pl.BlockSpec((1, Q_LORA), lambda i: (0, 0)),
                  pl.BlockSpec(w.shape, lambda i: (0, 0)),
                  pl.BlockSpec((tm, 1), lambda i: (i, 0)),
                  pl.BlockSpec((1, LANES), lambda i: (0, 0))],
        out_specs=[pl.BlockSpec((tm, MLA_WIDTH), lambda i: (i, 0)),
                   pl.BlockSpec((tm, n_rope), lambda i: (i, 0))],
        out_shape=[jax.ShapeDtypeStruct((m, MLA_WIDTH), BF16),
                   jax.ShapeDtypeStruct((m, n_rope), BF16)],
        compiler_params=_cparams(("parallel",)),
        name="q_up",
    )(proj, g, w, pos_col, invf)


def _kv_up_kernel(ckv_ref, g_ref, wk_ref, wvt_ref, krab_ref, pos_ref, invf_ref, kn_ref, vt_ref, kr_ref):
    h = _rms(ckv_ref[...].astype(F32), g_ref[...]).astype(BF16)
    kn_ref[...] = jnp.dot(h, wk_ref[...], preferred_element_type=F32).astype(BF16)
    vt = lax.dot_general(wvt_ref[...], h, (((1,), (1,)), ((), ())), preferred_element_type=F32)
    ones = jnp.ones((VT_ROWS - MLA_V, vt.shape[1]), BF16)
    for hh in range(MLA_HEADS):
        vt_ref[0, hh, 0, :MLA_V, :] = vt[hh * MLA_V:(hh + 1) * MLA_V, :].astype(BF16)
        vt_ref[0, hh, 0, MLA_V:, :] = ones
    cos, sin = _rope_tables(pos_ref[...], invf_ref[...])
    lane = lax.broadcasted_iota(jnp.int32, cos.shape, 1)
    lo = lane < MLA_ROPE
    prod = krab_ref[...].astype(F32) * jnp.where(lo, cos, sin)
    kk = prod + pltpu.roll(prod, MLA_ROPE, 1)
    zero = jnp.zeros_like(kk)
    kr_ref[:, :LANES] = jnp.where(lo, kk, zero).astype(BF16)
    kr_ref[:, LANES:] = jnp.where(lo, zero, kk).astype(BF16)


def kv_up(proj, g, wk, wvt, pos_col, invf, batch, seq, tm=MLA_TILE):
    m = proj.shape[0]
    nk = seq // tm
    return pl.pallas_call(
        _kv_up_kernel,
        grid=(m // tm,),
        in_specs=[pl.BlockSpec((tm, KV_LORA), lambda i: (i, COL_CKV // KV_LORA)),
                  pl.BlockSpec((1, KV_LORA), lambda i: (0, 0)),
                  pl.BlockSpec(wk.shape, lambda i: (0, 0)),
                  pl.BlockSpec(wvt.shape, lambda i: (0, 0)),
                  pl.BlockSpec((tm, LANES), lambda i: (i, COL_KR // LANES)),
                  pl.BlockSpec((tm, 1), lambda i: (i, 0)),
                  pl.BlockSpec((1, LANES), lambda i: (0, 0))],
        out_specs=[pl.BlockSpec((tm, MLA_WIDTH), lambda i: (i, 0)),
                   pl.BlockSpec((1, MLA_HEADS, 1, VT_ROWS, tm), lambda i: (i // nk, 0, i % nk, 0, 0)),
                   pl.BlockSpec((tm, 2 * LANES), lambda i: (i, 0))],
        out_shape=[jax.ShapeDtypeStruct((m, MLA_WIDTH), BF16),
                   jax.ShapeDtypeStruct((batch, MLA_HEADS, nk, VT_ROWS, tm), BF16),
                   jax.ShapeDtypeStruct((m, 2 * LANES), BF16)],
        compiler_params=_cparams(("parallel",)),
        name="kv_up",
    )(proj, g, wk, wvt, proj, pos_col, invf)


def _mla_kernel(qn_ref, qr_ref, kn_ref, kr_ref, vt_ref, o_ref, s_a, s_b, mx_a, mx_b, m_sc, acc_sc, *, tq, tk):
    qi = pl.program_id(2)
    q = jnp.concatenate([qn_ref[...], qr_ref[...]], axis=-1)
    m_sc[...] = jnp.full(m_sc.shape, -jnp.inf, F32)
    acc_sc[...] = jnp.zeros(acc_sc.shape, F32)

    n_chunk = tq // MXU_N
    blk_chunks = tk // MXU_N
    n_diag = tq // tk
    bufs = ((s_a, mx_a), (s_b, mx_b))

    def live_chunks(diag):
        return range(0 if diag is None else diag * blk_chunks, n_chunk)

    def scores(kb, buf, diag=None):
        s_ref, mx_ref = buf
        off = pl.multiple_of(kb * tk, tk)
        k = jnp.concatenate([kn_ref[pl.ds(off, tk), :], kr_ref[pl.ds(off, tk), :]], axis=-1)
        for c in live_chunks(diag):
            s = lax.dot_general(k, q[c * MXU_N:(c + 1) * MXU_N, :], (((1,), (1,)), ((), ())),
                                preferred_element_type=F32)
            if diag is not None and c < (diag + 1) * blk_chunks:
                key = lax.broadcasted_iota(jnp.int32, s.shape, 0) + diag * tk
                qry = lax.broadcasted_iota(jnp.int32, s.shape, 1) + c * MXU_N
                s = jnp.where(qry >= key, s, -jnp.inf)
            s_ref[c] = s
            mx_ref[:, c * MXU_N:(c + 1) * MXU_N] = jnp.max(s, axis=0, keepdims=True)

    def softmax_pv(kb, buf, diag=None, mask_here=False):
        s_ref, mx_ref = buf
        vt = vt_ref[0, 0, kb]
        for c in live_chunks(diag):
            sl = slice(c * MXU_N, (c + 1) * MXU_N)
            remask = mask_here and c < blk_chunks

            def s_chunk():
                s = s_ref[c]
                if remask:
                    key = lax.broadcasted_iota(jnp.int32, s.shape, 0)
                    qry = lax.broadcasted_iota(jnp.int32, s.shape, 1) + c * MXU_N
                    s = jnp.where(qry >= key, s, -jnp.inf)
                return s

            mx = jnp.max(s_chunk(), axis=0, keepdims=True) if remask else mx_ref[:, sl]
            m_old = m_sc[:, sl]
            m_new = jnp.maximum(m_old, mx)
            alpha = jnp.exp2(m_old - m_new)
            p = jnp.exp2(s_chunk() - m_new).astype(BF16)
            acc_sc[:, sl] = alpha * acc_sc[:, sl] + jnp.dot(vt, p, preferred_element_type=F32)
            m_sc[:, sl] = m_new

    n_full = n_diag * qi
    scores(0, bufs[0])

    def body(j, carry):
        scores(2 * j + 1, bufs[1])
        softmax_pv(2 * j, bufs[0])
        scores(2 * j + 2, bufs[0])
        softmax_pv(2 * j + 1, bufs[1])
        return carry

    lax.fori_loop(0, n_full // 2, body, 0)

    for d in range(n_diag):
        if d + 1 < n_diag:
            scores(n_full + d + 1, bufs[(d + 1) % 2], diag=d + 1)
        softmax_pv(n_full + d, bufs[d % 2], diag=d, mask_here=(d == 0))

    o_ref[...] = (acc_sc[:MLA_V, :] / acc_sc[MLA_V:MLA_V + 1, :]).T.astype(o_ref.dtype)


def mla_attention(qn, qr, kn, kr, vt, batch, seq, tq=4 * MLA_TILE, tk=MLA_TILE):
    assert tq % (2 * tk) == 0
    nq = seq // tq
    return pl.pallas_call(
        functools.partial(_mla_kernel, tq=tq, tk=tk),
        grid=(batch, MLA_HEADS, nq),
        in_specs=[pl.BlockSpec((tq, MLA_NOPE), lambda b, h, i: (b * nq + i, h)),
                  pl.BlockSpec((tq, LANES), lambda b, h, i: (b * nq + i, h // 2)),
                  pl.BlockSpec((seq, MLA_NOPE), lambda b, h, i: (b, h)),
                  pl.BlockSpec((seq, LANES), lambda b, h, i: (b, h % 2)),
                  pl.BlockSpec((1, 1, seq // tk, VT_ROWS, tk), lambda b, h, i: (b, h, 0, 0, 0))],
        out_specs=pl.BlockSpec((tq, MLA_V), lambda b, h, i: (b * nq + i, h)),
        out_shape=jax.ShapeDtypeStruct((batch * seq, MLA_WIDTH), BF16),
        scratch_shapes=[pltpu.VMEM((tq // MXU_N, tk, MXU_N), F32), pltpu.VMEM((tq // MXU_N, tk, MXU_N), F32),
                        pltpu.VMEM((1, tq), F32), pltpu.VMEM((1, tq), F32),
                        pltpu.VMEM((1, tq), F32), pltpu.VMEM((VT_ROWS, tq), F32)],
        compiler_params=_cparams(("parallel", "parallel", "parallel")),
        name="mla_attention",
    )(qn, qr, kn, kr, vt)


def _swa_kernel(sinks_ref, q0_ref, q1_ref, q2_ref, q3_ref, kc_ref, kp_ref, vc_ref, vp_ref,
                pq_ref, pkc_ref, pkp_ref, o_ref):
    blk = pl.program_id(1)
    q_refs = (q0_ref, q1_ref, q2_ref, q3_ref)
    k_all = jnp.concatenate([kp_ref[...], kc_ref[...]], axis=0).astype(F32)
    v_all = jnp.concatenate([vp_ref[...], vc_ref[...]], axis=0).astype(F32)
    pk = jnp.concatenate([pkp_ref[0], pkc_ref[0]], axis=-1).astype(F32)
    dist = jnp.abs(pq_ref[...].astype(F32) - pk)
    row = lax.broadcasted_iota(jnp.int32, dist.shape, 0)
    col = lax.broadcasted_iota(jnp.int32, dist.shape, 1)
    rel = BLOCK + row - col
    first_key = jnp.where(blk > 0, 0, BLOCK)
    valid = (rel >= 0) & (rel < WINDOW) & (col >= first_key)
    dist_w = jnp.where(valid, dist, jnp.inf)
    lane =lax.broadcasted_iota(jnp.int32, (2 * BLOCK, LANES), 1)
    lo = lane < SWA_HEAD_DIM
    lane_o = lax.broadcasted_iota(jnp.int32, (BLOCK, LANES), 1)
    lo_o = lane_o < SWA_HEAD_DIM

    def dup(x_all, g):
        pair = x_all[:, (g // 2) * LANES:(g // 2 + 1) * LANES]
        rolled = pltpu.roll(pair, SWA_HEAD_DIM, 1)
        return jnp.where(lo, pair, rolled) if g % 2 == 0 else jnp.where(lo, rolled, pair)

    for g in range(SWA_KV_HEADS):
        kk = dup(k_all, g)
        vv = dup(v_all, g).astype(BF16)
        zero = jnp.zeros_like(kk)
        k_half = (jnp.where(lo, kk, zero).astype(BF16), jnp.where(lo, zero, kk).astype(BF16))
        for i in range(SWA_GROUP // 2):
            q_pair = q_refs[g][:, i * LANES:(i + 1) * LANES]
            outs = []
            for j in range(2):
                head = g * SWA_GROUP + 2 * i + j
                slope2 = LOG2E * 2.0 ** (-8.0 * (head + 1) / SWA_HEADS)
                s = lax.dot_general(q_pair, k_half[j], (((1,), (1,)), ((), ())),
                                    preferred_element_type=F32)
                s = s - slope2 * dist_w
                sink2 = LOG2E * sinks_ref[head]
                m = jnp.maximum(jnp.max(s, axis=-1, keepdims=True), sink2)
                e = jnp.exp2(s - m)
                denom = jnp.sum(e, axis=-1, keepdims=True) + jnp.exp2(sink2 - m)
                pv = jnp.dot(e.astype(BF16), vv, preferred_element_type=F32)
                outs.append(pv / denom)
            col0 = (g * SWA_GROUP // 2 + i) * LANES
            o_ref[:, col0:col0 + LANES] = jnp.where(lo_o, outs[0], outs[1]).astype(o_ref.dtype)


def swa_attention(proj, sinks, pos_col, pos_row, batch, seq):
    nblk = seq // BLOCK
    kvw = SWA_KV_HEADS * SWA_HEAD_DIM
    gw = SWA_GROUP * SWA_HEAD_DIM
    cur = lambda b, n: b * nblk + n
    prev = lambda b, n: b * nblk + jnp.maximum(n - 1, 0)
    q_specs = [pl.BlockSpec((BLOCK, gw), functools.partial(lambda b, n, g: (cur(b, n), COL_QSWA // gw + g), g=g))
               for g in range(SWA_KV_HEADS)]
    return pl.pallas_call(
        _swa_kernel,
        grid=(batch, nblk),
        in_specs=[pl.BlockSpec(memory_space=pltpu.SMEM)] + q_specs + [
            pl.BlockSpec((BLOCK, kvw), lambda b, n: (cur(b, n), COL_KSWA // kvw)),
            pl.BlockSpec((BLOCK, kvw), lambda b, n: (prev(b, n), COL_KSWA // kvw)),
            pl.BlockSpec((BLOCK, kvw), lambda b, n: (cur(b, n), COL_VSWA // kvw)),
            pl.BlockSpec((BLOCK, kvw), lambda b, n: (prev(b, n), COL_VSWA // kvw)),
            pl.BlockSpec((BLOCK, 1), lambda b, n: (cur(b, n), 0)),
            pl.BlockSpec((1, 1, BLOCK), lambda b, n: (cur(b, n), 0, 0)),
            pl.BlockSpec((1, 1, BLOCK), lambda b, n: (prev(b, n), 0, 0))],
        out_specs=pl.BlockSpec((BLOCK, SWA_WIDTH), lambda b, n: (cur(b, n), 0)),
        out_shape=jax.ShapeDtypeStruct((batch * seq, SWA_WIDTH), BF16),
        compiler_params=_cparams(("parallel", "parallel")),
        name="swa_attention",
    )(sinks, proj, proj, proj, proj, proj, proj, proj, proj, pos_col, pos_row, pos_row)


def _wo_kernel(oa_ref, ob_ref, ga_ref, gb_ref, wa_ref, wb_ref, y_ref, na_sc, nb_sc):
    @pl.when(pl.program_id(1) == 0)
    def _():
        na_sc[...] = _rms(oa_ref[...].astype(F32), ga_ref[...]).astype(BF16)
        nb_sc[...] = _rms(ob_ref[...].astype(F32), gb_ref[...]).astype(BF16)

    y_ref[...] = (jnp.dot(na_sc[...], wa_ref[...], preferred_element_type=F32)
                  + jnp.dot(nb_sc[...], wb_ref[...], preferred_element_type=F32)).astype(y_ref.dtype)


def out_proj(oa, ob, ga, gb, w, tm=512, tn=1024):
    m, ka = oa.shape
    kb = ob.shape[1]
    n = w.shape[1]
    return pl.pallas_call(
        _wo_kernel,
        grid=(m // tm, n // tn),
        in_specs=[pl.BlockSpec((tm, ka), lambda i, j: (i, 0)),
                  pl.BlockSpec((tm, kb), lambda i, j: (i, 0)),
                  pl.BlockSpec((1, ka), lambda i, j: (0, 0)),
                  pl.BlockSpec((1, kb), lambda i, j: (0, 0)),
                  pl.BlockSpec((ka, tn), lambda i, j: (0, j)),
                  pl.BlockSpec((kb, tn), lambda i, j: (1, j))],
        out_specs=pl.BlockSpec((tm, tn), lambda i, j: (i, j)),
        out_shape=jax.ShapeDtypeStruct((m, n), BF16),
        scratch_shapes=[pltpu.VMEM((tm, ka), BF16), pltpu.VMEM((tm, kb), BF16)],
        compiler_params=_cparams(("parallel", "arbitrary")),
        name="out_proj",
    )(oa, ob, ga, gb, w, w)


def _resid_kernel(y_ref, x_ref, gy_ref, gn_ref, xo_ref, ho_ref):
    xn = x_ref[...] + _rms(y_ref[...].astype(F32), gy_ref[...])
    xo_ref[...] = xn
    ho_ref[...] = _rms(xn, gn_ref[...]).astype(ho_ref.dtype)


def resid_norm(y, x, gy, gn, tm=256):
    m, d = x.shape
    row = pl.BlockSpec((tm, d), lambda i: (i, 0))
    vec = pl.BlockSpec((1, d), lambda i: (0, 0))
    return pl.pallas_call(
        _resid_kernel,
        grid=(m // tm,),
        in_specs=[row, row, vec, vec],
        out_specs=[row, row],
        out_shape=[jax.ShapeDtypeStruct((m, d), F32), jax.ShapeDtypeStruct((m, d), BF16)],
        compiler_params=_cparams(("parallel",)),
        name="resid_norm",
    )(y, x, gy, gn)


def _ffn_kernel(h_ref, wg_ref, wu_ref, wd_ref, x_ref, gp_ref, o_ref, *, dn, rows):
    j = pl.program_id(1)
    tm, d = o_ref.shape

    @pl.when(j == 0)
    def _():
        o_ref[...] = jnp.zeros(o_ref.shape, o_ref.dtype)

    h = h_ref[...]
    g = jnp.dot(h, wg_ref[...], preferred_element_type=F32)
    u = jnp.dot(h, wu_ref[...], preferred_element_type=F32)
    a = (g * jax.nn.sigmoid(g) * u).astype(BF16)
    for c in range(d // dn):
        o_ref[:, c * dn:(c + 1) * dn] += jnp.dot(a, wd_ref[:, c * dn:(c + 1) * dn],
                                                 preferred_element_type=F32)

    @pl.when(j == pl.num_programs(1) - 1)
    def _():
        def body(r, carry):
            rs = pl.ds(pl.multiple_of(r * rows, rows), rows)
            o_ref[rs, :] = x_ref[rs, :] + _rms(o_ref[rs, :], gp_ref[...])
            return carry

        lax.fori_loop(0, tm // rows, body, 0)


def ffn(h, wg, wu, wd, x, g_post, tm=1024, tf=FFN_TILE):
    m, d = h.shape
    f = wg.shape[1]
    once = dict(pipeline_mode=pl.Buffered(1))
    return pl.pallas_call(
        functools.partial(_ffn_kernel, dn=1024, rows=32),
        grid=(m // tm, f // tf),
        in_specs=[pl.BlockSpec((tm, d), lambda i, j: (i, 0), **once),
                  pl.BlockSpec((d, tf), lambda i, j: (0, j)),
                  pl.BlockSpec((d, tf), lambda i, j: (0, j)),
                  pl.BlockSpec((tf, d), lambda i, j: (j, 0)),
                  pl.BlockSpec((tm, d), lambda i, j: (i, 0), **once),
                  pl.BlockSpec((1, d), lambda i, j: (0, 0))],
        out_specs=pl.BlockSpec((tm, d), lambda i, j: (i, 0), **once),
        out_shape=jax.ShapeDtypeStruct((m, d), F32),
        compiler_params=_cparams(("parallel", "arbitrary")),
        name="ffn",
    )(h, wg, wu, wd, x, g_post)


def _ple_kernel(xrow_ref, wg_ref, p_ref, wp_ref, x_ref, o_ref, xb_sc):
    @pl.when(pl.program_id(1) == 0)
    def _():
        xb_sc[...] = xrow_ref[...].astype(BF16)

    gate = jax.nn.sigmoid(jnp.dot(xb_sc[...], wg_ref[...], preferred_element_type=F32))
    e = jnp.dot(p_ref[...].astype(BF16), wp_ref[...], preferred_element_type=F32)
    o_ref[...] = x_ref[...] + gate * e


def ple(x, wg, p, wp, tm=512, tn=1024):
    m, d = x.shape
    n = wg.shape[1]
    pd = p.shape[1]
    return pl.pallas_call(
        _ple_kernel,
        grid=(m // tm, n // tn),
        in_specs=[pl.BlockSpec((tm, d), lambda i, j: (i, 0)),
                  pl.BlockSpec((d, tn), lambda i, j: (0, j)),
                  pl.BlockSpec((tm, pd), lambda i, j: (i, 0)),
                  pl.BlockSpec((pd, tn), lambda i, j: (0, j)),
                  pl.BlockSpec((tm, tn), lambda i, j: (i, j))],
        out_specs=pl.BlockSpec((tm, tn), lambda i, j: (i, j)),
        out_shape=jax.ShapeDtypeStruct((m, n), F32),
        scratch_shapes=[pltpu.VMEM((tm, d), BF16)],
        compiler_params=_cparams(("parallel", "arbitrary")),
        name="ple",
    )(x, wg, p, wp, x)


def _prep_w_in(w):
    c_q, c_kv, k_r, q_s, k_s, v_s = jnp.split(
        w, [Q_LORA, Q_LORA + KV_LORA, Q_LORA + KV_LORA + MLA_ROPE,
            Q_LORA + KV_LORA + MLA_ROPE + SWA_WIDTH,
            Q_LORA + KV_LORA + MLA_ROPE + SWA_WIDTH + SWA_KV_HEADS * SWA_HEAD_DIM], axis=1)
    q_s = q_s * (LOG2E / math.sqrt(SWA_HEAD_DIM))
    parts = [c_q, c_kv, q_s, k_s, v_s, k_r, _swap_half(k_r)]
    return jnp.concatenate([part.astype(BF16) for part in parts], axis=1)


def _prep_w_q_up(w):
    w = (w * (LOG2E / math.sqrt(MLA_NOPE + MLA_ROPE))).reshape(Q_LORA, MLA_HEADS, MLA_NOPE + MLA_ROPE)
    nope = w[:, :, :MLA_NOPE].reshape(Q_LORA, MLA_WIDTH)
    rope = w[:, :, MLA_NOPE:]
    n_rope = MLA_HEADS * MLA_ROPE
    return jnp.concatenate([nope, rope.reshape(Q_LORA, n_rope),
                            _swap_half(rope).reshape(Q_LORA, n_rope)], axis=1).astype(BF16)


def _prep_w_kv_up(w):
    w = w.reshape(KV_LORA, MLA_HEADS, MLA_NOPE + MLA_V)
    wk = w[:, :, :MLA_NOPE].reshape(KV_LORA, MLA_WIDTH).astype(BF16)
    wvt = w[:, :, MLA_NOPE:].reshape(KV_LORA, MLA_WIDTH).T.astype(BF16)
    return wk, wvt


def _layer(x, p, pos_col, pos_row, invf, batch, seq, attn_pre_norm, w_in, q_a_norm, w_q_up,
           kv_a_norm, w_kv_up, sinks, mla_out_norm, swa_out_norm, w_o, attn_post_norm,
           ffn_pre_norm, w_gate, w_up, w_down, ffn_post_norm, w_ple_gate, w_ple_proj):
    row = lambda g: g.reshape(1, -1)
    proj = in_proj(x, row(attn_pre_norm), _prep_w_in(w_in))
    qn, qr = q_up(proj, row(q_a_norm), _prep_w_q_up(w_q_up), pos_col, invf)
    wk, wvt = _prep_w_kv_up(w_kv_up)
    kn, vt, kr = kv_up(proj, row(kv_a_norm), wk, wvt, pos_col, invf, batch, seq)
    o_mla = mla_attention(qn, qr, kn, kr, vt, batch, seq)
    o_swa = swa_attention(proj, sinks, pos_col, pos_row, batch, seq)
    y = out_proj(o_mla, o_swa, row(mla_out_norm), row(swa_out_norm), w_o.astype(BF16))
    x1, h2 = resid_norm(y, x, row(attn_post_norm), row(ffn_pre_norm))
    x2 = ffn(h2, w_gate.astype(BF16), w_up.astype(BF16), w_down.astype(BF16), x1, row(ffn_post_norm))
    return ple(x2, w_ple_gate.astype(BF16), p, w_ple_proj.astype(BF16))


def kernel(x, p, positions, attn_pre_norm, w_in, q_a_norm, w_q_up, kv_a_norm, w_kv_up, sinks,
           mla_out_norm, swa_out_norm, w_o, attn_post_norm, ffn_pre_norm, w_gate, w_up, w_down,
           ffn_post_norm, w_ple_gate, w_ple_proj):
    batch, seq, d = x.shape
    depth = w_in.shape[0]
    t = batch * seq
    pos_col = positions.reshape(t, 1)
    pos_row = positions.reshape(t // BLOCK, 1, BLOCK)
    half = MLA_ROPE // 2
    invf = np.asarray(ROPE_THETA ** (-(np.arange(LANES) % half) * 2.0 / MLA_ROPE), np.float32).reshape(1, LANES)
    invf = jnp.asarray(invf)
    xf = x.reshape(t, d)
    for i in range(depth):
        xf = _layer(xf, p[i].reshape(t, PLE_DIM), pos_col, pos_row, invf, batch, seq,
                    attn_pre_norm[i], w_in[i], q_a_norm[i], w_q_up[i], kv_a_norm[i], w_kv_up[i],
                    sinks[i], mla_out_norm[i], swa_out_norm[i], w_o[i], attn_post_norm[i],
                    ffn_pre_norm[i], w_gate[i], w_up[i], w_down[i], ffn_post_norm[i],
                    w_ple_gate[i], w_ple_proj[i])
    return xf.reshape(batch, seq, d)
```

```python
import functools
import math

import numpy as np
import jax
import jax.numpy as jnp
from jax import lax
from jax.experimental import pallas as pl
from jax.experimental.pallas import tpu as pltpu

D_MODEL = 4096
PLE_DIM = 256
MLA_HEADS = 16
MLA_NOPE = 128
MLA_ROPE = 64
MLA_V = 128
Q_LORA = 1024
KV_LORA = 512
MLA_WIDTH = MLA_HEADS * MLA_V
SWA_HEADS = 32
SWA_KV_HEADS = 4
SWA_HEAD_DIM = 64
SWA_GROUP = SWA_HEADS // SWA_KV_HEADS
SWA_WIDTH = SWA_HEADS * SWA_HEAD_DIM
WINDOW = 128
BLOCK = 128
ROPE_THETA = 10000.0
NORM_EPS = 1e-6
LOG2E = math.log2(math.e)
D_FF = 11008

LANES = 128
MXU_N = 256
MLA_TILE = 512
ROW_TILE = 256
FFN_TILE = 256
VT_ROWS = MLA_V + 16
VMEM_LIMIT = 60 * 1024 * 1024

COL_CQ = 0
COL_CKV = COL_CQ + Q_LORA
COL_QSWA = COL_CKV + KV_LORA
COL_KSWA = COL_QSWA + SWA_WIDTH
COL_VSWA = COL_KSWA + SWA_KV_HEADS * SWA_HEAD_DIM
COL_KR = COL_VSWA + SWA_KV_HEADS * SWA_HEAD_DIM
D_IN2 = COL_KR + 2 * MLA_ROPE

F32 = jnp.float32
BF16 = jnp.bfloat16


def _cparams(sem):
    return pltpu.CompilerParams(dimension_semantics=sem, vmem_limit_bytes=VMEM_LIMIT)


def _rms(xf, g):
    ms = jnp.mean(xf * xf, axis=-1, keepdims=True)
    return xf * lax.rsqrt(ms + NORM_EPS) * g


def _swap_half(w):
    d = w.shape[-1]
    return jnp.concatenate([-w[..., d // 2:], w[..., : d // 2]], axis=-1)


def _rope_tables(pos_col, invf_row):
    ang = pos_col.astype(F32) * invf_row
    return jnp.cos(ang), jnp.sin(ang)


def _col_chunks(n, width):
    return [(c, min(c + width, n)) for c in range(0, n, width)]


def _in_proj_kernel(x_ref, g_ref, w_ref, o_ref, *, chunks):
    h = _rms(x_ref[...], g_ref[...]).astype(BF16)
    for lo, hi in chunks:
        o_ref[:, lo:hi] = jnp.dot(h, w_ref[:, lo:hi], preferred_element_type=F32).astype(o_ref.dtype)


def in_proj(x, g, w, tm=ROW_TILE):
    m, k = x.shape
    n = w.shape[1]
    return pl.pallas_call(
        functools.partial(_in_proj_kernel, chunks=_col_chunks(n, 6 * MXU_N)),
        grid=(m // tm,),
        in_specs=[pl.BlockSpec((tm, k), lambda i: (i, 0)),
                  pl.BlockSpec((1, k), lambda i: (0, 0)),
                  pl.BlockSpec((k, n), lambda i: (0, 0), pipeline_mode=pl.Buffered(1))],
        out_specs=pl.BlockSpec((tm, n), lambda i: (i, 0)),
        out_shape=jax.ShapeDtypeStruct((m, n), BF16),
        compiler_params=_cparams(("parallel",)),
        name="in_proj",
    )(x, g, w)


def _q_up_kernel(cq_ref, g_ref, w_ref, pos_ref, invf_ref, qn_ref, qr_ref):
    h = _rms(cq_ref[...].astype(F32), g_ref[...]).astype(BF16)
    r = jnp.dot(h, w_ref[...], preferred_element_type=F32)
    qn_ref[...] = r[:, :MLA_WIDTH].astype(BF16)
    cos, sin = _rope_tables(pos_ref[...], invf_ref[...])
    n_rope = MLA_HEADS * MLA_ROPE
    for c in range(n_rope // LANES):
        a = r[:, MLA_WIDTH + c * LANES: MLA_WIDTH + (c + 1) * LANES]
        b = r[:, MLA_WIDTH + n_rope + c * LANES: MLA_WIDTH + n_rope + (c + 1) * LANES]
        qr_ref[:, c * LANES:(c + 1) * LANES] = (a * cos + b * sin).astype(BF16)


def q_up(proj, g, w, pos_col, invf, tm=512):
    m = proj.shape[0]
    n_rope = MLA_HEADS * MLA_ROPE
    return pl.pallas_call(
        _q_up_kernel,
        grid=(m // tm,),
        in_specs=[pl.BlockSpec((tm, Q_LORA), lambda i: (i, COL_CQ // Q_LORA)),
                  pl.BlockSpec((1, Q_LORA), lambda i: (0, 0)),
                  pl.BlockSpec(w.shape, lambda i: (0, 0)),
                  pl.BlockSpec((tm, 1), lambda i: (i, 0)),
                  pl.BlockSpec((1, LANES), lambda i: (0, 0))],
        out_specs=[pl.BlockSpec((tm, MLA_WIDTH), lambda i: (i, 0)),
                   pl.BlockSpec((tm, n_rope), lambda i: (i, 0))],
        out_shape=[jax.ShapeDtypeStruct((m, MLA_WIDTH), BF16),
                   jax.ShapeDtypeStruct((m, n_rope), BF16)],
        compiler_params=_cparams(("parallel",)),
        name="q_up",
    )(proj, g, w, pos_col, invf)


def _kv_up_kernel(ckv_ref, g_ref, wk_ref, wvt_ref, krab_ref, pos_ref, invf_ref, kn_ref, vt_ref, kr_ref):
    h = _rms(ckv_ref[...].astype(F32), g_ref[...]).astype(BF16)
    kn_ref[...] = jnp.dot(h, wk_ref[...], preferred_element_type=F32).astype(BF16)
    vt = lax.dot_general(wvt_ref[...], h, (((1,), (1,)), ((), ())), preferred_element_type=F32)
    ones = jnp.ones((VT_ROWS - MLA_V, vt.shape[1]), BF16)
    for hh in range(MLA_HEADS):
        vt_ref[0, hh, 0, :MLA_V, :] = vt[hh * MLA_V:(hh + 1) * MLA_V, :].astype(BF16)
        vt_ref[0, hh, 0, MLA_V:, :] = ones
    cos, sin = _rope_tables(pos_ref[...], invf_ref[...])
    lane = lax.broadcasted_iota(jnp.int32, cos.shape, 1)
    lo = lane < MLA_ROPE
    prod = krab_ref[...].astype(F32) * jnp.where(lo, cos, sin)
    kk = prod + pltpu.roll(prod, MLA_ROPE, 1)
    zero = jnp.zeros_like(kk)
    kr_ref[:, :LANES] = jnp.where(lo, kk, zero).astype(BF16)
    kr_ref[:, LANES:] = jnp.where(lo, zero, kk).astype(BF16)


def kv_up(proj, g, wk, wvt, pos_col, invf, batch, seq, tm=MLA_TILE):
    m = proj.shape[0]
    nk = seq // tm
    return pl.pallas_call(
        _kv_up_kernel,
        grid=(m // tm,),
        in_specs=[pl.BlockSpec((tm, KV_LORA), lambda i: (i, COL_CKV // KV_LORA)),
                  pl.BlockSpec((1, KV_LORA), lambda i: (0, 0)),
                  pl.BlockSpec(wk.shape, lambda i: (0, 0)),
                  pl.BlockSpec(wvt.shape, lambda i: (0, 0)),
                  pl.BlockSpec((tm, LANES), lambda i: (i, COL_KR // LANES)),
                  pl.BlockSpec((tm, 1), lambda i: (i, 0)),
                  pl.BlockSpec((1, LANES), lambda i: (0, 0))],
        out_specs=[pl.BlockSpec((tm, MLA_WIDTH), lambda i: (i, 0)),
                   pl.BlockSpec((1, MLA_HEADS, 1, VT_ROWS, tm), lambda i: (i // nk, 0, i % nk, 0, 0)),
                   pl.BlockSpec((tm, 2 * LANES), lambda i: (i, 0))],
        out_shape=[jax.ShapeDtypeStruct((m, MLA_WIDTH), BF16),
                   jax.ShapeDtypeStruct((batch, MLA_HEADS, nk, VT_ROWS, tm), BF16),
                   jax.ShapeDtypeStruct((m, 2 * LANES), BF16)],
        compiler_params=_cparams(("parallel",)),
        name="kv_up",
    )(proj, g, wk, wvt, proj, pos_col, invf)


def _mla_kernel(qn_ref, qr_ref, kn_ref, kr_ref, vt_ref, o_ref, s_a, s_b, mx_a, mx_b, m_sc, acc_sc, *, tq, tk):
    qi = pl.program_id(2)
    q = jnp.concatenate([qn_ref[...], qr_ref[...]], axis=-1)
    m_sc[...] = jnp.full(m_sc.shape, -jnp.inf, F32)
    acc_sc[...] = jnp.zeros(acc_sc.shape, F32)

    n_chunk = tq // MXU_N
    blk_chunks = tk // MXU_N
    n_diag = tq // tk
    bufs = ((s_a, mx_a), (s_b, mx_b))

    def live_chunks(diag):
        return range(0 if diag is None else diag * blk_chunks, n_chunk)

    def scores(kb, buf, diag=None):
        s_ref, mx_ref = buf
        off = pl.multiple_of(kb * tk, tk)
        k = jnp.concatenate([kn_ref[pl.ds(off, tk), :], kr_ref[pl.ds(off, tk), :]], axis=-1)
        for c in live_chunks(diag):
            s = lax.dot_general(k, q[c * MXU_N:(c + 1) * MXU_N, :], (((1,), (1,)), ((), ())),
                                preferred_element_type=F32)
            if diag is not None and c < (diag + 1) * blk_chunks:
                key = lax.broadcasted_iota(jnp.int32, s.shape, 0) + diag * tk
                qry = lax.broadcasted_iota(jnp.int32, s.shape, 1) + c * MXU_N
                s = jnp.where(qry >= key, s, -jnp.inf)
            s_ref[c] = s
            mx_ref[:, c * MXU_N:(c + 1) * MXU_N] = jnp.max(s, axis=0, keepdims=True)

    def softmax_pv(kb, buf, diag=None, mask_here=False):
        s_ref, mx_ref = buf
        vt = vt_ref[0, 0, kb]
        for c in live_chunks(diag):
            sl = slice(c * MXU_N, (c + 1) * MXU_N)
            remask = mask_here and c < blk_chunks

            def s_chunk():
                s = s_ref[c]
                if remask:
                    key = lax.broadcasted_iota(jnp.int32, s.shape, 0)
                    qry = lax.broadcasted_iota(jnp.int32, s.shape, 1) + c * MXU_N
                    s = jnp.where(qry >= key, s, -jnp.inf)
                return s

            mx = jnp.max(s_chunk(), axis=0, keepdims=True) if remask else mx_ref[:, sl]
            m_old = m_sc[:, sl]
            m_new = jnp.maximum(m_old, mx)
            alpha = jnp.exp2(m_old - m_new)
            p = jnp.exp2(s_chunk() - m_new).astype(BF16)
            acc_sc[:, sl] = alpha * acc_sc[:, sl] + jnp.dot(vt, p, preferred_element_type=F32)
            m_sc[:, sl] = m_new

    n_full = n_diag * qi
    scores(0, bufs[0])

    def body(j, carry):
        scores(2 * j + 1, bufs[1])
        softmax_pv(2 * j, bufs[0])
        scores(2 * j + 2, bufs[0])
        softmax_pv(2 * j + 1, bufs[1])
        return carry

    lax.fori_loop(0, n_full // 2, body, 0)

    for d in range(n_diag):
        if d + 1 < n_diag:
            scores(n_full + d + 1, bufs[(d + 1) % 2], diag=d + 1)
        softmax_pv(n_full + d, bufs[d % 2], diag=d, mask_here=(d == 0))

    o_ref[...] = (acc_sc[:MLA_V, :] / acc_sc[MLA_V:MLA_V + 1, :]).T.astype(o_ref.dtype)


def mla_attention(qn, qr, kn, kr, vt, batch, seq, tq=4 * MLA_TILE, tk=MLA_TILE):
    assert tq % (2 * tk) == 0
    nq = seq // tq
    return pl.pallas_call(
        functools.partial(_mla_kernel, tq=tq, tk=tk),
        grid=(batch, MLA_HEADS, nq),
        in_specs=[pl.BlockSpec((tq, MLA_NOPE), lambda b, h, i: (b * nq + i, h)),
                  pl.BlockSpec((tq, LANES), lambda b, h, i: (b * nq + i, h // 2)),
                  pl.BlockSpec((seq, MLA_NOPE), lambda b, h, i: (b, h)),
                  pl.BlockSpec((seq, LANES), lambda b, h, i: (b, h % 2)),
                  pl.BlockSpec((1, 1, seq // tk, VT_ROWS, tk), lambda b, h, i: (b, h, 0, 0, 0))],
        out_specs=pl.BlockSpec((tq, MLA_V), lambda b, h, i: (b * nq + i, h)),
        out_shape=jax.ShapeDtypeStruct((batch * seq, MLA_WIDTH), BF16),
        scratch_shapes=[pltpu.VMEM((tq // MXU_N, tk, MXU_N), F32), pltpu.VMEM((tq // MXU_N, tk, MXU_N), F32),
                        pltpu.VMEM((1, tq), F32), pltpu.VMEM((1, tq), F32),
                        pltpu.VMEM((1, tq), F32), pltpu.VMEM((VT_ROWS, tq), F32)],
        compiler_params=_cparams(("parallel", "parallel", "parallel")),
        name="mla_attention",
    )(qn, qr, kn, kr, vt)


def _swa_kernel(sinks_ref, q0_ref, q1_ref, q2_ref, q3_ref, kc_ref, kp_ref, vc_ref, vp_ref,
                pq_ref, pkc_ref, pkp_ref, o_ref):
    blk = pl.program_id(1)
    q_refs = (q0_ref, q1_ref, q2_ref, q3_ref)
    k_all = jnp.concatenate([kp_ref[...], kc_ref[...]], axis=0).astype(F32)
    v_all = jnp.concatenate([vp_ref[...], vc_ref[...]], axis=0).astype(F32)
    pk = jnp.concatenate([pkp_ref[0], pkc_ref[0]], axis=-1).astype(F32)
    dist = jnp.abs(pq_ref[...].astype(F32) - pk)
    row = lax.broadcasted_iota(jnp.int32, dist.shape, 0)
    col = lax.broadcasted_iota(jnp.int32, dist.shape, 1)
    rel = BLOCK + row - col
    first_key = jnp.where(blk > 0, 0, BLOCK)
    valid = (rel >= 0) & (rel < WINDOW) & (col >= first_key)
    dist_w = jnp.where(valid, dist, jnp.inf)
    lane =lax.broadcasted_iota(jnp.int32, (2 * BLOCK, LANES), 1)
    lo = lane < SWA_HEAD_DIM
    lane_o = lax.broadcasted_iota(jnp.int32, (BLOCK, LANES), 1)
    lo_o = lane_o < SWA_HEAD_DIM

    def dup(x_all, g):
        pair = x_all[:, (g // 2) * LANES:(g // 2 + 1) * LANES]
        rolled = pltpu.roll(pair, SWA_HEAD_DIM, 1)
        return jnp.where(lo, pair, rolled) if g % 2 == 0 else jnp.where(lo, rolled, pair)

    for g in range(SWA_KV_HEADS):
        kk = dup(k_all, g)
        vv = dup(v_all, g).astype(BF16)
        zero = jnp.zeros_like(kk)
        k_half = (jnp.where(lo, kk, zero).astype(BF16), jnp.where(lo, zero, kk).astype(BF16))
        for i in range(SWA_GROUP // 2):
            q_pair = q_refs[g][:, i * LANES:(i + 1) * LANES]
            outs = []
            for j in range(2):
                head = g * SWA_GROUP + 2 * i + j
                slope2 = LOG2E * 2.0 ** (-8.0 * (head + 1) / SWA_HEADS)
                s = lax.dot_general(q_pair, k_half[j], (((1,), (1,)), ((), ())),
                                    preferred_element_type=F32)
                s = s - slope2 * dist_w
                sink2 = LOG2E * sinks_ref[head]
                m = jnp.maximum(jnp.max(s, axis=-1, keepdims=True), sink2)
                e = jnp.exp2(s - m)
                denom = jnp.sum(e, axis=-1, keepdims=True) + jnp.exp2(sink2 - m)
                pv = jnp.dot(e.astype(BF16), vv, preferred_element_type=F32)
                outs.append(pv / denom)
            col0 = (g * SWA_GROUP // 2 + i) * LANES
            o_ref[:, col0:col0 + LANES] = jnp.where(lo_o, outs[0], outs[1]).astype(o_ref.dtype)


def swa_attention(proj, sinks, pos_col, pos_row, batch, seq):
    nblk = seq // BLOCK
    kvw = SWA_KV_HEADS * SWA_HEAD_DIM
    gw = SWA_GROUP * SWA_HEAD_DIM
    cur = lambda b, n: b * nblk + n
    prev = lambda b, n: b * nblk + jnp.maximum(n - 1, 0)
    q_specs = [pl.BlockSpec((BLOCK, gw), functools.partial(lambda b, n, g: (cur(b, n), COL_QSWA // gw + g), g=g))
               for g in range(SWA_KV_HEADS)]
    return pl.pallas_call(
        _swa_kernel,
        grid=(batch, nblk),
        in_specs=[pl.BlockSpec(memory_space=pltpu.SMEM)] + q_specs + [
            pl.BlockSpec((BLOCK, kvw), lambda b, n: (cur(b, n), COL_KSWA // kvw)),
            pl.BlockSpec((BLOCK, kvw), lambda b, n: (prev(b, n), COL_KSWA // kvw)),
            pl.BlockSpec((BLOCK, kvw), lambda b, n: (cur(b, n), COL_VSWA // kvw)),
            pl.BlockSpec((BLOCK, kvw), lambda b, n: (prev(b, n), COL_VSWA // kvw)),
            pl.BlockSpec((BLOCK, 1), lambda b, n: (cur(b, n), 0)),
            pl.BlockSpec((1, 1, BLOCK), lambda b, n: (cur(b, n), 0, 0)),
            pl.BlockSpec((1, 1, BLOCK), lambda b, n: (prev(b, n), 0, 0))],
        out_specs=pl.BlockSpec((BLOCK, SWA_WIDTH), lambda b, n: (cur(b, n), 0)),
        out_shape=jax.ShapeDtypeStruct((batch * seq, SWA_WIDTH), BF16),
        compiler_params=_cparams(("parallel", "parallel")),
        name="swa_attention",
    )(sinks, proj, proj, proj, proj, proj, proj, proj, proj, pos_col, pos_row, pos_row)


def _wo_kernel(oa_ref, ob_ref, ga_ref, gb_ref, w_ref, y_ref, *, chunks):
    ka = oa_ref.shape[1]
    na = _rms(oa_ref[...].astype(F32), ga_ref[...]).astype(BF16)
    nb = _rms(ob_ref[...].astype(F32), gb_ref[...]).astype(BF16)
    for lo, hi in chunks:
        y_ref[:, lo:hi] = (jnp.dot(na, w_ref[:ka, lo:hi], preferred_element_type=F32)
                           + jnp.dot(nb, w_ref[ka:, lo:hi], preferred_element_type=F32)).astype(y_ref.dtype)


def out_proj(oa, ob, ga, gb, w, tm=ROW_TILE):
    m, ka = oa.shape
    kb = ob.shape[1]
    n = w.shape[1]
    return pl.pallas_call(
        functools.partial(_wo_kernel, chunks=_col_chunks(n, 4 * MXU_N)),
        grid=(m // tm,),
        in_specs=[pl.BlockSpec((tm, ka), lambda i: (i, 0)),
                  pl.BlockSpec((tm, kb), lambda i: (i, 0)),
                  pl.BlockSpec((1, ka), lambda i: (0, 0)),
                  pl.BlockSpec((1, kb), lambda i: (0, 0)),
                  pl.BlockSpec((ka + kb, n), lambda i: (0, 0), pipeline_mode=pl.Buffered(1))],
        out_specs=pl.BlockSpec((tm, n), lambda i: (i, 0)),
        out_shape=jax.ShapeDtypeStruct((m, n), BF16),
        compiler_params=_cparams(("parallel",)),
        name="out_proj",
    )(oa, ob, ga, gb, w)


def _resid_kernel(y_ref, x_ref, gy_ref, gn_ref, xo_ref, ho_ref):
    xn = x_ref[...] + _rms(y_ref[...].astype(F32), gy_ref[...])
    xo_ref[...] = xn
    ho_ref[...] = _rms(xn, gn_ref[...]).astype(ho_ref.dtype)


def resid_norm(y, x, gy, gn, tm=256):
    m, d = x.shape
    row = pl.BlockSpec((tm, d), lambda i: (i, 0))
    vec = pl.BlockSpec((1, d), lambda i: (0, 0))
    return pl.pallas_call(
        _resid_kernel,
        grid=(m // tm,),
        in_specs=[row, row, vec, vec],
        out_specs=[row, row],
        out_shape=[jax.ShapeDtypeStruct((m, d), F32), jax.ShapeDtypeStruct((m, d), BF16)],
        compiler_params=_cparams(("parallel",)),
        name="resid_norm",
    )(y, x, gy, gn)


def _ffn_kernel(h_ref, wg_ref, wu_ref, wd_ref, x_ref, gp_ref, o_ref, *, dn, rows):
    j = pl.program_id(1)
    tm, d = o_ref.shape

    @pl.when(j == 0)
    def _():
        o_ref[...] = jnp.zeros(o_ref.shape, o_ref.dtype)

    h = h_ref[...]
    g = jnp.dot(h, wg_ref[...], preferred_element_type=F32)
    u = jnp.dot(h, wu_ref[...], preferred_element_type=F32)
    a = (g * jax.nn.sigmoid(g) * u).astype(BF16)
    for c in range(d // dn):
        o_ref[:, c * dn:(c + 1) * dn] += jnp.dot(a, wd_ref[:, c * dn:(c + 1) * dn],
                                                 preferred_element_type=F32)

    @pl.when(j == pl.num_programs(1) - 1)
    def _():
        def body(r, carry):
            rs = pl.ds(pl.multiple_of(r * rows, rows), rows)
            o_ref[rs, :] = x_ref[rs, :] + _rms(o_ref[rs, :], gp_ref[...])
            return carry

        lax.fori_loop(0, tm // rows, body, 0)


def ffn(h, wg, wu, wd, x, g_post, tm=1024, tf=FFN_TILE):
    m, d = h.shape
    f = wg.shape[1]
    once = dict(pipeline_mode=pl.Buffered(1))
    return pl.pallas_call(
        functools.partial(_ffn_kernel, dn=1024, rows=32),
        grid=(m // tm, f // tf),
        in_specs=[pl.BlockSpec((tm, d), lambda i, j: (i, 0), **once),
                  pl.BlockSpec((d, tf), lambda i, j: (0, j)),
                  pl.BlockSpec((d, tf), lambda i, j: (0, j)),
                  pl.BlockSpec((tf, d), lambda i, j: (j, 0)),
                  pl.BlockSpec((tm, d), lambda i, j: (i, 0), **once),
                  pl.BlockSpec((1, d), lambda i, j: (0, 0))],
        out_specs=pl.BlockSpec((tm, d), lambda i, j: (i, 0), **once),
        out_shape=jax.ShapeDtypeStruct((m, d), F32),
        compiler_params=_cparams(("parallel", "arbitrary")),
        name="ffn",
    )(h, wg, wu, wd, x, g_post)


def _ple_kernel(x_ref, wg_ref, p_ref, wp_ref, o_ref, *, chunks):
    xb = x_ref[...].astype(BF16)
    pb = p_ref[...].astype(BF16)
    for lo, hi in chunks:
        gate = jax.nn.sigmoid(jnp.dot(xb, wg_ref[:, lo:hi], preferred_element_type=F32))
        e = jnp.dot(pb, wp_ref[:, lo:hi], preferred_element_type=F32)
        o_ref[:, lo:hi] = x_ref[:, lo:hi] + gate * e


def ple(x, wg, p, wp, tm=ROW_TILE):
    m, d = x.shape
    n = wg.shape[1]
    pd = p.shape[1]
    once = dict(pipeline_mode=pl.Buffered(1))
    return pl.pallas_call(
        functools.partial(_ple_kernel, chunks=_col_chunks(n, 4 * MXU_N)),
        grid=(m // tm,),
        in_specs=[pl.BlockSpec((tm, d), lambda i: (i, 0)),
                  pl.BlockSpec((d, n), lambda i: (0, 0), **once),
                  pl.BlockSpec((tm, pd), lambda i: (i, 0)),
                  pl.BlockSpec((pd, n), lambda i: (0, 0), **once)],
        out_specs=pl.BlockSpec((tm, n), lambda i: (i, 0)),
        out_shape=jax.ShapeDtypeStruct((m, n), F32),
        compiler_params=_cparams(("parallel",)),
        name="ple",
    )(x, wg, p, wp)


def _prep_w_in(w):
    c_q, c_kv, k_r, q_s, k_s, v_s = jnp.split(
        w, [Q_LORA, Q_LORA + KV_LORA, Q_LORA + KV_LORA + MLA_ROPE,
            Q_LORA + KV_LORA + MLA_ROPE + SWA_WIDTH,
            Q_LORA + KV_LORA + MLA_ROPE + SWA_WIDTH + SWA_KV_HEADS * SWA_HEAD_DIM], axis=1)
    q_s = q_s * (LOG2E / math.sqrt(SWA_HEAD_DIM))
    parts = [c_q, c_kv, q_s, k_s, v_s, k_r, _swap_half(k_r)]
    return jnp.concatenate([part.astype(BF16) for part in parts], axis=1)


def _prep_w_q_up(w):
    w = (w * (LOG2E / math.sqrt(MLA_NOPE + MLA_ROPE))).reshape(Q_LORA, MLA_HEADS, MLA_NOPE + MLA_ROPE)
    nope = w[:, :, :MLA_NOPE].reshape(Q_LORA, MLA_WIDTH)
    rope = w[:, :, MLA_NOPE:]
    n_rope = MLA_HEADS * MLA_ROPE
    return jnp.concatenate([nope, rope.reshape(Q_LORA, n_rope),
                            _swap_half(rope).reshape(Q_LORA, n_rope)], axis=1).astype(BF16)


def _prep_w_kv_up(w):
    w = w.reshape(KV_LORA, MLA_HEADS, MLA_NOPE + MLA_V)
    wk = w[:, :, :MLA_NOPE].reshape(KV_LORA, MLA_WIDTH).astype(BF16)
    wvt = w[:, :, MLA_NOPE:].reshape(KV_LORA, MLA_WIDTH).T.astype(BF16)
    return wk, wvt


def _layer(x, p, pos_col, pos_row, invf, batch, seq, attn_pre_norm, w_in, q_a_norm, w_q_up,
           kv_a_norm, w_kv_up, sinks, mla_out_norm, swa_out_norm, w_o, attn_post_norm,
           ffn_pre_norm, w_gate, w_up, w_down, ffn_post_norm, w_ple_gate, w_ple_proj):
    row = lambda g: g.reshape(1, -1)
    proj = in_proj(x, row(attn_pre_norm), _prep_w_in(w_in))
    qn, qr = q_up(proj, row(q_a_norm), _prep_w_q_up(w_q_up), pos_col, invf)
    wk, wvt = _prep_w_kv_up(w_kv_up)
    kn, vt, kr = kv_up(proj, row(kv_a_norm), wk, wvt, pos_col, invf, batch, seq)
    o_mla = mla_attention(qn, qr, kn, kr, vt, batch, seq)
    o_swa = swa_attention(proj, sinks, pos_col, pos_row, batch, seq)
    y = out_proj(o_mla, o_swa, row(mla_out_norm), row(swa_out_norm), w_o.astype(BF16))
    x1, h2 = resid_norm(y, x, row(attn_post_norm), row(ffn_pre_norm))
    x2 = ffn(h2, w_gate.astype(BF16), w_up.astype(BF16), w_down.astype(BF16), x1, row(ffn_post_norm))
    return ple(x2, w_ple_gate.astype(BF16), p, w_ple_proj.astype(BF16))


def kernel(x, p, positions, attn_pre_norm, w_in, q_a_norm, w_q_up, kv_a_norm, w_kv_up, sinks,
           mla_out_norm, swa_out_norm, w_o, attn_post_norm, ffn_pre_norm, w_gate, w_up, w_down,
           ffn_post_norm, w_ple_gate, w_ple_proj):
    batch, seq, d = x.shape
    depth = w_in.shape[0]
    t = batch * seq
    pos_col = positions.reshape(t, 1)
    pos_row = positions.reshape(t // BLOCK, 1, BLOCK)
    half = MLA_ROPE // 2
    invf = np.asarray(ROPE_THETA ** (-(np.arange(LANES) % half) * 2.0 / MLA_ROPE), np.float32).reshape(1, LANES)
    invf = jnp.asarray(invf)
    xf = x.reshape(t, d)
    for i in range(depth):
        xf = _layer(xf, p[i].reshape(t, PLE_DIM), pos_col, pos_row, invf, batch, seq,
                    attn_pre_norm[i], w_in[i], q_a_norm[i], w_q_up[i], kv_a_norm[i], w_kv_up[i],
                    sinks[i], mla_out_norm[i], swa_out_norm[i], w_o[i], attn_post_norm[i],
                    ffn_pre_norm[i], w_gate[i], w_up[i], w_down[i], ffn_post_norm[i],
                    w_ple_gate[i], w_ple_proj[i])
    return xf.reshape(batch, seq, d)
```

```python
import functools
import math

import numpy as np
import jax
import jax.numpy as jnp
from jax import lax
from jax.experimental import pallas as pl
from jax.experimental.pallas import tpu as pltpu

D_MODEL = 4096
PLE_DIM = 256
MLA_HEADS = 16
MLA_NOPE = 128
MLA_ROPE = 64
MLA_V = 128
Q_LORA = 1024
KV_LORA = 512
MLA_WIDTH = MLA_HEADS * MLA_V
SWA_HEADS = 32
SWA_KV_HEADS = 4
SWA_HEAD_DIM = 64
SWA_GROUP = SWA_HEADS // SWA_KV_HEADS
SWA_WIDTH = SWA_HEADS * SWA_HEAD_DIM
WINDOW = 128
BLOCK = 128
ROPE_THETA = 10000.0
NORM_EPS = 1e-6
LOG2E = math.log2(math.e)
D_FF = 11008

LANES = 128
MXU_N = 256
MLA_TILE = 512
ROW_TILE = 256
FFN_TILE = 256
VT_ROWS = MLA_V + 16
VMEM_LIMIT = 60 * 1024 * 1024

COL_CQ = 0
COL_CKV = COL_CQ + Q_LORA
COL_QSWA = COL_CKV + KV_LORA
COL_KSWA = COL_QSWA + SWA_WIDTH
COL_VSWA = COL_KSWA + SWA_KV_HEADS * SWA_HEAD_DIM
COL_KR = COL_VSWA + SWA_KV_HEADS * SWA_HEAD_DIM
D_IN2 = COL_KR + 2 * MLA_ROPE

F32 = jnp.float32
BF16 = jnp.bfloat16


def _cparams(sem):
    return pltpu.CompilerParams(dimension_semantics=sem, vmem_limit_bytes=VMEM_LIMIT)


def _rms(xf, g):
    ms = jnp.mean(xf * xf, axis=-1, keepdims=True)
    return xf * lax.rsqrt(ms + NORM_EPS) * g


def _swap_half(w):
    d = w.shape[-1]
    return jnp.concatenate([-w[..., d // 2:], w[..., : d // 2]], axis=-1)


def _rope_tables(pos_col, invf_row):
    ang = pos_col.astype(F32) * invf_row
    return jnp.cos(ang), jnp.sin(ang)


def _col_chunks(n, width):
    return [(c, min(c + width, n)) for c in range(0, n, width)]


def _in_proj_kernel(x_ref, g_ref, w_ref, o_ref, *, chunks):
    h = _rms(x_ref[...], g_ref[...]).astype(BF16)
    for lo, hi in chunks:
        o_ref[:, lo:hi] = jnp.dot(h, w_ref[:, lo:hi], preferred_element_type=F32).astype(o_ref.dtype)


def in_proj(x, g, w, tm=ROW_TILE):
    m, k = x.shape
    n = w.shape[1]
    return pl.pallas_call(
        functools.partial(_in_proj_kernel, chunks=_col_chunks(n, 6 * MXU_N)),
        grid=(m // tm,),
        in_specs=[pl.BlockSpec((tm, k), lambda i: (i, 0)),
                  pl.BlockSpec((1, k), lambda i: (0, 0)),
                  pl.BlockSpec((k, n), lambda i: (0, 0), pipeline_mode=pl.Buffered(1))],
        out_specs=pl.BlockSpec((tm, n), lambda i: (i, 0)),
        out_shape=jax.ShapeDtypeStruct((m, n), BF16),
        compiler_params=_cparams(("parallel",)),
        name="in_proj",
    )(x, g, w)


def _q_up_kernel(cq_ref, g_ref, w_ref, pos_ref, invf_ref, qn_ref, qr_ref):
    h = _rms(cq_ref[...].astype(F32), g_ref[...]).astype(BF16)
    r = jnp.dot(h, w_ref[...], preferred_element_type=F32)
    qn_ref[...] = r[:, :MLA_WIDTH].astype(BF16)
    cos, sin = _rope_tables(pos_ref[...], invf_ref[...])
    n_rope = MLA_HEADS * MLA_ROPE
    for c in range(n_rope // LANES):
        a = r[:, MLA_WIDTH + c * LANES: MLA_WIDTH + (c + 1) * LANES]
        b = r[:, MLA_WIDTH + n_rope + c * LANES: MLA_WIDTH + n_rope + (c + 1) * LANES]
        qr_ref[:, c * LANES:(c + 1) * LANES] = (a * cos + b * sin).astype(BF16)


def q_up(proj, g, w, pos_col, invf, tm=512):
    m = proj.shape[0]
    n_rope = MLA_HEADS * MLA_ROPE
    return pl.pallas_call(
        _q_up_kernel,
        grid=(m // tm,),
        in_specs=[pl.BlockSpec((tm, Q_LORA), lambda i: (i, COL_CQ // Q_LORA)),
                  pl.BlockSpec((1, Q_LORA), lambda i: (0, 0)),
                  pl.BlockSpec(w.shape, lambda i: (0, 0)),
                  pl.BlockSpec((tm, 1), lambda i: (i, 0)),
                  pl.BlockSpec((1, LANES), lambda i: (0, 0))],
        out_specs=[pl.BlockSpec((tm, MLA_WIDTH), lambda i: (i, 0)),
                   pl.BlockSpec((tm, n_rope), lambda i: (i, 0))],
        out_shape=[jax.ShapeDtypeStruct((m, MLA_WIDTH), BF16),
                   jax.ShapeDtypeStruct((m, n_rope), BF16)],
        compiler_params=_cparams(("parallel",)),
        name="q_up",
    )(proj, g, w, pos_col, invf)


def _kv_up_kernel(ckv_ref, g_ref, wk_ref, wvt_ref, krab_ref, pos_ref, invf_ref, kn_ref, vt_ref, kr_ref):
    h = _rms(ckv_ref[...].astype(F32), g_ref[...]).astype(BF16)
    kn_ref[...] = jnp.dot(h, wk_ref[...], preferred_element_type=F32).astype(BF16)
    vt = lax.dot_general(wvt_ref[...], h, (((1,), (1,)), ((), ())), preferred_element_type=F32)
    ones = jnp.ones((VT_ROWS - MLA_V, vt.shape[1]), BF16)
    for hh in range(MLA_HEADS):
        vt_ref[0, hh, 0, :MLA_V, :] = vt[hh * MLA_V:(hh + 1) * MLA_V, :].astype(BF16)
        vt_ref[0, hh, 0, MLA_V:, :] = ones
    cos, sin = _rope_tables(pos_ref[...], invf_ref[...])
    lane = lax.broadcasted_iota(jnp.int32, cos.shape, 1)
    lo = lane < MLA_ROPE
    prod = krab_ref[...].astype(F32) * jnp.where(lo, cos, sin)
    kk = prod + pltpu.roll(prod, MLA_ROPE, 1)
    zero = jnp.zeros_like(kk)
    kr_ref[:, :LANES] = jnp.where(lo, kk, zero).astype(BF16)
    kr_ref[:, LANES:] = jnp.where(lo, zero, kk).astype(BF16)


def kv_up(proj, g, wk, wvt, pos_col, invf, batch, seq, tm=MLA_TILE):
    m = proj.shape[0]
    nk = seq // tm
    return pl.pallas_call(
        _kv_up_kernel,
        grid=(m // tm,),
        in_specs=[pl.BlockSpec((tm, KV_LORA), lambda i: (i, COL_CKV // KV_LORA)),
                  pl.BlockSpec((1, KV_LORA), lambda i: (0, 0)),
                  pl.BlockSpec(wk.shape, lambda i: (0, 0)),
                  pl.BlockSpec(wvt.shape, lambda i: (0, 0)),
                  pl.BlockSpec((tm, LANES), lambda i: (i, COL_KR // LANES)),
                  pl.BlockSpec((tm, 1), lambda i: (i, 0)),
                  pl.BlockSpec((1, LANES), lambda i: (0, 0))],
        out_specs=[pl.BlockSpec((tm, MLA_WIDTH), lambda i: (i, 0)),
                   pl.BlockSpec((1, MLA_HEADS, 1, VT_ROWS, tm), lambda i: (i // nk, 0, i % nk, 0, 0)),
                   pl.BlockSpec((tm, 2 * LANES), lambda i: (i, 0))],
        out_shape=[jax.ShapeDtypeStruct((m, MLA_WIDTH), BF16),
                   jax.ShapeDtypeStruct((batch, MLA_HEADS, nk, VT_ROWS, tm), BF16),
                   jax.ShapeDtypeStruct((m, 2 * LANES), BF16)],
        compiler_params=_cparams(("parallel",)),
        name="kv_up",
    )(proj, g, wk, wvt, proj, pos_col, invf)


def _mla_kernel(qn_ref, qr_ref, kn_ref, kr_ref, vt_ref, o_ref, s_a, s_b, mx_a, mx_b, m_sc, acc_sc, *, tq, tk):
    qi = pl.program_id(2)
    q = jnp.concatenate([qn_ref[...], qr_ref[...]], axis=-1)
    m_sc[...] = jnp.full(m_sc.shape, -jnp.inf, F32)
    acc_sc[...] = jnp.zeros(acc_sc.shape, F32)

    n_chunk = tq // MXU_N
    blk_chunks = tk // MXU_N
    n_diag = tq // tk
    bufs = ((s_a, mx_a), (s_b, mx_b))

    def live_chunks(diag):
        return range(0 if diag is None else diag * blk_chunks, n_chunk)

    def scores(kb, buf, diag=None):
        s_ref, mx_ref = buf
        off = pl.multiple_of(kb * tk, tk)
        k = jnp.concatenate([kn_ref[pl.ds(off, tk), :], kr_ref[pl.ds(off, tk), :]], axis=-1)
        for c in live_chunks(diag):
            s = lax.dot_general(k, q[c * MXU_N:(c + 1) * MXU_N, :], (((1,), (1,)), ((), ())),
                                preferred_element_type=F32)
            if diag is not None and c < (diag + 1) * blk_chunks:
                key = lax.broadcasted_iota(jnp.int32, s.shape, 0) + diag * tk
                qry = lax.broadcasted_iota(jnp.int32, s.shape, 1) + c * MXU_N
                s = jnp.where(qry >= key, s, -jnp.inf)
            s_ref[c] = s
            mx_ref[:, c * MXU_N:(c + 1) * MXU_N] = jnp.max(s, axis=0, keepdims=True)

    def softmax_pv(kb, buf, diag=None, mask_here=False):
        s_ref, mx_ref = buf
        vt = vt_ref[0, 0, kb]
        for c in live_chunks(diag):
            sl = slice(c * MXU_N, (c + 1) * MXU_N)
            remask = mask_here and c < blk_chunks

            def s_chunk():
                s = s_ref[c]
                if remask:
                    key = lax.broadcasted_iota(jnp.int32, s.shape, 0)
                    qry = lax.broadcasted_iota(jnp.int32, s.shape, 1) + c * MXU_N
                    s = jnp.where(qry >= key, s, -jnp.inf)
                return s

            mx = jnp.max(s_chunk(), axis=0, keepdims=True) if remask else mx_ref[:, sl]
            m_old = m_sc[:, sl]
            m_new = jnp.maximum(m_old, mx)
            alpha = jnp.exp2(m_old - m_new)
            p = jnp.exp2(s_chunk() - m_new).astype(BF16)
            acc_sc[:, sl] = alpha * acc_sc[:, sl] + jnp.dot(vt, p, preferred_element_type=F32)
            m_sc[:, sl] = m_new

    n_full = n_diag * qi
    scores(0, bufs[0])

    def body(j, carry):
        scores(2 * j + 1, bufs[1])
        softmax_pv(2 * j, bufs[0])
        scores(2 * j + 2, bufs[0])
        softmax_pv(2 * j + 1, bufs[1])
        return carry

    lax.fori_loop(0, n_full // 2, body, 0)

    for d in range(n_diag):
        if d + 1 < n_diag:
            scores(n_full + d + 1, bufs[(d + 1) % 2], diag=d + 1)
        softmax_pv(n_full + d, bufs[d % 2], diag=d, mask_here=(d == 0))

    o_ref[...] = (acc_sc[:MLA_V, :] / acc_sc[MLA_V:MLA_V + 1, :]).T.astype(o_ref.dtype)


def mla_attention(qn, qr, kn, kr, vt, batch, seq, tq=4 * MLA_TILE, tk=MLA_TILE):
    assert tq % (2 * tk) == 0
    nq = seq // tq
    return pl.pallas_call(
        functools.partial(_mla_kernel, tq=tq, tk=tk),
        grid=(batch, MLA_HEADS, nq),
        in_specs=[pl.BlockSpec((tq, MLA_NOPE), lambda b, h, i: (b * nq + i, h)),
                  pl.BlockSpec((tq, LANES), lambda b, h, i: (b * nq + i, h // 2)),
                  pl.BlockSpec((seq, MLA_NOPE), lambda b, h, i: (b, h)),
                  pl.BlockSpec((seq, LANES), lambda b, h, i: (b, h % 2)),
                  pl.BlockSpec((1, 1, seq // tk, VT_ROWS, tk), lambda b, h, i: (b, h, 0, 0, 0))],
        out_specs=pl.BlockSpec((tq, MLA_V), lambda b, h, i: (b * nq + i, h)),
        out_shape=jax.ShapeDtypeStruct((batch * seq, MLA_WIDTH), BF16),
        scratch_shapes=[pltpu.VMEM((tq // MXU_N, tk, MXU_N), F32), pltpu.VMEM((tq // MXU_N, tk, MXU_N), F32),
                        pltpu.VMEM((1, tq), F32), pltpu.VMEM((1, tq), F32),
                        pltpu.VMEM((1, tq), F32), pltpu.VMEM((VT_ROWS, tq), F32)],
        compiler_params=_cparams(("parallel", "parallel", "parallel")),
        name="mla_attention",
    )(qn, qr, kn, kr, vt)


def _swa_kernel(sinks_ref, q0_ref, q1_ref, q2_ref, q3_ref, kc_ref, kp_ref, vc_ref, vp_ref,
                pq_ref, pkc_ref, pkp_ref, o_ref):
    blk = pl.program_id(1)
    q_refs = (q0_ref, q1_ref, q2_ref, q3_ref)
    k_all = jnp.concatenate([kp_ref[...], kc_ref[...]], axis=0).astype(F32)
    v_all = jnp.concatenate([vp_ref[...], vc_ref[...]], axis=0).astype(F32)
    pk = jnp.concatenate([pkp_ref[0], pkc_ref[0]], axis=-1).astype(F32)
    dist = jnp.abs(pq_ref[...].astype(F32) - pk)
    row = lax.broadcasted_iota(jnp.int32, dist.shape, 0)
    col = lax.broadcasted_iota(jnp.int32, dist.shape, 1)
    rel = BLOCK + row - col
    first_key = jnp.where(blk > 0, 0, BLOCK)
    valid = (rel >= 0) & (rel < WINDOW) & (col >= first_key)
    dist_w = jnp.where(valid, dist, jnp.inf)
    lane =lax.broadcasted_iota(jnp.int32, (2 * BLOCK, LANES), 1)
    lo = lane < SWA_HEAD_DIM
    lane_o = lax.broadcasted_iota(jnp.int32, (BLOCK, LANES), 1)
    lo_o = lane_o < SWA_HEAD_DIM

    def dup(x_all, g):
        pair = x_all[:, (g // 2) * LANES:(g // 2 + 1) * LANES]
        rolled = pltpu.roll(pair, SWA_HEAD_DIM, 1)
        return jnp.where(lo, pair, rolled) if g % 2 == 0 else jnp.where(lo, rolled, pair)

    for g in range(SWA_KV_HEADS):
        kk = dup(k_all, g)
        vv = dup(v_all, g).astype(BF16)
        zero = jnp.zeros_like(kk)
        k_half = (jnp.where(lo, kk, zero).astype(BF16), jnp.where(lo, zero, kk).astype(BF16))
        for i in range(SWA_GROUP // 2):
            q_pair = q_refs[g][:, i * LANES:(i + 1) * LANES]
            outs = []
            for j in range(2):
                head = g * SWA_GROUP + 2 * i + j
                slope2 = LOG2E * 2.0 ** (-8.0 * (head + 1) / SWA_HEADS)
                s = lax.dot_general(q_pair, k_half[j], (((1,), (1,)), ((), ())),
                                    preferred_element_type=F32)
                s = s - slope2 * dist_w
                sink2 = LOG2E * sinks_ref[head]
                m = jnp.maximum(jnp.max(s, axis=-1, keepdims=True), sink2)
                e = jnp.exp2(s - m)
                denom = jnp.sum(e, axis=-1, keepdims=True) + jnp.exp2(sink2 - m)
                pv = jnp.dot(e.astype(BF16), vv, preferred_element_type=F32)
                outs.append(pv / denom)
            col0 = (g * SWA_GROUP // 2 + i) * LANES
            o_ref[:, col0:col0 + LANES] = jnp.where(lo_o, outs[0], outs[1]).astype(o_ref.dtype)


def swa_attention(proj, sinks, pos_col, pos_row, batch, seq):
    nblk = seq // BLOCK
    kvw = SWA_KV_HEADS * SWA_HEAD_DIM
    gw = SWA_GROUP * SWA_HEAD_DIM
    cur = lambda b, n: b * nblk + n
    prev = lambda b, n: b * nblk + jnp.maximum(n - 1, 0)
    q_specs = [pl.BlockSpec((BLOCK, gw), functools.partial(lambda b, n, g: (cur(b, n), COL_QSWA // gw + g), g=g))
               for g in range(SWA_KV_HEADS)]
    return pl.pallas_call(
        _swa_kernel,
        grid=(batch, nblk),
        in_specs=[pl.BlockSpec(memory_space=pltpu.SMEM)] + q_specs + [
            pl.BlockSpec((BLOCK, kvw), lambda b, n: (cur(b, n), COL_KSWA // kvw)),
            pl.BlockSpec((BLOCK, kvw), lambda b, n: (prev(b, n), COL_KSWA // kvw)),
            pl.BlockSpec((BLOCK, kvw), lambda b, n: (cur(b, n), COL_VSWA // kvw)),
            pl.BlockSpec((BLOCK, kvw), lambda b, n: (prev(b, n), COL_VSWA // kvw)),
            pl.BlockSpec((BLOCK, 1), lambda b, n: (cur(b, n), 0)),
            pl.BlockSpec((1, 1, BLOCK), lambda b, n: (cur(b, n), 0, 0)),
            pl.BlockSpec((1, 1, BLOCK), lambda b, n: (prev(b, n), 0, 0))],
        out_specs=pl.BlockSpec((BLOCK, SWA_WIDTH), lambda b, n: (cur(b, n), 0)),
        out_shape=jax.ShapeDtypeStruct((batch * seq, SWA_WIDTH), BF16),
        compiler_params=_cparams(("parallel", "parallel")),
        name="swa_attention",
    )(sinks, proj, proj, proj, proj, proj, proj, proj, proj, pos_col, pos_row, pos_row)


def _wo_kernel(oa_ref, ob_ref, ga_ref, gb_ref, w_ref, y_ref, *, chunks):
    ka = oa_ref.shape[1]
    na = _rms(oa_ref[...].astype(F32), ga_ref[...]).astype(BF16)
    nb = _rms(ob_ref[...].astype(F32), gb_ref[...]).astype(BF16)
    for lo, hi in chunks:
        y_ref[:, lo:hi] = (jnp.dot(na, w_ref[:ka, lo:hi], preferred_element_type=F32)
                           + jnp.dot(nb, w_ref[ka:, lo:hi], preferred_element_type=F32)).astype(y_ref.dtype)


def out_proj(oa, ob, ga, gb, w, tm=ROW_TILE):
    m, ka = oa.shape
    kb = ob.shape[1]
    n = w.shape[1]
    return pl.pallas_call(
        functools.partial(_wo_kernel, chunks=_col_chunks(n, 4 * MXU_N)),
        grid=(m // tm,),
        in_specs=[pl.BlockSpec((tm, ka), lambda i: (i, 0)),
                  pl.BlockSpec((tm, kb), lambda i: (i, 0)),
                  pl.BlockSpec((1, ka), lambda i: (0, 0)),
                  pl.BlockSpec((1, kb), lambda i: (0, 0)),
                  pl.BlockSpec((ka + kb, n), lambda i: (0, 0), pipeline_mode=pl.Buffered(1))],
        out_specs=pl.BlockSpec((tm, n), lambda i: (i, 0)),
        out_shape=jax.ShapeDtypeStruct((m, n), BF16),
        compiler_params=_cparams(("parallel",)),
        name="out_proj",
    )(oa, ob, ga, gb, w)


def _resid_kernel(y_ref, x_ref, gy_ref, gn_ref, xo_ref, ho_ref):
    xn = x_ref[...] + _rms(y_ref[...].astype(F32), gy_ref[...])
    xo_ref[...] = xn
    ho_ref[...] = _rms(xn, gn_ref[...]).astype(ho_ref.dtype)


def resid_norm(y, x, gy, gn, tm=256):
    m, d = x.shape
    row = pl.BlockSpec((tm, d), lambda i: (i, 0))
    vec = pl.BlockSpec((1, d), lambda i: (0, 0))
    return pl.pallas_call(
        _resid_kernel,
        grid=(m // tm,),
        in_specs=[row, row, vec, vec],
        out_specs=[row, row],
        out_shape=[jax.ShapeDtypeStruct((m, d), F32), jax.ShapeDtypeStruct((m, d), BF16)],
        compiler_params=_cparams(("parallel",)),
        name="resid_norm",
    )(y, x, gy, gn)


def _ffn_kernel(h_ref, wg_ref, wu_ref, wd_ref, x_ref, gp_ref, o_ref, a_sc, *, dn, rows):
    j = pl.program_id(1)
    last = pl.num_programs(1) - 1
    tm, d = o_ref.shape

    def gate_up():
        h = h_ref[...]
        g = jnp.dot(h, wg_ref[...], preferred_element_type=F32)
        u = jnp.dot(h, wu_ref[...], preferred_element_type=F32)
        a_sc[...] = (g * jax.nn.sigmoid(g) * u).astype(BF16)

    def down(first):
        a = a_sc[...]
        for c in range(d // dn):
            y = jnp.dot(a, wd_ref[:, c * dn:(c + 1) * dn], preferred_element_type=F32)
            if first:
                o_ref[:, c * dn:(c + 1) * dn] = y
            else:
                o_ref[:, c * dn:(c + 1) * dn] += y

    @pl.when(j == 0)
    def _():
        gate_up()

    @pl.when(j == 1)
    def _():
        down(True)
        gate_up()

    @pl.when((j > 1) & (j < last))
    def _():
        down(False)
        gate_up()

    @pl.when(j == last)
    def _():
        down(False)

        def body(r, carry):
            rs = pl.ds(pl.multiple_of(r * rows, rows), rows)
            o_ref[rs, :] = x_ref[rs, :] + _rms(o_ref[rs, :], gp_ref[...])
            return carry

        lax.fori_loop(0, tm // rows, body, 0)


def ffn(h, wg, wu, wd, x, g_post, tm=1024, tf=FFN_TILE):
    m, d = h.shape
    f = wg.shape[1]
    nf = f // tf
    once = dict(pipeline_mode=pl.Buffered(1))
    return pl.pallas_call(
        functools.partial(_ffn_kernel, dn=1024, rows=32),
        grid=(m // tm, nf + 1),
        in_specs=[pl.BlockSpec((tm, d), lambda i, j: (i, 0), **once),
                  pl.BlockSpec((d, tf), lambda i, j: (0, jnp.minimum(j, nf - 1))),
                  pl.BlockSpec((d, tf), lambda i, j: (0, jnp.minimum(j, nf - 1))),
                  pl.BlockSpec((tf, d), lambda i, j: (jnp.maximum(j - 1, 0), 0)),
                  pl.BlockSpec((tm, d), lambda i, j: (i, 0), **once),
                  pl.BlockSpec((1, d), lambda i, j: (0, 0))],
        out_specs=pl.BlockSpec((tm, d), lambda i, j: (i, 0), **once),
        out_shape=jax.ShapeDtypeStruct((m, d), F32),
        scratch_shapes=[pltpu.VMEM((tm, tf), BF16)],
        compiler_params=_cparams(("parallel", "arbitrary")),
        name="ffn",
    )(h, wg, wu, wd, x, g_post)


def _ple_kernel(x_ref, wg_ref, p_ref, wp_ref, o_ref, *, chunks):
    xb = x_ref[...].astype(BF16)
    pb = p_ref[...].astype(BF16)
    for lo, hi in chunks:
        gate = jax.nn.sigmoid(jnp.dot(xb, wg_ref[:, lo:hi], preferred_element_type=F32))
        e = jnp.dot(pb, wp_ref[:, lo:hi], preferred_element_type=F32)
        o_ref[:, lo:hi] = x_ref[:, lo:hi] + gate * e


def ple(x, wg, p, wp, tm=ROW_TILE):
    m, d = x.shape
    n = wg.shape[1]
    pd = p.shape[1]
    once = dict(pipeline_mode=pl.Buffered(1))
    return pl.pallas_call(
        functools.partial(_ple_kernel, chunks=_col_chunks(n, 4 * MXU_N)),
        grid=(m // tm,),
        in_specs=[pl.BlockSpec((tm, d), lambda i: (i, 0)),
                  pl.BlockSpec((d, n), lambda i: (0, 0), **once),
                  pl.BlockSpec((tm, pd), lambda i: (i, 0)),
                  pl.BlockSpec((pd, n), lambda i: (0, 0), **once)],
        out_specs=pl.BlockSpec((tm, n), lambda i: (i, 0)),
        out_shape=jax.ShapeDtypeStruct((m, n), F32),
        compiler_params=_cparams(("parallel",)),
        name="ple",
    )(x, wg, p, wp)


def _prep_w_in(w):
    col = lax.broadcasted_iota(jnp.int32, (1, w.shape[1]), 1)
    q_lo = Q_LORA + KV_LORA + MLA_ROPE
    scale = jnp.where((col >= q_lo) & (col < q_lo + SWA_WIDTH), LOG2E / math.sqrt(SWA_HEAD_DIM), 1.0)
    w = lax.optimization_barrier((w * scale).astype(BF16))
    c_q, c_kv, k_r, q_s, k_s, v_s = jnp.split(
        w, [Q_LORA, Q_LORA + KV_LORA, q_lo, q_lo + SWA_WIDTH,
            q_lo + SWA_WIDTH + SWA_KV_HEADS * SWA_HEAD_DIM], axis=1)
    return jnp.concatenate([c_q, c_kv, q_s, k_s, v_s, k_r, _swap_half(k_r)], axis=1)


def _prep_w_q_up(w):
    w = (w * (LOG2E / math.sqrt(MLA_NOPE + MLA_ROPE))).reshape(Q_LORA, MLA_HEADS, MLA_NOPE + MLA_ROPE)
    nope = w[:, :, :MLA_NOPE].reshape(Q_LORA, MLA_WIDTH)
    rope = w[:, :, MLA_NOPE:]
    n_rope = MLA_HEADS * MLA_ROPE
    return jnp.concatenate([nope, rope.reshape(Q_LORA, n_rope),
                            _swap_half(rope).reshape(Q_LORA, n_rope)], axis=1).astype(BF16)


def _prep_w_kv_up(w):
    w = w.reshape(KV_LORA, MLA_HEADS, MLA_NOPE + MLA_V)
    wk = w[:, :, :MLA_NOPE].reshape(KV_LORA, MLA_WIDTH).astype(BF16)
    wvt = w[:, :, MLA_NOPE:].reshape(KV_LORA, MLA_WIDTH).T.astype(BF16)
    return wk, wvt


def _layer(x, p, pos_col, pos_row, invf, batch, seq, attn_pre_norm, w_in, q_a_norm, w_q_up,
           kv_a_norm, w_kv_up, sinks, mla_out_norm, swa_out_norm, w_o, attn_post_norm,
           ffn_pre_norm, w_gate, w_up, w_down, ffn_post_norm, w_ple_gate, w_ple_proj):
    row = lambda g: g.reshape(1, -1)
    proj = in_proj(x, row(attn_pre_norm), _prep_w_in(w_in))
    qn, qr = q_up(proj, row(q_a_norm), _prep_w_q_up(w_q_up), pos_col, invf)
    wk, wvt = _prep_w_kv_up(w_kv_up)
    kn, vt, kr = kv_up(proj, row(kv_a_norm), wk, wvt, pos_col, invf, batch, seq)
    o_mla = mla_attention(qn, qr, kn, kr, vt, batch, seq)
    o_swa = swa_attention(proj, sinks, pos_col, pos_row, batch, seq)
    y = out_proj(o_mla, o_swa, row(mla_out_norm), row(swa_out_norm), w_o.astype(BF16))
    x1, h2 = resid_norm(y, x, row(attn_post_norm), row(ffn_pre_norm))
    x2 = ffn(h2, w_gate.astype(BF16), w_up.astype(BF16), w_down.astype(BF16), x1, row(ffn_post_norm))
    return ple(x2, w_ple_gate.astype(BF16), p, w_ple_proj.astype(BF16))


def kernel(x, p, positions, attn_pre_norm, w_in, q_a_norm, w_q_up, kv_a_norm, w_kv_up, sinks,
           mla_out_norm, swa_out_norm, w_o, attn_post_norm, ffn_pre_norm, w_gate, w_up, w_down,
           ffn_post_norm, w_ple_gate, w_ple_proj):
    batch, seq, d = x.shape
    depth = w_in.shape[0]
    t = batch * seq
    pos_col = positions.reshape(t, 1)
    pos_row = positions.reshape(t // BLOCK, 1, BLOCK)
    half = MLA_ROPE // 2
    invf = np.asarray(ROPE_THETA ** (-(np.arange(LANES) % half) * 2.0 / MLA_ROPE), np.float32).reshape(1, LANES)
    invf = jnp.asarray(invf)
    xf = x.reshape(t, d)
    for i in range(depth):
        xf = _layer(xf, p[i].reshape(t, PLE_DIM), pos_col, pos_row, invf, batch, seq,
                    attn_pre_norm[i], w_in[i], q_a_norm[i], w_q_up[i], kv_a_norm[i], w_kv_up[i],
                    sinks[i], mla_out_norm[i], swa_out_norm[i], w_o[i], attn_post_norm[i],
                    ffn_pre_norm[i], w_gate[i], w_up[i], w_down[i], ffn_post_norm[i],
                    w_ple_gate[i], w_ple_proj[i])
    return xf.reshape(batch, seq, d)
```

```python
import functools
import math

import numpy as np
import jax
import jax.numpy as jnp
from jax import lax
from jax.experimental import pallas as pl
from jax.experimental.pallas import tpu as pltpu

D_MODEL = 4096
PLE_DIM = 256
MLA_HEADS = 16
MLA_NOPE = 128
MLA_ROPE = 64
MLA_V = 128
Q_LORA = 1024
KV_LORA = 512
MLA_WIDTH = MLA_HEADS * MLA_V
SWA_HEADS = 32
SWA_KV_HEADS = 4
SWA_HEAD_DIM = 64
SWA_GROUP = SWA_HEADS // SWA_KV_HEADS
SWA_WIDTH = SWA_HEADS * SWA_HEAD_DIM
WINDOW = 128
BLOCK = 128
ROPE_THETA = 10000.0
NORM_EPS = 1e-6
LOG2E = math.log2(math.e)
D_FF = 11008

LANES = 128
MXU_N = 256
MLA_TILE = 512
ROW_TILE = 256
FFN_TILE = 256
VT_ROWS = MLA_V + 16
VMEM_LIMIT = 60 * 1024 * 1024

COL_CQ = 0
COL_CKV = COL_CQ + Q_LORA
COL_QSWA = COL_CKV + KV_LORA
COL_KSWA = COL_QSWA + SWA_WIDTH
COL_VSWA = COL_KSWA + SWA_KV_HEADS * SWA_HEAD_DIM
COL_KR = COL_VSWA + SWA_KV_HEADS * SWA_HEAD_DIM
D_IN2 = COL_KR + 2 * MLA_ROPE

F32 = jnp.float32
BF16 = jnp.bfloat16


def _cparams(sem):
    return pltpu.CompilerParams(dimension_semantics=sem, vmem_limit_bytes=VMEM_LIMIT)


def _rms(xf, g):
    ms = jnp.mean(xf * xf, axis=-1, keepdims=True)
    return xf * lax.rsqrt(ms + NORM_EPS) * g


def _swap_half(w):
    d = w.shape[-1]
    return jnp.concatenate([-w[..., d // 2:], w[..., : d // 2]], axis=-1)


def _rope_tables(pos_col, invf_row):
    ang = pos_col.astype(F32) * invf_row
    return jnp.cos(ang), jnp.sin(ang)


def _col_chunks(n, width):
    return [(c, min(c + width, n)) for c in range(0, n, width)]


def _in_proj_kernel(x_ref, g_ref, w_ref, o_ref, *, chunks):
    h = _rms(x_ref[...], g_ref[...]).astype(BF16)
    for lo, hi in chunks:
        o_ref[:, lo:hi] = jnp.dot(h, w_ref[:, lo:hi], preferred_element_type=F32).astype(o_ref.dtype)


def in_proj(x, g, w, tm=ROW_TILE):
    m, k = x.shape
    n = w.shape[1]
    return pl.pallas_call(
        functools.partial(_in_proj_kernel, chunks=_col_chunks(n, 6 * MXU_N)),
        grid=(m // tm,),
        in_specs=[pl.BlockSpec((tm, k), lambda i: (i, 0)),
                  pl.BlockSpec((1, k), lambda i: (0, 0)),
                  pl.BlockSpec((k, n), lambda i: (0, 0), pipeline_mode=pl.Buffered(1))],
        out_specs=pl.BlockSpec((tm, n), lambda i: (i, 0)),
        out_shape=jax.ShapeDtypeStruct((m, n), BF16),
        compiler_params=_cparams(("parallel",)),
        name="in_proj",
    )(x, g, w)


def _q_up_kernel(cq_ref, g_ref, w_ref, pos_ref, invf_ref, qn_ref, qr_ref):
    h = _rms(cq_ref[...].astype(F32), g_ref[...]).astype(BF16)
    r = jnp.dot(h, w_ref[...], preferred_element_type=F32)
    qn_ref[...] = r[:, :MLA_WIDTH].astype(BF16)
    cos, sin = _rope_tables(pos_ref[...], invf_ref[...])
    n_rope = MLA_HEADS * MLA_ROPE
    for c in range(n_rope // LANES):
        a = r[:, MLA_WIDTH + c * LANES: MLA_WIDTH + (c + 1) * LANES]
        b = r[:, MLA_WIDTH + n_rope + c * LANES: MLA_WIDTH + n_rope + (c + 1) * LANES]
        qr_ref[:, c * LANES:(c + 1) * LANES] = (a * cos + b * sin).astype(BF16)


def q_up(proj, g, w, pos_col, invf, tm=512):
    m = proj.shape[0]
    n_rope = MLA_HEADS * MLA_ROPE
    return pl.pallas_call(
        _q_up_kernel,
        grid=(m // tm,),
        in_specs=[pl.BlockSpec((tm, Q_LORA), lambda i: (i, COL_CQ // Q_LORA)),
                  pl.BlockSpec((1, Q_LORA), lambda i: (0, 0)),
                  pl.BlockSpec(w.shape, lambda i: (0, 0)),
                  pl.BlockSpec((tm, 1), lambda i: (i, 0)),
                  pl.BlockSpec((1, LANES), lambda i: (0, 0))],
        out_specs=[pl.BlockSpec((tm, MLA_WIDTH), lambda i: (i, 0)),
                   pl.BlockSpec((tm, n_rope), lambda i: (i, 0))],
        out_shape=[jax.ShapeDtypeStruct((m, MLA_WIDTH), BF16),
                   jax.ShapeDtypeStruct((m, n_rope), BF16)],
        compiler_params=_cparams(("parallel",)),
        name="q_up",
    )(proj, g, w, pos_col, invf)


def _kv_up_kernel(ckv_ref, g_ref, wk_ref, wvt_ref, krab_ref, pos_ref, invf_ref, kn_ref, vt_ref, kr_ref):
    h = _rms(ckv_ref[...].astype(F32), g_ref[...]).astype(BF16)
    kn_ref[...] = jnp.dot(h, wk_ref[...], preferred_element_type=F32).astype(BF16)
    vt = lax.dot_general(wvt_ref[...], h, (((1,), (1,)), ((), ())), preferred_element_type=F32)
    ones = jnp.ones((VT_ROWS - MLA_V, vt.shape[1]), BF16)
    for hh in range(MLA_HEADS):
        vt_ref[0, hh, 0, :MLA_V, :] = vt[hh * MLA_V:(hh + 1) * MLA_V, :].astype(BF16)
        vt_ref[0, hh, 0, MLA_V:, :] = ones
    cos, sin = _rope_tables(pos_ref[...], invf_ref[...])
    lane = lax.broadcasted_iota(jnp.int32, cos.shape, 1)
    lo = lane < MLA_ROPE
    prod = krab_ref[...].astype(F32) * jnp.where(lo, cos, sin)
    kk = prod + pltpu.roll(prod, MLA_ROPE, 1)
    zero = jnp.zeros_like(kk)
    kr_ref[:, :LANES] = jnp.where(lo, kk, zero).astype(BF16)
    kr_ref[:, LANES:] = jnp.where(lo, zero, kk).astype(BF16)


def kv_up(proj, g, wk, wvt, pos_col, invf, batch, seq, tm=MLA_TILE):
    m = proj.shape[0]
    nk = seq // tm
    return pl.pallas_call(
        _kv_up_kernel,
        grid=(m // tm,),
        in_specs=[pl.BlockSpec((tm, KV_LORA), lambda i: (i, COL_CKV // KV_LORA)),
                  pl.BlockSpec((1, KV_LORA), lambda i: (0, 0)),
                  pl.BlockSpec(wk.shape, lambda i: (0, 0)),
                  pl.BlockSpec(wvt.shape, lambda i: (0, 0)),
                  pl.BlockSpec((tm, LANES), lambda i: (i, COL_KR // LANES)),
                  pl.BlockSpec((tm, 1), lambda i: (i, 0)),
                  pl.BlockSpec((1, LANES), lambda i: (0, 0))],
        out_specs=[pl.BlockSpec((tm, MLA_WIDTH), lambda i: (i, 0)),
                   pl.BlockSpec((1, MLA_HEADS, 1, VT_ROWS, tm), lambda i: (i // nk, 0, i % nk, 0, 0)),
                   pl.BlockSpec((tm, 2 * LANES), lambda i: (i, 0))],
        out_shape=[jax.ShapeDtypeStruct((m, MLA_WIDTH), BF16),
                   jax.ShapeDtypeStruct((batch, MLA_HEADS, nk, VT_ROWS, tm), BF16),
                   jax.ShapeDtypeStruct((m, 2 * LANES), BF16)],
        compiler_params=_cparams(("parallel",)),
        name="kv_up",
    )(proj, g, wk, wvt, proj, pos_col, invf)


def _mla_kernel(*refs, tq, tk, n_cast):
    qn_ref, qr_ref, kn_ref, kr_ref, vt_ref = refs[:5]
    w_refs = refs[5:5 + n_cast]
    o_ref = refs[5 + n_cast]
    w16_refs = refs[6 + n_cast:6 + 2 * n_cast]
    s_a, s_b, mx_a, mx_b, m_sc, acc_sc = refs[6 + 2 * n_cast:]
    for w_ref, w16_ref in zip(w_refs, w16_refs):
        w16_ref[...] = w_ref[...].astype(BF16)

    qi = pl.program_id(2)
    q = jnp.concatenate([qn_ref[...], qr_ref[...]], axis=-1)
    m_sc[...] = jnp.full(m_sc.shape, -jnp.inf, F32)
    acc_sc[...] = jnp.zeros(acc_sc.shape, F32)

    n_chunk = tq // MXU_N
    blk_chunks = tk // MXU_N
    n_diag = tq // tk
    bufs = ((s_a, mx_a), (s_b, mx_b))

    def live_chunks(diag):
        return range(0 if diag is None else diag * blk_chunks, n_chunk)

    def scores(kb, buf, diag=None):
        s_ref, mx_ref = buf
        off = pl.multiple_of(kb * tk, tk)
        k = jnp.concatenate([kn_ref[pl.ds(off, tk), :], kr_ref[pl.ds(off, tk), :]], axis=-1)
        for c in live_chunks(diag):
            s = lax.dot_general(k, q[c * MXU_N:(c + 1) * MXU_N, :], (((1,), (1,)), ((), ())),
                                preferred_element_type=F32)
            if diag is not None and c < (diag + 1) * blk_chunks:
                key = lax.broadcasted_iota(jnp.int32, s.shape, 0) + diag * tk
                qry = lax.broadcasted_iota(jnp.int32, s.shape, 1) + c * MXU_N
                s = jnp.where(qry >= key, s, -jnp.inf)
            s_ref[c] = s
            mx_ref[:, c * MXU_N:(c + 1) * MXU_N] = jnp.max(s, axis=0, keepdims=True)

    def softmax_pv(kb, buf, diag=None, mask_here=False):
        s_ref, mx_ref = buf
        vt = vt_ref[0, 0, kb]
        for c in live_chunks(diag):
            sl = slice(c * MXU_N, (c + 1) * MXU_N)
            remask = mask_here and c < blk_chunks

            def s_chunk():
                s = s_ref[c]
                if remask:
                    key = lax.broadcasted_iota(jnp.int32, s.shape, 0)
                    qry = lax.broadcasted_iota(jnp.int32, s.shape, 1) + c * MXU_N
                    s = jnp.where(qry >= key, s, -jnp.inf)
                return s

            mx = jnp.max(s_chunk(), axis=0, keepdims=True) if remask else mx_ref[:, sl]
            m_old = m_sc[:, sl]
            m_new = jnp.maximum(m_old, mx)
            alpha = jnp.exp2(m_old - m_new)
            p = jnp.exp2(s_chunk() - m_new).astype(BF16)
            acc_sc[:, sl] = alpha * acc_sc[:, sl] + jnp.dot(vt, p, preferred_element_type=F32)
            m_sc[:, sl] = m_new

    n_full = n_diag * qi
    scores(0, bufs[0])

    def body(j, carry):
        scores(2 * j + 1, bufs[1])
        softmax_pv(2 * j, bufs[0])
        scores(2 * j + 2, bufs[0])
        softmax_pv(2 * j + 1, bufs[1])
        return carry

    lax.fori_loop(0, n_full // 2, body, 0)

    for d in range(n_diag):
        if d + 1 < n_diag:
            scores(n_full + d + 1, bufs[(d + 1) % 2], diag=d + 1)
        softmax_pv(n_full + d, bufs[d % 2], diag=d, mask_here=(d == 0))

    o_ref[...] = (acc_sc[:MLA_V, :] / acc_sc[MLA_V:MLA_V + 1, :]).T.astype(o_ref.dtype)


def mla_attention(qn, qr, kn, kr, vt, weights, batch, seq, tq=4 * MLA_TILE, tk=MLA_TILE):
    assert tq % (2 * tk) == 0
    nq = seq // tq
    steps = batch * MLA_HEADS * nq

    def slab(w):
        rows = w.shape[0] // steps
        assert rows * steps == w.shape[0] and rows % 16 == 0, w.shape
        return pl.BlockSpec((rows, w.shape[1]), lambda b, h, i: ((b * MLA_HEADS + h) * nq + i, 0))

    slabs = [slab(w) for w in weights]
    outs = pl.pallas_call(
        functools.partial(_mla_kernel, tq=tq, tk=tk, n_cast=len(weights)),
        grid=(batch, MLA_HEADS, nq),
        in_specs=[pl.BlockSpec((tq, MLA_NOPE), lambda b, h, i: (b * nq + i, h)),
                  pl.BlockSpec((tq, LANES), lambda b, h, i: (b * nq + i, h // 2)),
                  pl.BlockSpec((seq, MLA_NOPE), lambda b, h, i: (b, h)),
                  pl.BlockSpec((seq, LANES), lambda b, h, i: (b, h % 2)),
                  pl.BlockSpec((1, 1, seq // tk, VT_ROWS, tk), lambda b, h, i: (b, h, 0, 0, 0))] + slabs,
        out_specs=[pl.BlockSpec((tq, MLA_V), lambda b, h, i: (b * nq + i, h))] + slabs,
        out_shape=[jax.ShapeDtypeStruct((batch * seq, MLA_WIDTH), BF16)]
        + [jax.ShapeDtypeStruct(w.shape, BF16) for w in weights],
        scratch_shapes=[pltpu.VMEM((tq // MXU_N, tk, MXU_N), F32), pltpu.VMEM((tq // MXU_N, tk, MXU_N), F32),
                        pltpu.VMEM((1, tq), F32), pltpu.VMEM((1, tq), F32),
                        pltpu.VMEM((1, tq), F32), pltpu.VMEM((VT_ROWS, tq), F32)],
        compiler_params=_cparams(("parallel", "parallel", "parallel")),
        name="mla_attention",
    )(qn, qr, kn, kr, vt, *weights)
    return outs[0], outs[1:]


def _swa_kernel(sinks_ref, q0_ref, q1_ref, q2_ref, q3_ref, kc_ref, kp_ref, vc_ref, vp_ref,
                pq_ref, pkc_ref, pkp_ref, o_ref):
    blk = pl.program_id(1)
    q_refs = (q0_ref, q1_ref, q2_ref, q3_ref)
    k_all = jnp.concatenate([kp_ref[...], kc_ref[...]], axis=0).astype(F32)
    v_all = jnp.concatenate([vp_ref[...], vc_ref[...]], axis=0).astype(F32)
    pk = jnp.concatenate([pkp_ref[0], pkc_ref[0]], axis=-1).astype(F32)
    dist = jnp.abs(pq_ref[...].astype(F32) - pk)
    row = lax.broadcasted_iota(jnp.int32, dist.shape, 0)
    col = lax.broadcasted_iota(jnp.int32, dist.shape, 1)
    rel = BLOCK + row - col
    first_key = jnp.where(blk > 0, 0, BLOCK)
    valid = (rel >= 0) & (rel < WINDOW) & (col >= first_key)
    dist_w = jnp.where(valid, dist, jnp.inf)
    lane =lax.broadcasted_iota(jnp.int32, (2 * BLOCK, LANES), 1)
    lo = lane < SWA_HEAD_DIM
    lane_o = lax.broadcasted_iota(jnp.int32, (BLOCK, LANES), 1)
    lo_o = lane_o < SWA_HEAD_DIM

    def dup(x_all, g):
        pair = x_all[:, (g // 2) * LANES:(g // 2 + 1) * LANES]
        rolled = pltpu.roll(pair, SWA_HEAD_DIM, 1)
        return jnp.where(lo, pair, rolled) if g % 2 == 0 else jnp.where(lo, rolled, pair)

    for g in range(SWA_KV_HEADS):
        kk = dup(k_all, g)
        vv = dup(v_all, g).astype(BF16)
        zero = jnp.zeros_like(kk)
        k_half = (jnp.where(lo, kk, zero).astype(BF16), jnp.where(lo, zero, kk).astype(BF16))
        for i in range(SWA_GROUP // 2):
            q_pair = q_refs[g][:, i * LANES:(i + 1) * LANES]
            outs = []
            for j in range(2):
                head = g * SWA_GROUP + 2 * i + j
                slope2 = LOG2E * 2.0 ** (-8.0 * (head + 1) / SWA_HEADS)
                s = lax.dot_general(q_pair, k_half[j], (((1,), (1,)), ((), ())),
                                    preferred_element_type=F32)
                s = s - slope2 * dist_w
                sink2 = LOG2E * sinks_ref[head]
                m = jnp.maximum(jnp.max(s, axis=-1, keepdims=True), sink2)
                e = jnp.exp2(s - m)
                denom = jnp.sum(e, axis=-1, keepdims=True) + jnp.exp2(sink2 - m)
                pv = jnp.dot(e.astype(BF16), vv, preferred_element_type=F32)
                outs.append(pv / denom)
            col0 = (g * SWA_GROUP // 2 + i) * LANES
            o_ref[:, col0:col0 + LANES] = jnp.where(lo_o, outs[0], outs[1]).astype(o_ref.dtype)


def swa_attention(proj, sinks, pos_col, pos_row, batch, seq):
    nblk = seq // BLOCK
    kvw = SWA_KV_HEADS * SWA_HEAD_DIM
    gw = SWA_GROUP * SWA_HEAD_DIM
    cur = lambda b, n: b * nblk + n
    prev = lambda b, n: b * nblk + jnp.maximum(n - 1, 0)
    q_specs = [pl.BlockSpec((BLOCK, gw), functools.partial(lambda b, n, g: (cur(b, n), COL_QSWA // gw + g), g=g))
               for g in range(SWA_KV_HEADS)]
    return pl.pallas_call(
        _swa_kernel,
        grid=(batch, nblk),
        in_specs=[pl.BlockSpec(memory_space=pltpu.SMEM)] + q_specs + [
            pl.BlockSpec((BLOCK, kvw), lambda b, n: (cur(b, n), COL_KSWA // kvw)),
            pl.BlockSpec((BLOCK, kvw), lambda b, n: (prev(b, n), COL_KSWA // kvw)),
            pl.BlockSpec((BLOCK, kvw), lambda b, n: (cur(b, n), COL_VSWA // kvw)),
            pl.BlockSpec((BLOCK, kvw), lambda b, n: (prev(b, n), COL_VSWA // kvw)),
            pl.BlockSpec((BLOCK, 1), lambda b, n: (cur(b, n), 0)),
            pl.BlockSpec((1, 1, BLOCK), lambda b, n: (cur(b, n), 0, 0)),
            pl.BlockSpec((1, 1, BLOCK), lambda b, n: (prev(b, n), 0, 0))],
        out_specs=pl.BlockSpec((BLOCK, SWA_WIDTH), lambda b, n: (cur(b, n), 0)),
        out_shape=jax.ShapeDtypeStruct((batch * seq, SWA_WIDTH), BF16),
        compiler_params=_cparams(("parallel", "parallel")),
        name="swa_attention",
    )(sinks, proj, proj, proj, proj, proj, proj, proj, proj, pos_col, pos_row, pos_row)


def _wo_kernel(oa_ref, ob_ref, ga_ref, gb_ref, w_ref, y_ref, *, chunks):
    ka = oa_ref.shape[1]
    na = _rms(oa_ref[...].astype(F32), ga_ref[...]).astype(BF16)
    nb = _rms(ob_ref[...].astype(F32), gb_ref[...]).astype(BF16)
    for lo, hi in chunks:
        y_ref[:, lo:hi] = (jnp.dot(na, w_ref[:ka, lo:hi], preferred_element_type=F32)
                           + jnp.dot(nb, w_ref[ka:, lo:hi], preferred_element_type=F32)).astype(y_ref.dtype)


def out_proj(oa, ob, ga, gb, w, tm=ROW_TILE):
    m, ka = oa.shape
    kb = ob.shape[1]
    n = w.shape[1]
    return pl.pallas_call(
        functools.partial(_wo_kernel, chunks=_col_chunks(n, 4 * MXU_N)),
        grid=(m // tm,),
        in_specs=[pl.BlockSpec((tm, ka), lambda i: (i, 0)),
                  pl.BlockSpec((tm, kb), lambda i: (i, 0)),
                  pl.BlockSpec((1, ka), lambda i: (0, 0)),
                  pl.BlockSpec((1, kb), lambda i: (0, 0)),
                  pl.BlockSpec((ka + kb, n), lambda i: (0, 0), pipeline_mode=pl.Buffered(1))],
        out_specs=pl.BlockSpec((tm, n), lambda i: (i, 0)),
        out_shape=jax.ShapeDtypeStruct((m, n), BF16),
        compiler_params=_cparams(("parallel",)),
        name="out_proj",
    )(oa, ob, ga, gb, w)


def _resid_kernel(y_ref, x_ref, gy_ref, gn_ref, xo_ref, ho_ref):
    xn = x_ref[...] + _rms(y_ref[...].astype(F32), gy_ref[...])
    xo_ref[...] = xn
    ho_ref[...] = _rms(xn, gn_ref[...]).astype(ho_ref.dtype)


def resid_norm(y, x, gy, gn, tm=256):
    m, d = x.shape
    row = pl.BlockSpec((tm, d), lambda i: (i, 0))
    vec = pl.BlockSpec((1, d), lambda i: (0, 0))
    return pl.pallas_call(
        _resid_kernel,
        grid=(m // tm,),
        in_specs=[row, row, vec, vec],
        out_specs=[row, row],
        out_shape=[jax.ShapeDtypeStruct((m, d), F32), jax.ShapeDtypeStruct((m, d), BF16)],
        compiler_params=_cparams(("parallel",)),
        name="resid_norm",
    )(y, x, gy, gn)


def _ffn_kernel(h_ref, wg_ref, wu_ref, wd_ref, x_ref, gp_ref, o_ref, *, dn, rows):
    j = pl.program_id(1)
    tm, d = o_ref.shape

    @pl.when(j == 0)
    def _():
        o_ref[...] = jnp.zeros(o_ref.shape, o_ref.dtype)

    h = h_ref[...]
    g = jnp.dot(h, wg_ref[...], preferred_element_type=F32)
    u = jnp.dot(h, wu_ref[...], preferred_element_type=F32)
    a = (g * jax.nn.sigmoid(g) * u).astype(BF16)
    for c in range(d // dn):
        o_ref[:, c * dn:(c + 1) * dn] += jnp.dot(a, wd_ref[:, c * dn:(c + 1) * dn],
                                                 preferred_element_type=F32)

    @pl.when(j == pl.num_programs(1) - 1)
    def _():
        def body(r, carry):
            rs = pl.ds(pl.multiple_of(r * rows, rows), rows)
            o_ref[rs, :] = x_ref[rs, :] + _rms(o_ref[rs, :], gp_ref[...])
            return carry

        lax.fori_loop(0, tm // rows, body, 0)


def ffn(h, wg, wu, wd, x, g_post, tm=1024, tf=FFN_TILE):
    m, d = h.shape
    f = wg.shape[1]
    once = dict(pipeline_mode=pl.Buffered(1))
    return pl.pallas_call(
        functools.partial(_ffn_kernel, dn=1024, rows=32),
        grid=(m // tm, f // tf),
        in_specs=[pl.BlockSpec((tm, d), lambda i, j: (i, 0), **once),
                  pl.BlockSpec((d, tf), lambda i, j: (0, j)),
                  pl.BlockSpec((d, tf), lambda i, j: (0, j)),
                  pl.BlockSpec((tf, d), lambda i, j: (j, 0)),
                  pl.BlockSpec((tm, d), lambda i, j: (i, 0), **once),
                  pl.BlockSpec((1, d), lambda i, j: (0, 0))],
        out_specs=pl.BlockSpec((tm, d), lambda i, j: (i, 0), **once),
        out_shape=jax.ShapeDtypeStruct((m, d), F32),
        compiler_params=_cparams(("parallel", "arbitrary")),
        name="ffn",
    )(h, wg, wu, wd, x, g_post)


def _ple_kernel(x_ref, wg_ref, p_ref, wp_ref, o_ref, *, chunks):
    xb = x_ref[...].astype(BF16)
    pb = p_ref[...].astype(BF16)
    for lo, hi in chunks:
        gate = jax.nn.sigmoid(jnp.dot(xb, wg_ref[:, lo:hi], preferred_element_type=F32))
        e = jnp.dot(pb, wp_ref[:, lo:hi], preferred_element_type=F32)
        o_ref[:, lo:hi] = x_ref[:, lo:hi] + gate * e


def ple(x, wg, p, wp, tm=ROW_TILE):
    m, d = x.shape
    n = wg.shape[1]
    pd = p.shape[1]
    once = dict(pipeline_mode=pl.Buffered(1))
    return pl.pallas_call(
        functools.partial(_ple_kernel, chunks=_col_chunks(n, 4 * MXU_N)),
        grid=(m // tm,),
        in_specs=[pl.BlockSpec((tm, d), lambda i: (i, 0)),
                  pl.BlockSpec((d, n), lambda i: (0, 0), **once),
                  pl.BlockSpec((tm, pd), lambda i: (i, 0)),
                  pl.BlockSpec((pd, n), lambda i: (0, 0), **once)],
        out_specs=pl.BlockSpec((tm, n), lambda i: (i, 0)),
        out_shape=jax.ShapeDtypeStruct((m, n), F32),
        compiler_params=_cparams(("parallel",)),
        name="ple",
    )(x, wg, p, wp)


def _prep_w_in(w):
    c_q, c_kv, k_r, q_s, k_s, v_s = jnp.split(
        w, [Q_LORA, Q_LORA + KV_LORA, Q_LORA + KV_LORA + MLA_ROPE,
            Q_LORA + KV_LORA + MLA_ROPE + SWA_WIDTH,
            Q_LORA + KV_LORA + MLA_ROPE + SWA_WIDTH + SWA_KV_HEADS * SWA_HEAD_DIM], axis=1)
    q_s = q_s * (LOG2E / math.sqrt(SWA_HEAD_DIM))
    parts = [c_q, c_kv, q_s, k_s, v_s, k_r, _swap_half(k_r)]
    return jnp.concatenate([part.astype(BF16) for part in parts], axis=1)


def _prep_w_q_up(w):
    w = (w * (LOG2E / math.sqrt(MLA_NOPE + MLA_ROPE))).reshape(Q_LORA, MLA_HEADS, MLA_NOPE + MLA_ROPE)
    nope = w[:, :, :MLA_NOPE].reshape(Q_LORA, MLA_WIDTH)
    rope = w[:, :, MLA_NOPE:]
    n_rope = MLA_HEADS * MLA_ROPE
    return jnp.concatenate([nope, rope.reshape(Q_LORA, n_rope),
                            _swap_half(rope).reshape(Q_LORA, n_rope)], axis=1).astype(BF16)


def _prep_w_kv_up(w):
    w = w.reshape(KV_LORA, MLA_HEADS, MLA_NOPE + MLA_V)
    wk = w[:, :, :MLA_NOPE].reshape(KV_LORA, MLA_WIDTH).astype(BF16)
    wvt = w[:, :, MLA_NOPE:].reshape(KV_LORA, MLA_WIDTH).T.astype(BF16)
    return wk, wvt


def _layer(x, p, pos_col, pos_row, invf, batch, seq, attn_pre_norm, w_in, q_a_norm, w_q_up,
           kv_a_norm, w_kv_up, sinks, mla_out_norm, swa_out_norm, w_o, attn_post_norm,
           ffn_pre_norm, w_gate, w_up, w_down, ffn_post_norm, w_ple_gate, w_ple_proj):
    row = lambda g: g.reshape(1, -1)
    proj = in_proj(x, row(attn_pre_norm), _prep_w_in(w_in))
    qn, qr = q_up(proj, row(q_a_norm), _prep_w_q_up(w_q_up), pos_col, invf)
    wk, wvt = _prep_w_kv_up(w_kv_up)
    kn, vt, kr = kv_up(proj, row(kv_a_norm), wk, wvt, pos_col, invf, batch, seq)
    o_mla, (wo16, wg16, wu16, wd16, wpg16) = mla_attention(
        qn, qr, kn, kr, vt, [w_o, w_gate, w_up, w_down.reshape(w_gate.shape), w_ple_gate], batch, seq)
    o_swa = swa_attention(proj, sinks, pos_col, pos_row, batch, seq)
    y = out_proj(o_mla, o_swa, row(mla_out_norm), row(swa_out_norm), wo16)
    x1, h2 = resid_norm(y, x, row(attn_post_norm), row(ffn_pre_norm))
    x2 = ffn(h2, wg16, wu16, wd16.reshape(w_down.shape), x1, row(ffn_post_norm))
    return ple(x2, wpg16, p, w_ple_proj.astype(BF16))


def kernel(x, p, positions, attn_pre_norm, w_in, q_a_norm, w_q_up, kv_a_norm, w_kv_up, sinks,
           mla_out_norm, swa_out_norm, w_o, attn_post_norm, ffn_pre_norm, w_gate, w_up, w_down,
           ffn_post_norm, w_ple_gate, w_ple_proj):
    batch, seq, d = x.shape
    depth = w_in.shape[0]
    t = batch * seq
    pos_col = positions.reshape(t, 1)
    pos_row = positions.reshape(t // BLOCK, 1, BLOCK)
    half = MLA_ROPE // 2
    invf = np.asarray(ROPE_THETA ** (-(np.arange(LANES) % half) * 2.0 / MLA_ROPE), np.float32).reshape(1, LANES)
    invf = jnp.asarray(invf)
    xf = x.reshape(t, d)
    for i in range(depth):
        xf = _layer(xf, p[i].reshape(t, PLE_DIM), pos_col, pos_row, invf, batch, seq,
                    attn_pre_norm[i], w_in[i], q_a_norm[i], w_q_up[i], kv_a_norm[i], w_kv_up[i],
                    sinks[i], mla_out_norm[i], swa_out_norm[i], w_o[i], attn_post_norm[i],
                    ffn_pre_norm[i], w_gate[i], w_up[i], w_down[i], ffn_post_norm[i],
                    w_ple_gate[i], w_ple_proj[i])
    return xf.reshape(batch, seq, d)
```

```python
import functools
import math

import numpy as np
import jax
import jax.numpy as jnp
from jax import lax
from jax.experimental import pallas as pl
from jax.experimental.pallas import tpu as pltpu

D_MODEL = 4096
PLE_DIM = 256
MLA_HEADS = 16
MLA_NOPE = 128
MLA_ROPE = 64
MLA_V = 128
Q_LORA = 1024
KV_LORA = 512
MLA_WIDTH = MLA_HEADS * MLA_V
SWA_HEADS = 32
SWA_KV_HEADS = 4
SWA_HEAD_DIM = 64
SWA_GROUP = SWA_HEADS // SWA_KV_HEADS
SWA_WIDTH = SWA_HEADS * SWA_HEAD_DIM
WINDOW = 128
BLOCK = 128
ROPE_THETA = 10000.0
NORM_EPS = 1e-6
LOG2E = math.log2(math.e)
D_FF = 11008

LANES = 128
MXU_N = 256
MLA_TILE = 512
ROW_TILE = 256
FFN_TILE = 256
VT_ROWS = MLA_V + 16
VMEM_LIMIT = 60 * 1024 * 1024

COL_CQ = 0
COL_CKV = COL_CQ + Q_LORA
COL_QSWA = COL_CKV + KV_LORA
COL_KSWA = COL_QSWA + SWA_WIDTH
COL_VSWA = COL_KSWA + SWA_KV_HEADS * SWA_HEAD_DIM
COL_KR = COL_VSWA + SWA_KV_HEADS * SWA_HEAD_DIM
D_IN2 = COL_KR + 2 * MLA_ROPE

F32 = jnp.float32
BF16 = jnp.bfloat16


def _cparams(sem):
    return pltpu.CompilerParams(dimension_semantics=sem, vmem_limit_bytes=VMEM_LIMIT)


def _rms(xf, g):
    ms = jnp.mean(xf * xf, axis=-1, keepdims=True)
    return xf * lax.rsqrt(ms + NORM_EPS) * g


def _swap_half(w):
    d = w.shape[-1]
    return jnp.concatenate([-w[..., d // 2:], w[..., : d // 2]], axis=-1)


def _rope_tables(pos_col, invf_row):
    ang = pos_col.astype(F32) * invf_row
    return jnp.cos(ang), jnp.sin(ang)


def _col_chunks(n, width):
    return [(c, min(c + width, n)) for c in range(0, n, width)]


def _in_proj_kernel(x_ref, g_ref, w_ref, o_ref, *, chunks):
    h = _rms(x_ref[...], g_ref[...]).astype(BF16)
    for lo, hi in chunks:
        o_ref[:, lo:hi] = jnp.dot(h, w_ref[:, lo:hi], preferred_element_type=F32).astype(o_ref.dtype)


def in_proj(x, g, w, tm=ROW_TILE):
    m, k = x.shape
    n = w.shape[1]
    return pl.pallas_call(
        functools.partial(_in_proj_kernel, chunks=_col_chunks(n, 6 * MXU_N)),
        grid=(m // tm,),
        in_specs=[pl.BlockSpec((tm, k), lambda i: (i, 0)),
                  pl.BlockSpec((1, k), lambda i: (0, 0)),
                  pl.BlockSpec((k, n), lambda i: (0, 0), pipeline_mode=pl.Buffered(1))],
        out_specs=pl.BlockSpec((tm, n), lambda i: (i, 0)),
        out_shape=jax.ShapeDtypeStruct((m, n), BF16),
        compiler_params=_cparams(("parallel",)),
        name="in_proj",
    )(x, g, w)


def _q_up_kernel(cq_ref, g_ref, w_ref, pos_ref, invf_ref, qn_ref, qr_ref):
    h = _rms(cq_ref[...].astype(F32), g_ref[...]).astype(BF16)
    r = jnp.dot(h, w_ref[...], preferred_element_type=F32)
    qn_ref[...] = r[:, :MLA_WIDTH].astype(BF16)
    cos, sin = _rope_tables(pos_ref[...], invf_ref[...])
    n_rope = MLA_HEADS * MLA_ROPE
    for c in range(n_rope // LANES):
        a = r[:, MLA_WIDTH + c * LANES: MLA_WIDTH + (c + 1) * LANES]
        b = r[:, MLA_WIDTH + n_rope + c * LANES: MLA_WIDTH + n_rope + (c + 1) * LANES]
        qr_ref[:, c * LANES:(c + 1) * LANES] = (a * cos + b * sin).astype(BF16)


def q_up(proj, g, w, pos_col, invf, tm=512):
    m = proj.shape[0]
    n_rope = MLA_HEADS * MLA_ROPE
    return pl.pallas_call(
        _q_up_kernel,
        grid=(m // tm,),
        in_specs=[pl.BlockSpec((tm, Q_LORA), lambda i: (i, COL_CQ // Q_LORA)),
                  pl.BlockSpec((1, Q_LORA), lambda i: (0, 0)),
                  pl.BlockSpec(w.shape, lambda i: (0, 0)),
                  pl.BlockSpec((tm, 1), lambda i: (i, 0)),
                  pl.BlockSpec((1, LANES), lambda i: (0, 0))],
        out_specs=[pl.BlockSpec((tm, MLA_WIDTH), lambda i: (i, 0)),
                   pl.BlockSpec((tm, n_rope), lambda i: (i, 0))],
        out_shape=[jax.ShapeDtypeStruct((m, MLA_WIDTH), BF16),
                   jax.ShapeDtypeStruct((m, n_rope), BF16)],
        compiler_params=_cparams(("parallel",)),
        name="q_up",
    )(proj, g, w, pos_col, invf)


def _kv_up_kernel(ckv_ref, g_ref, wk_ref, wvt_ref, krab_ref, pos_ref, invf_ref, kn_ref, vt_ref, kr_ref):
    h = _rms(ckv_ref[...].astype(F32), g_ref[...]).astype(BF16)
    kn_ref[...] = jnp.dot(h, wk_ref[...], preferred_element_type=F32).astype(BF16)
    vt = lax.dot_general(wvt_ref[...], h, (((1,), (1,)), ((), ())), preferred_element_type=F32)
    ones = jnp.ones((VT_ROWS - MLA_V, vt.shape[1]), BF16)
    for hh in range(MLA_HEADS):
        vt_ref[0, hh, 0, :MLA_V, :] = vt[hh * MLA_V:(hh + 1) * MLA_V, :].astype(BF16)
        vt_ref[0, hh, 0, MLA_V:, :] = ones
    cos, sin = _rope_tables(pos_ref[...], invf_ref[...])
    lane = lax.broadcasted_iota(jnp.int32, cos.shape, 1)
    lo = lane < MLA_ROPE
    prod = krab_ref[...].astype(F32) * jnp.where(lo, cos, sin)
    kk = prod + pltpu.roll(prod, MLA_ROPE, 1)
    zero = jnp.zeros_like(kk)
    kr_ref[:, :LANES] = jnp.where(lo, kk, zero).astype(BF16)
    kr_ref[:, LANES:] = jnp.where(lo, zero, kk).astype(BF16)


def kv_up(proj, g, wk, wvt, pos_col, invf, batch, seq, tm=MLA_TILE):
    m = proj.shape[0]
    nk = seq // tm
    return pl.pallas_call(
        _kv_up_kernel,
        grid=(m // tm,),
        in_specs=[pl.BlockSpec((tm, KV_LORA), lambda i: (i, COL_CKV // KV_LORA)),
                  pl.BlockSpec((1, KV_LORA), lambda i: (0, 0)),
                  pl.BlockSpec(wk.shape, lambda i: (0, 0)),
                  pl.BlockSpec(wvt.shape, lambda i: (0, 0)),
                  pl.BlockSpec((tm, LANES), lambda i: (i, COL_KR // LANES)),
                  pl.BlockSpec((tm, 1), lambda i: (i, 0)),
                  pl.BlockSpec((1, LANES), lambda i: (0, 0))],
        out_specs=[pl.BlockSpec((tm, MLA_WIDTH), lambda i: (i, 0)),
                   pl.BlockSpec((1, MLA_HEADS, 1, VT_ROWS, tm), lambda i: (i // nk, 0, i % nk, 0, 0)),
                   pl.BlockSpec((tm, 2 * LANES), lambda i: (i, 0))],
        out_shape=[jax.ShapeDtypeStruct((m, MLA_WIDTH), BF16),
                   jax.ShapeDtypeStruct((batch, MLA_HEADS, nk, VT_ROWS, tm), BF16),
                   jax.ShapeDtypeStruct((m, 2 * LANES), BF16)],
        compiler_params=_cparams(("parallel",)),
        name="kv_up",
    )(proj, g, wk, wvt, proj, pos_col, invf)


def _mla_kernel(*refs, tq, tk, n_cast):
    qn_ref, qr_ref, kn_ref, kr_ref, vt_ref = refs[:5]
    w_refs = refs[5:5 + n_cast]
    o_ref = refs[5 + n_cast]
    w16_refs = refs[6 + n_cast:6 + 2 * n_cast]
    s_a, s_b, mx_a, mx_b, m_sc, acc_sc = refs[6 + 2 * n_cast:]
    for w_ref, w16_ref in zip(w_refs, w16_refs):
        w16_ref[...] = w_ref[...].astype(BF16)

    qi = pl.program_id(2)
    q = jnp.concatenate([qn_ref[...], qr_ref[...]], axis=-1)
    m_sc[...] = jnp.full(m_sc.shape, -jnp.inf, F32)
    acc_sc[...] = jnp.zeros(acc_sc.shape, F32)

    n_chunk = tq // MXU_N
    blk_chunks = tk // MXU_N
    n_diag = tq // tk
    bufs = ((s_a, mx_a), (s_b, mx_b))

    def live_chunks(diag):
        return range(0 if diag is None else diag * blk_chunks, n_chunk)

    def scores(kb, buf, diag=None):
        s_ref, mx_ref = buf
        off = pl.multiple_of(kb * tk, tk)
        k = jnp.concatenate([kn_ref[pl.ds(off, tk), :], kr_ref[pl.ds(off, tk), :]], axis=-1)
        for c in live_chunks(diag):
            s = lax.dot_general(k, q[c * MXU_N:(c + 1) * MXU_N, :], (((1,), (1,)), ((), ())),
                                preferred_element_type=F32)
            if diag is not None and c < (diag + 1) * blk_chunks:
                key = lax.broadcasted_iota(jnp.int32, s.shape, 0) + diag * tk
                qry = lax.broadcasted_iota(jnp.int32, s.shape, 1) + c * MXU_N
                s = jnp.where(qry >= key, s, -jnp.inf)
            s_ref[c] = s
            mx_ref[:, c * MXU_N:(c + 1) * MXU_N] = jnp.max(s, axis=0, keepdims=True)

    def softmax_pv(kb, buf, diag=None, mask_here=False):
        s_ref, mx_ref = buf
        vt = vt_ref[0, 0, kb]
        for c in live_chunks(diag):
            sl = slice(c * MXU_N, (c + 1) * MXU_N)
            remask = mask_here and c < blk_chunks

            def s_chunk():
                s = s_ref[c]
                if remask:
                    key = lax.broadcasted_iota(jnp.int32, s.shape, 0)
                    qry = lax.broadcasted_iota(jnp.int32, s.shape, 1) + c * MXU_N
                    s = jnp.where(qry >= key, s, -jnp.inf)
                return s

            mx = jnp.max(s_chunk(), axis=0, keepdims=True) if remask else mx_ref[:, sl]
            m_old = m_sc[:, sl]
            m_new = jnp.maximum(m_old, mx)
            alpha = jnp.exp2(m_old - m_new)
            p = jnp.exp2(s_chunk() - m_new).astype(BF16)
            acc_sc[:, sl] = alpha * acc_sc[:, sl] + jnp.dot(vt, p, preferred_element_type=F32)
            m_sc[:, sl] = m_new

    n_full = n_diag * qi
    scores(0, bufs[0])

    def body(j, carry):
        scores(2 * j + 1, bufs[1])
        softmax_pv(2 * j, bufs[0])
        scores(2 * j + 2, bufs[0])
        softmax_pv(2 * j + 1, bufs[1])
        return carry

    lax.fori_loop(0, n_full // 2, body, 0)

    for d in range(n_diag):
        if d + 1 < n_diag:
            scores(n_full + d + 1, bufs[(d + 1) % 2], diag=d + 1)
        softmax_pv(n_full + d, bufs[d % 2], diag=d, mask_here=(d == 0))

    o_ref[...] = (acc_sc[:MLA_V, :] / acc_sc[MLA_V:MLA_V + 1, :]).T.astype(o_ref.dtype)


def mla_attention(qn, qr, kn, kr, vt, weights, batch, seq, tq=4 * MLA_TILE, tk=MLA_TILE):
    assert tq % (2 * tk) == 0
    nq = seq // tq
    steps = batch * MLA_HEADS * nq

    def slab(w):
        tiles = w.shape[0] // 16
        assert tiles * 16 == w.shape[0], w.shape
        n_slab = max(n for n in range(1, steps + 1) if tiles % n == 0)
        rows = w.shape[0] // n_slab
        return pl.BlockSpec((rows, w.shape[1]),
                            lambda b, h, i: (jnp.minimum((b * MLA_HEADS + h) * nq + i, n_slab - 1), 0))

    slabs = [slab(w) for w in weights]
    outs = pl.pallas_call(
        functools.partial(_mla_kernel, tq=tq, tk=tk, n_cast=len(weights)),
        grid=(batch, MLA_HEADS, nq),
        in_specs=[pl.BlockSpec((tq, MLA_NOPE), lambda b, h, i: (b * nq + i, h)),
                  pl.BlockSpec((tq, LANES), lambda b, h, i: (b * nq + i, h // 2)),
                  pl.BlockSpec((seq, MLA_NOPE), lambda b, h, i: (b, h)),
                  pl.BlockSpec((seq, LANES), lambda b, h, i: (b, h % 2)),
                  pl.BlockSpec((1, 1, seq // tk, VT_ROWS, tk), lambda b, h, i: (b, h, 0, 0, 0))] + slabs,
        out_specs=[pl.BlockSpec((tq, MLA_V), lambda b, h, i: (b * nq + i, h))] + slabs,
        out_shape=[jax.ShapeDtypeStruct((batch * seq, MLA_WIDTH), BF16)]
        + [jax.ShapeDtypeStruct(w.shape, BF16) for w in weights],
        scratch_shapes=[pltpu.VMEM((tq // MXU_N, tk, MXU_N), F32), pltpu.VMEM((tq // MXU_N, tk, MXU_N), F32),
                        pltpu.VMEM((1, tq), F32), pltpu.VMEM((1, tq), F32),
                        pltpu.VMEM((1, tq), F32), pltpu.VMEM((VT_ROWS, tq), F32)],
        compiler_params=_cparams(("parallel", "parallel", "parallel")),
        name="mla_attention",
    )(qn, qr, kn, kr, vt, *weights)
    return outs[0], outs[1:]


def _swa_kernel(sinks_ref, q0_ref, q1_ref, q2_ref, q3_ref, kc_ref, kp_ref, vc_ref, vp_ref,
                pq_ref, pkc_ref, pkp_ref, o_ref):
    blk = pl.program_id(1)
    q_refs = (q0_ref, q1_ref, q2_ref, q3_ref)
    k_all = jnp.concatenate([kp_ref[...], kc_ref[...]], axis=0).astype(F32)
    v_all = jnp.concatenate([vp_ref[...], vc_ref[...]], axis=0).astype(F32)
    pk = jnp.concatenate([pkp_ref[0], pkc_ref[0]], axis=-1).astype(F32)
    dist = jnp.abs(pq_ref[...].astype(F32) - pk)
    row = lax.broadcasted_iota(jnp.int32, dist.shape, 0)
    col = lax.broadcasted_iota(jnp.int32, dist.shape, 1)
    rel = BLOCK + row - col
    first_key = jnp.where(blk > 0, 0, BLOCK)
    valid = (rel >= 0) & (rel < WINDOW) & (col >= first_key)
    dist_w = jnp.where(valid, dist, jnp.inf)
    lane =lax.broadcasted_iota(jnp.int32, (2 * BLOCK, LANES), 1)
    lo = lane < SWA_HEAD_DIM
    lane_o = lax.broadcasted_iota(jnp.int32, (BLOCK, LANES), 1)
    lo_o = lane_o < SWA_HEAD_DIM

    def dup(x_all, g):
        pair = x_all[:, (g // 2) * LANES:(g // 2 + 1) * LANES]
        rolled = pltpu.roll(pair, SWA_HEAD_DIM, 1)
        return jnp.where(lo, pair, rolled) if g % 2 == 0 else jnp.where(lo, rolled, pair)

    for g in range(SWA_KV_HEADS):
        kk = dup(k_all, g)
        vv = dup(v_all, g).astype(BF16)
        zero = jnp.zeros_like(kk)
        k_half = (jnp.where(lo, kk, zero).astype(BF16), jnp.where(lo, zero, kk).astype(BF16))
        for i in range(SWA_GROUP // 2):
            q_pair = q_refs[g][:, i * LANES:(i + 1) * LANES]
            outs = []
            for j in range(2):
                head = g * SWA_GROUP + 2 * i + j
                slope2 = LOG2E * 2.0 ** (-8.0 * (head + 1) / SWA_HEADS)
                s = lax.dot_general(q_pair, k_half[j], (((1,), (1,)), ((), ())),
                                    preferred_element_type=F32)
                s = s - slope2 * dist_w
                sink2 = LOG2E * sinks_ref[head]
                m = jnp.maximum(jnp.max(s, axis=-1, keepdims=True), sink2)
                e = jnp.exp2(s - m)
                denom = jnp.sum(e, axis=-1, keepdims=True) + jnp.exp2(sink2 - m)
                pv = jnp.dot(e.astype(BF16), vv, preferred_element_type=F32)
                outs.append(pv / denom)
            col0 = (g * SWA_GROUP // 2 + i) * LANES
            o_ref[:, col0:col0 + LANES] = jnp.where(lo_o, outs[0], outs[1]).astype(o_ref.dtype)


def swa_attention(proj, sinks, pos_col, pos_row, batch, seq):
    nblk = seq // BLOCK
    kvw = SWA_KV_HEADS * SWA_HEAD_DIM
    gw = SWA_GROUP * SWA_HEAD_DIM
    cur = lambda b, n: b * nblk + n
    prev = lambda b, n: b * nblk + jnp.maximum(n - 1, 0)
    q_specs = [pl.BlockSpec((BLOCK, gw), functools.partial(lambda b, n, g: (cur(b, n), COL_QSWA // gw + g), g=g))
               for g in range(SWA_KV_HEADS)]
    return pl.pallas_call(
        _swa_kernel,
        grid=(batch, nblk),
        in_specs=[pl.BlockSpec(memory_space=pltpu.SMEM)] + q_specs + [
            pl.BlockSpec((BLOCK, kvw), lambda b, n: (cur(b, n), COL_KSWA // kvw)),
            pl.BlockSpec((BLOCK, kvw), lambda b, n: (prev(b, n), COL_KSWA // kvw)),
            pl.BlockSpec((BLOCK, kvw), lambda b, n: (cur(b, n), COL_VSWA // kvw)),
            pl.BlockSpec((BLOCK, kvw), lambda b, n: (prev(b, n), COL_VSWA // kvw)),
            pl.BlockSpec((BLOCK, 1), lambda b, n: (cur(b, n), 0)),
            pl.BlockSpec((1, 1, BLOCK), lambda b, n: (cur(b, n), 0, 0)),
            pl.BlockSpec((1, 1, BLOCK), lambda b, n: (prev(b, n), 0, 0))],
        out_specs=pl.BlockSpec((BLOCK, SWA_WIDTH), lambda b, n: (cur(b, n), 0)),
        out_shape=jax.ShapeDtypeStruct((batch * seq, SWA_WIDTH), BF16),
        compiler_params=_cparams(("parallel", "parallel")),
        name="swa_attention",
    )(sinks, proj, proj, proj, proj, proj, proj, proj, proj, pos_col, pos_row, pos_row)


def _wo_kernel(oa_ref, ob_ref, ga_ref, gb_ref, w_ref, y_ref, *, chunks):
    ka = oa_ref.shape[1]
    na = _rms(oa_ref[...].astype(F32), ga_ref[...]).astype(BF16)
    nb = _rms(ob_ref[...].astype(F32), gb_ref[...]).astype(BF16)
    for lo, hi in chunks:
        y_ref[:, lo:hi] = (jnp.dot(na, w_ref[:ka, lo:hi], preferred_element_type=F32)
                           + jnp.dot(nb, w_ref[ka:, lo:hi], preferred_element_type=F32)).astype(y_ref.dtype)


def out_proj(oa, ob, ga, gb, w, tm=ROW_TILE):
    m, ka = oa.shape
    kb = ob.shape[1]
    n = w.shape[1]
    return pl.pallas_call(
        functools.partial(_wo_kernel, chunks=_col_chunks(n, 4 * MXU_N)),
        grid=(m // tm,),
        in_specs=[pl.BlockSpec((tm, ka), lambda i: (i, 0)),
                  pl.BlockSpec((tm, kb), lambda i: (i, 0)),
                  pl.BlockSpec((1, ka), lambda i: (0, 0)),
                  pl.BlockSpec((1, kb), lambda i: (0, 0)),
                  pl.BlockSpec((ka + kb, n), lambda i: (0, 0), pipeline_mode=pl.Buffered(1))],
        out_specs=pl.BlockSpec((tm, n), lambda i: (i, 0)),
        out_shape=jax.ShapeDtypeStruct((m, n), BF16),
        compiler_params=_cparams(("parallel",)),
        name="out_proj",
    )(oa, ob, ga, gb, w)


def _resid_kernel(y_ref, x_ref, gy_ref, gn_ref, xo_ref, ho_ref):
    xn = x_ref[...] + _rms(y_ref[...].astype(F32), gy_ref[...])
    xo_ref[...] = xn
    ho_ref[...] = _rms(xn, gn_ref[...]).astype(ho_ref.dtype)


def resid_norm(y, x, gy, gn, tm=256):
    m, d = x.shape
    row = pl.BlockSpec((tm, d), lambda i: (i, 0))
    vec = pl.BlockSpec((1, d), lambda i: (0, 0))
    return pl.pallas_call(
        _resid_kernel,
        grid=(m // tm,),
        in_specs=[row, row, vec, vec],
        out_specs=[row, row],
        out_shape=[jax.ShapeDtypeStruct((m, d), F32), jax.ShapeDtypeStruct((m, d), BF16)],
        compiler_params=_cparams(("parallel",)),
        name="resid_norm",
    )(y, x, gy, gn)


def _ffn_kernel(h_ref, wg_ref, wu_ref, wd_ref, x_ref, gp_ref, o_ref, *, dn, rows):
    j = pl.program_id(1)
    tm, d = o_ref.shape

    @pl.when(j == 0)
    def _():
        o_ref[...] = jnp.zeros(o_ref.shape, o_ref.dtype)

    h = h_ref[...]
    g = jnp.dot(h, wg_ref[...], preferred_element_type=F32)
    u = jnp.dot(h, wu_ref[...], preferred_element_type=F32)
    a = (g * jax.nn.sigmoid(g) * u).astype(BF16)
    for c in range(d // dn):
        o_ref[:, c * dn:(c + 1) * dn] += jnp.dot(a, wd_ref[:, c * dn:(c + 1) * dn],
                                                 preferred_element_type=F32)

    @pl.when(j == pl.num_programs(1) - 1)
    def _():
        def body(r, carry):
            rs = pl.ds(pl.multiple_of(r * rows, rows), rows)
            o_ref[rs, :] = x_ref[rs, :] + _rms(o_ref[rs, :], gp_ref[...])
            return carry

        lax.fori_loop(0, tm // rows, body, 0)


def ffn(h, wg, wu, wd, x, g_post, tm=1024, tf=FFN_TILE):
    m, d = h.shape
    f = wg.shape[1]
    once = dict(pipeline_mode=pl.Buffered(1))
    return pl.pallas_call(
        functools.partial(_ffn_kernel, dn=1024, rows=32),
        grid=(m // tm, f // tf),
        in_specs=[pl.BlockSpec((tm, d), lambda i, j: (i, 0), **once),
                  pl.BlockSpec((d, tf), lambda i, j: (0, j)),
                  pl.BlockSpec((d, tf), lambda i, j: (0, j)),
                  pl.BlockSpec((tf, d), lambda i, j: (j, 0)),
                  pl.BlockSpec((tm, d), lambda i, j: (i, 0), **once),
                  pl.BlockSpec((1, d), lambda i, j: (0, 0))],
        out_specs=pl.BlockSpec((tm, d), lambda i, j: (i, 0), **once),
        out_shape=jax.ShapeDtypeStruct((m, d), F32),
        compiler_params=_cparams(("parallel", "arbitrary")),
        name="ffn",
    )(h, wg, wu, wd, x, g_post)


def _ple_kernel(x_ref, wg_ref, p_ref, wp_ref, o_ref, *, chunks):
    xb = x_ref[...].astype(BF16)
    pb = p_ref[...].astype(BF16)
    for lo, hi in chunks:
        gate = jax.nn.sigmoid(jnp.dot(xb, wg_ref[:, lo:hi], preferred_element_type=F32))
        e = jnp.dot(pb, wp_ref[:, lo:hi], preferred_element_type=F32)
        o_ref[:, lo:hi] = x_ref[:, lo:hi] + gate * e


def ple(x, wg, p, wp, tm=ROW_TILE):
    m, d = x.shape
    n = wg.shape[1]
    pd = p.shape[1]
    once = dict(pipeline_mode=pl.Buffered(1))
    return pl.pallas_call(
        functools.partial(_ple_kernel, chunks=_col_chunks(n, 4 * MXU_N)),
        grid=(m // tm,),
        in_specs=[pl.BlockSpec((tm, d), lambda i: (i, 0)),
                  pl.BlockSpec((d, n), lambda i: (0, 0), **once),
                  pl.BlockSpec((tm, pd), lambda i: (i, 0)),
                  pl.BlockSpec((pd, n), lambda i: (0, 0), **once)],
        out_specs=pl.BlockSpec((tm, n), lambda i: (i, 0)),
        out_shape=jax.ShapeDtypeStruct((m, n), F32),
        compiler_params=_cparams(("parallel",)),
        name="ple",
    )(x, wg, p, wp)


def _prep_w_in(w):
    c_q, c_kv, k_r, q_s, k_s, v_s = jnp.split(
        w, [Q_LORA, Q_LORA + KV_LORA, Q_LORA + KV_LORA + MLA_ROPE,
            Q_LORA + KV_LORA + MLA_ROPE + SWA_WIDTH,
            Q_LORA + KV_LORA + MLA_ROPE + SWA_WIDTH + SWA_KV_HEADS * SWA_HEAD_DIM], axis=1)
    q_s = q_s * (LOG2E / math.sqrt(SWA_HEAD_DIM))
    parts = [c_q, c_kv, q_s, k_s, v_s, k_r, _swap_half(k_r)]
    return jnp.concatenate([part.astype(BF16) for part in parts], axis=1)


def _prep_w_q_up(w):
    w = (w * (LOG2E / math.sqrt(MLA_NOPE + MLA_ROPE))).reshape(Q_LORA, MLA_HEADS, MLA_NOPE + MLA_ROPE)
    nope = w[:, :, :MLA_NOPE].reshape(Q_LORA, MLA_WIDTH)
    rope = w[:, :, MLA_NOPE:]
    n_rope = MLA_HEADS * MLA_ROPE
    return jnp.concatenate([nope, rope.reshape(Q_LORA, n_rope),
                            _swap_half(rope).reshape(Q_LORA, n_rope)], axis=1).astype(BF16)


def _prep_w_kv_up(w):
    w = w.reshape(KV_LORA, MLA_HEADS, MLA_NOPE + MLA_V)
    wk = w[:, :, :MLA_NOPE].reshape(KV_LORA, MLA_WIDTH).astype(BF16)
    wvt = w[:, :, MLA_NOPE:].reshape(KV_LORA, MLA_WIDTH).T.astype(BF16)
    return wk, wvt


def _layer(x, p, pos_col, pos_row, invf, batch, seq, attn_pre_norm, w_in, q_a_norm, w_q_up,
           kv_a_norm, w_kv_up, sinks, mla_out_norm, swa_out_norm, w_o, attn_post_norm,
           ffn_pre_norm, w_gate, w_up, w_down, ffn_post_norm, w_ple_gate, w_ple_proj):
    row = lambda g: g.reshape(1, -1)
    proj = in_proj(x, row(attn_pre_norm), _prep_w_in(w_in))
    qn, qr = q_up(proj, row(q_a_norm), _prep_w_q_up(w_q_up), pos_col, invf)
    wk, wvt = _prep_w_kv_up(w_kv_up)
    kn, vt, kr = kv_up(proj, row(kv_a_norm), wk, wvt, pos_col, invf, batch, seq)
    o_mla, (wo16, wg16, wu16, wd16, wpg16) = mla_attention(
        qn, qr, kn, kr, vt, [w_o, w_gate, w_up, w_down, w_ple_gate], batch, seq)
    o_swa = swa_attention(proj, sinks, pos_col, pos_row, batch, seq)
    y = out_proj(o_mla, o_swa, row(mla_out_norm), row(swa_out_norm), wo16)
    x1, h2 = resid_norm(y, x, row(attn_post_norm), row(ffn_pre_norm))
    x2 = ffn(h2, wg16, wu16, wd16, x1, row(ffn_post_norm))
    return ple(x2, wpg16, p, w_ple_proj.astype(BF16))


def kernel(x, p, positions, attn_pre_norm, w_in, q_a_norm, w_q_up, kv_a_norm, w_kv_up, sinks,
           mla_out_norm, swa_out_norm, w_o, attn_post_norm, ffn_pre_norm, w_gate, w_up, w_down,
           ffn_post_norm, w_ple_gate, w_ple_proj):
    batch, seq, d = x.shape
    depth = w_in.shape[0]
    t = batch * seq
    pos_col = positions.reshape(t, 1)
    pos_row = positions.reshape(t // BLOCK, 1, BLOCK)
    half = MLA_ROPE // 2
    invf = np.asarray(ROPE_THETA ** (-(np.arange(LANES) % half) * 2.0 / MLA_ROPE), np.float32).reshape(1, LANES)
    invf = jnp.asarray(invf)
    xf = x.reshape(t, d)
    for i in range(depth):
        xf = _layer(xf, p[i].reshape(t, PLE_DIM), pos_col, pos_row, invf, batch, seq,
                    attn_pre_norm[i], w_in[i], q_a_norm[i], w_q_up[i], kv_a_norm[i], w_kv_up[i],
                    sinks[i], mla_out_norm[i], swa_out_norm[i], w_o[i], attn_post_norm[i],
                    ffn_pre_norm[i], w_gate[i], w_up[i], w_down[i], ffn_post_norm[i],
                    w_ple_gate[i], w_ple_proj[i])
    return xf.reshape(batch, seq, d)
```

```python
import functools
import math

import numpy as np
import jax
import jax.numpy as jnp
from jax import lax
from jax.experimental import pallas as pl
from jax.experimental.pallas import tpu as pltpu

D_MODEL = 4096
PLE_DIM = 256
MLA_HEADS = 16
MLA_NOPE = 128
MLA_ROPE = 64
MLA_V = 128
Q_LORA = 1024
KV_LORA = 512
MLA_WIDTH = MLA_HEADS * MLA_V
SWA_HEADS = 32
SWA_KV_HEADS = 4
SWA_HEAD_DIM = 64
SWA_GROUP = SWA_HEADS // SWA_KV_HEADS
SWA_WIDTH = SWA_HEADS * SWA_HEAD_DIM
WINDOW = 128
BLOCK = 128
ROPE_THETA = 10000.0
NORM_EPS = 1e-6
LOG2E = math.log2(math.e)
D_FF = 11008

LANES = 128
MXU_N = 256
MLA_TILE = 512
ROW_TILE = 256
FFN_TILE = 256
VT_ROWS = MLA_V + 16
VMEM_LIMIT = 60 * 1024 * 1024

COL_CQ = 0
COL_CKV = COL_CQ + Q_LORA
COL_QSWA = COL_CKV + KV_LORA
COL_KSWA = COL_QSWA + SWA_WIDTH
COL_VSWA = COL_KSWA + SWA_KV_HEADS * SWA_HEAD_DIM
COL_KR = COL_VSWA + SWA_KV_HEADS * SWA_HEAD_DIM
D_IN2 = COL_KR + 2 * MLA_ROPE

F32 = jnp.float32
BF16 = jnp.bfloat16


def _cparams(sem):
    return pltpu.CompilerParams(dimension_semantics=sem, vmem_limit_bytes=VMEM_LIMIT)


def _rms(xf, g):
    ms = jnp.mean(xf * xf, axis=-1, keepdims=True)
    return xf * lax.rsqrt(ms + NORM_EPS) * g


def _swap_half(w):
    d = w.shape[-1]
    return jnp.concatenate([-w[..., d // 2:], w[..., : d // 2]], axis=-1)


def _rope_tables(pos_col, invf_row):
    ang = pos_col.astype(F32) * invf_row
    return jnp.cos(ang), jnp.sin(ang)


def _col_chunks(n, width):
    return [(c, min(c + width, n)) for c in range(0, n, width)]


def _in_proj_kernel(x_ref, g_ref, w_ref, o_ref, *, chunks):
    h = _rms(x_ref[...], g_ref[...]).astype(BF16)
    for lo, hi in chunks:
        o_ref[:, lo:hi] = jnp.dot(h, w_ref[:, lo:hi], preferred_element_type=F32).astype(o_ref.dtype)


def in_proj(x, g, w, tm=ROW_TILE):
    m, k = x.shape
    n = w.shape[1]
    return pl.pallas_call(
        functools.partial(_in_proj_kernel, chunks=_col_chunks(n, 6 * MXU_N)),
        grid=(m // tm,),
        in_specs=[pl.BlockSpec((tm, k), lambda i: (i, 0)),
                  pl.BlockSpec((1, k), lambda i: (0, 0)),
                  pl.BlockSpec((k, n), lambda i: (0, 0), pipeline_mode=pl.Buffered(1))],
        out_specs=pl.BlockSpec((tm, n), lambda i: (i, 0)),
        out_shape=jax.ShapeDtypeStruct((m, n), BF16),
        compiler_params=_cparams(("parallel",)),
        name="in_proj",
    )(x, g, w)


def _q_up_kernel(cq_ref, g_ref, w_ref, pos_ref, invf_ref, qn_ref, qr_ref):
    h = _rms(cq_ref[...].astype(F32), g_ref[...]).astype(BF16)
    n_rope = MLA_HEADS * MLA_ROPE
    rot = jnp.dot(h, w_ref[:, :n_rope], preferred_element_type=F32)
    qn_ref[...] = jnp.dot(h, w_ref[:, n_rope:], preferred_element_type=F32).astype(BF16)
    cos, sin = _rope_tables(pos_ref[...], invf_ref[...])
    half = MLA_ROPE // 2
    lane = lax.broadcasted_iota(jnp.int32, cos.shape, 1)
    first_half = (lane % MLA_ROPE) < half
    for c in range(n_rope // LANES):
        a = rot[:, c * LANES:(c + 1) * LANES]
        b = jnp.where(first_half, -pltpu.roll(a, LANES - half, 1), pltpu.roll(a, half, 1))
        qr_ref[:, c * LANES:(c + 1) * LANES] = (a * cos + b * sin).astype(BF16)


def q_up(proj, g, w, pos_col, invf, tm=512):
    m = proj.shape[0]
    n_rope = MLA_HEADS * MLA_ROPE
    return pl.pallas_call(
        _q_up_kernel,
        grid=(m // tm,),
        in_specs=[pl.BlockSpec((tm, Q_LORA), lambda i: (i, COL_CQ // Q_LORA)),
                  pl.BlockSpec((1, Q_LORA), lambda i: (0, 0)),
                  pl.BlockSpec(w.shape, lambda i: (0, 0)),
                  pl.BlockSpec((tm, 1), lambda i: (i, 0)),
                  pl.BlockSpec((1, LANES), lambda i: (0, 0))],
        out_specs=[pl.BlockSpec((tm, MLA_WIDTH), lambda i: (i, 0)),
                   pl.BlockSpec((tm, n_rope), lambda i: (i, 0))],
        out_shape=[jax.ShapeDtypeStruct((m, MLA_WIDTH), BF16),
                   jax.ShapeDtypeStruct((m, n_rope), BF16)],
        compiler_params=_cparams(("parallel",)),
        name="q_up",
    )(proj, g, w, pos_col, invf)


def _kv_up_kernel(ckv_ref, g_ref, wk_ref, wvt_ref, krab_ref, pos_ref, invf_ref, kn_ref, vt_ref, kr_ref):
    h = _rms(ckv_ref[...].astype(F32), g_ref[...]).astype(BF16)
    kn_ref[...] = jnp.dot(h, wk_ref[...], preferred_element_type=F32).astype(BF16)
    vt = lax.dot_general(wvt_ref[...], h, (((1,), (1,)), ((), ())), preferred_element_type=F32)
    ones = jnp.ones((VT_ROWS - MLA_V, vt.shape[1]), BF16)
    for hh in range(MLA_HEADS):
        vt_ref[0, hh, 0, :MLA_V, :] = vt[hh * MLA_V:(hh + 1) * MLA_V, :].astype(BF16)
        vt_ref[0, hh, 0, MLA_V:, :] = ones
    cos, sin = _rope_tables(pos_ref[...], invf_ref[...])
    lane = lax.broadcasted_iota(jnp.int32, cos.shape, 1)
    lo = lane < MLA_ROPE
    prod = krab_ref[...].astype(F32) * jnp.where(lo, cos, sin)
    kk = prod + pltpu.roll(prod, MLA_ROPE, 1)
    zero = jnp.zeros_like(kk)
    kr_ref[:, :LANES] = jnp.where(lo, kk, zero).astype(BF16)
    kr_ref[:, LANES:] = jnp.where(lo, zero, kk).astype(BF16)


def kv_up(proj, g, wk, wvt, pos_col, invf, batch, seq, tm=MLA_TILE):
    m = proj.shape[0]
    nk = seq // tm
    return pl.pallas_call(
        _kv_up_kernel,
        grid=(m // tm,),
        in_specs=[pl.BlockSpec((tm, KV_LORA), lambda i: (i, COL_CKV // KV_LORA)),
                  pl.BlockSpec((1, KV_LORA), lambda i: (0, 0)),
                  pl.BlockSpec(wk.shape, lambda i: (0, 0)),
                  pl.BlockSpec(wvt.shape, lambda i: (0, 0)),
                  pl.BlockSpec((tm, LANES), lambda i: (i, COL_KR // LANES)),
                  pl.BlockSpec((tm, 1), lambda i: (i, 0)),
                  pl.BlockSpec((1, LANES), lambda i: (0, 0))],
        out_specs=[pl.BlockSpec((tm, MLA_WIDTH), lambda i: (i, 0)),
                   pl.BlockSpec((1, MLA_HEADS, 1, VT_ROWS, tm), lambda i: (i // nk, 0, i % nk, 0, 0)),
                   pl.BlockSpec((tm, 2 * LANES), lambda i: (i, 0))],
        out_shape=[jax.ShapeDtypeStruct((m, MLA_WIDTH), BF16),
                   jax.ShapeDtypeStruct((batch, MLA_HEADS, nk, VT_ROWS, tm), BF16),
                   jax.ShapeDtypeStruct((m, 2 * LANES), BF16)],
        compiler_params=_cparams(("parallel",)),
        name="kv_up",
    )(proj, g, wk, wvt, proj, pos_col, invf)


def _mla_kernel(*refs, tq, tk, n_cast):
    qn_ref, qr_ref, kn_ref, kr_ref, vt_ref = refs[:5]
    w_refs = refs[5:5 + n_cast]
    o_ref = refs[5 + n_cast]
    w16_refs = refs[6 + n_cast:6 + 2 * n_cast]
    s_a, s_b, mx_a, mx_b, m_sc, acc_sc = refs[6 + 2 * n_cast:]
    for w_ref, w16_ref in zip(w_refs, w16_refs):
        w16_ref[...] = w_ref[...].astype(BF16)

    qi = pl.program_id(2)
    q = jnp.concatenate([qn_ref[...], qr_ref[...]], axis=-1)
    m_sc[...] = jnp.full(m_sc.shape, -jnp.inf, F32)
    acc_sc[...] = jnp.zeros(acc_sc.shape, F32)

    n_chunk = tq // MXU_N
    blk_chunks = tk // MXU_N
    n_diag = tq // tk
    bufs = ((s_a, mx_a), (s_b, mx_b))

    def live_chunks(diag):
        return range(0 if diag is None else diag * blk_chunks, n_chunk)

    def scores(kb, buf, diag=None):
        s_ref, mx_ref = buf
        off = pl.multiple_of(kb * tk, tk)
        k = jnp.concatenate([kn_ref[pl.ds(off, tk), :], kr_ref[pl.ds(off, tk), :]], axis=-1)
        for c in live_chunks(diag):
            s = lax.dot_general(k, q[c * MXU_N:(c + 1) * MXU_N, :], (((1,), (1,)), ((), ())),
                                preferred_element_type=F32)
            if diag is not None and c < (diag + 1) * blk_chunks:
                key = lax.broadcasted_iota(jnp.int32, s.shape, 0) + diag * tk
                qry = lax.broadcasted_iota(jnp.int32, s.shape, 1) + c * MXU_N
                s = jnp.where(qry >= key, s, -jnp.inf)
            s_ref[c] = s
            mx_ref[:, c * MXU_N:(c + 1) * MXU_N] = jnp.max(s, axis=0, keepdims=True)

    def softmax_pv(kb, buf, diag=None, mask_here=False):
        s_ref, mx_ref = buf
        vt = vt_ref[0, 0, kb]
        for c in live_chunks(diag):
            sl = slice(c * MXU_N, (c + 1) * MXU_N)
            remask = mask_here and c < blk_chunks

            def s_chunk():
                s = s_ref[c]
                if remask:
                    key = lax.broadcasted_iota(jnp.int32, s.shape, 0)
                    qry = lax.broadcasted_iota(jnp.int32, s.shape, 1) + c * MXU_N
                    s = jnp.where(qry >= key, s, -jnp.inf)
                return s

            mx = jnp.max(s_chunk(), axis=0, keepdims=True) if remask else mx_ref[:, sl]
            m_old = m_sc[:, sl]
            m_new = jnp.maximum(m_old, mx)
            alpha = jnp.exp2(m_old - m_new)
            p = jnp.exp2(s_chunk() - m_new).astype(BF16)
            acc_sc[:, sl] = alpha * acc_sc[:, sl] + jnp.dot(vt, p, preferred_element_type=F32)
            m_sc[:, sl] = m_new

    n_full = n_diag * qi
    scores(0, bufs[0])

    def body(j, carry):
        scores(2 * j + 1, bufs[1])
        softmax_pv(2 * j, bufs[0])
        scores(2 * j + 2, bufs[0])
        softmax_pv(2 * j + 1, bufs[1])
        return carry

    lax.fori_loop(0, n_full // 2, body, 0)

    for d in range(n_diag):
        if d + 1 < n_diag:
            scores(n_full + d + 1, bufs[(d + 1) % 2], diag=d + 1)
        softmax_pv(n_full + d, bufs[d % 2], diag=d, mask_here=(d == 0))

    o_ref[...] = (acc_sc[:MLA_V, :] / acc_sc[MLA_V:MLA_V + 1, :]).T.astype(o_ref.dtype)


def mla_attention(qn, qr, kn, kr, vt, weights, batch, seq, tq=4 * MLA_TILE, tk=MLA_TILE):
    assert tq % (2 * tk) == 0
    nq = seq // tq
    steps = batch * MLA_HEADS * nq

    def slab(w):
        tiles = w.shape[0] // 16
        assert tiles * 16 == w.shape[0], w.shape
        n_slab = max(n for n in range(1, steps + 1) if tiles % n == 0)
        rows = w.shape[0] // n_slab
        return pl.BlockSpec((rows, w.shape[1]),
                            lambda b, h, i: (jnp.minimum((b * MLA_HEADS + h) * nq + i, n_slab - 1), 0))

    slabs = [slab(w) for w in weights]
    outs = pl.pallas_call(
        functools.partial(_mla_kernel, tq=tq, tk=tk, n_cast=len(weights)),
        grid=(batch, MLA_HEADS, nq),
        in_specs=[pl.BlockSpec((tq, MLA_NOPE), lambda b, h, i: (b * nq + i, h)),
                  pl.BlockSpec((tq, LANES), lambda b, h, i: (b * nq + i, h // 2)),
                  pl.BlockSpec((seq, MLA_NOPE), lambda b, h, i: (b, h)),
                  pl.BlockSpec((seq, LANES), lambda b, h, i: (b, h % 2)),
                  pl.BlockSpec((1, 1, seq // tk, VT_ROWS, tk), lambda b, h, i: (b, h, 0, 0, 0))] + slabs,
        out_specs=[pl.BlockSpec((tq, MLA_V), lambda b, h, i: (b * nq + i, h))] + slabs,
        out_shape=[jax.ShapeDtypeStruct((batch * seq, MLA_WIDTH), BF16)]
        + [jax.ShapeDtypeStruct(w.shape, BF16) for w in weights],
        scratch_shapes=[pltpu.VMEM((tq // MXU_N, tk, MXU_N), F32), pltpu.VMEM((tq // MXU_N, tk, MXU_N), F32),
                        pltpu.VMEM((1, tq), F32), pltpu.VMEM((1, tq), F32),
                        pltpu.VMEM((1, tq), F32), pltpu.VMEM((VT_ROWS, tq), F32)],
        compiler_params=_cparams(("parallel", "parallel", "parallel")),
        name="mla_attention",
    )(qn, qr, kn, kr, vt, *weights)
    return outs[0], outs[1:]


def _swa_kernel(sinks_ref, q0_ref, q1_ref, q2_ref, q3_ref, kc_ref, kp_ref, vc_ref, vp_ref,
                pq_ref, pkc_ref, pkp_ref, o_ref):
    blk = pl.program_id(1)
    q_refs = (q0_ref, q1_ref, q2_ref, q3_ref)
    k_all = jnp.concatenate([kp_ref[...], kc_ref[...]], axis=0).astype(F32)
    v_all = jnp.concatenate([vp_ref[...], vc_ref[...]], axis=0).astype(F32)
    pk = jnp.concatenate([pkp_ref[0], pkc_ref[0]], axis=-1).astype(F32)
    dist = jnp.abs(pq_ref[...].astype(F32) - pk)
    row = lax.broadcasted_iota(jnp.int32, dist.shape, 0)
    col = lax.broadcasted_iota(jnp.int32, dist.shape, 1)
    rel = BLOCK + row - col
    first_key = jnp.where(blk > 0, 0, BLOCK)
    valid = (rel >= 0) & (rel < WINDOW) & (col >= first_key)
    dist_w = jnp.where(valid, dist, jnp.inf)
    lane =lax.broadcasted_iota(jnp.int32, (2 * BLOCK, LANES), 1)
    lo = lane < SWA_HEAD_DIM
    lane_o = lax.broadcasted_iota(jnp.int32, (BLOCK, LANES), 1)
    lo_o = lane_o < SWA_HEAD_DIM

    def dup(x_all, g):
        pair = x_all[:, (g // 2) * LANES:(g // 2 + 1) * LANES]
        rolled = pltpu.roll(pair, SWA_HEAD_DIM, 1)
        return jnp.where(lo, pair, rolled) if g % 2 == 0 else jnp.where(lo, rolled, pair)

    for g in range(SWA_KV_HEADS):
        kk = dup(k_all, g)
        vv = dup(v_all, g).astype(BF16)
        zero = jnp.zeros_like(kk)
        k_half = (jnp.where(lo, kk, zero).astype(BF16), jnp.where(lo, zero, kk).astype(BF16))
        for i in range(SWA_GROUP // 2):
            q_pair = q_refs[g][:, i * LANES:(i + 1) * LANES]
            outs = []
            for j in range(2):
                head = g * SWA_GROUP + 2 * i + j
                slope2 = LOG2E * 2.0 ** (-8.0 * (head + 1) / SWA_HEADS)
                s = lax.dot_general(q_pair, k_half[j], (((1,), (1,)), ((), ())),
                                    preferred_element_type=F32)
                s = s - slope2 * dist_w
                sink2 = LOG2E * sinks_ref[head]
                m = jnp.maximum(jnp.max(s, axis=-1, keepdims=True), sink2)
                e = jnp.exp2(s - m)
                denom = jnp.sum(e, axis=-1, keepdims=True) + jnp.exp2(sink2 - m)
                pv = jnp.dot(e.astype(BF16), vv, preferred_element_type=F32)
                outs.append(pv / denom)
            col0 = (g * SWA_GROUP // 2 + i) * LANES
            o_ref[:, col0:col0 + LANES] = jnp.where(lo_o, outs[0], outs[1]).astype(o_ref.dtype)


def swa_attention(proj, sinks, pos_col, pos_row, batch, seq):
    nblk = seq // BLOCK
    kvw = SWA_KV_HEADS * SWA_HEAD_DIM
    gw = SWA_GROUP * SWA_HEAD_DIM
    cur = lambda b, n: b * nblk + n
    prev = lambda b, n: b * nblk + jnp.maximum(n - 1, 0)
    q_specs = [pl.BlockSpec((BLOCK, gw), functools.partial(lambda b, n, g: (cur(b, n), COL_QSWA // gw + g), g=g))
               for g in range(SWA_KV_HEADS)]
    return pl.pallas_call(
        _swa_kernel,
        grid=(batch, nblk),
        in_specs=[pl.BlockSpec(memory_space=pltpu.SMEM)] + q_specs + [
            pl.BlockSpec((BLOCK, kvw), lambda b, n: (cur(b, n), COL_KSWA // kvw)),
            pl.BlockSpec((BLOCK, kvw), lambda b, n: (prev(b, n), COL_KSWA // kvw)),
            pl.BlockSpec((BLOCK, kvw), lambda b, n: (cur(b, n), COL_VSWA // kvw)),
            pl.BlockSpec((BLOCK, kvw), lambda b, n: (prev(b, n), COL_VSWA // kvw)),
            pl.BlockSpec((BLOCK, 1), lambda b, n: (cur(b, n), 0)),
            pl.BlockSpec((1, 1, BLOCK), lambda b, n: (cur(b, n), 0, 0)),
            pl.BlockSpec((1, 1, BLOCK), lambda b, n: (prev(b, n), 0, 0))],
        out_specs=pl.BlockSpec((BLOCK, SWA_WIDTH), lambda b, n: (cur(b, n), 0)),
        out_shape=jax.ShapeDtypeStruct((batch * seq, SWA_WIDTH), BF16),
        compiler_params=_cparams(("parallel", "parallel")),
        name="swa_attention",
    )(sinks, proj, proj, proj, proj, proj, proj, proj, proj, pos_col, pos_row, pos_row)


def _wo_kernel(oa_ref, ob_ref, ga_ref, gb_ref, w_ref, y_ref, *, chunks):
    ka = oa_ref.shape[1]
    na = _rms(oa_ref[...].astype(F32), ga_ref[...]).astype(BF16)
    nb = _rms(ob_ref[...].astype(F32), gb_ref[...]).astype(BF16)
    for lo, hi in chunks:
        y_ref[:, lo:hi] = (jnp.dot(na, w_ref[:ka, lo:hi], preferred_element_type=F32)
                           + jnp.dot(nb, w_ref[ka:, lo:hi], preferred_element_type=F32)).astype(y_ref.dtype)


def out_proj(oa, ob, ga, gb, w, tm=ROW_TILE):
    m, ka = oa.shape
    kb = ob.shape[1]
    n = w.shape[1]
    return pl.pallas_call(
        functools.partial(_wo_kernel, chunks=_col_chunks(n, 4 * MXU_N)),
        grid=(m // tm,),
        in_specs=[pl.BlockSpec((tm, ka), lambda i: (i, 0)),
                  pl.BlockSpec((tm, kb), lambda i: (i, 0)),
                  pl.BlockSpec((1, ka), lambda i: (0, 0)),
                  pl.BlockSpec((1, kb), lambda i: (0, 0)),
                  pl.BlockSpec((ka + kb, n), lambda i: (0, 0), pipeline_mode=pl.Buffered(1))],
        out_specs=pl.BlockSpec((tm, n), lambda i: (i, 0)),
        out_shape=jax.ShapeDtypeStruct((m, n), BF16),
        compiler_params=_cparams(("parallel",)),
        name="out_proj",
    )(oa, ob, ga, gb, w)


def _resid_kernel(y_ref, x_ref, gy_ref, gn_ref, xo_ref, ho_ref):
    xn = x_ref[...] + _rms(y_ref[...].astype(F32), gy_ref[...])
    xo_ref[...] = xn
    ho_ref[...] = _rms(xn, gn_ref[...]).astype(ho_ref.dtype)


def resid_norm(y, x, gy, gn, tm=256):
    m, d = x.shape
    row = pl.BlockSpec((tm, d), lambda i: (i, 0))
    vec = pl.BlockSpec((1, d), lambda i: (0, 0))
    return pl.pallas_call(
        _resid_kernel,
        grid=(m // tm,),
        in_specs=[row, row, vec, vec],
        out_specs=[row, row],
        out_shape=[jax.ShapeDtypeStruct((m, d), F32), jax.ShapeDtypeStruct((m, d), BF16)],
        compiler_params=_cparams(("parallel",)),
        name="resid_norm",
    )(y, x, gy, gn)


def _ffn_kernel(h_ref, wg_ref, wu_ref, wd_ref, x_ref, gp_ref, o_ref, *, dn, rows):
    j = pl.program_id(1)
    tm, d = o_ref.shape

    @pl.when(j == 0)
    def _():
        o_ref[...] = jnp.zeros(o_ref.shape, o_ref.dtype)

    h = h_ref[...]
    g = jnp.dot(h, wg_ref[...], preferred_element_type=F32)
    u = jnp.dot(h, wu_ref[...], preferred_element_type=F32)
    a = (g * jax.nn.sigmoid(g) * u).astype(BF16)
    for c in range(d // dn):
        o_ref[:, c * dn:(c + 1) * dn] += jnp.dot(a, wd_ref[:, c * dn:(c + 1) * dn],
                                                 preferred_element_type=F32)

    @pl.when(j == pl.num_programs(1) - 1)
    def _():
        def body(r, carry):
            rs = pl.ds(pl.multiple_of(r * rows, rows), rows)
            o_ref[rs, :] = x_ref[rs, :] + _rms(o_ref[rs, :], gp_ref[...])
            return carry

        lax.fori_loop(0, tm // rows, body, 0)


def ffn(h, wg, wu, wd, x, g_post, tm=1024, tf=FFN_TILE):
    m, d = h.shape
    f = wg.shape[1]
    once = dict(pipeline_mode=pl.Buffered(1))
    return pl.pallas_call(
        functools.partial(_ffn_kernel, dn=1024, rows=32),
        grid=(m // tm, f // tf),
        in_specs=[pl.BlockSpec((tm, d), lambda i, j: (i, 0), **once),
                  pl.BlockSpec((d, tf), lambda i, j: (0, j)),
                  pl.BlockSpec((d, tf), lambda i, j: (0, j)),
                  pl.BlockSpec((tf, d), lambda i, j: (j, 0)),
                  pl.BlockSpec((tm, d), lambda i, j: (i, 0), **once),
                  pl.BlockSpec((1, d), lambda i, j: (0, 0))],
        out_specs=pl.BlockSpec((tm, d), lambda i, j: (i, 0), **once),
        out_shape=jax.ShapeDtypeStruct((m, d), F32),
        compiler_params=_cparams(("parallel", "arbitrary")),
        name="ffn",
    )(h, wg, wu, wd, x, g_post)


def _ple_kernel(x_ref, wg_ref, p_ref, wp_ref, o_ref, *, chunks):
    xb = x_ref[...].astype(BF16)
    pb = p_ref[...].astype(BF16)
    for lo, hi in chunks:
        gate = jax.nn.sigmoid(jnp.dot(xb, wg_ref[:, lo:hi], preferred_element_type=F32))
        e = jnp.dot(pb, wp_ref[:, lo:hi], preferred_element_type=F32)
        o_ref[:, lo:hi] = x_ref[:, lo:hi] + gate * e


def ple(x, wg, p, wp, tm=ROW_TILE):
    m, d = x.shape
    n = wg.shape[1]
    pd = p.shape[1]
    once = dict(pipeline_mode=pl.Buffered(1))
    return pl.pallas_call(
        functools.partial(_ple_kernel, chunks=_col_chunks(n, 4 * MXU_N)),
        grid=(m // tm,),
        in_specs=[pl.BlockSpec((tm, d), lambda i: (i, 0)),
                  pl.BlockSpec((d, n), lambda i: (0, 0), **once),
                  pl.BlockSpec((tm, pd), lambda i: (i, 0)),
                  pl.BlockSpec((pd, n), lambda i: (0, 0), **once)],
        out_specs=pl.BlockSpec((tm, n), lambda i: (i, 0)),
        out_shape=jax.ShapeDtypeStruct((m, n), F32),
        compiler_params=_cparams(("parallel",)),
        name="ple",
    )(x, wg, p, wp)


def _prep_w_in(w):
    c_q, c_kv, k_r, q_s, k_s, v_s = jnp.split(
        w, [Q_LORA, Q_LORA + KV_LORA, Q_LORA + KV_LORA + MLA_ROPE,
            Q_LORA + KV_LORA + MLA_ROPE + SWA_WIDTH,
            Q_LORA + KV_LORA + MLA_ROPE + SWA_WIDTH + SWA_KV_HEADS * SWA_HEAD_DIM], axis=1)
    q_s = q_s * (LOG2E / math.sqrt(SWA_HEAD_DIM))
    parts = [c_q, c_kv, q_s, k_s, v_s, k_r, _swap_half(k_r)]
    return jnp.concatenate([part.astype(BF16) for part in parts], axis=1)


def _prep_w_q_up(w):
    w = (w * (LOG2E / math.sqrt(MLA_NOPE + MLA_ROPE))).reshape(Q_LORA, MLA_HEADS, MLA_NOPE + MLA_ROPE)
    nope = w[:, :, :MLA_NOPE].reshape(Q_LORA, MLA_WIDTH)
    rope = w[:, :, MLA_NOPE:].reshape(Q_LORA, MLA_HEADS * MLA_ROPE)
    return jnp.concatenate([rope, nope], axis=1).astype(BF16)


def _prep_w_kv_up(w):
    w = w.reshape(KV_LORA, MLA_HEADS, MLA_NOPE + MLA_V)
    wk = w[:, :, :MLA_NOPE].reshape(KV_LORA, MLA_WIDTH).astype(BF16)
    wvt = w[:, :, MLA_NOPE:].reshape(KV_LORA, MLA_WIDTH).T.astype(BF16)
    return wk, wvt


def _layer(x, p, pos_col, pos_row, invf, batch, seq, attn_pre_norm, w_in, q_a_norm, w_q_up,
           kv_a_norm, w_kv_up, sinks, mla_out_norm, swa_out_norm, w_o, attn_post_norm,
           ffn_pre_norm, w_gate, w_up, w_down, ffn_post_norm, w_ple_gate, w_ple_proj):
    row = lambda g: g.reshape(1, -1)
    proj = in_proj(x, row(attn_pre_norm), _prep_w_in(w_in))
    qn, qr = q_up(proj, row(q_a_norm), _prep_w_q_up(w_q_up), pos_col, invf)
    wk, wvt = _prep_w_kv_up(w_kv_up)
    kn, vt, kr = kv_up(proj, row(kv_a_norm), wk, wvt, pos_col, invf, batch, seq)
    o_mla, (wo16, wg16, wu16, wd16, wpg16) = mla_attention(
        qn, qr, kn, kr, vt, [w_o, w_gate, w_up, w_down, w_ple_gate], batch, seq)
    o_swa = swa_attention(proj, sinks, pos_col, pos_row, batch, seq)
    y = out_proj(o_mla, o_swa, row(mla_out_norm), row(swa_out_norm), wo16)
    x1, h2 = resid_norm(y, x, row(attn_post_norm), row(ffn_pre_norm))
    x2 = ffn(h2, wg16, wu16, wd16, x1, row(ffn_post_norm))
    return ple(x2, wpg16, p, w_ple_proj.astype(BF16))


def kernel(x, p, positions, attn_pre_norm, w_in, q_a_norm, w_q_up, kv_a_norm, w_kv_up, sinks,
           mla_out_norm, swa_out_norm, w_o, attn_post_norm, ffn_pre_norm, w_gate, w_up, w_down,
           ffn_post_norm, w_ple_gate, w_ple_proj):
    batch, seq, d = x.shape
    depth = w_in.shape[0]
    t = batch * seq
    pos_col = positions.reshape(t, 1)
    pos_row = positions.reshape(t // BLOCK, 1, BLOCK)
    half = MLA_ROPE // 2
    invf = np.asarray(ROPE_THETA ** (-(np.arange(LANES) % half) * 2.0 / MLA_ROPE), np.float32).reshape(1, LANES)
    invf = jnp.asarray(invf)
    xf = x.reshape(t, d)
    for i in range(depth):
        xf = _layer(xf, p[i].reshape(t, PLE_DIM), pos_col, pos_row, invf, batch, seq,
                    attn_pre_norm[i], w_in[i], q_a_norm[i], w_q_up[i], kv_a_norm[i], w_kv_up[i],
                    sinks[i], mla_out_norm[i], swa_out_norm[i], w_o[i], attn_post_norm[i],
                    ffn_pre_norm[i], w_gate[i], w_up[i], w_down[i], ffn_post_norm[i],
                    w_ple_gate[i], w_ple_proj[i])
    return xf.reshape(batch, seq, d)
```

```python
import functools
import math

import numpy as np
import jax
import jax.numpy as jnp
from jax import lax
from jax.experimental import pallas as pl
from jax.experimental.pallas import tpu as pltpu

D_MODEL = 4096
PLE_DIM = 256
MLA_HEADS = 16
MLA_NOPE = 128
MLA_ROPE = 64
MLA_V = 128
Q_LORA = 1024
KV_LORA = 512
MLA_WIDTH = MLA_HEADS * MLA_V
SWA_HEADS = 32
SWA_KV_HEADS = 4
SWA_HEAD_DIM = 64
SWA_GROUP = SWA_HEADS // SWA_KV_HEADS
SWA_WIDTH = SWA_HEADS * SWA_HEAD_DIM
WINDOW = 128
BLOCK = 128
ROPE_THETA = 10000.0
NORM_EPS = 1e-6
LOG2E = math.log2(math.e)
D_FF = 11008

LANES = 128
MXU_N = 256
MLA_TILE = 512
ROW_TILE = 256
FFN_TILE = 256
VT_ROWS = MLA_V + 16
VMEM_LIMIT = 60 * 1024 * 1024

COL_CQ = 0
COL_CKV = COL_CQ + Q_LORA
COL_QSWA = COL_CKV + KV_LORA
COL_KSWA = COL_QSWA + SWA_WIDTH
COL_VSWA = COL_KSWA + SWA_KV_HEADS * SWA_HEAD_DIM
COL_KR = COL_VSWA + SWA_KV_HEADS * SWA_HEAD_DIM
D_IN2 = COL_KR + 2 * MLA_ROPE

F32 = jnp.float32
BF16 = jnp.bfloat16


def _cparams(sem):
    return pltpu.CompilerParams(dimension_semantics=sem, vmem_limit_bytes=VMEM_LIMIT)


def _rms(xf, g):
    ms = jnp.mean(xf * xf, axis=-1, keepdims=True)
    return xf * lax.rsqrt(ms + NORM_EPS) * g


def _swap_half(w):
    d = w.shape[-1]
    return jnp.concatenate([-w[..., d // 2:], w[..., : d // 2]], axis=-1)


def _rope_tables(pos_col, invf_row):
    ang = pos_col.astype(F32) * invf_row
    return jnp.cos(ang), jnp.sin(ang)


def _col_chunks(n, width):
    return [(c, min(c + width, n)) for c in range(0, n, width)]


def _in_proj_kernel(x_ref, g_ref, w_ref, o_ref, *, chunks):
    h = _rms(x_ref[...], g_ref[...]).astype(BF16)
    for lo, hi in chunks:
        o_ref[:, lo:hi] = jnp.dot(h, w_ref[:, lo:hi], preferred_element_type=F32).astype(o_ref.dtype)


def in_proj(x, g, w, tm=ROW_TILE):
    m, k = x.shape
    n = w.shape[1]
    return pl.pallas_call(
        functools.partial(_in_proj_kernel, chunks=_col_chunks(n, 6 * MXU_N)),
        grid=(m // tm,),
        in_specs=[pl.BlockSpec((tm, k), lambda i: (i, 0)),
                  pl.BlockSpec((1, k), lambda i: (0, 0)),
                  pl.BlockSpec((k, n), lambda i: (0, 0), pipeline_mode=pl.Buffered(1))],
        out_specs=pl.BlockSpec((tm, n), lambda i: (i, 0)),
        out_shape=jax.ShapeDtypeStruct((m, n), BF16),
        compiler_params=_cparams(("parallel",)),
        name="in_proj",
    )(x, g, w)


def _q_up_kernel(cq_ref, g_ref, w_ref, pos_ref, invf_ref, qn_ref, qr_ref):
    h = _rms(cq_ref[...].astype(F32), g_ref[...]).astype(BF16)
    n_rope = MLA_HEADS * MLA_ROPE
    rot = jnp.dot(h, w_ref[:, :n_rope], preferred_element_type=F32)
    qn_ref[...] = jnp.dot(h, w_ref[:, n_rope:], preferred_element_type=F32).astype(BF16)
    cos, sin = _rope_tables(pos_ref[...], invf_ref[...])
    half = MLA_ROPE // 2
    lane = lax.broadcasted_iota(jnp.int32, cos.shape, 1)
    first_half = (lane % MLA_ROPE) < half
    for c in range(n_rope // LANES):
        a = rot[:, c * LANES:(c + 1) * LANES]
        b = jnp.where(first_half, -pltpu.roll(a, LANES - half, 1), pltpu.roll(a, half, 1))
        qr_ref[:, c * LANES:(c + 1) * LANES] = (a * cos + b * sin).astype(BF16)


def q_up(proj, g, w, pos_col, invf, tm=512):
    m = proj.shape[0]
    n_rope = MLA_HEADS * MLA_ROPE
    return pl.pallas_call(
        _q_up_kernel,
        grid=(m // tm,),
        in_specs=[pl.BlockSpec((tm, Q_LORA), lambda i: (i, COL_CQ // Q_LORA)),
                  pl.BlockSpec((1, Q_LORA), lambda i: (0, 0)),
                  pl.BlockSpec(w.shape, lambda i: (0, 0)),
                  pl.BlockSpec((tm, 1), lambda i: (i, 0)),
                  pl.BlockSpec((1, LANES), lambda i: (0, 0))],
        out_specs=[pl.BlockSpec((tm, MLA_WIDTH), lambda i: (i, 0)),
                   pl.BlockSpec((tm, n_rope), lambda i: (i, 0))],
        out_shape=[jax.ShapeDtypeStruct((m, MLA_WIDTH), BF16),
                   jax.ShapeDtypeStruct((m, n_rope), BF16)],
        compiler_params=_cparams(("parallel",)),
        name="q_up",
    )(proj, g, w, pos_col, invf)


def _kv_up_kernel(ckv_ref, g_ref, wk_ref, wvt_ref, krab_ref, pos_ref, invf_ref, kn_ref, vt_ref, kr_ref):
    h = _rms(ckv_ref[...].astype(F32), g_ref[...]).astype(BF16)
    kn_ref[...] = jnp.dot(h, wk_ref[...], preferred_element_type=F32).astype(BF16)
    vt = lax.dot_general(wvt_ref[...], h, (((1,), (1,)), ((), ())), preferred_element_type=F32)
    ones = jnp.ones((VT_ROWS - MLA_V, vt.shape[1]), BF16)
    for hh in range(MLA_HEADS):
        vt_ref[0, hh, 0, :MLA_V, :] = vt[hh * MLA_V:(hh + 1) * MLA_V, :].astype(BF16)
        vt_ref[0, hh, 0, MLA_V:, :] = ones
    cos, sin = _rope_tables(pos_ref[...], invf_ref[...])
    lane = lax.broadcasted_iota(jnp.int32, cos.shape, 1)
    lo = lane < MLA_ROPE
    prod = krab_ref[...].astype(F32) * jnp.where(lo, cos, sin)
    kk = prod + pltpu.roll(prod, MLA_ROPE, 1)
    zero = jnp.zeros_like(kk)
    kr_ref[:, :LANES] = jnp.where(lo, kk, zero).astype(BF16)
    kr_ref[:, LANES:] = jnp.where(lo, zero, kk).astype(BF16)


def kv_up(proj, g, wk, wvt, pos_col, invf, batch, seq, tm=MLA_TILE):
    m = proj.shape[0]
    nk = seq // tm
    return pl.pallas_call(
        _kv_up_kernel,
        grid=(m // tm,),
        in_specs=[pl.BlockSpec((tm, KV_LORA), lambda i: (i, COL_CKV // KV_LORA)),
                  pl.BlockSpec((1, KV_LORA), lambda i: (0, 0)),
                  pl.BlockSpec(wk.shape, lambda i: (0, 0)),
                  pl.BlockSpec(wvt.shape, lambda i: (0, 0)),
                  pl.BlockSpec((tm, LANES), lambda i: (i, COL_KR // LANES)),
                  pl.BlockSpec((tm, 1), lambda i: (i, 0)),
                  pl.BlockSpec((1, LANES), lambda i: (0, 0))],
        out_specs=[pl.BlockSpec((tm, MLA_WIDTH), lambda i: (i, 0)),
                   pl.BlockSpec((1, MLA_HEADS, 1, VT_ROWS, tm), lambda i: (i // nk, 0, i % nk, 0, 0)),
                   pl.BlockSpec((tm, 2 * LANES), lambda i: (i, 0))],
        out_shape=[jax.ShapeDtypeStruct((m, MLA_WIDTH), BF16),
                   jax.ShapeDtypeStruct((batch, MLA_HEADS, nk, VT_ROWS, tm), BF16),
                   jax.ShapeDtypeStruct((m, 2 * LANES), BF16)],
        compiler_params=_cparams(("parallel",)),
        name="kv_up",
    )(proj, g, wk, wvt, proj, pos_col, invf)


def _mla_kernel(*refs, tq, tk, n_cast):
    qn_ref, qr_ref, kn_ref, kr_ref, vt_ref = refs[:5]
    w_refs = refs[5:5 + n_cast]
    o_ref = refs[5 + n_cast]
    w16_refs = refs[6 + n_cast:6 + 2 * n_cast]
    s_a, s_b, mx_a, mx_b, m_sc, acc_sc = refs[6 + 2 * n_cast:]
    for w_ref, w16_ref in zip(w_refs, w16_refs):
        w16_ref[...] = w_ref[...].astype(BF16)

    qi = pl.program_id(2)
    q = jnp.concatenate([qn_ref[...], qr_ref[...]], axis=-1)
    m_sc[...] = jnp.full(m_sc.shape, -jnp.inf, F32)
    acc_sc[...] = jnp.zeros(acc_sc.shape, F32)

    n_chunk = tq // MXU_N
    blk_chunks = tk // MXU_N
    n_diag = tq // tk
    bufs = ((s_a, mx_a), (s_b, mx_b))

    def live_chunks(diag):
        return range(0 if diag is None else diag * blk_chunks, n_chunk)

    def scores(kb, buf, diag=None):
        s_ref, mx_ref = buf
        off = pl.multiple_of(kb * tk, tk)
        k = jnp.concatenate([kn_ref[pl.ds(off, tk), :], kr_ref[pl.ds(off, tk), :]], axis=-1)
        for c in live_chunks(diag):
            s = lax.dot_general(k, q[c * MXU_N:(c + 1) * MXU_N, :], (((1,), (1,)), ((), ())),
                                preferred_element_type=F32)
            if diag is not None and c < (diag + 1) * blk_chunks:
                key = lax.broadcasted_iota(jnp.int32, s.shape, 0) + diag * tk
                qry = lax.broadcasted_iota(jnp.int32, s.shape, 1) + c * MXU_N
                s = jnp.where(qry >= key, s, -jnp.inf)
            s_ref[c] = s
            mx_ref[:, c * MXU_N:(c + 1) * MXU_N] = jnp.max(s, axis=0, keepdims=True)

    def softmax_pv(kb, buf, diag=None, mask_here=False):
        s_ref, mx_ref = buf
        vt = vt_ref[0, 0, kb]
        for c in live_chunks(diag):
            sl = slice(c * MXU_N, (c + 1) * MXU_N)
            remask = mask_here and c < blk_chunks

            def s_chunk():
                s = s_ref[c]
                if remask:
                    key = lax.broadcasted_iota(jnp.int32, s.shape, 0)
                    qry = lax.broadcasted_iota(jnp.int32, s.shape, 1) + c * MXU_N
                    s = jnp.where(qry >= key, s, -jnp.inf)
                return s

            mx = jnp.max(s_chunk(), axis=0, keepdims=True) if remask else mx_ref[:, sl]
            m_old = m_sc[:, sl]
            m_new = jnp.maximum(m_old, mx)
            alpha = jnp.exp2(m_old - m_new)
            p = jnp.exp2(s_chunk() - m_new).astype(BF16)
            acc_sc[:, sl] = alpha * acc_sc[:, sl] + jnp.dot(vt, p, preferred_element_type=F32)
            m_sc[:, sl] = m_new

    n_full = n_diag * qi
    scores(0, bufs[0])

    def body(j, carry):
        scores(2 * j + 1, bufs[1])
        softmax_pv(2 * j, bufs[0])
        scores(2 * j + 2, bufs[0])
        softmax_pv(2 * j + 1, bufs[1])
        return carry

    lax.fori_loop(0, n_full // 2, body, 0)

    for d in range(n_diag):
        if d + 1 < n_diag:
            scores(n_full + d + 1, bufs[(d + 1) % 2], diag=d + 1)
        softmax_pv(n_full + d, bufs[d % 2], diag=d, mask_here=(d == 0))

    o_ref[...] = (acc_sc[:MLA_V, :] / acc_sc[MLA_V:MLA_V + 1, :]).T.astype(o_ref.dtype)


def mla_attention(qn, qr, kn, kr, vt, weights, batch, seq, tq=4 * MLA_TILE, tk=MLA_TILE):
    assert tq % (2 * tk) == 0
    nq = seq // tq
    steps = batch * MLA_HEADS * nq

    def slab(w):
        tiles = w.shape[0] // 16
        assert tiles * 16 == w.shape[0], w.shape
        n_slab = max(n for n in range(1, steps + 1) if tiles % n == 0)
        rows = w.shape[0] // n_slab
        return pl.BlockSpec((rows, w.shape[1]),
                            lambda b, h, i: (jnp.minimum((b * MLA_HEADS + h) * nq + i, n_slab - 1), 0))

    slabs = [slab(w) for w in weights]
    outs = pl.pallas_call(
        functools.partial(_mla_kernel, tq=tq, tk=tk, n_cast=len(weights)),
        grid=(batch, MLA_HEADS, nq),
        in_specs=[pl.BlockSpec((tq, MLA_NOPE), lambda b, h, i: (b * nq + i, h)),
                  pl.BlockSpec((tq, LANES), lambda b, h, i: (b * nq + i, h // 2)),
                  pl.BlockSpec((seq, MLA_NOPE), lambda b, h, i: (b, h)),
                  pl.BlockSpec((seq, LANES), lambda b, h, i: (b, h % 2)),
                  pl.BlockSpec((1, 1, seq // tk, VT_ROWS, tk), lambda b, h, i: (b, h, 0, 0, 0))] + slabs,
        out_specs=[pl.BlockSpec((tq, MLA_V), lambda b, h, i: (b * nq + i, h))] + slabs,
        out_shape=[jax.ShapeDtypeStruct((batch * seq, MLA_WIDTH), BF16)]
        + [jax.ShapeDtypeStruct(w.shape, BF16) for w in weights],
        scratch_shapes=[pltpu.VMEM((tq // MXU_N, tk, MXU_N), F32), pltpu.VMEM((tq // MXU_N, tk, MXU_N), F32),
                        pltpu.VMEM((1, tq), F32), pltpu.VMEM((1, tq), F32),
                        pltpu.VMEM((1, tq), F32), pltpu.VMEM((VT_ROWS, tq), F32)],
        compiler_params=_cparams(("parallel", "parallel", "parallel")),
        name="mla_attention",
    )(qn, qr, kn, kr, vt, *weights)
    return outs[0], outs[1:]


def _swa_kernel(sinks_ref, q0_ref, q1_ref, q2_ref, q3_ref, kc_ref, kp_ref, vc_ref, vp_ref,
                pq_ref, pkc_ref, pkp_ref, o_ref):
    blk = pl.program_id(1)
    q_refs = (q0_ref, q1_ref, q2_ref, q3_ref)
    k_all = jnp.concatenate([kp_ref[...], kc_ref[...]], axis=0).astype(F32)
    v_all = jnp.concatenate([vp_ref[...], vc_ref[...]], axis=0).astype(F32)
    pk = jnp.concatenate([pkp_ref[0], pkc_ref[0]], axis=-1).astype(F32)
    dist = jnp.abs(pq_ref[...].astype(F32) - pk)
    row = lax.broadcasted_iota(jnp.int32, dist.shape, 0)
    col = lax.broadcasted_iota(jnp.int32, dist.shape, 1)
    rel = BLOCK + row - col
    first_key = jnp.where(blk > 0, 0, BLOCK)
    valid = (rel >= 0) & (rel < WINDOW) & (col >= first_key)
    dist_w = jnp.where(valid, dist, jnp.inf)
    lane =lax.broadcasted_iota(jnp.int32, (2 * BLOCK, LANES), 1)
    lo = lane < SWA_HEAD_DIM
    lane_o = lax.broadcasted_iota(jnp.int32, (BLOCK, LANES), 1)
    lo_o = lane_o < SWA_HEAD_DIM

    def dup(x_all, g):
        pair = x_all[:, (g // 2) * LANES:(g // 2 + 1) * LANES]
        rolled = pltpu.roll(pair, SWA_HEAD_DIM, 1)
        return jnp.where(lo, pair, rolled) if g % 2 == 0 else jnp.where(lo, rolled, pair)

    for g in range(SWA_KV_HEADS):
        kk = dup(k_all, g)
        vv = dup(v_all, g).astype(BF16)
        zero = jnp.zeros_like(kk)
        k_half = (jnp.where(lo, kk, zero).astype(BF16), jnp.where(lo, zero, kk).astype(BF16))
        for i in range(SWA_GROUP // 2):
            q_pair = q_refs[g][:, i * LANES:(i + 1) * LANES]
            outs = []
            for j in range(2):
                head = g * SWA_GROUP + 2 * i + j
                slope2 = LOG2E * 2.0 ** (-8.0 * (head + 1) / SWA_HEADS)
                s = lax.dot_general(q_pair, k_half[j], (((1,), (1,)), ((), ())),
                                    preferred_element_type=F32)
                s = s - slope2 * dist_w
                sink2 = LOG2E * sinks_ref[head]
                m = jnp.maximum(jnp.max(s, axis=-1, keepdims=True), sink2)
                e = jnp.exp2(s - m)
                denom = jnp.sum(e, axis=-1, keepdims=True) + jnp.exp2(sink2 - m)
                pv = jnp.dot(e.astype(BF16), vv, preferred_element_type=F32)
                outs.append(pv / denom)
            col0 = (g * SWA_GROUP // 2 + i) * LANES
            o_ref[:, col0:col0 + LANES] = jnp.where(lo_o, outs[0], outs[1]).astype(o_ref.dtype)


def swa_attention(proj, sinks, pos_col, pos_row, batch, seq):
    nblk = seq // BLOCK
    kvw = SWA_KV_HEADS * SWA_HEAD_DIM
    gw = SWA_GROUP * SWA_HEAD_DIM
    cur = lambda b, n: b * nblk + n
    prev = lambda b, n: b * nblk + jnp.maximum(n - 1, 0)
    q_specs = [pl.BlockSpec((BLOCK, gw), functools.partial(lambda b, n, g: (cur(b, n), COL_QSWA // gw + g), g=g))
               for g in range(SWA_KV_HEADS)]
    return pl.pallas_call(
        _swa_kernel,
        grid=(batch, nblk),
        in_specs=[pl.BlockSpec(memory_space=pltpu.SMEM)] + q_specs + [
            pl.BlockSpec((BLOCK, kvw), lambda b, n: (cur(b, n), COL_KSWA // kvw)),
            pl.BlockSpec((BLOCK, kvw), lambda b, n: (prev(b, n), COL_KSWA // kvw)),
            pl.BlockSpec((BLOCK, kvw), lambda b, n: (cur(b, n), COL_VSWA // kvw)),
            pl.BlockSpec((BLOCK, kvw), lambda b, n: (prev(b, n), COL_VSWA // kvw)),
            pl.BlockSpec((BLOCK, 1), lambda b, n: (cur(b, n), 0)),
            pl.BlockSpec((1, 1, BLOCK), lambda b, n: (cur(b, n), 0, 0)),
            pl.BlockSpec((1, 1, BLOCK), lambda b, n: (prev(b, n), 0, 0))],
        out_specs=pl.BlockSpec((BLOCK, SWA_WIDTH), lambda b, n: (cur(b, n), 0)),
        out_shape=jax.ShapeDtypeStruct((batch * seq, SWA_WIDTH), BF16),
        compiler_params=_cparams(("parallel", "parallel")),
        name="swa_attention",
    )(sinks, proj, proj, proj, proj, proj, proj, proj, proj, pos_col, pos_row, pos_row)


def _wo_kernel(oa_ref, ob_ref, ga_ref, gb_ref, w_ref, y_ref, *, chunks):
    ka = oa_ref.shape[1]
    na = _rms(oa_ref[...].astype(F32), ga_ref[...]).astype(BF16)
    nb = _rms(ob_ref[...].astype(F32), gb_ref[...]).astype(BF16)
    for lo, hi in chunks:
        y_ref[:, lo:hi] = (jnp.dot(na, w_ref[:ka, lo:hi], preferred_element_type=F32)
                           + jnp.dot(nb, w_ref[ka:, lo:hi], preferred_element_type=F32)).astype(y_ref.dtype)


def out_proj(oa, ob, ga, gb, w, tm=ROW_TILE):
    m, ka = oa.shape
    kb = ob.shape[1]
    n = w.shape[1]
    return pl.pallas_call(
        functools.partial(_wo_kernel, chunks=_col_chunks(n, 4 * MXU_N)),
        grid=(m // tm,),
        in_specs=[pl.BlockSpec((tm, ka), lambda i: (i, 0)),
                  pl.BlockSpec((tm, kb), lambda i: (i, 0)),
                  pl.BlockSpec((1, ka), lambda i: (0, 0)),
                  pl.BlockSpec((1, kb), lambda i: (0, 0)),
                  pl.BlockSpec((ka + kb, n), lambda i: (0, 0), pipeline_mode=pl.Buffered(1))],
        out_specs=pl.BlockSpec((tm, n), lambda i: (i, 0)),
        out_shape=jax.ShapeDtypeStruct((m, n), BF16),
        compiler_params=_cparams(("parallel",)),
        name="out_proj",
    )(oa, ob, ga, gb, w)


def _resid_kernel(y_ref, x_ref, gy_ref, gn_ref, xo_ref, ho_ref):
    xn = x_ref[...] + _rms(y_ref[...].astype(F32), gy_ref[...])
    xo_ref[...] = xn
    ho_ref[...] = _rms(xn, gn_ref[...]).astype(ho_ref.dtype)


def resid_norm(y, x, gy, gn, tm=256):
    m, d = x.shape
    row = pl.BlockSpec((tm, d), lambda i: (i, 0))
    vec = pl.BlockSpec((1, d), lambda i: (0, 0))
    return pl.pallas_call(
        _resid_kernel,
        grid=(m // tm,),
        in_specs=[row, row, vec, vec],
        out_specs=[row, row],
        out_shape=[jax.ShapeDtypeStruct((m, d), F32), jax.ShapeDtypeStruct((m, d), BF16)],
        compiler_params=_cparams(("parallel",)),
        name="resid_norm",
    )(y, x, gy, gn)


def _ffn_kernel(h_ref, wg_ref, wu_ref, wd_ref, o_ref, acc_sc, *, dn, rows):
    j = pl.program_id(1)
    tm, d = acc_sc.shape

    @pl.when(j == 0)
    def _():
        acc_sc[...] = jnp.zeros(acc_sc.shape, acc_sc.dtype)

    h = h_ref[...]
    g = jnp.dot(h, wg_ref[...], preferred_element_type=F32)
    u = jnp.dot(h, wu_ref[...], preferred_element_type=F32)
    a = (g * jax.nn.sigmoid(g) * u).astype(BF16)
    for c in range(d // dn):
        acc_sc[:, c * dn:(c + 1) * dn] += jnp.dot(a, wd_ref[:, c * dn:(c + 1) * dn],
                                                  preferred_element_type=F32)

    @pl.when(j == pl.num_programs(1) - 1)
    def _():
        def body(r, carry):
            rs = pl.ds(pl.multiple_of(r * rows, rows), rows)
            o_ref[rs, :] = acc_sc[rs, :].astype(o_ref.dtype)
            return carry

        lax.fori_loop(0, tm // rows, body, 0)


def ffn(h, wg, wu, wd, tm=1024, tf=FFN_TILE):
    m, d = h.shape
    f = wg.shape[1]
    return pl.pallas_call(
        functools.partial(_ffn_kernel, dn=1024, rows=64),
        grid=(m // tm, f // tf),
        in_specs=[pl.BlockSpec((tm, d), lambda i, j: (i, 0)),
                  pl.BlockSpec((d, tf), lambda i, j: (0, j)),
                  pl.BlockSpec((d, tf), lambda i, j: (0, j)),
                  pl.BlockSpec((tf, d), lambda i, j: (j, 0))],
        out_specs=pl.BlockSpec((tm, d), lambda i, j: (i, 0), pipeline_mode=pl.Buffered(1)),
        out_shape=jax.ShapeDtypeStruct((m, d), BF16),
        scratch_shapes=[pltpu.VMEM((tm, d), F32)],
        compiler_params=_cparams(("parallel", "arbitrary")),
        name="ffn",
    )(h, wg, wu, wd)


def _ple_kernel(x_ref, f_ref, gf_ref, wg_ref, p_ref, wp_ref, o_ref, *, chunks):
    o_ref[...] = x_ref[...] + _rms(f_ref[...].astype(F32), gf_ref[...])
    xb = o_ref[...].astype(BF16)
    pb = p_ref[...].astype(BF16)
    for lo, hi in chunks:
        gate = jax.nn.sigmoid(jnp.dot(xb, wg_ref[:, lo:hi], preferred_element_type=F32))
        e = jnp.dot(pb, wp_ref[:, lo:hi], preferred_element_type=F32)
        o_ref[:, lo:hi] += gate * e


def ple(x, f, g_f, wg, p, wp, tm=ROW_TILE):
    m, d = x.shape
    n = wg.shape[1]
    pd = p.shape[1]
    once = dict(pipeline_mode=pl.Buffered(1))
    return pl.pallas_call(
        functools.partial(_ple_kernel, chunks=_col_chunks(n, 4 * MXU_N)),
        grid=(m // tm,),
        in_specs=[pl.BlockSpec((tm, d), lambda i: (i, 0)),
                  pl.BlockSpec((tm, d), lambda i: (i, 0)),
                  pl.BlockSpec((1, d), lambda i: (0, 0)),
                  pl.BlockSpec((d, n), lambda i: (0, 0), **once),
                  pl.BlockSpec((tm, pd), lambda i: (i, 0)),
                  pl.BlockSpec((pd, n), lambda i: (0, 0), **once)],
        out_specs=pl.BlockSpec((tm, n), lambda i: (i, 0)),
        out_shape=jax.ShapeDtypeStruct((m, n), F32),
        compiler_params=_cparams(("parallel",)),
        name="ple",
    )(x, f, g_f, wg, p, wp)


def _prep_w_in(w):
    c_q, c_kv, k_r, q_s, k_s, v_s = jnp.split(
        w, [Q_LORA, Q_LORA + KV_LORA, Q_LORA + KV_LORA + MLA_ROPE,
            Q_LORA + KV_LORA + MLA_ROPE + SWA_WIDTH,
            Q_LORA + KV_LORA + MLA_ROPE + SWA_WIDTH + SWA_KV_HEADS * SWA_HEAD_DIM], axis=1)
    q_s = q_s * (LOG2E / math.sqrt(SWA_HEAD_DIM))
    parts = [c_q, c_kv, q_s, k_s, v_s, k_r, _swap_half(k_r)]
    return jnp.concatenate([part.astype(BF16) for part in parts], axis=1)


def _prep_w_q_up(w):
    w = (w * (LOG2E / math.sqrt(MLA_NOPE + MLA_ROPE))).reshape(Q_LORA, MLA_HEADS, MLA_NOPE + MLA_ROPE)
    nope = w[:, :, :MLA_NOPE].reshape(Q_LORA, MLA_WIDTH)
    rope = w[:, :, MLA_NOPE:].reshape(Q_LORA, MLA_HEADS * MLA_ROPE)
    return jnp.concatenate([rope, nope], axis=1).astype(BF16)


def _prep_w_kv_up(w):
    w = w.reshape(KV_LORA, MLA_HEADS, MLA_NOPE + MLA_V)
    wk = w[:, :, :MLA_NOPE].reshape(KV_LORA, MLA_WIDTH).astype(BF16)
    wvt = w[:, :, MLA_NOPE:].reshape(KV_LORA, MLA_WIDTH).T.astype(BF16)
    return wk, wvt


def _layer(x, p, pos_col, pos_row, invf, batch, seq, attn_pre_norm, w_in, q_a_norm, w_q_up,
           kv_a_norm, w_kv_up, sinks, mla_out_norm, swa_out_norm, w_o, attn_post_norm,
           ffn_pre_norm, w_gate, w_up, w_down, ffn_post_norm, w_ple_gate, w_ple_proj):
    row = lambda g: g.reshape(1, -1)
    proj = in_proj(x, row(attn_pre_norm), _prep_w_in(w_in))
    qn, qr = q_up(proj, row(q_a_norm), _prep_w_q_up(w_q_up), pos_col, invf)
    wk, wvt = _prep_w_kv_up(w_kv_up)
    kn, vt, kr = kv_up(proj, row(kv_a_norm), wk, wvt, pos_col, invf, batch, seq)
    o_mla, (wo16, wg16, wu16, wd16, wpg16) = mla_attention(
        qn, qr, kn, kr, vt, [w_o, w_gate, w_up, w_down, w_ple_gate], batch, seq)
    o_swa = swa_attention(proj, sinks, pos_col, pos_row, batch, seq)
    y = out_proj(o_mla, o_swa, row(mla_out_norm), row(swa_out_norm), wo16)
    x1, h2 = resid_norm(y, x, row(attn_post_norm), row(ffn_pre_norm))
    f = ffn(h2, wg16, wu16, wd16)
    return ple(x1, f, row(ffn_post_norm), wpg16, p, w_ple_proj.astype(BF16))


def kernel(x, p, positions, attn_pre_norm, w_in, q_a_norm, w_q_up, kv_a_norm, w_kv_up, sinks,
           mla_out_norm, swa_out_norm, w_o, attn_post_norm, ffn_pre_norm, w_gate, w_up, w_down,
           ffn_post_norm, w_ple_gate, w_ple_proj):
    batch, seq, d = x.shape
    depth = w_in.shape[0]
    t = batch * seq
    pos_col = positions.reshape(t, 1)
    pos_row = positions.reshape(t // BLOCK, 1, BLOCK)
    half = MLA_ROPE // 2
    invf = np.asarray(ROPE_THETA ** (-(np.arange(LANES) % half) * 2.0 / MLA_ROPE), np.float32).reshape(1, LANES)
    invf = jnp.asarray(invf)
    xf = x.reshape(t, d)
    for i in range(depth):
        xf = _layer(xf, p[i].reshape(t, PLE_DIM), pos_col, pos_row, invf, batch, seq,
                    attn_pre_norm[i], w_in[i], q_a_norm[i], w_q_up[i], kv_a_norm[i], w_kv_up[i],
                    sinks[i], mla_out_norm[i], swa_out_norm[i], w_o[i], attn_post_norm[i],
                    ffn_pre_norm[i], w_gate[i], w_up[i], w_down[i], ffn_post_norm[i],
                    w_ple_gate[i], w_ple_proj[i])
    return xf.reshape(batch, seq, d)
```

```python
import functools
import math

import numpy as np
import jax
import jax.numpy as jnp
from jax import lax
from jax.experimental import pallas as pl
from jax.experimental.pallas import tpu as pltpu

D_MODEL = 4096
PLE_DIM = 256
MLA_HEADS = 16
MLA_NOPE = 128
MLA_ROPE = 64
MLA_V = 128
Q_LORA = 1024
KV_LORA = 512
MLA_WIDTH = MLA_HEADS * MLA_V
SWA_HEADS = 32
SWA_KV_HEADS = 4
SWA_HEAD_DIM = 64
SWA_GROUP = SWA_HEADS // SWA_KV_HEADS
SWA_WIDTH = SWA_HEADS * SWA_HEAD_DIM
WINDOW = 128
BLOCK = 128
ROPE_THETA = 10000.0
NORM_EPS = 1e-6
LOG2E = math.log2(math.e)
D_FF = 11008

LANES = 128
MXU_N = 256
MLA_TILE = 512
ROW_TILE = 256
FFN_TILE = 256
VT_ROWS = MLA_V + 16
VMEM_LIMIT = 60 * 1024 * 1024

COL_CQ = 0
COL_CKV = COL_CQ + Q_LORA
COL_QSWA = COL_CKV + KV_LORA
COL_KSWA = COL_QSWA + SWA_WIDTH
COL_VSWA = COL_KSWA + SWA_KV_HEADS * SWA_HEAD_DIM
COL_KR = COL_VSWA + SWA_KV_HEADS * SWA_HEAD_DIM
D_IN2 = COL_KR + 2 * MLA_ROPE

F32 = jnp.float32
BF16 = jnp.bfloat16


def _cparams(sem):
    return pltpu.CompilerParams(dimension_semantics=sem, vmem_limit_bytes=VMEM_LIMIT)


def _rms(xf, g):
    ms = jnp.mean(xf * xf, axis=-1, keepdims=True)
    return xf * lax.rsqrt(ms + NORM_EPS) * g


def _swap_half(w):
    d = w.shape[-1]
    return jnp.concatenate([-w[..., d // 2:], w[..., : d // 2]], axis=-1)


def _rope_tables(pos_col, invf_row):
    ang = pos_col.astype(F32) * invf_row
    return jnp.cos(ang), jnp.sin(ang)


def _col_chunks(n, width):
    return [(c, min(c + width, n)) for c in range(0, n, width)]


def _in_proj_kernel(x_ref, g_ref, w_ref, o_ref, *, chunks):
    h = _rms(x_ref[...], g_ref[...]).astype(BF16)
    for lo, hi in chunks:
        o_ref[:, lo:hi] = jnp.dot(h, w_ref[:, lo:hi], preferred_element_type=F32).astype(o_ref.dtype)


def in_proj(x, g, w, tm=ROW_TILE):
    m, k = x.shape
    n = w.shape[1]
    return pl.pallas_call(
        functools.partial(_in_proj_kernel, chunks=_col_chunks(n, 6 * MXU_N)),
        grid=(m // tm,),
        in_specs=[pl.BlockSpec((tm, k), lambda i: (i, 0)),
                  pl.BlockSpec((1, k), lambda i: (0, 0)),
                  pl.BlockSpec((k, n), lambda i: (0, 0), pipeline_mode=pl.Buffered(1))],
        out_specs=pl.BlockSpec((tm, n), lambda i: (i, 0)),
        out_shape=jax.ShapeDtypeStruct((m, n), BF16),
        compiler_params=_cparams(("parallel",)),
        name="in_proj",
    )(x, g, w)


def _q_up_kernel(cq_ref, g_ref, w_ref, pos_ref, invf_ref, qn_ref, qr_ref):
    h = _rms(cq_ref[...].astype(F32), g_ref[...]).astype(BF16)
    n_rope = MLA_HEADS * MLA_ROPE
    rot = jnp.dot(h, w_ref[:, :n_rope], preferred_element_type=F32)
    qn_ref[...] = jnp.dot(h, w_ref[:, n_rope:], preferred_element_type=F32).astype(BF16)
    cos, sin = _rope_tables(pos_ref[...], invf_ref[...])
    half = MLA_ROPE // 2
    lane = lax.broadcasted_iota(jnp.int32, cos.shape, 1)
    first_half = (lane % MLA_ROPE) < half
    for c in range(n_rope // LANES):
        a = rot[:, c * LANES:(c + 1) * LANES]
        b = jnp.where(first_half, -pltpu.roll(a, LANES - half, 1), pltpu.roll(a, half, 1))
        qr_ref[:, c * LANES:(c + 1) * LANES] = (a * cos + b * sin).astype(BF16)


def q_up(proj, g, w, pos_col, invf, tm=512):
    m = proj.shape[0]
    n_rope = MLA_HEADS * MLA_ROPE
    return pl.pallas_call(
        _q_up_kernel,
        grid=(m // tm,),
        in_specs=[pl.BlockSpec((tm, Q_LORA), lambda i: (i, COL_CQ // Q_LORA)),
                  pl.BlockSpec((1, Q_LORA), lambda i: (0, 0)),
                  pl.BlockSpec(w.shape, lambda i: (0, 0)),
                  pl.BlockSpec((tm, 1), lambda i: (i, 0)),
                  pl.BlockSpec((1, LANES), lambda i: (0, 0))],
        out_specs=[pl.BlockSpec((tm, MLA_WIDTH), lambda i: (i, 0)),
                   pl.BlockSpec((tm, n_rope), lambda i: (i, 0))],
        out_shape=[jax.ShapeDtypeStruct((m, MLA_WIDTH), BF16),
                   jax.ShapeDtypeStruct((m, n_rope), BF16)],
        compiler_params=_cparams(("parallel",)),
        name="q_up",
    )(proj, g, w, pos_col, invf)


def _kv_up_kernel(ckv_ref, g_ref, wk_ref, wvt_ref, krab_ref, pos_ref, invf_ref, kn_ref, vt_ref, kr_ref):
    h = _rms(ckv_ref[...].astype(F32), g_ref[...]).astype(BF16)
    kn_ref[...] = jnp.dot(h, wk_ref[...], preferred_element_type=F32).astype(BF16)
    vt = lax.dot_general(wvt_ref[...], h, (((1,), (1,)), ((), ())), preferred_element_type=F32)
    ones = jnp.ones((VT_ROWS - MLA_V, vt.shape[1]), BF16)
    for hh in range(MLA_HEADS):
        vt_ref[0, hh, 0, :MLA_V, :] = vt[hh * MLA_V:(hh + 1) * MLA_V, :].astype(BF16)
        vt_ref[0, hh, 0, MLA_V:, :] = ones
    cos, sin = _rope_tables(pos_ref[...], invf_ref[...])
    lane = lax.broadcasted_iota(jnp.int32, cos.shape, 1)
    lo = lane < MLA_ROPE
    prod = krab_ref[...].astype(F32) * jnp.where(lo, cos, sin)
    kk = prod + pltpu.roll(prod, MLA_ROPE, 1)
    zero = jnp.zeros_like(kk)
    kr_ref[:, :LANES] = jnp.where(lo, kk, zero).astype(BF16)
    kr_ref[:, LANES:] = jnp.where(lo, zero, kk).astype(BF16)


def kv_up(proj, g, wk, wvt, pos_col, invf, batch, seq, tm=MLA_TILE):
    m = proj.shape[0]
    nk = seq // tm
    return pl.pallas_call(
        _kv_up_kernel,
        grid=(m // tm,),
        in_specs=[pl.BlockSpec((tm, KV_LORA), lambda i: (i, COL_CKV // KV_LORA)),
                  pl.BlockSpec((1, KV_LORA), lambda i: (0, 0)),
                  pl.BlockSpec(wk.shape, lambda i: (0, 0)),
                  pl.BlockSpec(wvt.shape, lambda i: (0, 0)),
                  pl.BlockSpec((tm, LANES), lambda i: (i, COL_KR // LANES)),
                  pl.BlockSpec((tm, 1), lambda i: (i, 0)),
                  pl.BlockSpec((1, LANES), lambda i: (0, 0))],
        out_specs=[pl.BlockSpec((tm, MLA_WIDTH), lambda i: (i, 0)),
                   pl.BlockSpec((1, MLA_HEADS, 1, VT_ROWS, tm), lambda i: (i // nk, 0, i % nk, 0, 0)),
                   pl.BlockSpec((tm, 2 * LANES), lambda i: (i, 0))],
        out_shape=[jax.ShapeDtypeStruct((m, MLA_WIDTH), BF16),
                   jax.ShapeDtypeStruct((batch, MLA_HEADS, nk, VT_ROWS, tm), BF16),
                   jax.ShapeDtypeStruct((m, 2 * LANES), BF16)],
        compiler_params=_cparams(("parallel",)),
        name="kv_up",
    )(proj, g, wk, wvt, proj, pos_col, invf)


def _mla_kernel(*refs, tq, tk, n_cast):
    qn_ref, qr_ref, kn_ref, kr_ref, vt_ref = refs[:5]
    w_refs = refs[5:5 + n_cast]
    o_ref = refs[5 + n_cast]
    w16_refs = refs[6 + n_cast:6 + 2 * n_cast]
    s_a, s_b, mx_a, mx_b, m_sc, acc_sc = refs[6 + 2 * n_cast:]

    qi = pl.program_id(2)
    q = jnp.concatenate([qn_ref[...], qr_ref[...]], axis=-1)
    m_sc[...] = jnp.full(m_sc.shape, -jnp.inf, F32)
    acc_sc[...] = jnp.zeros(acc_sc.shape, F32)

    n_chunk = tq // MXU_N
    blk_chunks = tk // MXU_N
    n_diag = tq // tk
    bufs = ((s_a, mx_a), (s_b, mx_b))

    def live_chunks(diag):
        return range(0 if diag is None else diag * blk_chunks, n_chunk)

    def scores(kb, buf, diag=None):
        s_ref, mx_ref = buf
        off = pl.multiple_of(kb * tk, tk)
        k = jnp.concatenate([kn_ref[pl.ds(off, tk), :], kr_ref[pl.ds(off, tk), :]], axis=-1)
        for c in live_chunks(diag):
            s = lax.dot_general(k, q[c * MXU_N:(c + 1) * MXU_N, :], (((1,), (1,)), ((), ())),
                                preferred_element_type=F32)
            if diag is not None and c < (diag + 1) * blk_chunks:
                key = lax.broadcasted_iota(jnp.int32, s.shape, 0) + diag * tk
                qry = lax.broadcasted_iota(jnp.int32, s.shape, 1) + c * MXU_N
                s = jnp.where(qry >= key, s, -jnp.inf)
            s_ref[c] = s
            mx_ref[:, c * MXU_N:(c + 1) * MXU_N] = jnp.max(s, axis=0, keepdims=True)

    def softmax_pv(kb, buf, diag=None, mask_here=False):
        s_ref, mx_ref = buf
        vt = vt_ref[0, 0, kb]
        for c in live_chunks(diag):
            sl = slice(c * MXU_N, (c + 1) * MXU_N)
            remask = mask_here and c < blk_chunks

            def s_chunk():
                s = s_ref[c]
                if remask:
                    key = lax.broadcasted_iota(jnp.int32, s.shape, 0)
                    qry = lax.broadcasted_iota(jnp.int32, s.shape, 1) + c * MXU_N
                    s = jnp.where(qry >= key, s, -jnp.inf)
                return s

            mx = jnp.max(s_chunk(), axis=0, keepdims=True) if remask else mx_ref[:, sl]
            m_old = m_sc[:, sl]
            m_new = jnp.maximum(m_old, mx)
            alpha = jnp.exp2(m_old - m_new)
            p = jnp.exp2(s_chunk() - m_new).astype(BF16)
            acc_sc[:, sl] = alpha * acc_sc[:, sl] + jnp.dot(vt, p, preferred_element_type=F32)
            m_sc[:, sl] = m_new

    n_full = n_diag * qi
    scores(0, bufs[0])

    def body(j, carry):
        scores(2 * j + 1, bufs[1])
        softmax_pv(2 * j, bufs[0])
        scores(2 * j + 2, bufs[0])
        softmax_pv(2 * j + 1, bufs[1])
        return carry

    lax.fori_loop(0, n_full // 2, body, 0)

    for w_ref, w16_ref in zip(w_refs, w16_refs):
        w16_ref[...] = w_ref[...].astype(BF16)

    for d in range(n_diag):
        if d + 1 < n_diag:
            scores(n_full + d + 1, bufs[(d + 1) % 2], diag=d + 1)
        softmax_pv(n_full + d, bufs[d % 2], diag=d, mask_here=(d == 0))

    o_ref[...] = (acc_sc[:MLA_V, :] / acc_sc[MLA_V:MLA_V + 1, :]).T.astype(o_ref.dtype)


def mla_attention(qn, qr, kn, kr, vt, weights, batch, seq, tq=4 * MLA_TILE, tk=MLA_TILE):
    assert tq % (2 * tk) == 0
    nq = seq // tq
    steps = batch * MLA_HEADS * nq

    def slab(w):
        tiles = w.shape[0] // 16
        assert tiles * 16 == w.shape[0], w.shape
        n_slab = max(n for n in range(1, steps + 1) if tiles % n == 0)
        rows = w.shape[0] // n_slab
        return pl.BlockSpec((rows, w.shape[1]),
                            lambda b, h, i: (jnp.minimum((b * MLA_HEADS + h) * nq + i, n_slab - 1), 0))

    slabs = [slab(w) for w in weights]
    outs = pl.pallas_call(
        functools.partial(_mla_kernel, tq=tq, tk=tk, n_cast=len(weights)),
        grid=(batch, MLA_HEADS, nq),
        in_specs=[pl.BlockSpec((tq, MLA_NOPE), lambda b, h, i: (b * nq + i, h)),
                  pl.BlockSpec((tq, LANES), lambda b, h, i: (b * nq + i, h // 2)),
                  pl.BlockSpec((seq, MLA_NOPE), lambda b, h, i: (b, h)),
                  pl.BlockSpec((seq, LANES), lambda b, h, i: (b, h % 2)),
                  pl.BlockSpec((1, 1, seq // tk, VT_ROWS, tk), lambda b, h, i: (b, h, 0, 0, 0))] + slabs,
        out_specs=[pl.BlockSpec((tq, MLA_V), lambda b, h, i: (b * nq + i, h))] + slabs,
        out_shape=[jax.ShapeDtypeStruct((batch * seq, MLA_WIDTH), BF16)]
        + [jax.ShapeDtypeStruct(w.shape, BF16) for w in weights],
        scratch_shapes=[pltpu.VMEM((tq // MXU_N, tk, MXU_N), F32), pltpu.VMEM((tq // MXU_N, tk, MXU_N), F32),
                        pltpu.VMEM((1, tq), F32), pltpu.VMEM((1, tq), F32),
                        pltpu.VMEM((1, tq), F32), pltpu.VMEM((VT_ROWS, tq), F32)],
        compiler_params=_cparams(("parallel", "parallel", "parallel")),
        name="mla_attention",
    )(qn, qr, kn, kr, vt, *weights)
    return outs[0], outs[1:]


def _swa_kernel(sinks_ref, q0_ref, q1_ref, q2_ref, q3_ref, kc_ref, kp_ref, vc_ref, vp_ref,
                pq_ref, pkc_ref, pkp_ref, o_ref):
    blk = pl.program_id(1)
    q_refs = (q0_ref, q1_ref, q2_ref, q3_ref)
    k_all = jnp.concatenate([kp_ref[...], kc_ref[...]], axis=0).astype(F32)
    v_all = jnp.concatenate([vp_ref[...], vc_ref[...]], axis=0).astype(F32)
    pk = jnp.concatenate([pkp_ref[0], pkc_ref[0]], axis=-1).astype(F32)
    dist = jnp.abs(pq_ref[...].astype(F32) - pk)
    row = lax.broadcasted_iota(jnp.int32, dist.shape, 0)
    col = lax.broadcasted_iota(jnp.int32, dist.shape, 1)
    rel = BLOCK + row - col
    first_key = jnp.where(blk > 0, 0, BLOCK)
    valid = (rel >= 0) & (rel < WINDOW) & (col >= first_key)
    dist_w = jnp.where(valid, dist, jnp.inf)
    lane =lax.broadcasted_iota(jnp.int32, (2 * BLOCK, LANES), 1)
    lo = lane < SWA_HEAD_DIM
    lane_o = lax.broadcasted_iota(jnp.int32, (BLOCK, LANES), 1)
    lo_o = lane_o < SWA_HEAD_DIM

    def dup(x_all, g):
        pair = x_all[:, (g // 2) * LANES:(g // 2 + 1) * LANES]
        rolled = pltpu.roll(pair, SWA_HEAD_DIM, 1)
        return jnp.where(lo, pair, rolled) if g % 2 == 0 else jnp.where(lo, rolled, pair)

    for g in range(SWA_KV_HEADS):
        kk = dup(k_all, g)
        vv = dup(v_all, g).astype(BF16)
        zero = jnp.zeros_like(kk)
        k_half = (jnp.where(lo, kk, zero).astype(BF16), jnp.where(lo, zero, kk).astype(BF16))
        for i in range(SWA_GROUP // 2):
            q_pair = q_refs[g][:, i * LANES:(i + 1) * LANES]
            outs = []
            for j in range(2):
                head = g * SWA_GROUP + 2 * i + j
                slope2 = LOG2E * 2.0 ** (-8.0 * (head + 1) / SWA_HEADS)
                s = lax.dot_general(q_pair, k_half[j], (((1,), (1,)), ((), ())),
                                    preferred_element_type=F32)
                s = s - slope2 * dist_w
                sink2 = LOG2E * sinks_ref[head]
                m = jnp.maximum(jnp.max(s, axis=-1, keepdims=True), sink2)
                e = jnp.exp2(s - m)
                denom = jnp.sum(e, axis=-1, keepdims=True) + jnp.exp2(sink2 - m)
                pv = jnp.dot(e.astype(BF16), vv, preferred_element_type=F32)
                outs.append(pv / denom)
            col0 = (g * SWA_GROUP // 2 + i) * LANES
            o_ref[:, col0:col0 + LANES] = jnp.where(lo_o, outs[0], outs[1]).astype(o_ref.dtype)


def swa_attention(proj, sinks, pos_col, pos_row, batch, seq):
    nblk = seq // BLOCK
    kvw = SWA_KV_HEADS * SWA_HEAD_DIM
    gw = SWA_GROUP * SWA_HEAD_DIM
    cur = lambda b, n: b * nblk + n
    prev = lambda b, n: b * nblk + jnp.maximum(n - 1, 0)
    q_specs = [pl.BlockSpec((BLOCK, gw), functools.partial(lambda b, n, g: (cur(b, n), COL_QSWA // gw + g), g=g))
               for g in range(SWA_KV_HEADS)]
    return pl.pallas_call(
        _swa_kernel,
        grid=(batch, nblk),
        in_specs=[pl.BlockSpec(memory_space=pltpu.SMEM)] + q_specs + [
            pl.BlockSpec((BLOCK, kvw), lambda b, n: (cur(b, n), COL_KSWA // kvw)),
            pl.BlockSpec((BLOCK, kvw), lambda b, n: (prev(b, n), COL_KSWA // kvw)),
            pl.BlockSpec((BLOCK, kvw), lambda b, n: (cur(b, n), COL_VSWA // kvw)),
            pl.BlockSpec((BLOCK, kvw), lambda b, n: (prev(b, n), COL_VSWA // kvw)),
            pl.BlockSpec((BLOCK, 1), lambda b, n: (cur(b, n), 0)),
            pl.BlockSpec((1, 1, BLOCK), lambda b, n: (cur(b, n), 0, 0)),
            pl.BlockSpec((1, 1, BLOCK), lambda b, n: (prev(b, n), 0, 0))],
        out_specs=pl.BlockSpec((BLOCK, SWA_WIDTH), lambda b, n: (cur(b, n), 0)),
        out_shape=jax.ShapeDtypeStruct((batch * seq, SWA_WIDTH), BF16),
        compiler_params=_cparams(("parallel", "parallel")),
        name="swa_attention",
    )(sinks, proj, proj, proj, proj, proj, proj, proj, proj, pos_col, pos_row, pos_row)


def _wo_kernel(oa_ref, ob_ref, ga_ref, gb_ref, w_ref, x_ref, gy_ref, gn_ref, x1_ref, h_ref, *, chunks):
    ka = oa_ref.shape[1]
    na = _rms(oa_ref[...].astype(F32), ga_ref[...]).astype(BF16)
    nb = _rms(ob_ref[...].astype(F32), gb_ref[...]).astype(BF16)
    for lo, hi in chunks:
        x1_ref[:, lo:hi] = (jnp.dot(na, w_ref[:ka, lo:hi], preferred_element_type=F32)
                            + jnp.dot(nb, w_ref[ka:, lo:hi], preferred_element_type=F32))
    rows = 32
    for r in range(x1_ref.shape[0] // rows):
        rs = slice(r * rows, (r + 1) * rows)
        x1 = x_ref[rs, :] + _rms(x1_ref[rs, :], gy_ref[...])
        x1_ref[rs, :] = x1
        h_ref[rs, :] = _rms(x1, gn_ref[...]).astype(h_ref.dtype)


def out_proj_resid(oa, ob, ga, gb, w, x, gy, gn, tm=ROW_TILE // 2):
    m, ka = oa.shape
    kb = ob.shape[1]
    n = w.shape[1]
    row = pl.BlockSpec((tm, n), lambda i: (i, 0))
    vec = pl.BlockSpec((1, n), lambda i: (0, 0))
    return pl.pallas_call(
        functools.partial(_wo_kernel, chunks=_col_chunks(n, 4 * MXU_N)),
        grid=(m // tm,),
        in_specs=[pl.BlockSpec((tm, ka), lambda i: (i, 0)),
                  pl.BlockSpec((tm, kb), lambda i: (i, 0)),
                  pl.BlockSpec((1, ka), lambda i: (0, 0)),
                  pl.BlockSpec((1, kb), lambda i: (0, 0)),
                  pl.BlockSpec((ka + kb, n), lambda i: (0, 0), pipeline_mode=pl.Buffered(1)),
                  row, vec, vec],
        out_specs=[row, row],
        out_shape=[jax.ShapeDtypeStruct((m, n), F32), jax.ShapeDtypeStruct((m, n), BF16)],
        compiler_params=_cparams(("parallel",)),
        name="out_proj",
    )(oa, ob, ga, gb, w, x, gy, gn)


def _ffn_kernel(h_ref, wg_ref, wu_ref, wd_ref, o_ref, acc_sc, *, dn, rows):
    j = pl.program_id(1)
    tm, d = acc_sc.shape

    @pl.when(j == 0)
    def _():
        acc_sc[...] = jnp.zeros(acc_sc.shape, acc_sc.dtype)

    h = h_ref[...]
    g = jnp.dot(h, wg_ref[...], preferred_element_type=F32)
    u = jnp.dot(h, wu_ref[...], preferred_element_type=F32)
    a = (g * jax.nn.sigmoid(g) * u).astype(BF16)
    for c in range(d // dn):
        acc_sc[:, c * dn:(c + 1) * dn] += jnp.dot(a, wd_ref[:, c * dn:(c + 1) * dn],
                                                  preferred_element_type=F32)

    @pl.when(j == pl.num_programs(1) - 1)
    def _():
        def body(r, carry):
            rs = pl.ds(pl.multiple_of(r * rows, rows), rows)
            o_ref[rs, :] = acc_sc[rs, :].astype(o_ref.dtype)
            return carry

        lax.fori_loop(0, tm // rows, body, 0)


def ffn(h, wg, wu, wd, tm=1024, tf=FFN_TILE):
    m, d = h.shape
    f = wg.shape[1]
    return pl.pallas_call(
        functools.partial(_ffn_kernel, dn=1024, rows=64),
        grid=(m // tm, f // tf),
        in_specs=[pl.BlockSpec((tm, d), lambda i, j: (i, 0)),
                  pl.BlockSpec((d, tf), lambda i, j: (0, j)),
                  pl.BlockSpec((d, tf), lambda i, j: (0, j)),
                  pl.BlockSpec((tf, d), lambda i, j: (j, 0))],
        out_specs=pl.BlockSpec((tm, d), lambda i, j: (i, 0), pipeline_mode=pl.Buffered(1)),
        out_shape=jax.ShapeDtypeStruct((m, d), BF16),
        scratch_shapes=[pltpu.VMEM((tm, d), F32)],
        compiler_params=_cparams(("parallel", "arbitrary")),
        name="ffn",
    )(h, wg, wu, wd)


def _ple_kernel(x_ref, f_ref, gf_ref, wg_ref, p_ref, wp_ref, o_ref, *, chunks):
    o_ref[...] = x_ref[...] + _rms(f_ref[...].astype(F32), gf_ref[...])
    xb = o_ref[...].astype(BF16)
    pb = p_ref[...].astype(BF16)
    for lo, hi in chunks:
        gate = jax.nn.sigmoid(jnp.dot(xb, wg_ref[:, lo:hi], preferred_element_type=F32))
        e = jnp.dot(pb, wp_ref[:, lo:hi], preferred_element_type=F32)
        o_ref[:, lo:hi] += gate * e


def ple(x, f, g_f, wg, p, wp, tm=ROW_TILE):
    m, d = x.shape
    n = wg.shape[1]
    pd = p.shape[1]
    once = dict(pipeline_mode=pl.Buffered(1))
    return pl.pallas_call(
        functools.partial(_ple_kernel, chunks=_col_chunks(n, 4 * MXU_N)),
        grid=(m // tm,),
        in_specs=[pl.BlockSpec((tm, d), lambda i: (i, 0)),
                  pl.BlockSpec((tm, d), lambda i: (i, 0)),
                  pl.BlockSpec((1, d), lambda i: (0, 0)),
                  pl.BlockSpec((d, n), lambda i: (0, 0), **once),
                  pl.BlockSpec((tm, pd), lambda i: (i, 0)),
                  pl.BlockSpec((pd, n), lambda i: (0, 0), **once)],
        out_specs=pl.BlockSpec((tm, n), lambda i: (i, 0)),
        out_shape=jax.ShapeDtypeStruct((m, n), F32),
        compiler_params=_cparams(("parallel",)),
        name="ple",
    )(x, f, g_f, wg, p, wp)


def _prep_w_in(w):
    c_q, c_kv, k_r, q_s, k_s, v_s = jnp.split(
        w, [Q_LORA, Q_LORA + KV_LORA, Q_LORA + KV_LORA + MLA_ROPE,
            Q_LORA + KV_LORA + MLA_ROPE + SWA_WIDTH,
            Q_LORA + KV_LORA + MLA_ROPE + SWA_WIDTH + SWA_KV_HEADS * SWA_HEAD_DIM], axis=1)
    q_s = q_s * (LOG2E / math.sqrt(SWA_HEAD_DIM))
    parts = [c_q, c_kv, q_s, k_s, v_s, k_r, _swap_half(k_r)]
    return jnp.concatenate([part.astype(BF16) for part in parts], axis=1)


def _prep_w_q_up(w):
    w = (w * (LOG2E / math.sqrt(MLA_NOPE + MLA_ROPE))).reshape(Q_LORA, MLA_HEADS, MLA_NOPE + MLA_ROPE)
    nope = w[:, :, :MLA_NOPE].reshape(Q_LORA, MLA_WIDTH)
    rope = w[:, :, MLA_NOPE:].reshape(Q_LORA, MLA_HEADS * MLA_ROPE)
    return jnp.concatenate([rope, nope], axis=1).astype(BF16)


def _prep_w_kv_up(w):
    w = w.reshape(KV_LORA, MLA_HEADS, MLA_NOPE + MLA_V)
    wk = w[:, :, :MLA_NOPE].reshape(KV_LORA, MLA_WIDTH).astype(BF16)
    wvt = w[:, :, MLA_NOPE:].reshape(KV_LORA, MLA_WIDTH).T.astype(BF16)
    return wk, wvt


def _layer(x, p, pos_col, pos_row, invf, batch, seq, attn_pre_norm, w_in, q_a_norm, w_q_up,
           kv_a_norm, w_kv_up, sinks, mla_out_norm, swa_out_norm, w_o, attn_post_norm,
           ffn_pre_norm, w_gate, w_up, w_down, ffn_post_norm, w_ple_gate, w_ple_proj):
    row = lambda g: g.reshape(1, -1)
    proj = in_proj(x, row(attn_pre_norm), _prep_w_in(w_in))
    qn, qr = q_up(proj, row(q_a_norm), _prep_w_q_up(w_q_up), pos_col, invf)
    wk, wvt = _prep_w_kv_up(w_kv_up)
    kn, vt, kr = kv_up(proj, row(kv_a_norm), wk, wvt, pos_col, invf, batch, seq)
    o_mla, (wo16, wg16, wu16, wd16, wpg16) = mla_attention(
        qn, qr, kn, kr, vt, [w_o, w_gate, w_up, w_down, w_ple_gate], batch, seq)
    o_swa = swa_attention(proj, sinks, pos_col, pos_row, batch, seq)
    x1, h2 = out_proj_resid(o_mla, o_swa, row(mla_out_norm), row(swa_out_norm), wo16,
                            x, row(attn_post_norm), row(ffn_pre_norm))
    f = ffn(h2, wg16, wu16, wd16)
    return ple(x1, f, row(ffn_post_norm), wpg16, p, w_ple_proj.astype(BF16))


def kernel(x, p, positions, attn_pre_norm, w_in, q_a_norm, w_q_up, kv_a_norm, w_kv_up, sinks,
           mla_out_norm, swa_out_norm, w_o, attn_post_norm, ffn_pre_norm, w_gate, w_up, w_down,
           ffn_post_norm, w_ple_gate, w_ple_proj):
    batch, seq, d = x.shape
    depth = w_in.shape[0]
    t = batch * seq
    pos_col = positions.reshape(t, 1)
    pos_row = positions.reshape(t // BLOCK, 1, BLOCK)
    half = MLA_ROPE // 2
    invf = np.asarray(ROPE_THETA ** (-(np.arange(LANES) % half) * 2.0 / MLA_ROPE), np.float32).reshape(1, LANES)
    invf = jnp.asarray(invf)
    xf = x.reshape(t, d)
    for i in range(depth):
        xf = _layer(xf, p[i].reshape(t, PLE_DIM), pos_col, pos_row, invf, batch, seq,
                    attn_pre_norm[i], w_in[i], q_a_norm[i], w_q_up[i], kv_a_norm[i], w_kv_up[i],
                    sinks[i], mla_out_norm[i], swa_out_norm[i], w_o[i], attn_post_norm[i],
                    ffn_pre_norm[i], w_gate[i], w_up[i], w_down[i], ffn_post_norm[i],
                    w_ple_gate[i], w_ple_proj[i])
    return xf.reshape(batch, seq, d)
```

```python
import functools
import math

import numpy as np
import jax
import jax.numpy as jnp
from jax import lax
from jax.experimental import pallas as pl
from jax.experimental.pallas import tpu as pltpu

D_MODEL = 4096
PLE_DIM = 256
MLA_HEADS = 16
MLA_NOPE = 128
MLA_ROPE = 64
MLA_V = 128
Q_LORA = 1024
KV_LORA = 512
MLA_WIDTH = MLA_HEADS * MLA_V
SWA_HEADS = 32
SWA_KV_HEADS = 4
SWA_HEAD_DIM = 64
SWA_GROUP = SWA_HEADS // SWA_KV_HEADS
SWA_WIDTH = SWA_HEADS * SWA_HEAD_DIM
WINDOW = 128
BLOCK = 128
ROPE_THETA = 10000.0
NORM_EPS = 1e-6
LOG2E = math.log2(math.e)
D_FF = 11008

LANES = 128
MXU_N = 256
MLA_TILE = 512
ROW_TILE = 256
FFN_TILE = 256
VT_ROWS = MLA_V + 16
VMEM_LIMIT = 60 * 1024 * 1024

COL_CQ = 0
COL_CKV = COL_CQ + Q_LORA
COL_QSWA = COL_CKV + KV_LORA
COL_KSWA = COL_QSWA + SWA_WIDTH
COL_VSWA = COL_KSWA + SWA_KV_HEADS * SWA_HEAD_DIM
COL_KR = COL_VSWA + SWA_KV_HEADS * SWA_HEAD_DIM
D_IN2 = COL_KR + 2 * MLA_ROPE

F32 = jnp.float32
BF16 = jnp.bfloat16


def _cparams(sem):
    return pltpu.CompilerParams(dimension_semantics=sem, vmem_limit_bytes=VMEM_LIMIT)


def _rms(xf, g):
    ms = jnp.mean(xf * xf, axis=-1, keepdims=True)
    return xf * lax.rsqrt(ms + NORM_EPS) * g


def _swap_half(w):
    d = w.shape[-1]
    return jnp.concatenate([-w[..., d // 2:], w[..., : d // 2]], axis=-1)


def _rope_tables(pos_col, invf_row):
    ang = pos_col.astype(F32) * invf_row
    return jnp.cos(ang), jnp.sin(ang)


def _col_chunks(n, width):
    return [(c, min(c + width, n)) for c in range(0, n, width)]


def _in_proj_kernel(x_ref, g_ref, w_ref, o_ref, *, chunks):
    h = _rms(x_ref[...], g_ref[...]).astype(BF16)
    for lo, hi in chunks:
        o_ref[:, lo:hi] = jnp.dot(h, w_ref[:, lo:hi], preferred_element_type=F32).astype(o_ref.dtype)


def in_proj(x, g, w, tm=ROW_TILE):
    m, k = x.shape
    n = w.shape[1]
    return pl.pallas_call(
        functools.partial(_in_proj_kernel, chunks=_col_chunks(n, 6 * MXU_N)),
        grid=(m // tm,),
        in_specs=[pl.BlockSpec((tm, k), lambda i: (i, 0)),
                  pl.BlockSpec((1, k), lambda i: (0, 0)),
                  pl.BlockSpec((k, n), lambda i: (0, 0), pipeline_mode=pl.Buffered(1))],
        out_specs=pl.BlockSpec((tm, n), lambda i: (i, 0)),
        out_shape=jax.ShapeDtypeStruct((m, n), BF16),
        compiler_params=_cparams(("parallel",)),
        name="in_proj",
    )(x, g, w)


def _q_up_kernel(cq_ref, g_ref, w_ref, pos_ref, invf_ref, qn_ref, qr_ref):
    h = _rms(cq_ref[...].astype(F32), g_ref[...]).astype(BF16)
    n_rope = MLA_HEADS * MLA_ROPE
    rot = jnp.dot(h, w_ref[:, :n_rope], preferred_element_type=F32)
    qn_ref[...] = jnp.dot(h, w_ref[:, n_rope:], preferred_element_type=F32).astype(BF16)
    cos, sin = _rope_tables(pos_ref[...], invf_ref[...])
    half = MLA_ROPE // 2
    lane = lax.broadcasted_iota(jnp.int32, cos.shape, 1)
    first_half = (lane % MLA_ROPE) < half
    for c in range(n_rope // LANES):
        a = rot[:, c * LANES:(c + 1) * LANES]
        b = jnp.where(first_half, -pltpu.roll(a, LANES - half, 1), pltpu.roll(a, half, 1))
        qr_ref[:, c * LANES:(c + 1) * LANES] = (a * cos + b * sin).astype(BF16)


def q_up(proj, g, w, pos_col, invf, tm=512):
    m = proj.shape[0]
    n_rope = MLA_HEADS * MLA_ROPE
    return pl.pallas_call(
        _q_up_kernel,
        grid=(m // tm,),
        in_specs=[pl.BlockSpec((tm, Q_LORA), lambda i: (i, COL_CQ // Q_LORA)),
                  pl.BlockSpec((1, Q_LORA), lambda i: (0, 0)),
                  pl.BlockSpec(w.shape, lambda i: (0, 0)),
                  pl.BlockSpec((tm, 1), lambda i: (i, 0)),
                  pl.BlockSpec((1, LANES), lambda i: (0, 0))],
        out_specs=[pl.BlockSpec((tm, MLA_WIDTH), lambda i: (i, 0)),
                   pl.BlockSpec((tm, n_rope), lambda i: (i, 0))],
        out_shape=[jax.ShapeDtypeStruct((m, MLA_WIDTH), BF16),
                   jax.ShapeDtypeStruct((m, n_rope), BF16)],
        compiler_params=_cparams(("parallel",)),
        name="q_up",
    )(proj, g, w, pos_col, invf)


def _kv_up_kernel(ckv_ref, g_ref, wk_ref, wvt_ref, krab_ref, pos_ref, invf_ref, kn_ref, vt_ref, kr_ref):
    h = _rms(ckv_ref[...].astype(F32), g_ref[...]).astype(BF16)
    kn_ref[...] = jnp.dot(h, wk_ref[...], preferred_element_type=F32).astype(BF16)
    vt = lax.dot_general(wvt_ref[...], h, (((1,), (1,)), ((), ())), preferred_element_type=F32)
    ones = jnp.ones((VT_ROWS - MLA_V, vt.shape[1]), BF16)
    for hh in range(MLA_HEADS):
        vt_ref[0, hh, 0, :MLA_V, :] = vt[hh * MLA_V:(hh + 1) * MLA_V, :].astype(BF16)
        vt_ref[0, hh, 0, MLA_V:, :] = ones
    cos, sin = _rope_tables(pos_ref[...], invf_ref[...])
    lane = lax.broadcasted_iota(jnp.int32, cos.shape, 1)
    lo = lane < MLA_ROPE
    prod = krab_ref[...].astype(F32) * jnp.where(lo, cos, sin)
    kk = prod + pltpu.roll(prod, MLA_ROPE, 1)
    zero = jnp.zeros_like(kk)
    kr_ref[:, :LANES] = jnp.where(lo, kk, zero).astype(BF16)
    kr_ref[:, LANES:] = jnp.where(lo, zero, kk).astype(BF16)


def kv_up(proj, g, wk, wvt, pos_col, invf, batch, seq, tm=MLA_TILE):
    m = proj.shape[0]
    nk = seq // tm
    return pl.pallas_call(
        _kv_up_kernel,
        grid=(m // tm,),
        in_specs=[pl.BlockSpec((tm, KV_LORA), lambda i: (i, COL_CKV // KV_LORA)),
                  pl.BlockSpec((1, KV_LORA), lambda i: (0, 0)),
                  pl.BlockSpec(wk.shape, lambda i: (0, 0)),
                  pl.BlockSpec(wvt.shape, lambda i: (0, 0)),
                  pl.BlockSpec((tm, LANES), lambda i: (i, COL_KR // LANES)),
                  pl.BlockSpec((tm, 1), lambda i: (i, 0)),
                  pl.BlockSpec((1, LANES), lambda i: (0, 0))],
        out_specs=[pl.BlockSpec((tm, MLA_WIDTH), lambda i: (i, 0)),
                   pl.BlockSpec((1, MLA_HEADS, 1, VT_ROWS, tm), lambda i: (i // nk, 0, i % nk, 0, 0)),
                   pl.BlockSpec((tm, 2 * LANES), lambda i: (i, 0))],
        out_shape=[jax.ShapeDtypeStruct((m, MLA_WIDTH), BF16),
                   jax.ShapeDtypeStruct((batch, MLA_HEADS, nk, VT_ROWS, tm), BF16),
                   jax.ShapeDtypeStruct((m, 2 * LANES), BF16)],
        compiler_params=_cparams(("parallel",)),
        name="kv_up",
    )(proj, g, wk, wvt, proj, pos_col, invf)


def _mla_kernel(*refs, tq, tk, n_cast):
    qn_ref, qr_ref, kn_ref, kr_ref, vt_ref = refs[:5]
    w_refs = refs[5:5 + n_cast]
    o_ref = refs[5 + n_cast]
    w16_refs = refs[6 + n_cast:6 + 2 * n_cast]
    s_a, s_b, mx_a, mx_b, m_sc, acc_sc = refs[6 + 2 * n_cast:]

    qi = pl.program_id(2)
    q = jnp.concatenate([qn_ref[...], qr_ref[...]], axis=-1)
    m_sc[...] = jnp.full(m_sc.shape, -jnp.inf, F32)
    acc_sc[...] = jnp.zeros(acc_sc.shape, F32)

    n_chunk = tq // MXU_N
    blk_chunks = tk // MXU_N
    n_diag = tq // tk
    bufs = ((s_a, mx_a), (s_b, mx_b))

    def live_chunks(diag):
        return range(0 if diag is None else diag * blk_chunks, n_chunk)

    def scores(kb, buf, diag=None):
        s_ref, mx_ref = buf
        off = pl.multiple_of(kb * tk, tk)
        k = jnp.concatenate([kn_ref[pl.ds(off, tk), :], kr_ref[pl.ds(off, tk), :]], axis=-1)
        for c in live_chunks(diag):
            s = lax.dot_general(k, q[c * MXU_N:(c + 1) * MXU_N, :], (((1,), (1,)), ((), ())),
                                preferred_element_type=F32)
            if diag is not None and c < (diag + 1) * blk_chunks:
                key = lax.broadcasted_iota(jnp.int32, s.shape, 0) + diag * tk
                qry = lax.broadcasted_iota(jnp.int32, s.shape, 1) + c * MXU_N
                s = jnp.where(qry >= key, s, -jnp.inf)
            s_ref[c] = s
            mx_ref[:, c * MXU_N:(c + 1) * MXU_N] = jnp.max(s, axis=0, keepdims=True)

    def softmax_pv(kb, buf, diag=None, mask_here=False):
        s_ref, mx_ref = buf
        vt = vt_ref[0, 0, kb]
        for c in live_chunks(diag):
            sl = slice(c * MXU_N, (c + 1) * MXU_N)
            remask = mask_here and c < blk_chunks

            def s_chunk():
                s = s_ref[c]
                if remask:
                    key = lax.broadcasted_iota(jnp.int32, s.shape, 0)
                    qry = lax.broadcasted_iota(jnp.int32, s.shape, 1) + c * MXU_N
                    s = jnp.where(qry >= key, s, -jnp.inf)
                return s

            mx = jnp.max(s_chunk(), axis=0, keepdims=True) if remask else mx_ref[:, sl]
            m_old = m_sc[:, sl]
            m_new = jnp.maximum(m_old, mx)
            alpha = jnp.exp2(m_old - m_new)
            p = jnp.exp2(s_chunk() - m_new).astype(BF16)
            acc_sc[:, sl] = alpha * acc_sc[:, sl] + jnp.dot(vt, p, preferred_element_type=F32)
            m_sc[:, sl] = m_new

    n_full = n_diag * qi
    scores(0, bufs[0])

    def body(j, carry):
        for t in range(n_diag):
            kb = n_diag * j + t
            scores(kb + 1, bufs[(t + 1) % 2])
            softmax_pv(kb, bufs[t % 2])
        return carry

    lax.fori_loop(0, qi, body, 0)

    for w_ref, w16_ref in zip(w_refs, w16_refs):
        w16_ref[...] = w_ref[...].astype(BF16)

    for d in range(n_diag):
        if d + 1 < n_diag:
            scores(n_full + d + 1, bufs[(d + 1) % 2], diag=d + 1)
        softmax_pv(n_full + d, bufs[d % 2], diag=d, mask_here=(d == 0))

    o_ref[...] = (acc_sc[:MLA_V, :] / acc_sc[MLA_V:MLA_V + 1, :]).T.astype(o_ref.dtype)


def mla_attention(qn, qr, kn, kr, vt, weights, batch, seq, tq=4 * MLA_TILE, tk=MLA_TILE):
    assert tq % (2 * tk) == 0
    nq = seq // tq
    steps = batch * MLA_HEADS * nq

    def slab(w):
        tiles = w.shape[0] // 16
        assert tiles * 16 == w.shape[0], w.shape
        n_slab = max(n for n in range(1, steps + 1) if tiles % n == 0)
        rows = w.shape[0] // n_slab
        return pl.BlockSpec((rows, w.shape[1]),
                            lambda b, h, i: (jnp.minimum((b * MLA_HEADS + h) * nq + i, n_slab - 1), 0))

    slabs = [slab(w) for w in weights]
    outs = pl.pallas_call(
        functools.partial(_mla_kernel, tq=tq, tk=tk, n_cast=len(weights)),
        grid=(batch, MLA_HEADS, nq),
        in_specs=[pl.BlockSpec((tq, MLA_NOPE), lambda b, h, i: (b * nq + i, h)),
                  pl.BlockSpec((tq, LANES), lambda b, h, i: (b * nq + i, h // 2)),
                  pl.BlockSpec((seq, MLA_NOPE), lambda b, h, i: (b, h)),
                  pl.BlockSpec((seq, LANES), lambda b, h, i: (b, h % 2)),
                  pl.BlockSpec((1, 1, seq // tk, VT_ROWS, tk), lambda b, h, i: (b, h, 0, 0, 0))] + slabs,
        out_specs=[pl.BlockSpec((tq, MLA_V), lambda b, h, i: (b * nq + i, h))] + slabs,
        out_shape=[jax.ShapeDtypeStruct((batch * seq, MLA_WIDTH), BF16)]
        + [jax.ShapeDtypeStruct(w.shape, BF16) for w in weights],
        scratch_shapes=[pltpu.VMEM((tq // MXU_N, tk, MXU_N), F32), pltpu.VMEM((tq // MXU_N, tk, MXU_N), F32),
                        pltpu.VMEM((1, tq), F32), pltpu.VMEM((1, tq), F32),
                        pltpu.VMEM((1, tq), F32), pltpu.VMEM((VT_ROWS, tq), F32)],
        compiler_params=_cparams(("parallel", "parallel", "parallel")),
        name="mla_attention",
    )(qn, qr, kn, kr, vt, *weights)
    return outs[0], outs[1:]


def _swa_kernel(sinks_ref, q0_ref, q1_ref, q2_ref, q3_ref, kc_ref, kp_ref, vc_ref, vp_ref,
                pq_ref, pkc_ref, pkp_ref, o_ref):
    blk = pl.program_id(1)
    q_refs = (q0_ref, q1_ref, q2_ref, q3_ref)
    k_all = jnp.concatenate([kp_ref[...], kc_ref[...]], axis=0).astype(F32)
    v_all = jnp.concatenate([vp_ref[...], vc_ref[...]], axis=0).astype(F32)
    pk = jnp.concatenate([pkp_ref[0], pkc_ref[0]], axis=-1).astype(F32)
    dist = jnp.abs(pq_ref[...].astype(F32) - pk)
    row = lax.broadcasted_iota(jnp.int32, dist.shape, 0)
    col = lax.broadcasted_iota(jnp.int32, dist.shape, 1)
    rel = BLOCK + row - col
    first_key = jnp.where(blk > 0, 0, BLOCK)
    valid = (rel >= 0) & (rel < WINDOW) & (col >= first_key)
    dist_w = jnp.where(valid, dist, jnp.inf)
    lane =lax.broadcasted_iota(jnp.int32, (2 * BLOCK, LANES), 1)
    lo = lane < SWA_HEAD_DIM
    lane_o = lax.broadcasted_iota(jnp.int32, (BLOCK, LANES), 1)
    lo_o = lane_o < SWA_HEAD_DIM

    def dup(x_all, g):
        pair = x_all[:, (g // 2) * LANES:(g // 2 + 1) * LANES]
        rolled = pltpu.roll(pair, SWA_HEAD_DIM, 1)
        return jnp.where(lo, pair, rolled) if g % 2 == 0 else jnp.where(lo, rolled, pair)

    for g in range(SWA_KV_HEADS):
        kk = dup(k_all, g)
        vv = dup(v_all, g).astype(BF16)
        zero = jnp.zeros_like(kk)
        k_half = (jnp.where(lo, kk, zero).astype(BF16), jnp.where(lo, zero, kk).astype(BF16))
        for i in range(SWA_GROUP // 2):
            q_pair = q_refs[g][:, i * LANES:(i + 1) * LANES]
            outs = []
            for j in range(2):
                head = g * SWA_GROUP + 2 * i + j
                slope2 = LOG2E * 2.0 ** (-8.0 * (head + 1) / SWA_HEADS)
                s = lax.dot_general(q_pair, k_half[j], (((1,), (1,)), ((), ())),
                                    preferred_element_type=F32)
                s = s - slope2 * dist_w
                sink2 = LOG2E * sinks_ref[head]
                m = jnp.maximum(jnp.max(s, axis=-1, keepdims=True), sink2)
                e = jnp.exp2(s - m)
                denom = jnp.sum(e, axis=-1, keepdims=True) + jnp.exp2(sink2 - m)
                pv = jnp.dot(e.astype(BF16), vv, preferred_element_type=F32)
                outs.append(pv / denom)
            col0 = (g * SWA_GROUP // 2 + i) * LANES
            o_ref[:, col0:col0 + LANES] = jnp.where(lo_o, outs[0], outs[1]).astype(o_ref.dtype)


def swa_attention(proj, sinks, pos_col, pos_row, batch, seq):
    nblk = seq // BLOCK
    kvw = SWA_KV_HEADS * SWA_HEAD_DIM
    gw = SWA_GROUP * SWA_HEAD_DIM
    cur = lambda b, n: b * nblk + n
    prev = lambda b, n: b * nblk + jnp.maximum(n - 1, 0)
    q_specs = [pl.BlockSpec((BLOCK, gw), functools.partial(lambda b, n, g: (cur(b, n), COL_QSWA // gw + g), g=g))
               for g in range(SWA_KV_HEADS)]
    return pl.pallas_call(
        _swa_kernel,
        grid=(batch, nblk),
        in_specs=[pl.BlockSpec(memory_space=pltpu.SMEM)] + q_specs + [
            pl.BlockSpec((BLOCK, kvw), lambda b, n: (cur(b, n), COL_KSWA // kvw)),
            pl.BlockSpec((BLOCK, kvw), lambda b, n: (prev(b, n), COL_KSWA // kvw)),
            pl.BlockSpec((BLOCK, kvw), lambda b, n: (cur(b, n), COL_VSWA // kvw)),
            pl.BlockSpec((BLOCK, kvw), lambda b, n: (prev(b, n), COL_VSWA // kvw)),
            pl.BlockSpec((BLOCK, 1), lambda b, n: (cur(b, n), 0)),
            pl.BlockSpec((1, 1, BLOCK), lambda b, n: (cur(b, n), 0, 0)),
            pl.BlockSpec((1, 1, BLOCK), lambda b, n: (prev(b, n), 0, 0))],
        out_specs=pl.BlockSpec((BLOCK, SWA_WIDTH), lambda b, n: (cur(b, n), 0)),
        out_shape=jax.ShapeDtypeStruct((batch * seq, SWA_WIDTH), BF16),
        compiler_params=_cparams(("parallel", "parallel")),
        name="swa_attention",
    )(sinks, proj, proj, proj, proj, proj, proj, proj, proj, pos_col, pos_row, pos_row)


def _wo_kernel(oa_ref, ob_ref, ga_ref, gb_ref, w_ref, x_ref, gy_ref, gn_ref, x1_ref, h_ref, *, chunks):
    ka = oa_ref.shape[1]
    na = _rms(oa_ref[...].astype(F32), ga_ref[...]).astype(BF16)
    nb = _rms(ob_ref[...].astype(F32), gb_ref[...]).astype(BF16)
    for lo, hi in chunks:
        x1_ref[:, lo:hi] = (jnp.dot(na, w_ref[:ka, lo:hi], preferred_element_type=F32)
                            + jnp.dot(nb, w_ref[ka:, lo:hi], preferred_element_type=F32))
    rows = 32
    for r in range(x1_ref.shape[0] // rows):
        rs = slice(r * rows, (r + 1) * rows)
        x1 = x_ref[rs, :] + _rms(x1_ref[rs, :], gy_ref[...])
        x1_ref[rs, :] = x1
        h_ref[rs, :] = _rms(x1, gn_ref[...]).astype(h_ref.dtype)


def out_proj_resid(oa, ob, ga, gb, w, x, gy, gn, tm=ROW_TILE // 2):
    m, ka = oa.shape
    kb = ob.shape[1]
    n = w.shape[1]
    row = pl.BlockSpec((tm, n), lambda i: (i, 0))
    vec = pl.BlockSpec((1, n), lambda i: (0, 0))
    return pl.pallas_call(
        functools.partial(_wo_kernel, chunks=_col_chunks(n, 4 * MXU_N)),
        grid=(m // tm,),
        in_specs=[pl.BlockSpec((tm, ka), lambda i: (i, 0)),
                  pl.BlockSpec((tm, kb), lambda i: (i, 0)),
                  pl.BlockSpec((1, ka), lambda i: (0, 0)),
                  pl.BlockSpec((1, kb), lambda i: (0, 0)),
                  pl.BlockSpec((ka + kb, n), lambda i: (0, 0), pipeline_mode=pl.Buffered(1)),
                  row, vec, vec],
        out_specs=[row, row],
        out_shape=[jax.ShapeDtypeStruct((m, n), F32), jax.ShapeDtypeStruct((m, n), BF16)],
        compiler_params=_cparams(("parallel",)),
        name="out_proj",
    )(oa, ob, ga, gb, w, x, gy, gn)


def _ffn_kernel(h_ref, wg_ref, wu_ref, wd_ref, o_ref, acc_sc, *, dn, rows):
    j = pl.program_id(1)
    tm, d = acc_sc.shape

    @pl.when(j == 0)
    def _():
        acc_sc[...] = jnp.zeros(acc_sc.shape, acc_sc.dtype)

    h = h_ref[...]
    g = jnp.dot(h, wg_ref[...], preferred_element_type=F32)
    u = jnp.dot(h, wu_ref[...], preferred_element_type=F32)
    a = (g * jax.nn.sigmoid(g) * u).astype(BF16)
    for c in range(d // dn):
        acc_sc[:, c * dn:(c + 1) * dn] += jnp.dot(a, wd_ref[:, c * dn:(c + 1) * dn],
                                                  preferred_element_type=F32)

    @pl.when(j == pl.num_programs(1) - 1)
    def _():
        def body(r, carry):
            rs = pl.ds(pl.multiple_of(r * rows, rows), rows)
            o_ref[rs, :] = acc_sc[rs, :].astype(o_ref.dtype)
            return carry

        lax.fori_loop(0, tm // rows, body, 0)


def ffn(h, wg, wu, wd, tm=1024, tf=FFN_TILE):
    m, d = h.shape
    f = wg.shape[1]
    return pl.pallas_call(
        functools.partial(_ffn_kernel, dn=1024, rows=64),
        grid=(m // tm, f // tf),
        in_specs=[pl.BlockSpec((tm, d), lambda i, j: (i, 0)),
                  pl.BlockSpec((d, tf), lambda i, j: (0, j)),
                  pl.BlockSpec((d, tf), lambda i, j: (0, j)),
                  pl.BlockSpec((tf, d), lambda i, j: (j, 0))],
        out_specs=pl.BlockSpec((tm, d), lambda i, j: (i, 0), pipeline_mode=pl.Buffered(1)),
        out_shape=jax.ShapeDtypeStruct((m, d), BF16),
        scratch_shapes=[pltpu.VMEM((tm, d), F32)],
        compiler_params=_cparams(("parallel", "arbitrary")),
        name="ffn",
    )(h, wg, wu, wd)


def _ple_kernel(x_ref, f_ref, gf_ref, wg_ref, p_ref, wp_ref, o_ref, *, chunks):
    o_ref[...] = x_ref[...] + _rms(f_ref[...].astype(F32), gf_ref[...])
    xb = o_ref[...].astype(BF16)
    pb = p_ref[...].astype(BF16)
    for lo, hi in chunks:
        gate = jax.nn.sigmoid(jnp.dot(xb, wg_ref[:, lo:hi], preferred_element_type=F32))
        e = jnp.dot(pb, wp_ref[:, lo:hi], preferred_element_type=F32)
        o_ref[:, lo:hi] += gate * e


def ple(x, f, g_f, wg, p, wp, tm=ROW_TILE):
    m, d = x.shape
    n = wg.shape[1]
    pd = p.shape[1]
    once = dict(pipeline_mode=pl.Buffered(1))
    return pl.pallas_call(
        functools.partial(_ple_kernel, chunks=_col_chunks(n, 4 * MXU_N)),
        grid=(m // tm,),
        in_specs=[pl.BlockSpec((tm, d), lambda i: (i, 0)),
                  pl.BlockSpec((tm, d), lambda i: (i, 0)),
                  pl.BlockSpec((1, d), lambda i: (0, 0)),
                  pl.BlockSpec((d, n), lambda i: (0, 0), **once),
                  pl.BlockSpec((tm, pd), lambda i: (i, 0)),
                  pl.BlockSpec((pd, n), lambda i: (0, 0), **once)],
        out_specs=pl.BlockSpec((tm, n), lambda i: (i, 0)),
        out_shape=jax.ShapeDtypeStruct((m, n), F32),
        compiler_params=_cparams(("parallel",)),
        name="ple",
    )(x, f, g_f, wg, p, wp)


def _w_in_prep_kernel(w_ref, o_ref):
    w = w_ref[0]
    kr0 = Q_LORA + KV_LORA
    q0 = kr0 + MLA_ROPE
    kv0 = q0 + SWA_WIDTH
    o_ref[:, :kr0] = w[:, :kr0].astype(BF16)
    o_ref[:, COL_QSWA:COL_KSWA] = (w[:, q0:kv0] * (LOG2E / math.sqrt(SWA_HEAD_DIM))).astype(BF16)
    o_ref[:, COL_KSWA:COL_KR] = w[:, kv0:].astype(BF16)
    blk = w[:, kr0:kr0 + LANES]
    lane = lax.broadcasted_iota(jnp.int32, blk.shape, 1)
    half = MLA_ROPE // 2
    partner = jnp.where(lane < MLA_ROPE + half, -pltpu.roll(blk, half, 1), pltpu.roll(blk, MLA_ROPE + half, 1))
    o_ref[:, COL_KR:] = jnp.where(lane < MLA_ROPE, blk, partner).astype(BF16)


def _prep_w_in(w3, layer, rows=256):
    _, k, n = w3.shape
    return pl.pallas_call(
        _w_in_prep_kernel,
        grid=(k // rows,),
        in_specs=[pl.BlockSpec((1, rows, n), lambda i: (layer, i, 0))],
        out_specs=pl.BlockSpec((rows, D_IN2), lambda i: (i, 0)),
        out_shape=jax.ShapeDtypeStruct((k, D_IN2), BF16),
        compiler_params=_cparams(("parallel",)),
        name="w_in_prep",
    )(w3)


def _prep_w_q_up(w):
    w = (w * (LOG2E / math.sqrt(MLA_NOPE + MLA_ROPE))).reshape(Q_LORA, MLA_HEADS, MLA_NOPE + MLA_ROPE)
    nope = w[:, :, :MLA_NOPE].reshape(Q_LORA, MLA_WIDTH)
    rope = w[:, :, MLA_NOPE:].reshape(Q_LORA, MLA_HEADS * MLA_ROPE)
    return jnp.concatenate([rope, nope], axis=1).astype(BF16)


def _prep_w_kv_up(w):
    w = w.reshape(KV_LORA, MLA_HEADS, MLA_NOPE + MLA_V)
    wk = w[:, :, :MLA_NOPE].reshape(KV_LORA, MLA_WIDTH).astype(BF16)
    wvt = w[:, :, MLA_NOPE:].reshape(KV_LORA, MLA_WIDTH).T.astype(BF16)
    return wk, wvt


def _layer(x, p, pos_col, pos_row, invf, batch, seq, attn_pre_norm, w_in, q_a_norm, w_q_up,
           kv_a_norm, w_kv_up, sinks, mla_out_norm, swa_out_norm, w_o, attn_post_norm,
           ffn_pre_norm, w_gate, w_up, w_down, ffn_post_norm, w_ple_gate, w_ple_proj):
    row = lambda g: g.reshape(1, -1)
    proj = in_proj(x, row(attn_pre_norm), _prep_w_in(*w_in))
    qn, qr = q_up(proj, row(q_a_norm), _prep_w_q_up(w_q_up), pos_col, invf)
    wk, wvt = _prep_w_kv_up(w_kv_up)
    kn, vt, kr = kv_up(proj, row(kv_a_norm), wk, wvt, pos_col, invf, batch, seq)
    o_mla, (wo16, wg16, wu16, wd16, wpg16) = mla_attention(
        qn, qr, kn, kr, vt, [w_o, w_gate, w_up, w_down, w_ple_gate], batch, seq)
    o_swa = swa_attention(proj, sinks, pos_col, pos_row, batch, seq)
    x1, h2 = out_proj_resid(o_mla, o_swa, row(mla_out_norm), row(swa_out_norm), wo16,
                            x, row(attn_post_norm), row(ffn_pre_norm))
    f = ffn(h2, wg16, wu16, wd16)
    return ple(x1, f, row(ffn_post_norm), wpg16, p, w_ple_proj.astype(BF16))


def kernel(x, p, positions, attn_pre_norm, w_in, q_a_norm, w_q_up, kv_a_norm, w_kv_up, sinks,
           mla_out_norm, swa_out_norm, w_o, attn_post_norm, ffn_pre_norm, w_gate, w_up, w_down,
           ffn_post_norm, w_ple_gate, w_ple_proj):
    batch, seq, d = x.shape
    depth = w_in.shape[0]
    t = batch * seq
    pos_col = positions.reshape(t, 1)
    pos_row = positions.reshape(t // BLOCK, 1, BLOCK)
    half = MLA_ROPE // 2
    invf = np.asarray(ROPE_THETA ** (-(np.arange(LANES) % half) * 2.0 / MLA_ROPE), np.float32).reshape(1, LANES)
    invf = jnp.asarray(invf)
    xf = x.reshape(t, d)
    for i in range(depth):
        xf = _layer(xf, p[i].reshape(t, PLE_DIM), pos_col, pos_row, invf, batch, seq,
                    attn_pre_norm[i], (w_in, i), q_a_norm[i], w_q_up[i], kv_a_norm[i], w_kv_up[i],
                    sinks[i], mla_out_norm[i], swa_out_norm[i], w_o[i], attn_post_norm[i],
                    ffn_pre_norm[i], w_gate[i], w_up[i], w_down[i], ffn_post_norm[i],
                    w_ple_gate[i], w_ple_proj[i])
    return xf.reshape(batch, seq, d)
```

```python
import functools
import math

import numpy as np
import jax
import jax.numpy as jnp
from jax import lax
from jax.experimental import pallas as pl
from jax.experimental.pallas import tpu as pltpu

D_MODEL = 4096
PLE_DIM = 256
MLA_HEADS = 16
MLA_NOPE = 128
MLA_ROPE = 64
MLA_V = 128
Q_LORA = 1024
KV_LORA = 512
MLA_WIDTH = MLA_HEADS * MLA_V
SWA_HEADS = 32
SWA_KV_HEADS = 4
SWA_HEAD_DIM = 64
SWA_GROUP = SWA_HEADS // SWA_KV_HEADS
SWA_WIDTH = SWA_HEADS * SWA_HEAD_DIM
WINDOW = 128
BLOCK = 128
ROPE_THETA = 10000.0
NORM_EPS = 1e-6
LOG2E = math.log2(math.e)
D_FF = 11008

LANES = 128
MXU_N = 256
MLA_TILE = 512
ROW_TILE = 256
FFN_TILE = 256
VT_ROWS = MLA_V + 16
VMEM_LIMIT = 60 * 1024 * 1024

COL_CQ = 0
COL_CKV = COL_CQ + Q_LORA
COL_QSWA = COL_CKV + KV_LORA
COL_KSWA = COL_QSWA + SWA_WIDTH
COL_VSWA = COL_KSWA + SWA_KV_HEADS * SWA_HEAD_DIM
COL_KR = COL_VSWA + SWA_KV_HEADS * SWA_HEAD_DIM
D_IN2 = COL_KR + 2 * MLA_ROPE

F32 = jnp.float32
BF16 = jnp.bfloat16


def _cparams(sem):
    return pltpu.CompilerParams(dimension_semantics=sem, vmem_limit_bytes=VMEM_LIMIT)


def _rms(xf, g):
    ms = jnp.mean(xf * xf, axis=-1, keepdims=True)
    return xf * lax.rsqrt(ms + NORM_EPS) * g


def _swap_half(w):
    d = w.shape[-1]
    return jnp.concatenate([-w[..., d // 2:], w[..., : d // 2]], axis=-1)


def _rope_tables(pos_col, invf_row):
    ang = pos_col.astype(F32) * invf_row
    return jnp.cos(ang), jnp.sin(ang)


def _col_chunks(n, width):
    return [(c, min(c + width, n)) for c in range(0, n, width)]


def _in_proj_kernel(x_ref, g_ref, w_ref, o_ref, *, chunks):
    h = _rms(x_ref[...], g_ref[...]).astype(BF16)
    for lo, hi in chunks:
        o_ref[:, lo:hi] = jnp.dot(h, w_ref[:, lo:hi], preferred_element_type=F32).astype(o_ref.dtype)


def in_proj(x, g, w, tm=ROW_TILE):
    m, k = x.shape
    n = w.shape[1]
    return pl.pallas_call(
        functools.partial(_in_proj_kernel, chunks=_col_chunks(n, 6 * MXU_N)),
        grid=(m // tm,),
        in_specs=[pl.BlockSpec((tm, k), lambda i: (i, 0)),
                  pl.BlockSpec((1, k), lambda i: (0, 0)),
                  pl.BlockSpec((k, n), lambda i: (0, 0), pipeline_mode=pl.Buffered(1))],
        out_specs=pl.BlockSpec((tm, n), lambda i: (i, 0)),
        out_shape=jax.ShapeDtypeStruct((m, n), BF16),
        compiler_params=_cparams(("parallel",)),
        name="in_proj",
    )(x, g, w)


def _q_up_kernel(cq_ref, g_ref, w_ref, pos_ref, invf_ref, qn_ref, qr_ref):
    h = _rms(cq_ref[...].astype(F32), g_ref[...]).astype(BF16)
    n_rope = MLA_HEADS * MLA_ROPE
    rot = jnp.dot(h, w_ref[:, :n_rope], preferred_element_type=F32)
    qn_ref[...] = jnp.dot(h, w_ref[:, n_rope:], preferred_element_type=F32).astype(BF16)
    cos, sin = _rope_tables(pos_ref[...], invf_ref[...])
    half = MLA_ROPE // 2
    lane = lax.broadcasted_iota(jnp.int32, cos.shape, 1)
    first_half = (lane % MLA_ROPE) < half
    for c in range(n_rope // LANES):
        a = rot[:, c * LANES:(c + 1) * LANES]
        b = jnp.where(first_half, -pltpu.roll(a, LANES - half, 1), pltpu.roll(a, half, 1))
        qr_ref[:, c * LANES:(c + 1) * LANES] = (a * cos + b * sin).astype(BF16)


def q_up(proj, g, w, pos_col, invf, tm=512):
    m = proj.shape[0]
    n_rope = MLA_HEADS * MLA_ROPE
    return pl.pallas_call(
        _q_up_kernel,
        grid=(m // tm,),
        in_specs=[pl.BlockSpec((tm, Q_LORA), lambda i: (i, COL_CQ // Q_LORA)),
                  pl.BlockSpec((1, Q_LORA), lambda i: (0, 0)),
                  pl.BlockSpec(w.shape, lambda i: (0, 0)),
                  pl.BlockSpec((tm, 1), lambda i: (i, 0)),
                  pl.BlockSpec((1, LANES), lambda i: (0, 0))],
        out_specs=[pl.BlockSpec((tm, MLA_WIDTH), lambda i: (i, 0)),
                   pl.BlockSpec((tm, n_rope), lambda i: (i, 0))],
        out_shape=[jax.ShapeDtypeStruct((m, MLA_WIDTH), BF16),
                   jax.ShapeDtypeStruct((m, n_rope), BF16)],
        compiler_params=_cparams(("parallel",)),
        name="q_up",
    )(proj, g, w, pos_col, invf)


def _kv_up_kernel(ckv_ref, g_ref, wk_ref, wvt_ref, krab_ref, pos_ref, invf_ref, kn_ref, vt_ref, kr_ref):
    h = _rms(ckv_ref[...].astype(F32), g_ref[...]).astype(BF16)
    kn_ref[...] = jnp.dot(h, wk_ref[...], preferred_element_type=F32).astype(BF16)
    vt = lax.dot_general(wvt_ref[...], h, (((1,), (1,)), ((), ())), preferred_element_type=F32)
    ones = jnp.ones((VT_ROWS - MLA_V, vt.shape[1]), BF16)
    for hh in range(MLA_HEADS):
        vt_ref[0, hh, 0, :MLA_V, :] = vt[hh * MLA_V:(hh + 1) * MLA_V, :].astype(BF16)
        vt_ref[0, hh, 0, MLA_V:, :] = ones
    cos, sin = _rope_tables(pos_ref[...], invf_ref[...])
    lane = lax.broadcasted_iota(jnp.int32, cos.shape, 1)
    lo = lane < MLA_ROPE
    prod = krab_ref[...].astype(F32) * jnp.where(lo, cos, sin)
    kk = prod + pltpu.roll(prod, MLA_ROPE, 1)
    zero = jnp.zeros_like(kk)
    kr_ref[:, :LANES] = jnp.where(lo, kk, zero).astype(BF16)
    kr_ref[:, LANES:] = jnp.where(lo, zero, kk).astype(BF16)


def kv_up(proj, g, wk, wvt, pos_col, invf, batch, seq, tm=MLA_TILE):
    m = proj.shape[0]
    nk = seq // tm
    return pl.pallas_call(
        _kv_up_kernel,
        grid=(m // tm,),
        in_specs=[pl.BlockSpec((tm, KV_LORA), lambda i: (i, COL_CKV // KV_LORA)),
                  pl.BlockSpec((1, KV_LORA), lambda i: (0, 0)),
                  pl.BlockSpec(wk.shape, lambda i: (0, 0)),
                  pl.BlockSpec(wvt.shape, lambda i: (0, 0)),
                  pl.BlockSpec((tm, LANES), lambda i: (i, COL_KR // LANES)),
                  pl.BlockSpec((tm, 1), lambda i: (i, 0)),
                  pl.BlockSpec((1, LANES), lambda i: (0, 0))],
        out_specs=[pl.BlockSpec((tm, MLA_WIDTH), lambda i: (i, 0)),
                   pl.BlockSpec((1, MLA_HEADS, 1, VT_ROWS, tm), lambda i: (i // nk, 0, i % nk, 0, 0)),
                   pl.BlockSpec((tm, 2 * LANES), lambda i: (i, 0))],
        out_shape=[jax.ShapeDtypeStruct((m, MLA_WIDTH), BF16),
                   jax.ShapeDtypeStruct((batch, MLA_HEADS, nk, VT_ROWS, tm), BF16),
                   jax.ShapeDtypeStruct((m, 2 * LANES), BF16)],
        compiler_params=_cparams(("parallel",)),
        name="kv_up",
    )(proj, g, wk, wvt, proj, pos_col, invf)


def _mla_kernel(*refs, tq, tk, n_cast):
    qn_ref, qr_ref, kn_ref, kr_ref, vt_ref = refs[:5]
    w_refs = refs[5:5 + n_cast]
    o_ref = refs[5 + n_cast]
    w16_refs = refs[6 + n_cast:6 + 2 * n_cast]
    s_a, s_b, mx_a, mx_b, m_sc, acc_sc = refs[6 + 2 * n_cast:]

    qi = pl.program_id(2)
    q = jnp.concatenate([qn_ref[...], qr_ref[...]], axis=-1)
    m_sc[...] = jnp.full(m_sc.shape, -jnp.inf, F32)
    acc_sc[...] = jnp.zeros(acc_sc.shape, F32)

    n_chunk = tq // MXU_N
    blk_chunks = tk // MXU_N
    n_diag = tq // tk
    bufs = ((s_a, mx_a), (s_b, mx_b))

    def live_chunks(diag):
        return range(0 if diag is None else diag * blk_chunks, n_chunk)

    def scores(kb, buf, diag=None):
        s_ref, mx_ref = buf
        off = pl.multiple_of(kb * tk, tk)
        k = jnp.concatenate([kn_ref[pl.ds(off, tk), :], kr_ref[pl.ds(off, tk), :]], axis=-1)
        for c in live_chunks(diag):
            s = lax.dot_general(k, q[c * MXU_N:(c + 1) * MXU_N, :], (((1,), (1,)), ((), ())),
                                preferred_element_type=F32)
            if diag is not None and c < (diag + 1) * blk_chunks:
                key = lax.broadcasted_iota(jnp.int32, s.shape, 0) + diag * tk
                qry = lax.broadcasted_iota(jnp.int32, s.shape, 1) + c * MXU_N
                s = jnp.where(qry >= key, s, -jnp.inf)
            s_ref[c] = s
            mx_ref[:, c * MXU_N:(c + 1) * MXU_N] = jnp.max(s, axis=0, keepdims=True)

    def softmax_pv(kb, buf, diag=None, mask_here=False):
        s_ref, mx_ref = buf
        vt = vt_ref[0, 0, kb]
        for c in live_chunks(diag):
            sl = slice(c * MXU_N, (c + 1) * MXU_N)
            remask = mask_here and c < blk_chunks

            def s_chunk():
                s = s_ref[c]
                if remask:
                    key = lax.broadcasted_iota(jnp.int32, s.shape, 0)
                    qry = lax.broadcasted_iota(jnp.int32, s.shape, 1) + c * MXU_N
                    s = jnp.where(qry >= key, s, -jnp.inf)
                return s

            mx = jnp.max(s_chunk(), axis=0, keepdims=True) if remask else mx_ref[:, sl]
            m_old = m_sc[:, sl]
            m_new = jnp.maximum(m_old, mx)
            alpha = jnp.exp2(m_old - m_new)
            p = jnp.exp2(s_chunk() - m_new).astype(BF16)
            acc_sc[:, sl] = alpha * acc_sc[:, sl] + jnp.dot(vt, p, preferred_element_type=F32)
            m_sc[:, sl] = m_new

    n_full = n_diag * qi
    scores(0, bufs[0])

    def body(j, carry):
        for t in range(n_diag):
            kb = n_diag * j + t
            scores(kb + 1, bufs[(t + 1) % 2])
            softmax_pv(kb, bufs[t % 2])
        return carry

    lax.fori_loop(0, qi, body, 0)

    for w_ref, w16_ref in zip(w_refs, w16_refs):
        w16_ref[...] = w_ref[...].astype(BF16)

    for d in range(n_diag):
        if d + 1 < n_diag:
            scores(n_full + d + 1, bufs[(d + 1) % 2], diag=d + 1)
        softmax_pv(n_full + d, bufs[d % 2], diag=d, mask_here=(d == 0))

    o_ref[...] = (acc_sc[:MLA_V, :] / acc_sc[MLA_V:MLA_V + 1, :]).T.astype(o_ref.dtype)


def mla_attention(qn, qr, kn, kr, vt, weights, batch, seq, tq=4 * MLA_TILE, tk=MLA_TILE):
    assert tq % (2 * tk) == 0
    nq = seq // tq
    steps = batch * MLA_HEADS * nq

    def slab(w):
        tiles = w.shape[0] // 16
        assert tiles * 16 == w.shape[0], w.shape
        n_slab = max(n for n in range(1, steps + 1) if tiles % n == 0)
        rows = w.shape[0] // n_slab
        return pl.BlockSpec((rows, w.shape[1]),
                            lambda b, h, i: (jnp.minimum((b * MLA_HEADS + h) * nq + i, n_slab - 1), 0))

    slabs = [slab(w) for w in weights]
    outs = pl.pallas_call(
        functools.partial(_mla_kernel, tq=tq, tk=tk, n_cast=len(weights)),
        grid=(batch, MLA_HEADS, nq),
        in_specs=[pl.BlockSpec((tq, MLA_NOPE), lambda b, h, i: (b * nq + i, h)),
                  pl.BlockSpec((tq, LANES), lambda b, h, i: (b * nq + i, h // 2)),
                  pl.BlockSpec((seq, MLA_NOPE), lambda b, h, i: (b, h)),
                  pl.BlockSpec((seq, LANES), lambda b, h, i: (b, h % 2)),
                  pl.BlockSpec((1, 1, seq // tk, VT_ROWS, tk), lambda b, h, i: (b, h, 0, 0, 0))] + slabs,
        out_specs=[pl.BlockSpec((tq, MLA_V), lambda b, h, i: (b * nq + i, h))] + slabs,
        out_shape=[jax.ShapeDtypeStruct((batch * seq, MLA_WIDTH), BF16)]
        + [jax.ShapeDtypeStruct(w.shape, BF16) for w in weights],
        scratch_shapes=[pltpu.VMEM((tq // MXU_N, tk, MXU_N), F32), pltpu.VMEM((tq // MXU_N, tk, MXU_N), F32),
                        pltpu.VMEM((1, tq), F32), pltpu.VMEM((1, tq), F32),
                        pltpu.VMEM((1, tq), F32), pltpu.VMEM((VT_ROWS, tq), F32)],
        compiler_params=_cparams(("parallel", "parallel", "parallel")),
        name="mla_attention",
    )(qn, qr, kn, kr, vt, *weights)
    return outs[0], outs[1:]


def _swa_kernel(sinks_ref, q0_ref, q1_ref, q2_ref, q3_ref, kc_ref, kp_ref, vc_ref, vp_ref,
                pq_ref, pkc_ref, pkp_ref, o_ref):
    blk = pl.program_id(1)
    q_refs = (q0_ref, q1_ref, q2_ref, q3_ref)
    k_all = jnp.concatenate([kp_ref[...], kc_ref[...]], axis=0).astype(F32)
    v_all = jnp.concatenate([vp_ref[...], vc_ref[...]], axis=0).astype(F32)
    pk = jnp.concatenate([pkp_ref[0], pkc_ref[0]], axis=-1).astype(F32)
    dist = jnp.abs(pq_ref[...].astype(F32) - pk)
    row = lax.broadcasted_iota(jnp.int32, dist.shape, 0)
    col = lax.broadcasted_iota(jnp.int32, dist.shape, 1)
    rel = BLOCK + row - col
    first_key = jnp.where(blk > 0, 0, BLOCK)
    valid = (rel >= 0) & (rel < WINDOW) & (col >= first_key)
    dist_w = jnp.where(valid, dist, jnp.inf)
    lane =lax.broadcasted_iota(jnp.int32, (2 * BLOCK, LANES), 1)
    lo = lane < SWA_HEAD_DIM
    lane_o = lax.broadcasted_iota(jnp.int32, (BLOCK, LANES), 1)
    lo_o = lane_o < SWA_HEAD_DIM

    def dup(x_all, g):
        pair = x_all[:, (g // 2) * LANES:(g // 2 + 1) * LANES]
        rolled = pltpu.roll(pair, SWA_HEAD_DIM, 1)
        return jnp.where(lo, pair, rolled) if g % 2 == 0 else jnp.where(lo, rolled, pair)

    for g in range(SWA_KV_HEADS):
        kk = dup(k_all, g)
        vv = dup(v_all, g).astype(BF16)
        zero = jnp.zeros_like(kk)
        k_half = (jnp.where(lo, kk, zero).astype(BF16), jnp.where(lo, zero, kk).astype(BF16))
        for i in range(SWA_GROUP // 2):
            q_pair = q_refs[g][:, i * LANES:(i + 1) * LANES]
            outs = []
            for j in range(2):
                head = g * SWA_GROUP + 2 * i + j
                slope2 = LOG2E * 2.0 ** (-8.0 * (head + 1) / SWA_HEADS)
                s = lax.dot_general(q_pair, k_half[j], (((1,), (1,)), ((), ())),
                                    preferred_element_type=F32)
                s = s - slope2 * dist_w
                sink2 = LOG2E * sinks_ref[head]
                m = jnp.maximum(jnp.max(s, axis=-1, keepdims=True), sink2)
                e = jnp.exp2(s - m)
                denom = jnp.sum(e, axis=-1, keepdims=True) + jnp.exp2(sink2 - m)
                pv = jnp.dot(e.astype(BF16), vv, preferred_element_type=F32)
                outs.append(pv / denom)
            col0 = (g * SWA_GROUP // 2 + i) * LANES
            o_ref[:, col0:col0 + LANES] = jnp.where(lo_o, outs[0], outs[1]).astype(o_ref.dtype)


def swa_attention(proj, sinks, pos_col, pos_row, batch, seq):
    nblk = seq // BLOCK
    kvw = SWA_KV_HEADS * SWA_HEAD_DIM
    gw = SWA_GROUP * SWA_HEAD_DIM
    cur = lambda b, n: b * nblk + n
    prev = lambda b, n: b * nblk + jnp.maximum(n - 1, 0)
    q_specs = [pl.BlockSpec((BLOCK, gw), functools.partial(lambda b, n, g: (cur(b, n), COL_QSWA // gw + g), g=g))
               for g in range(SWA_KV_HEADS)]
    return pl.pallas_call(
        _swa_kernel,
        grid=(batch, nblk),
        in_specs=[pl.BlockSpec(memory_space=pltpu.SMEM)] + q_specs + [
            pl.BlockSpec((BLOCK, kvw), lambda b, n: (cur(b, n), COL_KSWA // kvw)),
            pl.BlockSpec((BLOCK, kvw), lambda b, n: (prev(b, n), COL_KSWA // kvw)),
            pl.BlockSpec((BLOCK, kvw), lambda b, n: (cur(b, n), COL_VSWA // kvw)),
            pl.BlockSpec((BLOCK, kvw), lambda b, n: (prev(b, n), COL_VSWA // kvw)),
            pl.BlockSpec((BLOCK, 1), lambda b, n: (cur(b, n), 0)),
            pl.BlockSpec((1, 1, BLOCK), lambda b, n: (cur(b, n), 0, 0)),
            pl.BlockSpec((1, 1, BLOCK), lambda b, n: (prev(b, n), 0, 0))],
        out_specs=pl.BlockSpec((BLOCK, SWA_WIDTH), lambda b, n: (cur(b, n), 0)),
        out_shape=jax.ShapeDtypeStruct((batch * seq, SWA_WIDTH), BF16),
        compiler_params=_cparams(("parallel", "parallel")),
        name="swa_attention",
    )(sinks, proj, proj, proj, proj, proj, proj, proj, proj, pos_col, pos_row, pos_row)


def _wo_kernel(oa_ref, ob_ref, ga_ref, gb_ref, w_ref, x_ref, gy_ref, gn_ref, x1_ref, h_ref, *, chunks):
    ka = oa_ref.shape[1]
    na = _rms(oa_ref[...].astype(F32), ga_ref[...]).astype(BF16)
    nb = _rms(ob_ref[...].astype(F32), gb_ref[...]).astype(BF16)
    for lo, hi in chunks:
        x1_ref[:, lo:hi] = (jnp.dot(na, w_ref[:ka, lo:hi], preferred_element_type=F32)
                            + jnp.dot(nb, w_ref[ka:, lo:hi], preferred_element_type=F32))
    rows = 32
    for r in range(x1_ref.shape[0] // rows):
        rs = slice(r * rows, (r + 1) * rows)
        x1 = x_ref[rs, :] + _rms(x1_ref[rs, :], gy_ref[...])
        x1_ref[rs, :] = x1
        h_ref[rs, :] = _rms(x1, gn_ref[...]).astype(h_ref.dtype)


def out_proj_resid(oa, ob, ga, gb, w, x, gy, gn, tm=ROW_TILE // 2):
    m, ka = oa.shape
    kb = ob.shape[1]
    n = w.shape[1]
    row = pl.BlockSpec((tm, n), lambda i: (i, 0))
    vec = pl.BlockSpec((1, n), lambda i: (0, 0))
    return pl.pallas_call(
        functools.partial(_wo_kernel, chunks=_col_chunks(n, 4 * MXU_N)),
        grid=(m // tm,),
        in_specs=[pl.BlockSpec((tm, ka), lambda i: (i, 0)),
                  pl.BlockSpec((tm, kb), lambda i: (i, 0)),
                  pl.BlockSpec((1, ka), lambda i: (0, 0)),
                  pl.BlockSpec((1, kb), lambda i: (0, 0)),
                  pl.BlockSpec((ka + kb, n), lambda i: (0, 0), pipeline_mode=pl.Buffered(1)),
                  row, vec, vec],
        out_specs=[row, row],
        out_shape=[jax.ShapeDtypeStruct((m, n), F32), jax.ShapeDtypeStruct((m, n), BF16)],
        compiler_params=_cparams(("parallel",)),
        name="out_proj",
    )(oa, ob, ga, gb, w, x, gy, gn)


def _ffn_kernel(h_ref, wg_ref, wu_ref, wd_ref, o_ref, acc_sc, *, dn, rows):
    j = pl.program_id(1)
    tm, d = acc_sc.shape

    @pl.when(j == 0)
    def _():
        acc_sc[...] = jnp.zeros(acc_sc.shape, acc_sc.dtype)

    h = h_ref[...]
    g = jnp.dot(h, wg_ref[...], preferred_element_type=F32)
    u = jnp.dot(h, wu_ref[...], preferred_element_type=F32)
    a = (g * jax.nn.sigmoid(g) * u).astype(BF16)
    for c in range(d // dn):
        acc_sc[:, c * dn:(c + 1) * dn] += jnp.dot(a, wd_ref[:, c * dn:(c + 1) * dn],
                                                  preferred_element_type=F32)

    @pl.when(j == pl.num_programs(1) - 1)
    def _():
        def body(r, carry):
            rs = pl.ds(pl.multiple_of(r * rows, rows), rows)
            o_ref[rs, :] = acc_sc[rs, :].astype(o_ref.dtype)
            return carry

        lax.fori_loop(0, tm // rows, body, 0)


def ffn(h, wg, wu, wd, tm=1024, tf=FFN_TILE):
    m, d = h.shape
    f = wg.shape[1]
    return pl.pallas_call(
        functools.partial(_ffn_kernel, dn=1024, rows=64),
        grid=(m // tm, f // tf),
        in_specs=[pl.BlockSpec((tm, d), lambda i, j: (i, 0)),
                  pl.BlockSpec((d, tf), lambda i, j: (0, j)),
                  pl.BlockSpec((d, tf), lambda i, j: (0, j)),
                  pl.BlockSpec((tf, d), lambda i, j: (j, 0))],
        out_specs=pl.BlockSpec((tm, d), lambda i, j: (i, 0), pipeline_mode=pl.Buffered(1)),
        out_shape=jax.ShapeDtypeStruct((m, d), BF16),
        scratch_shapes=[pltpu.VMEM((tm, d), F32)],
        compiler_params=_cparams(("parallel", "arbitrary")),
        name="ffn",
    )(h, wg, wu, wd)


def _ple_kernel(x_ref, f_ref, gf_ref, wg_ref, p_ref, wp_ref, o_ref, *, chunks):
    o_ref[...] = x_ref[...] + _rms(f_ref[...].astype(F32), gf_ref[...])
    xb = o_ref[...].astype(BF16)
    pb = p_ref[...].astype(BF16)
    for lo, hi in chunks:
        gate = jax.nn.sigmoid(jnp.dot(xb, wg_ref[:, lo:hi], preferred_element_type=F32))
        e = jnp.dot(pb, wp_ref[:, lo:hi], preferred_element_type=F32)
        o_ref[:, lo:hi] += gate * e


def ple(x, f, g_f, wg, p, wp, tm=ROW_TILE):
    m, d = x.shape
    n = wg.shape[1]
    pd = p.shape[1]
    once = dict(pipeline_mode=pl.Buffered(1))
    return pl.pallas_call(
        functools.partial(_ple_kernel, chunks=_col_chunks(n, 4 * MXU_N)),
        grid=(m // tm,),
        in_specs=[pl.BlockSpec((tm, d), lambda i: (i, 0)),
                  pl.BlockSpec((tm, d), lambda i: (i, 0)),
                  pl.BlockSpec((1, d), lambda i: (0, 0)),
                  pl.BlockSpec((d, n), lambda i: (0, 0), **once),
                  pl.BlockSpec((tm, pd), lambda i: (i, 0)),
                  pl.BlockSpec((pd, n), lambda i: (0, 0), **once)],
        out_specs=pl.BlockSpec((tm, n), lambda i: (i, 0)),
        out_shape=jax.ShapeDtypeStruct((m, n), F32),
        compiler_params=_cparams(("parallel",)),
        name="ple",
    )(x, f, g_f, wg, p, wp)


_KR_BLOCK = (Q_LORA + KV_LORA) // MLA_ROPE
_QSWA_BLOCKS = (COL_QSWA // LANES, COL_KSWA // LANES)


def _w_in_prep_kernel(wa_ref, wb_ref, o_ref):
    ob = pl.program_id(0)
    a, b = wa_ref[0], wb_ref[0]
    half = MLA_ROPE // 2
    partner = jnp.concatenate([-b[half:], b[:half]], axis=0)
    b = jnp.where(ob == pl.num_programs(0) - 1, partner, b)
    scale = jnp.where((ob >= _QSWA_BLOCKS[0]) & (ob < _QSWA_BLOCKS[1]), LOG2E / math.sqrt(SWA_HEAD_DIM), 1.0)
    blk = jnp.concatenate([a, b], axis=0) * scale
    o_ref[...] = blk.T.astype(BF16)


def _prep_w_in(w3, layer):
    wt = jnp.swapaxes(w3, 1, 2)
    _, n, k = wt.shape
    n_out = D_IN2 // LANES
    shifted = COL_QSWA // LANES

    def src(ob, part):
        blk = 2 * ob + part + jnp.where(ob >= shifted, 1, 0)
        return jnp.where(ob == n_out - 1, _KR_BLOCK, blk)

    return pl.pallas_call(
        _w_in_prep_kernel,
        grid=(n_out,),
        in_specs=[pl.BlockSpec((1, MLA_ROPE, k), lambda ob: (layer, src(ob, 0), 0)),
                  pl.BlockSpec((1, MLA_ROPE, k), lambda ob: (layer, src(ob, 1), 0))],
        out_specs=pl.BlockSpec((k, LANES), lambda ob: (0, ob)),
        out_shape=jax.ShapeDtypeStruct((k, D_IN2), BF16),
        compiler_params=_cparams(("parallel",)),
        name="w_in_prep",
    )(wt, wt)


def _prep_w_q_up(w):
    w = (w * (LOG2E / math.sqrt(MLA_NOPE + MLA_ROPE))).reshape(Q_LORA, MLA_HEADS, MLA_NOPE + MLA_ROPE)
    nope = w[:, :, :MLA_NOPE].reshape(Q_LORA, MLA_WIDTH)
    rope = w[:, :, MLA_NOPE:].reshape(Q_LORA, MLA_HEADS * MLA_ROPE)
    return jnp.concatenate([rope, nope], axis=1).astype(BF16)


def _prep_w_kv_up(w):
    w = w.reshape(KV_LORA, MLA_HEADS, MLA_NOPE + MLA_V)
    wk = w[:, :, :MLA_NOPE].reshape(KV_LORA, MLA_WIDTH).astype(BF16)
    wvt = w[:, :, MLA_NOPE:].reshape(KV_LORA, MLA_WIDTH).T.astype(BF16)
    return wk, wvt


def _layer(x, p, pos_col, pos_row, invf, batch, seq, attn_pre_norm, w_in, q_a_norm, w_q_up,
           kv_a_norm, w_kv_up, sinks, mla_out_norm, swa_out_norm, w_o, attn_post_norm,
           ffn_pre_norm, w_gate, w_up, w_down, ffn_post_norm, w_ple_gate, w_ple_proj):
    row = lambda g: g.reshape(1, -1)
    proj = in_proj(x, row(attn_pre_norm), _prep_w_in(*w_in))
    qn, qr = q_up(proj, row(q_a_norm), _prep_w_q_up(w_q_up), pos_col, invf)
    wk, wvt = _prep_w_kv_up(w_kv_up)
    kn, vt, kr = kv_up(proj, row(kv_a_norm), wk, wvt, pos_col, invf, batch, seq)
    o_mla, (wo16, wg16, wu16, wd16, wpg16) = mla_attention(
        qn, qr, kn, kr, vt, [w_o, w_gate, w_up, w_down, w_ple_gate], batch, seq)
    o_swa = swa_attention(proj, sinks, pos_col, pos_row, batch, seq)
    x1, h2 = out_proj_resid(o_mla, o_swa, row(mla_out_norm), row(swa_out_norm), wo16,
                            x, row(attn_post_norm), row(ffn_pre_norm))
    f = ffn(h2, wg16, wu16, wd16)
    return ple(x1, f, row(ffn_post_norm), wpg16, p, w_ple_proj.astype(BF16))


def kernel(x, p, positions, attn_pre_norm, w_in, q_a_norm, w_q_up, kv_a_norm, w_kv_up, sinks,
           mla_out_norm, swa_out_norm, w_o, attn_post_norm, ffn_pre_norm, w_gate, w_up, w_down,
           ffn_post_norm, w_ple_gate, w_ple_proj):
    batch, seq, d = x.shape
    depth = w_in.shape[0]
    t = batch * seq
    pos_col = positions.reshape(t, 1)
    pos_row = positions.reshape(t // BLOCK, 1, BLOCK)
    half = MLA_ROPE // 2
    invf = np.asarray(ROPE_THETA ** (-(np.arange(LANES) % half) * 2.0 / MLA_ROPE), np.float32).reshape(1, LANES)
    invf = jnp.asarray(invf)
    xf = x.reshape(t, d)
    for i in range(depth):
        xf = _layer(xf, p[i].reshape(t, PLE_DIM), pos_col, pos_row, invf, batch, seq,
                    attn_pre_norm[i], (w_in, i), q_a_norm[i], w_q_up[i], kv_a_norm[i], w_kv_up[i],
                    sinks[i], mla_out_norm[i], swa_out_norm[i], w_o[i], attn_post_norm[i],
                    ffn_pre_norm[i], w_gate[i], w_up[i], w_down[i], ffn_post_norm[i],
                    w_ple_gate[i], w_ple_proj[i])
    return xf.reshape(batch, seq, d)
```

```python
import functools
import math

import numpy as np
import jax
import jax.numpy as jnp
from jax import lax
from jax.experimental import pallas as pl
from jax.experimental.pallas import tpu as pltpu

D_MODEL = 4096
PLE_DIM = 256
MLA_HEADS = 16
MLA_NOPE = 128
MLA_ROPE = 64
MLA_V = 128
Q_LORA = 1024
KV_LORA = 512
MLA_WIDTH = MLA_HEADS * MLA_V
SWA_HEADS = 32
SWA_KV_HEADS = 4
SWA_HEAD_DIM = 64
SWA_GROUP = SWA_HEADS // SWA_KV_HEADS
SWA_WIDTH = SWA_HEADS * SWA_HEAD_DIM
WINDOW = 128
BLOCK = 128
ROPE_THETA = 10000.0
NORM_EPS = 1e-6
LOG2E = math.log2(math.e)
D_FF = 11008

LANES = 128
MXU_N = 256
MLA_TILE = 512
ROW_TILE = 256
FFN_TILE = 256
VT_ROWS = MLA_V + 16
VMEM_LIMIT = 60 * 1024 * 1024

COL_CQ = 0
COL_CKV = COL_CQ + Q_LORA
COL_QSWA = COL_CKV + KV_LORA
COL_KSWA = COL_QSWA + SWA_WIDTH
COL_VSWA = COL_KSWA + SWA_KV_HEADS * SWA_HEAD_DIM
COL_KR = COL_VSWA + SWA_KV_HEADS * SWA_HEAD_DIM
D_IN2 = COL_KR + 2 * MLA_ROPE

F32 = jnp.float32
BF16 = jnp.bfloat16


def _cparams(sem):
    return pltpu.CompilerParams(dimension_semantics=sem, vmem_limit_bytes=VMEM_LIMIT)


def _rms(xf, g):
    ms = jnp.mean(xf * xf, axis=-1, keepdims=True)
    return xf * lax.rsqrt(ms + NORM_EPS) * g


def _swap_half(w):
    d = w.shape[-1]
    return jnp.concatenate([-w[..., d // 2:], w[..., : d // 2]], axis=-1)


def _rope_tables(pos_col, invf_row):
    ang = pos_col.astype(F32) * invf_row
    return jnp.cos(ang), jnp.sin(ang)


def _col_chunks(n, width):
    return [(c, min(c + width, n)) for c in range(0, n, width)]


def _in_proj_kernel(x_ref, g_ref, w_ref, o_ref, *, chunks):
    h = _rms(x_ref[...], g_ref[...]).astype(BF16)
    for lo, hi in chunks:
        o_ref[:, lo:hi] = jnp.dot(h, w_ref[:, lo:hi], preferred_element_type=F32).astype(o_ref.dtype)


def in_proj(x, g, w, tm=ROW_TILE):
    m, k = x.shape
    n = w.shape[1]
    return pl.pallas_call(
        functools.partial(_in_proj_kernel, chunks=_col_chunks(n, 6 * MXU_N)),
        grid=(m // tm,),
        in_specs=[pl.BlockSpec((tm, k), lambda i: (i, 0)),
                  pl.BlockSpec((1, k), lambda i: (0, 0)),
                  pl.BlockSpec((k, n), lambda i: (0, 0), pipeline_mode=pl.Buffered(1))],
        out_specs=pl.BlockSpec((tm, n), lambda i: (i, 0)),
        out_shape=jax.ShapeDtypeStruct((m, n), BF16),
        compiler_params=_cparams(("parallel",)),
        name="in_proj",
    )(x, g, w)


def _q_up_kernel(cq_ref, g_ref, w_ref, pos_ref, invf_ref, qn_ref, qr_ref):
    h = _rms(cq_ref[...].astype(F32), g_ref[...]).astype(BF16)
    n_rope = MLA_HEADS * MLA_ROPE
    rot = jnp.dot(h, w_ref[:, :n_rope], preferred_element_type=F32)
    qn_ref[...] = jnp.dot(h, w_ref[:, n_rope:], preferred_element_type=F32).astype(BF16)
    cos, sin = _rope_tables(pos_ref[...], invf_ref[...])
    half = MLA_ROPE // 2
    lane = lax.broadcasted_iota(jnp.int32, cos.shape, 1)
    first_half = (lane % MLA_ROPE) < half
    for c in range(n_rope // LANES):
        a = rot[:, c * LANES:(c + 1) * LANES]
        b = jnp.where(first_half, -pltpu.roll(a, LANES - half, 1), pltpu.roll(a, half, 1))
        qr_ref[:, c * LANES:(c + 1) * LANES] = (a * cos + b * sin).astype(BF16)


def q_up(proj, g, w, pos_col, invf, tm=512):
    m = proj.shape[0]
    n_rope = MLA_HEADS * MLA_ROPE
    return pl.pallas_call(
        _q_up_kernel,
        grid=(m // tm,),
        in_specs=[pl.BlockSpec((tm, Q_LORA), lambda i: (i, COL_CQ // Q_LORA)),
                  pl.BlockSpec((1, Q_LORA), lambda i: (0, 0)),
                  pl.BlockSpec(w.shape, lambda i: (0, 0)),
                  pl.BlockSpec((tm, 1), lambda i: (i, 0)),
                  pl.BlockSpec((1, LANES), lambda i: (0, 0))],
        out_specs=[pl.BlockSpec((tm, MLA_WIDTH), lambda i: (i, 0)),
                   pl.BlockSpec((tm, n_rope), lambda i: (i, 0))],
        out_shape=[jax.ShapeDtypeStruct((m, MLA_WIDTH), BF16),
                   jax.ShapeDtypeStruct((m, n_rope), BF16)],
        compiler_params=_cparams(("parallel",)),
        name="q_up",
    )(proj, g, w, pos_col, invf)


def _kv_up_kernel(ckv_ref, g_ref, wk_ref, wvt_ref, krab_ref, pos_ref, invf_ref, kn_ref, vt_ref, kr_ref):
    h = _rms(ckv_ref[...].astype(F32), g_ref[...]).astype(BF16)
    kn_ref[...] = jnp.dot(h, wk_ref[...], preferred_element_type=F32).astype(BF16)
    vt = lax.dot_general(wvt_ref[...], h, (((1,), (1,)), ((), ())), preferred_element_type=F32)
    ones = jnp.ones((VT_ROWS - MLA_V, vt.shape[1]), BF16)
    for hh in range(MLA_HEADS):
        vt_ref[0, hh, 0, :MLA_V, :] = vt[hh * MLA_V:(hh + 1) * MLA_V, :].astype(BF16)
        vt_ref[0, hh, 0, MLA_V:, :] = ones
    cos, sin = _rope_tables(pos_ref[...], invf_ref[...])
    lane = lax.broadcasted_iota(jnp.int32, cos.shape, 1)
    lo = lane < MLA_ROPE
    prod = krab_ref[...].astype(F32) * jnp.where(lo, cos, sin)
    kk = prod + pltpu.roll(prod, MLA_ROPE, 1)
    zero = jnp.zeros_like(kk)
    kr_ref[:, :LANES] = jnp.where(lo, kk, zero).astype(BF16)
    kr_ref[:, LANES:] = jnp.where(lo, zero, kk).astype(BF16)


def kv_up(proj, g, wk, wvt, pos_col, invf, batch, seq, tm=MLA_TILE):
    m = proj.shape[0]
    nk = seq // tm
    return pl.pallas_call(
        _kv_up_kernel,
        grid=(m // tm,),
        in_specs=[pl.BlockSpec((tm, KV_LORA), lambda i: (i, COL_CKV // KV_LORA)),
                  pl.BlockSpec((1, KV_LORA), lambda i: (0, 0)),
                  pl.BlockSpec(wk.shape, lambda i: (0, 0)),
                  pl.BlockSpec(wvt.shape, lambda i: (0, 0)),
                  pl.BlockSpec((tm, LANES), lambda i: (i, COL_KR // LANES)),
                  pl.BlockSpec((tm, 1), lambda i: (i, 0)),
                  pl.BlockSpec((1, LANES), lambda i: (0, 0))],
        out_specs=[pl.BlockSpec((tm, MLA_WIDTH), lambda i: (i, 0)),
                   pl.BlockSpec((1, MLA_HEADS, 1, VT_ROWS, tm), lambda i: (i // nk, 0, i % nk, 0, 0)),
                   pl.BlockSpec((tm, 2 * LANES), lambda i: (i, 0))],
        out_shape=[jax.ShapeDtypeStruct((m, MLA_WIDTH), BF16),
                   jax.ShapeDtypeStruct((batch, MLA_HEADS, nk, VT_ROWS, tm), BF16),
                   jax.ShapeDtypeStruct((m, 2 * LANES), BF16)],
        compiler_params=_cparams(("parallel",)),
        name="kv_up",
    )(proj, g, wk, wvt, proj, pos_col, invf)


def _mla_kernel(*refs, tq, tk, n_cast):
    qn_ref, qr_ref, kn_ref, kr_ref, vt_ref = refs[:5]
    w_refs = refs[5:5 + n_cast]
    o_ref = refs[5 + n_cast]
    w16_refs = refs[6 + n_cast:6 + 2 * n_cast]
    s_a, s_b, mx_a, mx_b, m_sc, acc_sc = refs[6 + 2 * n_cast:]

    qi = pl.program_id(2)
    q = jnp.concatenate([qn_ref[...], qr_ref[...]], axis=-1)
    m_sc[...] = jnp.full(m_sc.shape, -jnp.inf, F32)
    acc_sc[...] = jnp.zeros(acc_sc.shape, F32)

    n_chunk = tq // MXU_N
    blk_chunks = tk // MXU_N
    n_diag = tq // tk
    bufs = ((s_a, mx_a), (s_b, mx_b))

    def live_chunks(diag):
        return range(0 if diag is None else diag * blk_chunks, n_chunk)

    def scores(kb, buf, diag=None):
        s_ref, mx_ref = buf
        off = pl.multiple_of(kb * tk, tk)
        k = jnp.concatenate([kn_ref[pl.ds(off, tk), :], kr_ref[pl.ds(off, tk), :]], axis=-1)
        for c in live_chunks(diag):
            s = lax.dot_general(k, q[c * MXU_N:(c + 1) * MXU_N, :], (((1,), (1,)), ((), ())),
                                preferred_element_type=F32)
            if diag is not None and c < (diag + 1) * blk_chunks:
                key = lax.broadcasted_iota(jnp.int32, s.shape, 0) + diag * tk
                qry = lax.broadcasted_iota(jnp.int32, s.shape, 1) + c * MXU_N
                s = jnp.where(qry >= key, s, -jnp.inf)
            s_ref[c] = s
            mx_ref[:, c * MXU_N:(c + 1) * MXU_N] = jnp.max(s, axis=0, keepdims=True)

    def softmax_pv(kb, buf, diag=None, mask_here=False):
        s_ref, mx_ref = buf
        vt = vt_ref[0, 0, kb]
        for c in live_chunks(diag):
            sl = slice(c * MXU_N, (c + 1) * MXU_N)
            remask = mask_here and c < blk_chunks

            def s_chunk():
                s = s_ref[c]
                if remask:
                    key = lax.broadcasted_iota(jnp.int32, s.shape, 0)
                    qry = lax.broadcasted_iota(jnp.int32, s.shape, 1) + c * MXU_N
                    s = jnp.where(qry >= key, s, -jnp.inf)
                return s

            mx = jnp.max(s_chunk(), axis=0, keepdims=True) if remask else mx_ref[:, sl]
            m_old = m_sc[:, sl]
            m_new = jnp.maximum(m_old, mx)
            alpha = jnp.exp2(m_old - m_new)
            p = jnp.exp2(s_chunk() - m_new).astype(BF16)
            acc_sc[:, sl] = alpha * acc_sc[:, sl] + jnp.dot(vt, p, preferred_element_type=F32)
            m_sc[:, sl] = m_new

    n_full = n_diag * qi
    scores(0, bufs[0])

    def body(j, carry):
        for t in range(n_diag):
            kb = n_diag * j + t
            scores(kb + 1, bufs[(t + 1) % 2])
            softmax_pv(kb, bufs[t % 2])
        return carry

    lax.fori_loop(0, qi, body, 0)

    for w_ref, w16_ref in zip(w_refs, w16_refs):
        w16_ref[...] = w_ref[...].astype(BF16)

    for d in range(n_diag):
        if d + 1 < n_diag:
            scores(n_full + d + 1, bufs[(d + 1) % 2], diag=d + 1)
        softmax_pv(n_full + d, bufs[d % 2], diag=d, mask_here=(d == 0))

    o_ref[...] = (acc_sc[:MLA_V, :] / acc_sc[MLA_V:MLA_V + 1, :]).T.astype(o_ref.dtype)


def mla_attention(qn, qr, kn, kr, vt, weights, batch, seq, tq=4 * MLA_TILE, tk=MLA_TILE):
    assert tq % (2 * tk) == 0
    nq = seq // tq
    steps = batch * MLA_HEADS * nq

    def slab(w):
        tiles = w.shape[0] // 16
        assert tiles * 16 == w.shape[0], w.shape
        n_slab = max(n for n in range(1, steps + 1) if tiles % n == 0)
        rows = w.shape[0] // n_slab
        return pl.BlockSpec((rows, w.shape[1]),
                            lambda b, h, i: (jnp.minimum((b * MLA_HEADS + h) * nq + i, n_slab - 1), 0))

    slabs = [slab(w) for w in weights]
    outs = pl.pallas_call(
        functools.partial(_mla_kernel, tq=tq, tk=tk, n_cast=len(weights)),
        grid=(batch, MLA_HEADS, nq),
        in_specs=[pl.BlockSpec((tq, MLA_NOPE), lambda b, h, i: (b * nq + i, h)),
                  pl.BlockSpec((tq, LANES), lambda b, h, i: (b * nq + i, h // 2)),
                  pl.BlockSpec((seq, MLA_NOPE), lambda b, h, i: (b, h)),
                  pl.BlockSpec((seq, LANES), lambda b, h, i: (b, h % 2)),
                  pl.BlockSpec((1, 1, seq // tk, VT_ROWS, tk), lambda b, h, i: (b, h, 0, 0, 0))] + slabs,
        out_specs=[pl.BlockSpec((tq, MLA_V), lambda b, h, i: (b * nq + i, h))] + slabs,
        out_shape=[jax.ShapeDtypeStruct((batch * seq, MLA_WIDTH), BF16)]
        + [jax.ShapeDtypeStruct(w.shape, BF16) for w in weights],
        scratch_shapes=[pltpu.VMEM((tq // MXU_N, tk, MXU_N), F32), pltpu.VMEM((tq // MXU_N, tk, MXU_N), F32),
                        pltpu.VMEM((1, tq), F32), pltpu.VMEM((1, tq), F32),
                        pltpu.VMEM((1, tq), F32), pltpu.VMEM((VT_ROWS, tq), F32)],
        compiler_params=_cparams(("parallel", "parallel", "parallel")),
        name="mla_attention",
    )(qn, qr, kn, kr, vt, *weights)
    return outs[0], outs[1:]


def _swa_kernel(sinks_ref, q0_ref, q1_ref, q2_ref, q3_ref, kc_ref, kp_ref, vc_ref, vp_ref,
                pq_ref, pkc_ref, pkp_ref, o_ref):
    blk = pl.program_id(1)
    q_refs = (q0_ref, q1_ref, q2_ref, q3_ref)
    k_all = jnp.concatenate([kp_ref[...], kc_ref[...]], axis=0).astype(F32)
    v_all = jnp.concatenate([vp_ref[...], vc_ref[...]], axis=0).astype(F32)
    pk = jnp.concatenate([pkp_ref[0], pkc_ref[0]], axis=-1).astype(F32)
    dist = jnp.abs(pq_ref[...].astype(F32) - pk)
    row = lax.broadcasted_iota(jnp.int32, dist.shape, 0)
    col = lax.broadcasted_iota(jnp.int32, dist.shape, 1)
    rel = BLOCK + row - col
    first_key = jnp.where(blk > 0, 0, BLOCK)
    valid = (rel >= 0) & (rel < WINDOW) & (col >= first_key)
    dist_w = jnp.where(valid, dist, jnp.inf)
    lane =lax.broadcasted_iota(jnp.int32, (2 * BLOCK, LANES), 1)
    lo = lane < SWA_HEAD_DIM
    lane_o = lax.broadcasted_iota(jnp.int32, (BLOCK, LANES), 1)
    lo_o = lane_o < SWA_HEAD_DIM

    def dup(x_all, g):
        pair = x_all[:, (g // 2) * LANES:(g // 2 + 1) * LANES]
        rolled = pltpu.roll(pair, SWA_HEAD_DIM, 1)
        return jnp.where(lo, pair, rolled) if g % 2 == 0 else jnp.where(lo, rolled, pair)

    for g in range(SWA_KV_HEADS):
        kk = dup(k_all, g)
        vv = dup(v_all, g).astype(BF16)
        zero = jnp.zeros_like(kk)
        k_half = (jnp.where(lo, kk, zero).astype(BF16), jnp.where(lo, zero, kk).astype(BF16))
        for i in range(SWA_GROUP // 2):
            q_pair = q_refs[g][:, i * LANES:(i + 1) * LANES]
            outs = []
            for j in range(2):
                head = g * SWA_GROUP + 2 * i + j
                slope2 = LOG2E * 2.0 ** (-8.0 * (head + 1) / SWA_HEADS)
                s = lax.dot_general(q_pair, k_half[j], (((1,), (1,)), ((), ())),
                                    preferred_element_type=F32)
                s = s - slope2 * dist_w
                sink2 = LOG2E * sinks_ref[head]
                m = jnp.maximum(jnp.max(s, axis=-1, keepdims=True), sink2)
                e = jnp.exp2(s - m)
                denom = jnp.sum(e, axis=-1, keepdims=True) + jnp.exp2(sink2 - m)
                pv = jnp.dot(e.astype(BF16), vv, preferred_element_type=F32)
                outs.append(pv / denom)
            col0 = (g * SWA_GROUP // 2 + i) * LANES
            o_ref[:, col0:col0 + LANES] = jnp.where(lo_o, outs[0], outs[1]).astype(o_ref.dtype)


def swa_attention(proj, sinks, pos_col, pos_row, batch, seq):
    nblk = seq // BLOCK
    kvw = SWA_KV_HEADS * SWA_HEAD_DIM
    gw = SWA_GROUP * SWA_HEAD_DIM
    cur = lambda b, n: b * nblk + n
    prev = lambda b, n: b * nblk + jnp.maximum(n - 1, 0)
    q_specs = [pl.BlockSpec((BLOCK, gw), functools.partial(lambda b, n, g: (cur(b, n), COL_QSWA // gw + g), g=g))
               for g in range(SWA_KV_HEADS)]
    return pl.pallas_call(
        _swa_kernel,
        grid=(batch, nblk),
        in_specs=[pl.BlockSpec(memory_space=pltpu.SMEM)] + q_specs + [
            pl.BlockSpec((BLOCK, kvw), lambda b, n: (cur(b, n), COL_KSWA // kvw)),
            pl.BlockSpec((BLOCK, kvw), lambda b, n: (prev(b, n), COL_KSWA // kvw)),
            pl.BlockSpec((BLOCK, kvw), lambda b, n: (cur(b, n), COL_VSWA // kvw)),
            pl.BlockSpec((BLOCK, kvw), lambda b, n: (prev(b, n), COL_VSWA // kvw)),
            pl.BlockSpec((BLOCK, 1), lambda b, n: (cur(b, n), 0)),
            pl.BlockSpec((1, 1, BLOCK), lambda b, n: (cur(b, n), 0, 0)),
            pl.BlockSpec((1, 1, BLOCK), lambda b, n: (prev(b, n), 0, 0))],
        out_specs=pl.BlockSpec((BLOCK, SWA_WIDTH), lambda b, n: (cur(b, n), 0)),
        out_shape=jax.ShapeDtypeStruct((batch * seq, SWA_WIDTH), BF16),
        compiler_params=_cparams(("parallel", "parallel")),
        name="swa_attention",
    )(sinks, proj, proj, proj, proj, proj, proj, proj, proj, pos_col, pos_row, pos_row)


def _wo_kernel(oa_ref, ob_ref, ga_ref, gb_ref, w_ref, x_ref, gy_ref, gn_ref, x1_ref, h_ref, y0_sc, y1_sc,
               *, chunks):
    i = pl.program_id(0)
    ka = oa_ref.shape[1]

    @pl.when(i == 0)
    def _():
        y1_sc[...] = jnp.zeros(y1_sc.shape, y1_sc.dtype)

    def step(y_new, y_old):
        na = _rms(oa_ref[...].astype(F32), ga_ref[...]).astype(BF16)
        nb = _rms(ob_ref[...].astype(F32), gb_ref[...]).astype(BF16)
        for lo, hi in chunks:
            y_new[:, lo:hi] = (jnp.dot(na, w_ref[:ka, lo:hi], preferred_element_type=F32)
                               + jnp.dot(nb, w_ref[ka:, lo:hi], preferred_element_type=F32))
        rows = 32
        for r in range(y_old.shape[0] // rows):
            rs = slice(r * rows, (r + 1) * rows)
            x1 = x_ref[rs, :] + _rms(y_old[rs, :], gy_ref[...])
            x1_ref[rs, :] = x1
            h_ref[rs, :] = _rms(x1, gn_ref[...]).astype(h_ref.dtype)

    @pl.when(i % 2 == 0)
    def _():
        step(y0_sc, y1_sc)

    @pl.when(i % 2 == 1)
    def _():
        step(y1_sc, y0_sc)


def out_proj_resid(oa, ob, ga, gb, w, x, gy, gn, tm=ROW_TILE // 2):
    m, ka = oa.shape
    kb = ob.shape[1]
    n = w.shape[1]
    nt = m // tm
    cur = lambda i: (jnp.minimum(i, nt - 1), 0)
    done = lambda i: (jnp.maximum(i - 1, 0), 0)
    vec = pl.BlockSpec((1, n), lambda i: (0, 0))
    return pl.pallas_call(
        functools.partial(_wo_kernel, chunks=_col_chunks(n, 4 * MXU_N)),
        grid=(nt + 1,),
        in_specs=[pl.BlockSpec((tm, ka), cur),
                  pl.BlockSpec((tm, kb), cur),
                  pl.BlockSpec((1, ka), lambda i: (0, 0)),
                  pl.BlockSpec((1, kb), lambda i: (0, 0)),
                  pl.BlockSpec((ka + kb, n), lambda i: (0, 0), pipeline_mode=pl.Buffered(1)),
                  pl.BlockSpec((tm, n), done), vec, vec],
        out_specs=[pl.BlockSpec((tm, n), done), pl.BlockSpec((tm, n), done)],
        out_shape=[jax.ShapeDtypeStruct((m, n), F32), jax.ShapeDtypeStruct((m, n), BF16)],
        scratch_shapes=[pltpu.VMEM((tm, n), F32), pltpu.VMEM((tm, n), F32)],
        compiler_params=_cparams(("arbitrary",)),
        name="out_proj",
    )(oa, ob, ga, gb, w, x, gy, gn)


def _ffn_kernel(h_ref, wg_ref, wu_ref, wd_ref, o_ref, acc_sc, *, dn, rows):
    j = pl.program_id(1)
    tm, d = acc_sc.shape

    @pl.when(j == 0)
    def _():
        acc_sc[...] = jnp.zeros(acc_sc.shape, acc_sc.dtype)

    h = h_ref[...]
    g = jnp.dot(h, wg_ref[...], preferred_element_type=F32)
    u = jnp.dot(h, wu_ref[...], preferred_element_type=F32)
    a = (g * jax.nn.sigmoid(g) * u).astype(BF16)
    for c in range(d // dn):
        acc_sc[:, c * dn:(c + 1) * dn] += jnp.dot(a, wd_ref[:, c * dn:(c + 1) * dn],
                                                  preferred_element_type=F32)

    @pl.when(j == pl.num_programs(1) - 1)
    def _():
        def body(r, carry):
            rs = pl.ds(pl.multiple_of(r * rows, rows), rows)
            o_ref[rs, :] = acc_sc[rs, :].astype(o_ref.dtype)
            return carry

        lax.fori_loop(0, tm // rows, body, 0)


def ffn(h, wg, wu, wd, tm=1024, tf=FFN_TILE):
    m, d = h.shape
    f = wg.shape[1]
    return pl.pallas_call(
        functools.partial(_ffn_kernel, dn=1024, rows=64),
        grid=(m // tm, f // tf),
        in_specs=[pl.BlockSpec((tm, d), lambda i, j: (i, 0)),
                  pl.BlockSpec((d, tf), lambda i, j: (0, j)),
                  pl.BlockSpec((d, tf), lambda i, j: (0, j)),
                  pl.BlockSpec((tf, d), lambda i, j: (j, 0))],
        out_specs=pl.BlockSpec((tm, d), lambda i, j: (i, 0), pipeline_mode=pl.Buffered(1)),
        out_shape=jax.ShapeDtypeStruct((m, d), BF16),
        scratch_shapes=[pltpu.VMEM((tm, d), F32)],
        compiler_params=_cparams(("parallel", "arbitrary")),
        name="ffn",
    )(h, wg, wu, wd)


def _ple_kernel(x_ref, f_ref, gf_ref, wg_ref, p_ref, wp_ref, o_ref, *, chunks):
    o_ref[...] = x_ref[...] + _rms(f_ref[...].astype(F32), gf_ref[...])
    xb = o_ref[...].astype(BF16)
    pb = p_ref[...].astype(BF16)
    for lo, hi in chunks:
        gate = jax.nn.sigmoid(jnp.dot(xb, wg_ref[:, lo:hi], preferred_element_type=F32))
        e = jnp.dot(pb, wp_ref[:, lo:hi], preferred_element_type=F32)
        o_ref[:, lo:hi] += gate * e


def ple(x, f, g_f, wg, p, wp, tm=ROW_TILE):
    m, d = x.shape
    n = wg.shape[1]
    pd = p.shape[1]
    once = dict(pipeline_mode=pl.Buffered(1))
    return pl.pallas_call(
        functools.partial(_ple_kernel, chunks=_col_chunks(n, 4 * MXU_N)),
        grid=(m // tm,),
        in_specs=[pl.BlockSpec((tm, d), lambda i: (i, 0)),
                  pl.BlockSpec((tm, d), lambda i: (i, 0)),
                  pl.BlockSpec((1, d), lambda i: (0, 0)),
                  pl.BlockSpec((d, n), lambda i: (0, 0), **once),
                  pl.BlockSpec((tm, pd), lambda i: (i, 0)),
                  pl.BlockSpec((pd, n), lambda i: (0, 0), **once)],
        out_specs=pl.BlockSpec((tm, n), lambda i: (i, 0)),
        out_shape=jax.ShapeDtypeStruct((m, n), F32),
        compiler_params=_cparams(("parallel",)),
        name="ple",
    )(x, f, g_f, wg, p, wp)


_KR_BLOCK = (Q_LORA + KV_LORA) // MLA_ROPE
_QSWA_BLOCKS = (COL_QSWA // LANES, COL_KSWA // LANES)


def _w_in_prep_kernel(wa_ref, wb_ref, o_ref):
    ob = pl.program_id(0)
    a, b = wa_ref[0], wb_ref[0]
    half = MLA_ROPE // 2
    partner = jnp.concatenate([-b[half:], b[:half]], axis=0)
    b = jnp.where(ob == pl.num_programs(0) - 1, partner, b)
    scale = jnp.where((ob >= _QSWA_BLOCKS[0]) & (ob < _QSWA_BLOCKS[1]), LOG2E / math.sqrt(SWA_HEAD_DIM), 1.0)
    blk = jnp.concatenate([a, b], axis=0) * scale
    o_ref[...] = blk.T.astype(BF16)


def _prep_w_in(w3, layer):
    wt = jnp.swapaxes(w3, 1, 2)
    _, n, k = wt.shape
    n_out = D_IN2 // LANES
    shifted = COL_QSWA // LANES

    def src(ob, part):
        blk = 2 * ob + part + jnp.where(ob >= shifted, 1, 0)
        return jnp.where(ob == n_out - 1, _KR_BLOCK, blk)

    return pl.pallas_call(
        _w_in_prep_kernel,
        grid=(n_out,),
        in_specs=[pl.BlockSpec((1, MLA_ROPE, k), lambda ob: (layer, src(ob, 0), 0)),
                  pl.BlockSpec((1, MLA_ROPE, k), lambda ob: (layer, src(ob, 1), 0))],
        out_specs=pl.BlockSpec((k, LANES), lambda ob: (0, ob)),
        out_shape=jax.ShapeDtypeStruct((k, D_IN2), BF16),
        compiler_params=_cparams(("parallel",)),
        name="w_in_prep",
    )(wt, wt)


def _prep_w_q_up(w):
    w = (w * (LOG2E / math.sqrt(MLA_NOPE + MLA_ROPE))).reshape(Q_LORA, MLA_HEADS, MLA_NOPE + MLA_ROPE)
    nope = w[:, :, :MLA_NOPE].reshape(Q_LORA, MLA_WIDTH)
    rope = w[:, :, MLA_NOPE:].reshape(Q_LORA, MLA_HEADS * MLA_ROPE)
    return jnp.concatenate([rope, nope], axis=1).astype(BF16)


def _prep_w_kv_up(w):
    w = w.reshape(KV_LORA, MLA_HEADS, MLA_NOPE + MLA_V)
    wk = w[:, :, :MLA_NOPE].reshape(KV_LORA, MLA_WIDTH).astype(BF16)
    wvt = w[:, :, MLA_NOPE:].reshape(KV_LORA, MLA_WIDTH).T.astype(BF16)
    return wk, wvt


def _layer(x, p, pos_col, pos_row, invf, batch, seq, attn_pre_norm, w_in, q_a_norm, w_q_up,
           kv_a_norm, w_kv_up, sinks, mla_out_norm, swa_out_norm, w_o, attn_post_norm,
           ffn_pre_norm, w_gate, w_up, w_down, ffn_post_norm, w_ple_gate, w_ple_proj):
    row = lambda g: g.reshape(1, -1)
    proj = in_proj(x, row(attn_pre_norm), _prep_w_in(*w_in))
    qn, qr = q_up(proj, row(q_a_norm), _prep_w_q_up(w_q_up), pos_col, invf)
    wk, wvt = _prep_w_kv_up(w_kv_up)
    kn, vt, kr = kv_up(proj, row(kv_a_norm), wk, wvt, pos_col, invf, batch, seq)
    o_mla, (wo16, wg16, wu16, wd16, wpg16) = mla_attention(
        qn, qr, kn, kr, vt, [w_o, w_gate, w_up, w_down, w_ple_gate], batch, seq)
    o_swa = swa_attention(proj, sinks, pos_col, pos_row, batch, seq)
    x1, h2 = out_proj_resid(o_mla, o_swa, row(mla_out_norm), row(swa_out_norm), wo16,
                            x, row(attn_post_norm), row(ffn_pre_norm))
    f = ffn(h2, wg16, wu16, wd16)
    return ple(x1, f, row(ffn_post_norm), wpg16, p, w_ple_proj.astype(BF16))


def kernel(x, p, positions, attn_pre_norm, w_in, q_a_norm, w_q_up, kv_a_norm, w_kv_up, sinks,
           mla_out_norm, swa_out_norm, w_o, attn_post_norm, ffn_pre_norm, w_gate, w_up, w_down,
           ffn_post_norm, w_ple_gate, w_ple_proj):
    batch, seq, d = x.shape
    depth = w_in.shape[0]
    t = batch * seq
    pos_col = positions.reshape(t, 1)
    pos_row = positions.reshape(t // BLOCK, 1, BLOCK)
    half = MLA_ROPE // 2
    invf = np.asarray(ROPE_THETA ** (-(np.arange(LANES) % half) * 2.0 / MLA_ROPE), np.float32).reshape(1, LANES)
    invf = jnp.asarray(invf)
    xf = x.reshape(t, d)
    for i in range(depth):
        xf = _layer(xf, p[i].reshape(t, PLE_DIM), pos_col, pos_row, invf, batch, seq,
                    attn_pre_norm[i], (w_in, i), q_a_norm[i], w_q_up[i], kv_a_norm[i], w_kv_up[i],
                    sinks[i], mla_out_norm[i], swa_out_norm[i], w_o[i], attn_post_norm[i],
                    ffn_pre_norm[i], w_gate[i], w_up[i], w_down[i], ffn_post_norm[i],
                    w_ple_gate[i], w_ple_proj[i])
    return xf.reshape(batch, seq, d)
```

```python
import functools
import math

import numpy as np
import jax
import jax.numpy as jnp
from jax import lax
from jax.experimental import pallas as pl
from jax.experimental.pallas import tpu as pltpu

PLE_DIM = 256
MLA_HEADS = 16
MLA_NOPE = 128
MLA_ROPE = 64
MLA_V = 128
Q_LORA = 1024
KV_LORA = 512
MLA_WIDTH = MLA_HEADS * MLA_V
SWA_HEADS = 32
SWA_KV_HEADS = 4
SWA_HEAD_DIM = 64
SWA_GROUP = SWA_HEADS // SWA_KV_HEADS
SWA_WIDTH = SWA_HEADS * SWA_HEAD_DIM
WINDOW = 128
BLOCK = 128
ROPE_THETA = 10000.0
NORM_EPS = 1e-6
LOG2E = math.log2(math.e)

LANES = 128
MXU_N = 256
BF16_ROWS = 16
MLA_TILE = 512
ROW_TILE = 256
FFN_TILE = MXU_N
VT_ROWS = MLA_V + BF16_ROWS
VMEM_LIMIT = 60 * 1024 * 1024

COL_CQ = 0
COL_CKV = COL_CQ + Q_LORA
COL_QSWA = COL_CKV + KV_LORA
COL_KSWA = COL_QSWA + SWA_WIDTH
COL_VSWA = COL_KSWA + SWA_KV_HEADS * SWA_HEAD_DIM
COL_KR = COL_VSWA + SWA_KV_HEADS * SWA_HEAD_DIM
D_IN2 = COL_KR + 2 * MLA_ROPE

F32 = jnp.float32
BF16 = jnp.bfloat16


def _cparams(sem):
    return pltpu.CompilerParams(dimension_semantics=sem, vmem_limit_bytes=VMEM_LIMIT)


def _rms(xf, g):
    ms = jnp.mean(xf * xf, axis=-1, keepdims=True)
    return xf * lax.rsqrt(ms + NORM_EPS) * g


def _rope_tables(pos_col, invf_row):
    ang = pos_col.astype(F32) * invf_row
    return jnp.cos(ang), jnp.sin(ang)


def _col_chunks(n, width):
    return [(c, min(c + width, n)) for c in range(0, n, width)]


def _in_proj_kernel(x_ref, g_ref, w_ref, o_ref, *, chunks):
    h = _rms(x_ref[...], g_ref[...]).astype(BF16)
    for lo, hi in chunks:
        o_ref[:, lo:hi] = jnp.dot(h, w_ref[:, lo:hi], preferred_element_type=F32).astype(o_ref.dtype)


def in_proj(x, g, w, tm=ROW_TILE):
    m, k = x.shape
    n = w.shape[1]
    return pl.pallas_call(
        functools.partial(_in_proj_kernel, chunks=_col_chunks(n, 6 * MXU_N)),
        grid=(m // tm,),
        in_specs=[pl.BlockSpec((tm, k), lambda i: (i, 0)),
                  pl.BlockSpec((1, k), lambda i: (0, 0)),
                  pl.BlockSpec((k, n), lambda i: (0, 0), pipeline_mode=pl.Buffered(1))],
        out_specs=pl.BlockSpec((tm, n), lambda i: (i, 0)),
        out_shape=jax.ShapeDtypeStruct((m, n), BF16),
        compiler_params=_cparams(("parallel",)),
        name="in_proj",
    )(x, g, w)


def _q_up_kernel(cq_ref, g_ref, w_ref, pos_ref, invf_ref, qn_ref, qr_ref):
    h = _rms(cq_ref[...].astype(F32), g_ref[...]).astype(BF16)
    n_rope = MLA_HEADS * MLA_ROPE
    rot = jnp.dot(h, w_ref[:, :n_rope], preferred_element_type=F32)
    qn_ref[...] = jnp.dot(h, w_ref[:, n_rope:], preferred_element_type=F32).astype(BF16)
    cos, sin = _rope_tables(pos_ref[...], invf_ref[...])
    half = MLA_ROPE // 2
    lane = lax.broadcasted_iota(jnp.int32, cos.shape, 1)
    first_half = (lane % MLA_ROPE) < half
    for c in range(n_rope // LANES):
        a = rot[:, c * LANES:(c + 1) * LANES]
        b = jnp.where(first_half, -pltpu.roll(a, LANES - half, 1), pltpu.roll(a, half, 1))
        qr_ref[:, c * LANES:(c + 1) * LANES] = (a * cos + b * sin).astype(BF16)


def q_up(proj, g, w, pos_col, invf, tm=512):
    m = proj.shape[0]
    n_rope = MLA_HEADS * MLA_ROPE
    return pl.pallas_call(
        _q_up_kernel,
        grid=(m // tm,),
        in_specs=[pl.BlockSpec((tm, Q_LORA), lambda i: (i, COL_CQ // Q_LORA)),
                  pl.BlockSpec((1, Q_LORA), lambda i: (0, 0)),
                  pl.BlockSpec(w.shape, lambda i: (0, 0)),
                  pl.BlockSpec((tm, 1), lambda i: (i, 0)),
                  pl.BlockSpec((1, LANES), lambda i: (0, 0))],
        out_specs=[pl.BlockSpec((tm, MLA_WIDTH), lambda i: (i, 0)),
                   pl.BlockSpec((tm, n_rope), lambda i: (i, 0))],
        out_shape=[jax.ShapeDtypeStruct((m, MLA_WIDTH), BF16),
                   jax.ShapeDtypeStruct((m, n_rope), BF16)],
        compiler_params=_cparams(("parallel",)),
        name="q_up",
    )(proj, g, w, pos_col, invf)


def _kv_up_kernel(ckv_ref, g_ref, wk_ref, wvt_ref, krab_ref, pos_ref, invf_ref, kn_ref, vt_ref, kr_ref):
    h = _rms(ckv_ref[...].astype(F32), g_ref[...]).astype(BF16)
    kn_ref[...] = jnp.dot(h, wk_ref[...], preferred_element_type=F32).astype(BF16)
    vt = lax.dot_general(wvt_ref[...], h, (((1,), (1,)), ((), ())), preferred_element_type=F32)
    ones = jnp.ones((VT_ROWS - MLA_V, vt.shape[1]), BF16)
    for hh in range(MLA_HEADS):
        vt_ref[0, hh, 0, :MLA_V, :] = vt[hh * MLA_V:(hh + 1) * MLA_V, :].astype(BF16)
        vt_ref[0, hh, 0, MLA_V:, :] = ones
    cos, sin = _rope_tables(pos_ref[...], invf_ref[...])
    lane = lax.broadcasted_iota(jnp.int32, cos.shape, 1)
    lo = lane < MLA_ROPE
    prod = krab_ref[...].astype(F32) * jnp.where(lo, cos, sin)
    kk = prod + pltpu.roll(prod, MLA_ROPE, 1)
    zero = jnp.zeros_like(kk)
    kr_ref[:, :LANES] = jnp.where(lo, kk, zero).astype(BF16)
    kr_ref[:, LANES:] = jnp.where(lo, zero, kk).astype(BF16)


def kv_up(proj, g, wk, wvt, pos_col, invf, batch, seq, tm=MLA_TILE):
    m = proj.shape[0]
    nk = seq // tm
    return pl.pallas_call(
        _kv_up_kernel,
        grid=(m // tm,),
        in_specs=[pl.BlockSpec((tm, KV_LORA), lambda i: (i, COL_CKV // KV_LORA)),
                  pl.BlockSpec((1, KV_LORA), lambda i: (0, 0)),
                  pl.BlockSpec(wk.shape, lambda i: (0, 0)),
                  pl.BlockSpec(wvt.shape, lambda i: (0, 0)),
                  pl.BlockSpec((tm, LANES), lambda i: (i, COL_KR // LANES)),
                  pl.BlockSpec((tm, 1), lambda i: (i, 0)),
                  pl.BlockSpec((1, LANES), lambda i: (0, 0))],
        out_specs=[pl.BlockSpec((tm, MLA_WIDTH), lambda i: (i, 0)),
                   pl.BlockSpec((1, MLA_HEADS, 1, VT_ROWS, tm), lambda i: (i // nk, 0, i % nk, 0, 0)),
                   pl.BlockSpec((tm, 2 * LANES), lambda i: (i, 0))],
        out_shape=[jax.ShapeDtypeStruct((m, MLA_WIDTH), BF16),
                   jax.ShapeDtypeStruct((batch, MLA_HEADS, nk, VT_ROWS, tm), BF16),
                   jax.ShapeDtypeStruct((m, 2 * LANES), BF16)],
        compiler_params=_cparams(("parallel",)),
        name="kv_up",
    )(proj, g, wk, wvt, proj, pos_col, invf)


def _mla_kernel(*refs, tq, tk, n_cast):
    qn_ref, qr_ref, kn_ref, kr_ref, vt_ref = refs[:5]
    w_refs = refs[5:5 + n_cast]
    o_ref = refs[5 + n_cast]
    w16_refs = refs[6 + n_cast:6 + 2 * n_cast]
    s_a, s_b, mx_a, mx_b, m_sc, acc_sc = refs[6 + 2 * n_cast:]

    qi = pl.program_id(2)
    q = jnp.concatenate([qn_ref[...], qr_ref[...]], axis=-1)
    m_sc[...] = jnp.full(m_sc.shape, -jnp.inf, F32)
    acc_sc[...] = jnp.zeros(acc_sc.shape, F32)

    n_chunk = tq // MXU_N
    blk_chunks = tk // MXU_N
    n_diag = tq // tk
    bufs = ((s_a, mx_a), (s_b, mx_b))

    def live_chunks(diag):
        return range(0 if diag is None else diag * blk_chunks, n_chunk)

    def scores(kb, buf, diag=None):
        s_ref, mx_ref = buf
        off = pl.multiple_of(kb * tk, tk)
        k = jnp.concatenate([kn_ref[pl.ds(off, tk), :], kr_ref[pl.ds(off, tk), :]], axis=-1)
        for c in live_chunks(diag):
            s = lax.dot_general(k, q[c * MXU_N:(c + 1) * MXU_N, :], (((1,), (1,)), ((), ())),
                                preferred_element_type=F32)
            if diag is not None and c < (diag + 1) * blk_chunks:
                key = lax.broadcasted_iota(jnp.int32, s.shape, 0) + diag * tk
                qry = lax.broadcasted_iota(jnp.int32, s.shape, 1) + c * MXU_N
                s = jnp.where(qry >= key, s, -jnp.inf)
            s_ref[c] = s
            mx_ref[:, c * MXU_N:(c + 1) * MXU_N] = jnp.max(s, axis=0, keepdims=True)

    def softmax_pv(kb, buf, diag=None, mask_here=False):
        s_ref, mx_ref = buf
        vt = vt_ref[0, 0, kb]
        for c in live_chunks(diag):
            sl = slice(c * MXU_N, (c + 1) * MXU_N)
            remask = mask_here and c < blk_chunks

            def s_chunk():
                s = s_ref[c]
                if remask:
                    key = lax.broadcasted_iota(jnp.int32, s.shape, 0)
                    qry = lax.broadcasted_iota(jnp.int32, s.shape, 1) + c * MXU_N
                    s = jnp.where(qry >= key, s, -jnp.inf)
                return s

            mx = jnp.max(s_chunk(), axis=0, keepdims=True) if remask else mx_ref[:, sl]
            m_old = m_sc[:, sl]
            m_new = jnp.maximum(m_old, mx)
            alpha = jnp.exp2(m_old - m_new)
            p = jnp.exp2(s_chunk() - m_new).astype(BF16)
            acc_sc[:, sl] = alpha * acc_sc[:, sl] + jnp.dot(vt, p, preferred_element_type=F32)
            m_sc[:, sl] = m_new

    n_full = n_diag * qi
    scores(0, bufs[0])

    def body(j, carry):
        for t in range(n_diag):
            kb = n_diag * j + t
            scores(kb + 1, bufs[(t + 1) % 2])
            softmax_pv(kb, bufs[t % 2])
        return carry

    lax.fori_loop(0, qi, body, 0)

    for w_ref, w16_ref in zip(w_refs, w16_refs):
        w16_ref[...] = w_ref[...].astype(BF16)

    for d in range(n_diag):
        if d + 1 < n_diag:
            scores(n_full + d + 1, bufs[(d + 1) % 2], diag=d + 1)
        softmax_pv(n_full + d, bufs[d % 2], diag=d, mask_here=(d == 0))

    o_ref[...] = (acc_sc[:MLA_V, :] / acc_sc[MLA_V:MLA_V + 1, :]).T.astype(o_ref.dtype)


def mla_attention(qn, qr, kn, kr, vt, weights, batch, seq, tq=4 * MLA_TILE, tk=MLA_TILE):
    assert tq % (2 * tk) == 0
    nq = seq // tq
    steps = batch * MLA_HEADS * nq

    def slab(w):
        tiles, rem = divmod(w.shape[0], BF16_ROWS)
        assert rem == 0, w.shape
        n_slab = max(n for n in range(1, steps + 1) if tiles % n == 0)
        rows = w.shape[0] // n_slab
        return pl.BlockSpec((rows, w.shape[1]),
                            lambda b, h, i: (jnp.minimum((b * MLA_HEADS + h) * nq + i, n_slab - 1), 0))

    slabs = [slab(w) for w in weights]
    outs = pl.pallas_call(
        functools.partial(_mla_kernel, tq=tq, tk=tk, n_cast=len(weights)),
        grid=(batch, MLA_HEADS, nq),
        in_specs=[pl.BlockSpec((tq, MLA_NOPE), lambda b, h, i: (b * nq + i, h)),
                  pl.BlockSpec((tq, LANES), lambda b, h, i: (b * nq + i, h // 2)),
                  pl.BlockSpec((seq, MLA_NOPE), lambda b, h, i: (b, h)),
                  pl.BlockSpec((seq, LANES), lambda b, h, i: (b, h % 2)),
                  pl.BlockSpec((1, 1, seq // tk, VT_ROWS, tk), lambda b, h, i: (b, h, 0, 0, 0))] + slabs,
        out_specs=[pl.BlockSpec((tq, MLA_V), lambda b, h, i: (b * nq + i, h))] + slabs,
        out_shape=[jax.ShapeDtypeStruct((batch * seq, MLA_WIDTH), BF16)]
        + [jax.ShapeDtypeStruct(w.shape, BF16) for w in weights],
        scratch_shapes=[pltpu.VMEM((tq // MXU_N, tk, MXU_N), F32), pltpu.VMEM((tq // MXU_N, tk, MXU_N), F32),
                        pltpu.VMEM((1, tq), F32), pltpu.VMEM((1, tq), F32),
                        pltpu.VMEM((1, tq), F32), pltpu.VMEM((VT_ROWS, tq), F32)],
        compiler_params=_cparams(("arbitrary", "arbitrary", "arbitrary")),
        name="mla_attention",
    )(qn, qr, kn, kr, vt, *weights)
    return outs[0], outs[1:]


def _swa_kernel(sinks_ref, q0_ref, q1_ref, q2_ref, q3_ref, kc_ref, kp_ref, vc_ref, vp_ref,
                pq_ref, pkc_ref, pkp_ref, o_ref):
    blk = pl.program_id(1)
    q_refs = (q0_ref, q1_ref, q2_ref, q3_ref)
    k_all = jnp.concatenate([kp_ref[...], kc_ref[...]], axis=0).astype(F32)
    v_all = jnp.concatenate([vp_ref[...], vc_ref[...]], axis=0).astype(F32)
    pk = jnp.concatenate([pkp_ref[0], pkc_ref[0]], axis=-1).astype(F32)
    dist = jnp.abs(pq_ref[...].astype(F32) - pk)
    row = lax.broadcasted_iota(jnp.int32, dist.shape, 0)
    col = lax.broadcasted_iota(jnp.int32, dist.shape, 1)
    rel = BLOCK + row - col
    first_key = jnp.where(blk > 0, 0, BLOCK)
    valid = (rel >= 0) & (rel < WINDOW) & (col >= first_key)
    dist_w = jnp.where(valid, dist, jnp.inf)
    lane =lax.broadcasted_iota(jnp.int32, (2 * BLOCK, LANES), 1)
    lo = lane < SWA_HEAD_DIM
    lane_o = lax.broadcasted_iota(jnp.int32, (BLOCK, LANES), 1)
    lo_o = lane_o < SWA_HEAD_DIM

    def dup(x_all, g):
        pair = x_all[:, (g // 2) * LANES:(g // 2 + 1) * LANES]
        rolled = pltpu.roll(pair, SWA_HEAD_DIM, 1)
        return jnp.where(lo, pair, rolled) if g % 2 == 0 else jnp.where(lo, rolled, pair)

    for g in range(SWA_KV_HEADS):
        kk = dup(k_all, g)
        vv = dup(v_all, g).astype(BF16)
        zero = jnp.zeros_like(kk)
        k_half = (jnp.where(lo, kk, zero).astype(BF16), jnp.where(lo, zero, kk).astype(BF16))
        for i in range(SWA_GROUP // 2):
            q_pair = q_refs[g][:, i * LANES:(i + 1) * LANES]
            outs = []
            for j in range(2):
                head = g * SWA_GROUP + 2 * i + j
                slope2 = LOG2E * 2.0 ** (-8.0 * (head + 1) / SWA_HEADS)
                s = lax.dot_general(q_pair, k_half[j], (((1,), (1,)), ((), ())),
                                    preferred_element_type=F32)
                s = s - slope2 * dist_w
                sink2 = LOG2E * sinks_ref[head]
                m = jnp.maximum(jnp.max(s, axis=-1, keepdims=True), sink2)
                e = jnp.exp2(s - m)
                denom = jnp.sum(e, axis=-1, keepdims=True) + jnp.exp2(sink2 - m)
                pv = jnp.dot(e.astype(BF16), vv, preferred_element_type=F32)
                outs.append(pv / denom)
            col0 = (g * SWA_GROUP // 2 + i) * LANES
            o_ref[:, col0:col0 + LANES] = jnp.where(lo_o, outs[0], outs[1]).astype(o_ref.dtype)


def swa_attention(proj, sinks, pos_col, pos_row, batch, seq):
    nblk = seq // BLOCK
    kvw = SWA_KV_HEADS * SWA_HEAD_DIM
    gw = SWA_GROUP * SWA_HEAD_DIM
    cur = lambda b, n: b * nblk + n
    prev = lambda b, n: b * nblk + jnp.maximum(n - 1, 0)
    q_specs = [pl.BlockSpec((BLOCK, gw), functools.partial(lambda b, n, g: (cur(b, n), COL_QSWA // gw + g), g=g))
               for g in range(SWA_KV_HEADS)]
    return pl.pallas_call(
        _swa_kernel,
        grid=(batch, nblk),
        in_specs=[pl.BlockSpec(memory_space=pltpu.SMEM)] + q_specs + [
            pl.BlockSpec((BLOCK, kvw), lambda b, n: (cur(b, n), COL_KSWA // kvw)),
            pl.BlockSpec((BLOCK, kvw), lambda b, n: (prev(b, n), COL_KSWA // kvw)),
            pl.BlockSpec((BLOCK, kvw), lambda b, n: (cur(b, n), COL_VSWA // kvw)),
            pl.BlockSpec((BLOCK, kvw), lambda b, n: (prev(b, n), COL_VSWA // kvw)),
            pl.BlockSpec((BLOCK, 1), lambda b, n: (cur(b, n), 0)),
            pl.BlockSpec((1, 1, BLOCK), lambda b, n: (cur(b, n), 0, 0)),
            pl.BlockSpec((1, 1, BLOCK), lambda b, n: (prev(b, n), 0, 0))],
        out_specs=pl.BlockSpec((BLOCK, SWA_WIDTH), lambda b, n: (cur(b, n), 0)),
        out_shape=jax.ShapeDtypeStruct((batch * seq, SWA_WIDTH), BF16),
        compiler_params=_cparams(("parallel", "parallel")),
        name="swa_attention",
    )(sinks, proj, proj, proj, proj, proj, proj, proj, proj, pos_col, pos_row, pos_row)


def _wo_kernel(oa_ref, ob_ref, ga_ref, gb_ref, w_ref, x_ref, gy_ref, gn_ref, x1_ref, h_ref, *, chunks):
    ka = oa_ref.shape[1]
    na = _rms(oa_ref[...].astype(F32), ga_ref[...]).astype(BF16)
    nb = _rms(ob_ref[...].astype(F32), gb_ref[...]).astype(BF16)
    for lo, hi in chunks:
        x1_ref[:, lo:hi] = (jnp.dot(na, w_ref[:ka, lo:hi], preferred_element_type=F32)
                            + jnp.dot(nb, w_ref[ka:, lo:hi], preferred_element_type=F32))
    rows = 32
    for r in range(x1_ref.shape[0] // rows):
        rs = slice(r * rows, (r + 1) * rows)
        x1 = x_ref[rs, :] + _rms(x1_ref[rs, :], gy_ref[...])
        x1_ref[rs, :] = x1
        h_ref[rs, :] = _rms(x1, gn_ref[...]).astype(h_ref.dtype)


def out_proj_resid(oa, ob, ga, gb, w, x, gy, gn, tm=ROW_TILE // 2):
    m, ka = oa.shape
    kb = ob.shape[1]
    n = w.shape[1]
    row = pl.BlockSpec((tm, n), lambda i: (i, 0))
    vec = pl.BlockSpec((1, n), lambda i: (0, 0))
    return pl.pallas_call(
        functools.partial(_wo_kernel, chunks=_col_chunks(n, 4 * MXU_N)),
        grid=(m // tm,),
        in_specs=[pl.BlockSpec((tm, ka), lambda i: (i, 0)),
                  pl.BlockSpec((tm, kb), lambda i: (i, 0)),
                  pl.BlockSpec((1, ka), lambda i: (0, 0)),
                  pl.BlockSpec((1, kb), lambda i: (0, 0)),
                  pl.BlockSpec((ka + kb, n), lambda i: (0, 0), pipeline_mode=pl.Buffered(1)),
                  row, vec, vec],
        out_specs=[row, row],
        out_shape=[jax.ShapeDtypeStruct((m, n), F32), jax.ShapeDtypeStruct((m, n), BF16)],
        compiler_params=_cparams(("parallel",)),
        name="out_proj",
    )(oa, ob, ga, gb, w, x, gy, gn)


def _ffn_kernel(h_ref, wg_ref, wu_ref, wd_ref, o_ref, acc_sc, *, dn, rows):
    j = pl.program_id(1)
    tm, d = acc_sc.shape

    def hidden_tile(first):
        h = h_ref[...]
        g = jnp.dot(h, wg_ref[...], preferred_element_type=F32)
        u = jnp.dot(h, wu_ref[...], preferred_element_type=F32)
        a = (g * jax.nn.sigmoid(g) * u).astype(BF16)
        for c in range(d // dn):
            cols = slice(c * dn, (c + 1) * dn)
            y = jnp.dot(a, wd_ref[:, cols], preferred_element_type=F32)
            acc_sc[:, cols] = y if first else acc_sc[:, cols] + y

    @pl.when(j == 0)
    def _():
        hidden_tile(True)

    @pl.when(j > 0)
    def _():
        hidden_tile(False)

    @pl.when(j == pl.num_programs(1) - 1)
    def _():
        def body(r, carry):
            rs = pl.ds(pl.multiple_of(r * rows, rows), rows)
            o_ref[rs, :] = acc_sc[rs, :].astype(o_ref.dtype)
            return carry

        lax.fori_loop(0, tm // rows, body, 0)


def ffn(h, wg, wu, wd, tm=1024, tf=FFN_TILE):
    m, d = h.shape
    f = wg.shape[1]
    return pl.pallas_call(
        functools.partial(_ffn_kernel, dn=1024, rows=64),
        grid=(m // tm, f // tf),
        in_specs=[pl.BlockSpec((tm, d), lambda i, j: (i, 0)),
                  pl.BlockSpec((d, tf), lambda i, j: (0, j)),
                  pl.BlockSpec((d, tf), lambda i, j: (0, j)),
                  pl.BlockSpec((tf, d), lambda i, j: (j, 0))],
        out_specs=pl.BlockSpec((tm, d), lambda i, j: (i, 0), pipeline_mode=pl.Buffered(1)),
        out_shape=jax.ShapeDtypeStruct((m, d), BF16),
        scratch_shapes=[pltpu.VMEM((tm, d), F32)],
        compiler_params=_cparams(("parallel", "arbitrary")),
        name="ffn",
    )(h, wg, wu, wd)


def _ple_kernel(x_ref, f_ref, gf_ref, wg_ref, p_ref, wp_ref, o_ref, *, chunks):
    o_ref[...] = x_ref[...] + _rms(f_ref[...].astype(F32), gf_ref[...])
    xb = o_ref[...].astype(BF16)
    pb = p_ref[...].astype(BF16)
    for lo, hi in chunks:
        gate = jax.nn.sigmoid(jnp.dot(xb, wg_ref[:, lo:hi], preferred_element_type=F32))
        e = jnp.dot(pb, wp_ref[:, lo:hi], preferred_element_type=F32)
        o_ref[:, lo:hi] += gate * e


def ple(x, f, g_f, wg, p, wp, tm=ROW_TILE):
    m, d = x.shape
    n = wg.shape[1]
    pd = p.shape[1]
    once = dict(pipeline_mode=pl.Buffered(1))
    return pl.pallas_call(
        functools.partial(_ple_kernel, chunks=_col_chunks(n, 4 * MXU_N)),
        grid=(m // tm,),
        in_specs=[pl.BlockSpec((tm, d), lambda i: (i, 0)),
                  pl.BlockSpec((tm, d), lambda i: (i, 0)),
                  pl.BlockSpec((1, d), lambda i: (0, 0)),
                  pl.BlockSpec((d, n), lambda i: (0, 0), **once),
                  pl.BlockSpec((tm, pd), lambda i: (i, 0)),
                  pl.BlockSpec((pd, n), lambda i: (0, 0), **once)],
        out_specs=pl.BlockSpec((tm, n), lambda i: (i, 0)),
        out_shape=jax.ShapeDtypeStruct((m, n), F32),
        compiler_params=_cparams(("parallel",)),
        name="ple",
    )(x, f, g_f, wg, p, wp)


_KR_BLOCK = (Q_LORA + KV_LORA) // MLA_ROPE
_QSWA_BLOCKS = (COL_QSWA // LANES, COL_KSWA // LANES)


def _w_in_prep_kernel(wa_ref, wb_ref, o_ref):
    ob = pl.program_id(0)
    a, b = wa_ref[0], wb_ref[0]
    half = MLA_ROPE // 2
    partner = jnp.concatenate([-b[half:], b[:half]], axis=0)
    b = jnp.where(ob == pl.num_programs(0) - 1, partner, b)
    scale = jnp.where((ob >= _QSWA_BLOCKS[0]) & (ob < _QSWA_BLOCKS[1]), LOG2E / math.sqrt(SWA_HEAD_DIM), 1.0)
    blk = jnp.concatenate([a, b], axis=0) * scale
    o_ref[...] = blk.T.astype(BF16)


def _prep_w_in(w3, layer):
    wt = jnp.swapaxes(w3, 1, 2)
    _, n, k = wt.shape
    n_out = D_IN2 // LANES
    shifted = COL_QSWA // LANES

    def src(ob, part):
        blk = 2 * ob + part + jnp.where(ob >= shifted, 1, 0)
        return jnp.where(ob == n_out - 1, _KR_BLOCK, blk)

    return pl.pallas_call(
        _w_in_prep_kernel,
        grid=(n_out,),
        in_specs=[pl.BlockSpec((1, MLA_ROPE, k), lambda ob: (layer, src(ob, 0), 0)),
                  pl.BlockSpec((1, MLA_ROPE, k), lambda ob: (layer, src(ob, 1), 0))],
        out_specs=pl.BlockSpec((k, LANES), lambda ob: (0, ob)),
        out_shape=jax.ShapeDtypeStruct((k, D_IN2), BF16),
        compiler_params=_cparams(("parallel",)),
        name="w_in_prep",
    )(wt, wt)


def _prep_w_q_up(w):
    w = (w * (LOG2E / math.sqrt(MLA_NOPE + MLA_ROPE))).reshape(Q_LORA, MLA_HEADS, MLA_NOPE + MLA_ROPE)
    nope = w[:, :, :MLA_NOPE].reshape(Q_LORA, MLA_WIDTH)
    rope = w[:, :, MLA_NOPE:].reshape(Q_LORA, MLA_HEADS * MLA_ROPE)
    return jnp.concatenate([rope, nope], axis=1).astype(BF16)


def _prep_w_kv_up(w):
    w = w.reshape(KV_LORA, MLA_HEADS, MLA_NOPE + MLA_V)
    wk = w[:, :, :MLA_NOPE].reshape(KV_LORA, MLA_WIDTH).astype(BF16)
    wvt = w[:, :, MLA_NOPE:].reshape(KV_LORA, MLA_WIDTH).T.astype(BF16)
    return wk, wvt


def _layer(x, p, pos_col, pos_row, invf, batch, seq, attn_pre_norm, w_in, q_a_norm, w_q_up,
           kv_a_norm, w_kv_up, sinks, mla_out_norm, swa_out_norm, w_o, attn_post_norm,
           ffn_pre_norm, w_gate, w_up, w_down, ffn_post_norm, w_ple_gate, w_ple_proj):
    row = lambda g: g.reshape(1, -1)
    proj = in_proj(x, row(attn_pre_norm), _prep_w_in(*w_in))
    qn, qr = q_up(proj, row(q_a_norm), _prep_w_q_up(w_q_up), pos_col, invf)
    wk, wvt = _prep_w_kv_up(w_kv_up)
    kn, vt, kr = kv_up(proj, row(kv_a_norm), wk, wvt, pos_col, invf, batch, seq)
    o_mla, (wo16, wg16, wu16, wd16, wpg16) = mla_attention(
        qn, qr, kn, kr, vt, [w_o, w_gate, w_up, w_down, w_ple_gate], batch, seq)
    o_swa = swa_attention(proj, sinks, pos_col, pos_row, batch, seq)
    x1, h2 = out_proj_resid(o_mla, o_swa, row(mla_out_norm), row(swa_out_norm), wo16,
                            x, row(attn_post_norm), row(ffn_pre_norm))
    f = ffn(h2, wg16, wu16, wd16)
    return ple(x1, f, row(ffn_post_norm), wpg16, p, w_ple_proj.astype(BF16))


def kernel(x, p, positions, attn_pre_norm, w_in, q_a_norm, w_q_up, kv_a_norm, w_kv_up, sinks,
           mla_out_norm, swa_out_norm, w_o, attn_post_norm, ffn_pre_norm, w_gate, w_up, w_down,
           ffn_post_norm, w_ple_gate, w_ple_proj):
    batch, seq, d = x.shape
    depth = w_in.shape[0]
    t = batch * seq
    pos_col = positions.reshape(t, 1)
    pos_row = positions.reshape(t // BLOCK, 1, BLOCK)
    half = MLA_ROPE // 2
    invf = np.asarray(ROPE_THETA ** (-(np.arange(LANES) % half) * 2.0 / MLA_ROPE), np.float32).reshape(1, LANES)
    invf = jnp.asarray(invf)
    xf = x.reshape(t, d)
    for i in range(depth):
        xf = _layer(xf, p[i].reshape(t, PLE_DIM), pos_col, pos_row, invf, batch, seq,
                    attn_pre_norm[i], (w_in, i), q_a_norm[i], w_q_up[i], kv_a_norm[i], w_kv_up[i],
                    sinks[i], mla_out_norm[i], swa_out_norm[i], w_o[i], attn_post_norm[i],
                    ffn_pre_norm[i], w_gate[i], w_up[i], w_down[i], ffn_post_norm[i],
                    w_ple_gate[i], w_ple_proj[i])
    return xf.reshape(batch, seq, d)
```

```python
import functools
import math

import numpy as np
import jax
import jax.numpy as jnp
from jax import lax
from jax.experimental import pallas as pl
from jax.experimental.pallas import tpu as pltpu

PLE_DIM = 256
MLA_HEADS = 16
MLA_NOPE = 128
MLA_ROPE = 64
MLA_V = 128
Q_LORA = 1024
KV_LORA = 512
MLA_WIDTH = MLA_HEADS * MLA_V
SWA_HEADS = 32
SWA_KV_HEADS = 4
SWA_HEAD_DIM = 64
SWA_GROUP = SWA_HEADS // SWA_KV_HEADS
SWA_WIDTH = SWA_HEADS * SWA_HEAD_DIM
WINDOW = 128
BLOCK = 128
ROPE_THETA = 10000.0
NORM_EPS = 1e-6
LOG2E = math.log2(math.e)

LANES = 128
MXU_N = 256
BF16_ROWS = 16
MLA_TILE = 512
ROW_TILE = 256
FFN_TILE = MXU_N
VT_ROWS = MLA_V + BF16_ROWS
VMEM_LIMIT = 60 * 1024 * 1024

COL_CQ = 0
COL_CKV = COL_CQ + Q_LORA
COL_QSWA = COL_CKV + KV_LORA
COL_KSWA = COL_QSWA + SWA_WIDTH
COL_VSWA = COL_KSWA + SWA_KV_HEADS * SWA_HEAD_DIM
COL_KR = COL_VSWA + SWA_KV_HEADS * SWA_HEAD_DIM
D_IN2 = COL_KR + 2 * MLA_ROPE

F32 = jnp.float32
BF16 = jnp.bfloat16


def _cparams(sem):
    return pltpu.CompilerParams(dimension_semantics=sem, vmem_limit_bytes=VMEM_LIMIT)


def _rms(xf, g):
    ms = jnp.mean(xf * xf, axis=-1, keepdims=True)
    return xf * lax.rsqrt(ms + NORM_EPS) * g


def _rope_tables(pos_col, invf_row):
    ang = pos_col.astype(F32) * invf_row
    return jnp.cos(ang), jnp.sin(ang)


def _col_chunks(n, width):
    return [(c, min(c + width, n)) for c in range(0, n, width)]


def _in_proj_kernel(x_ref, g_ref, w_ref, o_ref, *, chunks):
    h = _rms(x_ref[...], g_ref[...]).astype(BF16)
    for lo, hi in chunks:
        o_ref[:, lo:hi] = jnp.dot(h, w_ref[:, lo:hi], preferred_element_type=F32).astype(o_ref.dtype)


def in_proj(x, g, w, tm=ROW_TILE):
    m, k = x.shape
    n = w.shape[1]
    return pl.pallas_call(
        functools.partial(_in_proj_kernel, chunks=_col_chunks(n, 6 * MXU_N)),
        grid=(m // tm,),
        in_specs=[pl.BlockSpec((tm, k), lambda i: (i, 0)),
                  pl.BlockSpec((1, k), lambda i: (0, 0)),
                  pl.BlockSpec((k, n), lambda i: (0, 0), pipeline_mode=pl.Buffered(1))],
        out_specs=pl.BlockSpec((tm, n), lambda i: (i, 0)),
        out_shape=jax.ShapeDtypeStruct((m, n), BF16),
        compiler_params=_cparams(("parallel",)),
        name="in_proj",
    )(x, g, w)


def _qkv_up_kernel(cq_ref, gq_ref, wq_ref, ckv_ref, gkv_ref, wk_ref, wvt_ref, krab_ref, pos_ref, invf_ref,
                   qn_ref, qr_ref, kn_ref, vt_ref, kr_ref):
    cos, sin = _rope_tables(pos_ref[...], invf_ref[...])
    lane = lax.broadcasted_iota(jnp.int32, cos.shape, 1)
    half = MLA_ROPE // 2

    hq = _rms(cq_ref[...].astype(F32), gq_ref[...]).astype(BF16)
    n_rope = MLA_HEADS * MLA_ROPE
    rot = jnp.dot(hq, wq_ref[:, :n_rope], preferred_element_type=F32)
    qn_ref[...] = jnp.dot(hq, wq_ref[:, n_rope:], preferred_element_type=F32).astype(BF16)
    first_half = (lane % MLA_ROPE) < half
    for c in range(n_rope // LANES):
        a = rot[:, c * LANES:(c + 1) * LANES]
        b = jnp.where(first_half, -pltpu.roll(a, LANES - half, 1), pltpu.roll(a, half, 1))
        qr_ref[:, c * LANES:(c + 1) * LANES] = (a * cos + b * sin).astype(BF16)

    hkv = _rms(ckv_ref[...].astype(F32), gkv_ref[...]).astype(BF16)
    kn_ref[...] = jnp.dot(hkv, wk_ref[...], preferred_element_type=F32).astype(BF16)
    vt = lax.dot_general(wvt_ref[...], hkv, (((1,), (1,)), ((), ())), preferred_element_type=F32)
    ones = jnp.ones((VT_ROWS - MLA_V, vt.shape[1]), BF16)
    for hh in range(MLA_HEADS):
        vt_ref[0, hh, 0, :MLA_V, :] = vt[hh * MLA_V:(hh + 1) * MLA_V, :].astype(BF16)
        vt_ref[0, hh, 0, MLA_V:, :] = ones
    lo = lane < MLA_ROPE
    prod = krab_ref[...].astype(F32) * jnp.where(lo, cos, sin)
    kk = prod + pltpu.roll(prod, MLA_ROPE, 1)
    zero = jnp.zeros_like(kk)
    kr_ref[:, :LANES] = jnp.where(lo, kk, zero).astype(BF16)
    kr_ref[:, LANES:] = jnp.where(lo, zero, kk).astype(BF16)


def qkv_up(proj, gq, wq, gkv, wk, wvt, pos_col, invf, batch, seq, tm=MLA_TILE):
    m = proj.shape[0]
    nk = seq // tm
    n_rope = MLA_HEADS * MLA_ROPE
    const = lambda i: (0, 0)
    return pl.pallas_call(
        _qkv_up_kernel,
        grid=(m // tm,),
        in_specs=[pl.BlockSpec((tm, Q_LORA), lambda i: (i, COL_CQ // Q_LORA)),
                  pl.BlockSpec((1, Q_LORA), const),
                  pl.BlockSpec(wq.shape, const),
                  pl.BlockSpec((tm, KV_LORA), lambda i: (i, COL_CKV // KV_LORA)),
                  pl.BlockSpec((1, KV_LORA), const),
                  pl.BlockSpec(wk.shape, const),
                  pl.BlockSpec(wvt.shape, const),
                  pl.BlockSpec((tm, LANES), lambda i: (i, COL_KR // LANES)),
                  pl.BlockSpec((tm, 1), lambda i: (i, 0)),
                  pl.BlockSpec((1, LANES), const)],
        out_specs=[pl.BlockSpec((tm, MLA_WIDTH), lambda i: (i, 0)),
                   pl.BlockSpec((tm, n_rope), lambda i: (i, 0)),
                   pl.BlockSpec((tm, MLA_WIDTH), lambda i: (i, 0)),
                   pl.BlockSpec((1, MLA_HEADS, 1, VT_ROWS, tm), lambda i: (i // nk, 0, i % nk, 0, 0)),
                   pl.BlockSpec((tm, 2 * LANES), lambda i: (i, 0))],
        out_shape=[jax.ShapeDtypeStruct((m, MLA_WIDTH), BF16),
                   jax.ShapeDtypeStruct((m, n_rope), BF16),
                   jax.ShapeDtypeStruct((m, MLA_WIDTH), BF16),
                   jax.ShapeDtypeStruct((batch, MLA_HEADS, nk, VT_ROWS, tm), BF16),
                   jax.ShapeDtypeStruct((m, 2 * LANES), BF16)],
        compiler_params=_cparams(("parallel",)),
        name="qkv_up",
    )(proj, gq, wq, proj, gkv, wk, wvt, proj, pos_col, invf)


def _mla_kernel(*refs, tq, tk, n_cast):
    qn_ref, qr_ref, kn_ref, kr_ref, vt_ref = refs[:5]
    w_refs = refs[5:5 + n_cast]
    o_ref = refs[5 + n_cast]
    w16_refs = refs[6 + n_cast:6 + 2 * n_cast]
    s_a, s_b, mx_a, mx_b, m_sc, acc_sc = refs[6 + 2 * n_cast:]

    qi = pl.program_id(2)
    q = jnp.concatenate([qn_ref[...], qr_ref[...]], axis=-1)
    m_sc[...] = jnp.full(m_sc.shape, -jnp.inf, F32)
    acc_sc[...] = jnp.zeros(acc_sc.shape, F32)

    n_chunk = tq // MXU_N
    blk_chunks = tk // MXU_N
    n_diag = tq // tk
    bufs = ((s_a, mx_a), (s_b, mx_b))

    def live_chunks(diag):
        return range(0 if diag is None else diag * blk_chunks, n_chunk)

    def scores(kb, buf, diag=None):
        s_ref, mx_ref = buf
        off = pl.multiple_of(kb * tk, tk)
        k = jnp.concatenate([kn_ref[pl.ds(off, tk), :], kr_ref[pl.ds(off, tk), :]], axis=-1)
        for c in live_chunks(diag):
            s = lax.dot_general(k, q[c * MXU_N:(c + 1) * MXU_N, :], (((1,), (1,)), ((), ())),
                                preferred_element_type=F32)
            if diag is not None and c < (diag + 1) * blk_chunks:
                key = lax.broadcasted_iota(jnp.int32, s.shape, 0) + diag * tk
                qry = lax.broadcasted_iota(jnp.int32, s.shape, 1) + c * MXU_N
                s = jnp.where(qry >= key, s, -jnp.inf)
            s_ref[c] = s
            mx_ref[:, c * MXU_N:(c + 1) * MXU_N] = jnp.max(s, axis=0, keepdims=True)

    def softmax_pv(kb, buf, diag=None, mask_here=False):
        s_ref, mx_ref = buf
        vt = vt_ref[0, 0, kb]
        for c in live_chunks(diag):
            sl = slice(c * MXU_N, (c + 1) * MXU_N)
            remask = mask_here and c < blk_chunks

            def s_chunk():
                s = s_ref[c]
                if remask:
                    key = lax.broadcasted_iota(jnp.int32, s.shape, 0)
                    qry = lax.broadcasted_iota(jnp.int32, s.shape, 1) + c * MXU_N
                    s = jnp.where(qry >= key, s, -jnp.inf)
                return s

            mx = jnp.max(s_chunk(), axis=0, keepdims=True) if remask else mx_ref[:, sl]
            m_old = m_sc[:, sl]
            m_new = jnp.maximum(m_old, mx)
            alpha = jnp.exp2(m_old - m_new)
            p = jnp.exp2(s_chunk() - m_new).astype(BF16)
            acc_sc[:, sl] = alpha * acc_sc[:, sl] + jnp.dot(vt, p, preferred_element_type=F32)
            m_sc[:, sl] = m_new

    n_full = n_diag * qi
    scores(0, bufs[0])

    def body(j, carry):
        for t in range(n_diag):
            kb = n_diag * j + t
            scores(kb + 1, bufs[(t + 1) % 2])
            softmax_pv(kb, bufs[t % 2])
        return carry

    lax.fori_loop(0, qi, body, 0)

    for w_ref, w16_ref in zip(w_refs, w16_refs):
        w16_ref[...] = w_ref[...].astype(BF16)

    for d in range(n_diag):
        if d + 1 < n_diag:
            scores(n_full + d + 1, bufs[(d + 1) % 2], diag=d + 1)
        softmax_pv(n_full + d, bufs[d % 2], diag=d, mask_here=(d == 0))

    o_ref[...] = (acc_sc[:MLA_V, :] / acc_sc[MLA_V:MLA_V + 1, :]).T.astype(o_ref.dtype)


def mla_attention(qn, qr, kn, kr, vt, weights, batch, seq, tq=4 * MLA_TILE, tk=MLA_TILE):
    assert tq % (2 * tk) == 0
    nq = seq // tq
    steps = batch * MLA_HEADS * nq

    def slab(w):
        tiles, rem = divmod(w.shape[0], BF16_ROWS)
        assert rem == 0, w.shape
        n_slab = max(n for n in range(1, steps + 1) if tiles % n == 0)
        rows = w.shape[0] // n_slab
        return pl.BlockSpec((rows, w.shape[1]),
                            lambda b, h, i: (jnp.minimum((b * MLA_HEADS + h) * nq + i, n_slab - 1), 0))

    slabs = [slab(w) for w in weights]
    outs = pl.pallas_call(
        functools.partial(_mla_kernel, tq=tq, tk=tk, n_cast=len(weights)),
        grid=(batch, MLA_HEADS, nq),
        in_specs=[pl.BlockSpec((tq, MLA_NOPE), lambda b, h, i: (b * nq + i, h)),
                  pl.BlockSpec((tq, LANES), lambda b, h, i: (b * nq + i, h // 2)),
                  pl.BlockSpec((seq, MLA_NOPE), lambda b, h, i: (b, h)),
                  pl.BlockSpec((seq, LANES), lambda b, h, i: (b, h % 2)),
                  pl.BlockSpec((1, 1, seq // tk, VT_ROWS, tk), lambda b, h, i: (b, h, 0, 0, 0))] + slabs,
        out_specs=[pl.BlockSpec((tq, MLA_V), lambda b, h, i: (b * nq + i, h))] + slabs,
        out_shape=[jax.ShapeDtypeStruct((batch * seq, MLA_WIDTH), BF16)]
        + [jax.ShapeDtypeStruct(w.shape, BF16) for w in weights],
        scratch_shapes=[pltpu.VMEM((tq // MXU_N, tk, MXU_N), F32), pltpu.VMEM((tq // MXU_N, tk, MXU_N), F32),
                        pltpu.VMEM((1, tq), F32), pltpu.VMEM((1, tq), F32),
                        pltpu.VMEM((1, tq), F32), pltpu.VMEM((VT_ROWS, tq), F32)],
        compiler_params=_cparams(("arbitrary", "arbitrary", "arbitrary")),
        name="mla_attention",
    )(qn, qr, kn, kr, vt, *weights)
    return outs[0], outs[1:]


def _swa_kernel(sinks_ref, q0_ref, q1_ref, q2_ref, q3_ref, kc_ref, kp_ref, vc_ref, vp_ref,
                pq_ref, pkc_ref, pkp_ref, o_ref):
    blk = pl.program_id(1)
    q_refs = (q0_ref, q1_ref, q2_ref, q3_ref)
    k_all = jnp.concatenate([kp_ref[...], kc_ref[...]], axis=0).astype(F32)
    v_all = jnp.concatenate([vp_ref[...], vc_ref[...]], axis=0).astype(F32)
    pk = jnp.concatenate([pkp_ref[0], pkc_ref[0]], axis=-1).astype(F32)
    dist = jnp.abs(pq_ref[...].astype(F32) - pk)
    row = lax.broadcasted_iota(jnp.int32, dist.shape, 0)
    col = lax.broadcasted_iota(jnp.int32, dist.shape, 1)
    rel = BLOCK + row - col
    first_key = jnp.where(blk > 0, 0, BLOCK)
    valid = (rel >= 0) & (rel < WINDOW) & (col >= first_key)
    dist_w = jnp.where(valid, dist, jnp.inf)
    lane =lax.broadcasted_iota(jnp.int32, (2 * BLOCK, LANES), 1)
    lo = lane < SWA_HEAD_DIM
    lane_o = lax.broadcasted_iota(jnp.int32, (BLOCK, LANES), 1)
    lo_o = lane_o < SWA_HEAD_DIM

    def dup(x_all, g):
        pair = x_all[:, (g // 2) * LANES:(g // 2 + 1) * LANES]
        rolled = pltpu.roll(pair, SWA_HEAD_DIM, 1)
        return jnp.where(lo, pair, rolled) if g % 2 == 0 else jnp.where(lo, rolled, pair)

    for g in range(SWA_KV_HEADS):
        kk = dup(k_all, g)
        vv = dup(v_all, g).astype(BF16)
        zero = jnp.zeros_like(kk)
        k_half = (jnp.where(lo, kk, zero).astype(BF16), jnp.where(lo, zero, kk).astype(BF16))
        for i in range(SWA_GROUP // 2):
            q_pair = q_refs[g][:, i * LANES:(i + 1) * LANES]
            outs = []
            for j in range(2):
                head = g * SWA_GROUP + 2 * i + j
                slope2 = LOG2E * 2.0 ** (-8.0 * (head + 1) / SWA_HEADS)
                s = lax.dot_general(q_pair, k_half[j], (((1,), (1,)), ((), ())),
                                    preferred_element_type=F32)
                s = s - slope2 * dist_w
                sink2 = LOG2E * sinks_ref[head]
                m = jnp.maximum(jnp.max(s, axis=-1, keepdims=True), sink2)
                e = jnp.exp2(s - m)
                denom = jnp.sum(e, axis=-1, keepdims=True) + jnp.exp2(sink2 - m)
                pv = jnp.dot(e.astype(BF16), vv, preferred_element_type=F32)
                outs.append(pv / denom)
            col0 = (g * SWA_GROUP // 2 + i) * LANES
            o_ref[:, col0:col0 + LANES] = jnp.where(lo_o, outs[0], outs[1]).astype(o_ref.dtype)


def swa_attention(proj, sinks, pos_col, pos_row, batch, seq):
    nblk = seq // BLOCK
    kvw = SWA_KV_HEADS * SWA_HEAD_DIM
    gw = SWA_GROUP * SWA_HEAD_DIM
    cur = lambda b, n: b * nblk + n
    prev = lambda b, n: b * nblk + jnp.maximum(n - 1, 0)
    q_specs = [pl.BlockSpec((BLOCK, gw), functools.partial(lambda b, n, g: (cur(b, n), COL_QSWA // gw + g), g=g))
               for g in range(SWA_KV_HEADS)]
    return pl.pallas_call(
        _swa_kernel,
        grid=(batch, nblk),
        in_specs=[pl.BlockSpec(memory_space=pltpu.SMEM)] + q_specs + [
            pl.BlockSpec((BLOCK, kvw), lambda b, n: (cur(b, n), COL_KSWA // kvw)),
            pl.BlockSpec((BLOCK, kvw), lambda b, n: (prev(b, n), COL_KSWA // kvw)),
            pl.BlockSpec((BLOCK, kvw), lambda b, n: (cur(b, n), COL_VSWA // kvw)),
            pl.BlockSpec((BLOCK, kvw), lambda b, n: (prev(b, n), COL_VSWA // kvw)),
            pl.BlockSpec((BLOCK, 1), lambda b, n: (cur(b, n), 0)),
            pl.BlockSpec((1, 1, BLOCK), lambda b, n: (cur(b, n), 0, 0)),
            pl.BlockSpec((1, 1, BLOCK), lambda b, n: (prev(b, n), 0, 0))],
        out_specs=pl.BlockSpec((BLOCK, SWA_WIDTH), lambda b, n: (cur(b, n), 0)),
        out_shape=jax.ShapeDtypeStruct((batch * seq, SWA_WIDTH), BF16),
        compiler_params=_cparams(("parallel", "parallel")),
        name="swa_attention",
    )(sinks, proj, proj, proj, proj, proj, proj, proj, proj, pos_col, pos_row, pos_row)


def _wo_kernel(oa_ref, ob_ref, ga_ref, gb_ref, w_ref, x_ref, gy_ref, gn_ref, x1_ref, h_ref, *, chunks):
    ka = oa_ref.shape[1]
    na = _rms(oa_ref[...].astype(F32), ga_ref[...]).astype(BF16)
    nb = _rms(ob_ref[...].astype(F32), gb_ref[...]).astype(BF16)
    for lo, hi in chunks:
        x1_ref[:, lo:hi] = (jnp.dot(na, w_ref[:ka, lo:hi], preferred_element_type=F32)
                            + jnp.dot(nb, w_ref[ka:, lo:hi], preferred_element_type=F32))
    rows = 32
    for r in range(x1_ref.shape[0] // rows):
        rs = slice(r * rows, (r + 1) * rows)
        x1 = x_ref[rs, :] + _rms(x1_ref[rs, :], gy_ref[...])
        x1_ref[rs, :] = x1
        h_ref[rs, :] = _rms(x1, gn_ref[...]).astype(h_ref.dtype)


def out_proj_resid(oa, ob, ga, gb, w, x, gy, gn, tm=ROW_TILE // 2):
    m, ka = oa.shape
    kb = ob.shape[1]
    n = w.shape[1]
    row = pl.BlockSpec((tm, n), lambda i: (i, 0))
    vec = pl.BlockSpec((1, n), lambda i: (0, 0))
    return pl.pallas_call(
        functools.partial(_wo_kernel, chunks=_col_chunks(n, 4 * MXU_N)),
        grid=(m // tm,),
        in_specs=[pl.BlockSpec((tm, ka), lambda i: (i, 0)),
                  pl.BlockSpec((tm, kb), lambda i: (i, 0)),
                  pl.BlockSpec((1, ka), lambda i: (0, 0)),
                  pl.BlockSpec((1, kb), lambda i: (0, 0)),
                  pl.BlockSpec((ka + kb, n), lambda i: (0, 0), pipeline_mode=pl.Buffered(1)),
                  row, vec, vec],
        out_specs=[row, row],
        out_shape=[jax.ShapeDtypeStruct((m, n), F32), jax.ShapeDtypeStruct((m, n), BF16)],
        compiler_params=_cparams(("parallel",)),
        name="out_proj",
    )(oa, ob, ga, gb, w, x, gy, gn)


def _ffn_kernel(h_ref, wg_ref, wu_ref, wd_ref, o_ref, acc_sc, *, dn, rows):
    j = pl.program_id(1)
    tm, d = acc_sc.shape

    def hidden_tile(first):
        h = h_ref[...]
        g = jnp.dot(h, wg_ref[...], preferred_element_type=F32)
        u = jnp.dot(h, wu_ref[...], preferred_element_type=F32)
        a = (g * jax.nn.sigmoid(g) * u).astype(BF16)
        for c in range(d // dn):
            cols = slice(c * dn, (c + 1) * dn)
            y = jnp.dot(a, wd_ref[:, cols], preferred_element_type=F32)
            acc_sc[:, cols] = y if first else acc_sc[:, cols] + y

    @pl.when(j == 0)
    def _():
        hidden_tile(True)

    @pl.when(j > 0)
    def _():
        hidden_tile(False)

    @pl.when(j == pl.num_programs(1) - 1)
    def _():
        def body(r, carry):
            rs = pl.ds(pl.multiple_of(r * rows, rows), rows)
            o_ref[rs, :] = acc_sc[rs, :].astype(o_ref.dtype)
            return carry

        lax.fori_loop(0, tm // rows, body, 0)


def ffn(h, wg, wu, wd, tm=1024, tf=FFN_TILE):
    m, d = h.shape
    f = wg.shape[1]
    return pl.pallas_call(
        functools.partial(_ffn_kernel, dn=1024, rows=64),
        grid=(m // tm, f // tf),
        in_specs=[pl.BlockSpec((tm, d), lambda i, j: (i, 0)),
                  pl.BlockSpec((d, tf), lambda i, j: (0, j)),
                  pl.BlockSpec((d, tf), lambda i, j: (0, j)),
                  pl.BlockSpec((tf, d), lambda i, j: (j, 0))],
        out_specs=pl.BlockSpec((tm, d), lambda i, j: (i, 0), pipeline_mode=pl.Buffered(1)),
        out_shape=jax.ShapeDtypeStruct((m, d), BF16),
        scratch_shapes=[pltpu.VMEM((tm, d), F32)],
        compiler_params=_cparams(("parallel", "arbitrary")),
        name="ffn",
    )(h, wg, wu, wd)


def _ple_kernel(x_ref, f_ref, gf_ref, wg_ref, p_ref, wp_ref, o_ref, *, chunks):
    o_ref[...] = x_ref[...] + _rms(f_ref[...].astype(F32), gf_ref[...])
    xb = o_ref[...].astype(BF16)
    pb = p_ref[...].astype(BF16)
    for lo, hi in chunks:
        gate = jax.nn.sigmoid(jnp.dot(xb, wg_ref[:, lo:hi], preferred_element_type=F32))
        e = jnp.dot(pb, wp_ref[:, lo:hi], preferred_element_type=F32)
        o_ref[:, lo:hi] += gate * e


def ple(x, f, g_f, wg, p, wp, tm=ROW_TILE):
    m, d = x.shape
    n = wg.shape[1]
    pd = p.shape[1]
    once = dict(pipeline_mode=pl.Buffered(1))
    return pl.pallas_call(
        functools.partial(_ple_kernel, chunks=_col_chunks(n, 4 * MXU_N)),
        grid=(m // tm,),
        in_specs=[pl.BlockSpec((tm, d), lambda i: (i, 0)),
                  pl.BlockSpec((tm, d), lambda i: (i, 0)),
                  pl.BlockSpec((1, d), lambda i: (0, 0)),
                  pl.BlockSpec((d, n), lambda i: (0, 0), **once),
                  pl.BlockSpec((tm, pd), lambda i: (i, 0)),
                  pl.BlockSpec((pd, n), lambda i: (0, 0), **once)],
        out_specs=pl.BlockSpec((tm, n), lambda i: (i, 0)),
        out_shape=jax.ShapeDtypeStruct((m, n), F32),
        compiler_params=_cparams(("parallel",)),
        name="ple",
    )(x, f, g_f, wg, p, wp)


_KR_BLOCK = (Q_LORA + KV_LORA) // MLA_ROPE
_QSWA_BLOCKS = (COL_QSWA // LANES, COL_KSWA // LANES)


def _w_in_prep_kernel(wa_ref, wb_ref, o_ref):
    ob = pl.program_id(0)
    a, b = wa_ref[0], wb_ref[0]
    half = MLA_ROPE // 2
    partner = jnp.concatenate([-b[half:], b[:half]], axis=0)
    b = jnp.where(ob == pl.num_programs(0) - 1, partner, b)
    scale = jnp.where((ob >= _QSWA_BLOCKS[0]) & (ob < _QSWA_BLOCKS[1]), LOG2E / math.sqrt(SWA_HEAD_DIM), 1.0)
    blk = jnp.concatenate([a, b], axis=0) * scale
    o_ref[...] = blk.T.astype(BF16)


def _prep_w_in(w3, layer):
    wt = jnp.swapaxes(w3, 1, 2)
    _, n, k = wt.shape
    n_out = D_IN2 // LANES
    shifted = COL_QSWA // LANES

    def src(ob, part):
        blk = 2 * ob + part + jnp.where(ob >= shifted, 1, 0)
        return jnp.where(ob == n_out - 1, _KR_BLOCK, blk)

    return pl.pallas_call(
        _w_in_prep_kernel,
        grid=(n_out,),
        in_specs=[pl.BlockSpec((1, MLA_ROPE, k), lambda ob: (layer, src(ob, 0), 0)),
                  pl.BlockSpec((1, MLA_ROPE, k), lambda ob: (layer, src(ob, 1), 0))],
        out_specs=pl.BlockSpec((k, LANES), lambda ob: (0, ob)),
        out_shape=jax.ShapeDtypeStruct((k, D_IN2), BF16),
        compiler_params=_cparams(("parallel",)),
        name="w_in_prep",
    )(wt, wt)


def _prep_w_q_up(w):
    w = (w * (LOG2E / math.sqrt(MLA_NOPE + MLA_ROPE))).reshape(Q_LORA, MLA_HEADS, MLA_NOPE + MLA_ROPE)
    nope = w[:, :, :MLA_NOPE].reshape(Q_LORA, MLA_WIDTH)
    rope = w[:, :, MLA_NOPE:].reshape(Q_LORA, MLA_HEADS * MLA_ROPE)
    return jnp.concatenate([rope, nope], axis=1).astype(BF16)


def _prep_w_kv_up(w):
    w = w.reshape(KV_LORA, MLA_HEADS, MLA_NOPE + MLA_V)
    wk = w[:, :, :MLA_NOPE].reshape(KV_LORA, MLA_WIDTH).astype(BF16)
    wvt = w[:, :, MLA_NOPE:].reshape(KV_LORA, MLA_WIDTH).T.astype(BF16)
    return wk, wvt


def _layer(x, p, pos_col, pos_row, invf, batch, seq, attn_pre_norm, w_in, q_a_norm, w_q_up,
           kv_a_norm, w_kv_up, sinks, mla_out_norm, swa_out_norm, w_o, attn_post_norm,
           ffn_pre_norm, w_gate, w_up, w_down, ffn_post_norm, w_ple_gate, w_ple_proj):
    row = lambda g: g.reshape(1, -1)
    proj = in_proj(x, row(attn_pre_norm), _prep_w_in(*w_in))
    wk, wvt = _prep_w_kv_up(w_kv_up)
    qn, qr, kn, vt, kr = qkv_up(proj, row(q_a_norm), _prep_w_q_up(w_q_up), row(kv_a_norm), wk, wvt,
                                pos_col, invf, batch, seq)
    o_mla, (wo16, wg16, wu16, wd16, wpg16) = mla_attention(
        qn, qr, kn, kr, vt, [w_o, w_gate, w_up, w_down, w_ple_gate], batch, seq)
    o_swa = swa_attention(proj, sinks, pos_col, pos_row, batch, seq)
    x1, h2 = out_proj_resid(o_mla, o_swa, row(mla_out_norm), row(swa_out_norm), wo16,
                            x, row(attn_post_norm), row(ffn_pre_norm))
    f = ffn(h2, wg16, wu16, wd16)
    return ple(x1, f, row(ffn_post_norm), wpg16, p, w_ple_proj.astype(BF16))


def kernel(x, p, positions, attn_pre_norm, w_in, q_a_norm, w_q_up, kv_a_norm, w_kv_up, sinks,
           mla_out_norm, swa_out_norm, w_o, attn_post_norm, ffn_pre_norm, w_gate, w_up, w_down,
           ffn_post_norm, w_ple_gate, w_ple_proj):
    batch, seq, d = x.shape
    depth = w_in.shape[0]
    t = batch * seq
    pos_col = positions.reshape(t, 1)
    pos_row = positions.reshape(t // BLOCK, 1, BLOCK)
    half = MLA_ROPE // 2
    invf = np.asarray(ROPE_THETA ** (-(np.arange(LANES) % half) * 2.0 / MLA_ROPE), np.float32).reshape(1, LANES)
    invf = jnp.asarray(invf)
    xf = x.reshape(t, d)
    for i in range(depth):
        xf = _layer(xf, p[i].reshape(t, PLE_DIM), pos_col, pos_row, invf, batch, seq,
                    attn_pre_norm[i], (w_in, i), q_a_norm[i], w_q_up[i], kv_a_norm[i], w_kv_up[i],
                    sinks[i], mla_out_norm[i], swa_out_norm[i], w_o[i], attn_post_norm[i],
                    ffn_pre_norm[i], w_gate[i], w_up[i], w_down[i], ffn_post_norm[i],
                    w_ple_gate[i], w_ple_proj[i])
    return xf.reshape(batch, seq, d)
```

```python
import functools
import math

import numpy as np
import jax
import jax.numpy as jnp
from jax import lax
from jax.experimental import pallas as pl
from jax.experimental.pallas import tpu as pltpu

PLE_DIM = 256
MLA_HEADS = 16
MLA_NOPE = 128
MLA_ROPE = 64
MLA_V = 128
Q_LORA = 1024
KV_LORA = 512
MLA_WIDTH = MLA_HEADS * MLA_V
SWA_HEADS = 32
SWA_KV_HEADS = 4
SWA_HEAD_DIM = 64
SWA_GROUP = SWA_HEADS // SWA_KV_HEADS
SWA_WIDTH = SWA_HEADS * SWA_HEAD_DIM
WINDOW = 128
BLOCK = 128
ROPE_THETA = 10000.0
NORM_EPS = 1e-6
LOG2E = math.log2(math.e)

LANES = 128
MXU_N = 256
BF16_ROWS = 16
MLA_TILE = 512
ROW_TILE = 256
FFN_TILE = MXU_N
VT_ROWS = MLA_V + BF16_ROWS
VMEM_LIMIT = 62 * 1024 * 1024

COL_CQ = 0
COL_CKV = COL_CQ + Q_LORA
COL_QSWA = COL_CKV + KV_LORA
COL_KSWA = COL_QSWA + SWA_WIDTH
COL_VSWA = COL_KSWA + SWA_KV_HEADS * SWA_HEAD_DIM
COL_KR = COL_VSWA + SWA_KV_HEADS * SWA_HEAD_DIM
D_IN2 = COL_KR + 2 * MLA_ROPE

F32 = jnp.float32
BF16 = jnp.bfloat16


def _cparams(sem):
    return pltpu.CompilerParams(dimension_semantics=sem, vmem_limit_bytes=VMEM_LIMIT)


def _rms(xf, g):
    ms = jnp.mean(xf * xf, axis=-1, keepdims=True)
    return xf * lax.rsqrt(ms + NORM_EPS) * g


def _rope_tables(pos_col, invf_row):
    ang = pos_col.astype(F32) * invf_row
    return jnp.cos(ang), jnp.sin(ang)


def _col_chunks(n, width):
    return [(c, min(c + width, n)) for c in range(0, n, width)]


def _in_proj_kernel(x_ref, g_ref, w_ref, o_ref, *, chunks):
    h = _rms(x_ref[...], g_ref[...]).astype(BF16)
    for lo, hi in chunks:
        o_ref[:, lo:hi] = jnp.dot(h, w_ref[:, lo:hi], preferred_element_type=F32).astype(o_ref.dtype)


def in_proj(x, g, w, tm=ROW_TILE):
    m, k = x.shape
    n = w.shape[1]
    return pl.pallas_call(
        functools.partial(_in_proj_kernel, chunks=_col_chunks(n, 6 * MXU_N)),
        grid=(m // tm,),
        in_specs=[pl.BlockSpec((tm, k), lambda i: (i, 0)),
                  pl.BlockSpec((1, k), lambda i: (0, 0)),
                  pl.BlockSpec((k, n), lambda i: (0, 0), pipeline_mode=pl.Buffered(1))],
        out_specs=pl.BlockSpec((tm, n), lambda i: (i, 0)),
        out_shape=jax.ShapeDtypeStruct((m, n), BF16),
        compiler_params=_cparams(("parallel",)),
        name="in_proj",
    )(x, g, w)


def _q_up_kernel(cq_ref, g_ref, w_ref, pos_ref, invf_ref, qn_ref, qr_ref):
    h = _rms(cq_ref[...].astype(F32), g_ref[...]).astype(BF16)
    n_rope = MLA_HEADS * MLA_ROPE
    rot = jnp.dot(h, w_ref[:, :n_rope], preferred_element_type=F32)
    qn_ref[...] = jnp.dot(h, w_ref[:, n_rope:], preferred_element_type=F32).astype(BF16)
    cos, sin = _rope_tables(pos_ref[...], invf_ref[...])
    half = MLA_ROPE // 2
    lane = lax.broadcasted_iota(jnp.int32, cos.shape, 1)
    first_half = (lane % MLA_ROPE) < half
    for c in range(n_rope // LANES):
        a = rot[:, c * LANES:(c + 1) * LANES]
        b = jnp.where(first_half, -pltpu.roll(a, LANES - half, 1), pltpu.roll(a, half, 1))
        qr_ref[:, c * LANES:(c + 1) * LANES] = (a * cos + b * sin).astype(BF16)


def q_up(proj, g, w, pos_col, invf, tm=512):
    m = proj.shape[0]
    n_rope = MLA_HEADS * MLA_ROPE
    return pl.pallas_call(
        _q_up_kernel,
        grid=(m // tm,),
        in_specs=[pl.BlockSpec((tm, Q_LORA), lambda i: (i, COL_CQ // Q_LORA)),
                  pl.BlockSpec((1, Q_LORA), lambda i: (0, 0)),
                  pl.BlockSpec(w.shape, lambda i: (0, 0)),
                  pl.BlockSpec((tm, 1), lambda i: (i, 0)),
                  pl.BlockSpec((1, LANES), lambda i: (0, 0))],
        out_specs=[pl.BlockSpec((tm, MLA_WIDTH), lambda i: (i, 0)),
                   pl.BlockSpec((tm, n_rope), lambda i: (i, 0))],
        out_shape=[jax.ShapeDtypeStruct((m, MLA_WIDTH), BF16),
                   jax.ShapeDtypeStruct((m, n_rope), BF16)],
        compiler_params=_cparams(("parallel",)),
        name="q_up",
    )(proj, g, w, pos_col, invf)


def _kv_up_kernel(ckv_ref, g_ref, wk_ref, wvt_ref, krab_ref, pos_ref, invf_ref, kn_ref, vt_ref, kr_ref):
    h = _rms(ckv_ref[...].astype(F32), g_ref[...]).astype(BF16)
    kn_ref[...] = jnp.dot(h, wk_ref[...], preferred_element_type=F32).astype(BF16)
    vt = lax.dot_general(wvt_ref[...], h, (((1,), (1,)), ((), ())), preferred_element_type=F32)
    ones = jnp.ones((VT_ROWS - MLA_V, vt.shape[1]), BF16)
    for hh in range(MLA_HEADS):
        vt_ref[0, hh, 0, :MLA_V, :] = vt[hh * MLA_V:(hh + 1) * MLA_V, :].astype(BF16)
        vt_ref[0, hh, 0, MLA_V:, :] = ones
    cos, sin = _rope_tables(pos_ref[...], invf_ref[...])
    lane = lax.broadcasted_iota(jnp.int32, cos.shape, 1)
    lo = lane < MLA_ROPE
    prod = krab_ref[...].astype(F32) * jnp.where(lo, cos, sin)
    kk = prod + pltpu.roll(prod, MLA_ROPE, 1)
    zero = jnp.zeros_like(kk)
    kr_ref[:, :LANES] = jnp.where(lo, kk, zero).astype(BF16)
    kr_ref[:, LANES:] = jnp.where(lo, zero, kk).astype(BF16)


def kv_up(proj, g, wk, wvt, pos_col, invf, batch, seq, tm=MLA_TILE):
    m = proj.shape[0]
    nk = seq // tm
    return pl.pallas_call(
        _kv_up_kernel,
        grid=(m // tm,),
        in_specs=[pl.BlockSpec((tm, KV_LORA), lambda i: (i, COL_CKV // KV_LORA)),
                  pl.BlockSpec((1, KV_LORA), lambda i: (0, 0)),
                  pl.BlockSpec(wk.shape, lambda i: (0, 0)),
                  pl.BlockSpec(wvt.shape, lambda i: (0, 0)),
                  pl.BlockSpec((tm, LANES), lambda i: (i, COL_KR // LANES)),
                  pl.BlockSpec((tm, 1), lambda i: (i, 0)),
                  pl.BlockSpec((1, LANES), lambda i: (0, 0))],
        out_specs=[pl.BlockSpec((tm, MLA_WIDTH), lambda i: (i, 0)),
                   pl.BlockSpec((1, MLA_HEADS, 1, VT_ROWS, tm), lambda i: (i // nk, 0, i % nk, 0, 0)),
                   pl.BlockSpec((tm, 2 * LANES), lambda i: (i, 0))],
        out_shape=[jax.ShapeDtypeStruct((m, MLA_WIDTH), BF16),
                   jax.ShapeDtypeStruct((batch, MLA_HEADS, nk, VT_ROWS, tm), BF16),
                   jax.ShapeDtypeStruct((m, 2 * LANES), BF16)],
        compiler_params=_cparams(("parallel",)),
        name="kv_up",
    )(proj, g, wk, wvt, proj, pos_col, invf)


def _mla_kernel(*refs, tq, tk, n_cast):
    qn_ref, qr_ref, kn_ref, kr_ref, vt_ref = refs[:5]
    w_refs = refs[5:5 + n_cast]
    o_ref = refs[5 + n_cast]
    w16_refs = refs[6 + n_cast:6 + 2 * n_cast]
    s_a, s_b, mx_a, mx_b, m_sc, acc_sc = refs[6 + 2 * n_cast:]

    qi = pl.program_id(2)
    q = jnp.concatenate([qn_ref[...], qr_ref[...]], axis=-1)
    m_sc[...] = jnp.full(m_sc.shape, -jnp.inf, F32)
    acc_sc[...] = jnp.zeros(acc_sc.shape, F32)

    n_chunk = tq // MXU_N
    blk_chunks = tk // MXU_N
    n_diag = tq // tk
    bufs = ((s_a, mx_a), (s_b, mx_b))

    def live_chunks(diag):
        return range(0 if diag is None else diag * blk_chunks, n_chunk)

    def scores(kb, buf, diag=None):
        s_ref, mx_ref = buf
        off = pl.multiple_of(kb * tk, tk)
        k = jnp.concatenate([kn_ref[pl.ds(off, tk), :], kr_ref[pl.ds(off, tk), :]], axis=-1)
        for c in live_chunks(diag):
            s = lax.dot_general(k, q[c * MXU_N:(c + 1) * MXU_N, :], (((1,), (1,)), ((), ())),
                                preferred_element_type=F32)
            if diag is not None and c < (diag + 1) * blk_chunks:
                key = lax.broadcasted_iota(jnp.int32, s.shape, 0) + diag * tk
                qry = lax.broadcasted_iota(jnp.int32, s.shape, 1) + c * MXU_N
                s = jnp.where(qry >= key, s, -jnp.inf)
            s_ref[c] = s
            mx_ref[:, c * MXU_N:(c + 1) * MXU_N] = jnp.max(s, axis=0, keepdims=True)

    def softmax_pv(kb, buf, diag=None, mask_here=False):
        s_ref, mx_ref = buf
        vt = vt_ref[0, 0, kb]
        for c in live_chunks(diag):
            sl = slice(c * MXU_N, (c + 1) * MXU_N)
            remask = mask_here and c < blk_chunks

            def s_chunk():
                s = s_ref[c]
                if remask:
                    key = lax.broadcasted_iota(jnp.int32, s.shape, 0)
                    qry = lax.broadcasted_iota(jnp.int32, s.shape, 1) + c * MXU_N
                    s = jnp.where(qry >= key, s, -jnp.inf)
                return s

            mx = jnp.max(s_chunk(), axis=0, keepdims=True) if remask else mx_ref[:, sl]
            m_old = m_sc[:, sl]
            m_new = jnp.maximum(m_old, mx)
            alpha = jnp.exp2(m_old - m_new)
            p = jnp.exp2(s_chunk() - m_new).astype(BF16)
            acc_sc[:, sl] = alpha * acc_sc[:, sl] + jnp.dot(vt, p, preferred_element_type=F32)
            m_sc[:, sl] = m_new

    n_full = n_diag * qi
    scores(0, bufs[0])

    def body(j, carry):
        for t in range(n_diag):
            kb = n_diag * j + t
            scores(kb + 1, bufs[(t + 1) % 2])
            softmax_pv(kb, bufs[t % 2])
        return carry

    lax.fori_loop(0, qi, body, 0)

    for w_ref, w16_ref in zip(w_refs, w16_refs):
        w16_ref[...] = w_ref[...].astype(BF16)

    for d in range(n_diag):
        if d + 1 < n_diag:
            scores(n_full + d + 1, bufs[(d + 1) % 2], diag=d + 1)
        softmax_pv(n_full + d, bufs[d % 2], diag=d, mask_here=(d == 0))

    o_ref[...] = (acc_sc[:MLA_V, :] / acc_sc[MLA_V:MLA_V + 1, :]).T.astype(o_ref.dtype)


def mla_attention(qn, qr, kn, kr, vt, weights, batch, seq, tq=4 * MLA_TILE, tk=MLA_TILE):
    assert tq % (2 * tk) == 0
    nq = seq // tq
    steps = batch * MLA_HEADS * nq

    def slab(w):
        tiles, rem = divmod(w.shape[0], BF16_ROWS)
        assert rem == 0, w.shape
        n_slab = max(n for n in range(1, steps + 1) if tiles % n == 0)
        rows = w.shape[0] // n_slab
        return pl.BlockSpec((rows, w.shape[1]),
                            lambda b, h, i: (jnp.minimum((b * MLA_HEADS + h) * nq + i, n_slab - 1), 0))

    slabs = [slab(w) for w in weights]
    outs = pl.pallas_call(
        functools.partial(_mla_kernel, tq=tq, tk=tk, n_cast=len(weights)),
        grid=(batch, MLA_HEADS, nq),
        in_specs=[pl.BlockSpec((tq, MLA_NOPE), lambda b, h, i: (b * nq + i, h)),
                  pl.BlockSpec((tq, LANES), lambda b, h, i: (b * nq + i, h // 2)),
                  pl.BlockSpec((seq, MLA_NOPE), lambda b, h, i: (b, h)),
                  pl.BlockSpec((seq, LANES), lambda b, h, i: (b, h % 2)),
                  pl.BlockSpec((1, 1, seq // tk, VT_ROWS, tk), lambda b, h, i: (b, h, 0, 0, 0))] + slabs,
        out_specs=[pl.BlockSpec((tq, MLA_V), lambda b, h, i: (b * nq + i, h))] + slabs,
        out_shape=[jax.ShapeDtypeStruct((batch * seq, MLA_WIDTH), BF16)]
        + [jax.ShapeDtypeStruct(w.shape, BF16) for w in weights],
        scratch_shapes=[pltpu.VMEM((tq // MXU_N, tk, MXU_N), F32), pltpu.VMEM((tq // MXU_N, tk, MXU_N), F32),
                        pltpu.VMEM((1, tq), F32), pltpu.VMEM((1, tq), F32),
                        pltpu.VMEM((1, tq), F32), pltpu.VMEM((VT_ROWS, tq), F32)],
        compiler_params=_cparams(("arbitrary", "arbitrary", "arbitrary")),
        name="mla_attention",
    )(qn, qr, kn, kr, vt, *weights)
    return outs[0], outs[1:]


def _swa_kernel(sinks_ref, q0_ref, q1_ref, q2_ref, q3_ref, kc_ref, kp_ref, vc_ref, vp_ref,
                pq_ref, pkc_ref, pkp_ref, o_ref):
    blk = pl.program_id(1)
    q_refs = (q0_ref, q1_ref, q2_ref, q3_ref)
    k_all = jnp.concatenate([kp_ref[...], kc_ref[...]], axis=0).astype(F32)
    v_all = jnp.concatenate([vp_ref[...], vc_ref[...]], axis=0).astype(F32)
    pk = jnp.concatenate([pkp_ref[0], pkc_ref[0]], axis=-1).astype(F32)
    dist = jnp.abs(pq_ref[...].astype(F32) - pk)
    row = lax.broadcasted_iota(jnp.int32, dist.shape, 0)
    col = lax.broadcasted_iota(jnp.int32, dist.shape, 1)
    rel = BLOCK + row - col
    first_key = jnp.where(blk > 0, 0, BLOCK)
    valid = (rel >= 0) & (rel < WINDOW) & (col >= first_key)
    dist_w = jnp.where(valid, dist, jnp.inf)
    lane =lax.broadcasted_iota(jnp.int32, (2 * BLOCK, LANES), 1)
    lo = lane < SWA_HEAD_DIM
    lane_o = lax.broadcasted_iota(jnp.int32, (BLOCK, LANES), 1)
    lo_o = lane_o < SWA_HEAD_DIM

    def dup(x_all, g):
        pair = x_all[:, (g // 2) * LANES:(g // 2 + 1) * LANES]
        rolled = pltpu.roll(pair, SWA_HEAD_DIM, 1)
        return jnp.where(lo, pair, rolled) if g % 2 == 0 else jnp.where(lo, rolled, pair)

    for g in range(SWA_KV_HEADS):
        kk = dup(k_all, g)
        vv = dup(v_all, g).astype(BF16)
        zero = jnp.zeros_like(kk)
        k_half = (jnp.where(lo, kk, zero).astype(BF16), jnp.where(lo, zero, kk).astype(BF16))
        for i in range(SWA_GROUP // 2):
            q_pair = q_refs[g][:, i * LANES:(i + 1) * LANES]
            outs = []
            for j in range(2):
                head = g * SWA_GROUP + 2 * i + j
                slope2 = LOG2E * 2.0 ** (-8.0 * (head + 1) / SWA_HEADS)
                s = lax.dot_general(q_pair, k_half[j], (((1,), (1,)), ((), ())),
                                    preferred_element_type=F32)
                s = s - slope2 * dist_w
                sink2 = LOG2E * sinks_ref[head]
                m = jnp.maximum(jnp.max(s, axis=-1, keepdims=True), sink2)
                e = jnp.exp2(s - m)
                denom = jnp.sum(e, axis=-1, keepdims=True) + jnp.exp2(sink2 - m)
                pv = jnp.dot(e.astype(BF16), vv, preferred_element_type=F32)
                outs.append(pv / denom)
            col0 = (g * SWA_GROUP // 2 + i) * LANES
            o_ref[:, col0:col0 + LANES] = jnp.where(lo_o, outs[0], outs[1]).astype(o_ref.dtype)


def swa_attention(proj, sinks, pos_col, pos_row, batch, seq):
    nblk = seq // BLOCK
    kvw = SWA_KV_HEADS * SWA_HEAD_DIM
    gw = SWA_GROUP * SWA_HEAD_DIM
    cur = lambda b, n: b * nblk + n
    prev = lambda b, n: b * nblk + jnp.maximum(n - 1, 0)
    q_specs = [pl.BlockSpec((BLOCK, gw), functools.partial(lambda b, n, g: (cur(b, n), COL_QSWA // gw + g), g=g))
               for g in range(SWA_KV_HEADS)]
    return pl.pallas_call(
        _swa_kernel,
        grid=(batch, nblk),
        in_specs=[pl.BlockSpec(memory_space=pltpu.SMEM)] + q_specs + [
            pl.BlockSpec((BLOCK, kvw), lambda b, n: (cur(b, n), COL_KSWA // kvw)),
            pl.BlockSpec((BLOCK, kvw), lambda b, n: (prev(b, n), COL_KSWA // kvw)),
            pl.BlockSpec((BLOCK, kvw), lambda b, n: (cur(b, n), COL_VSWA // kvw)),
            pl.BlockSpec((BLOCK, kvw), lambda b, n: (prev(b, n), COL_VSWA // kvw)),
            pl.BlockSpec((BLOCK, 1), lambda b, n: (cur(b, n), 0)),
            pl.BlockSpec((1, 1, BLOCK), lambda b, n: (cur(b, n), 0, 0)),
            pl.BlockSpec((1, 1, BLOCK), lambda b, n: (prev(b, n), 0, 0))],
        out_specs=pl.BlockSpec((BLOCK, SWA_WIDTH), lambda b, n: (cur(b, n), 0)),
        out_shape=jax.ShapeDtypeStruct((batch * seq, SWA_WIDTH), BF16),
        compiler_params=_cparams(("parallel", "parallel")),
        name="swa_attention",
    )(sinks, proj, proj, proj, proj, proj, proj, proj, proj, pos_col, pos_row, pos_row)


def _wo_kernel(oa_ref, ob_ref, ga_ref, gb_ref, w_ref, x_ref, gy_ref, gn_ref, x1_ref, h_ref, *, chunks):
    ka = oa_ref.shape[1]
    na = _rms(oa_ref[...].astype(F32), ga_ref[...]).astype(BF16)
    nb = _rms(ob_ref[...].astype(F32), gb_ref[...]).astype(BF16)
    for lo, hi in chunks:
        x1_ref[:, lo:hi] = (jnp.dot(na, w_ref[:ka, lo:hi], preferred_element_type=F32)
                            + jnp.dot(nb, w_ref[ka:, lo:hi], preferred_element_type=F32))
    rows = 32
    for r in range(x1_ref.shape[0] // rows):
        rs = slice(r * rows, (r + 1) * rows)
        x1 = x_ref[rs, :] + _rms(x1_ref[rs, :], gy_ref[...])
        x1_ref[rs, :] = x1
        h_ref[rs, :] = _rms(x1, gn_ref[...]).astype(h_ref.dtype)


def out_proj_resid(oa, ob, ga, gb, w, x, gy, gn, tm=ROW_TILE):
    m, ka = oa.shape
    kb = ob.shape[1]
    n = w.shape[1]
    row = pl.BlockSpec((tm, n), lambda i: (i, 0))
    vec = pl.BlockSpec((1, n), lambda i: (0, 0))
    return pl.pallas_call(
        functools.partial(_wo_kernel, chunks=_col_chunks(n, 4 * MXU_N)),
        grid=(m // tm,),
        in_specs=[pl.BlockSpec((tm, ka), lambda i: (i, 0)),
                  pl.BlockSpec((tm, kb), lambda i: (i, 0)),
                  pl.BlockSpec((1, ka), lambda i: (0, 0)),
                  pl.BlockSpec((1, kb), lambda i: (0, 0)),
                  pl.BlockSpec((ka + kb, n), lambda i: (0, 0), pipeline_mode=pl.Buffered(1)),
                  row, vec, vec],
        out_specs=[row, row],
        out_shape=[jax.ShapeDtypeStruct((m, n), F32), jax.ShapeDtypeStruct((m, n), BF16)],
        compiler_params=_cparams(("parallel",)),
        name="out_proj",
    )(oa, ob, ga, gb, w, x, gy, gn)


def _ffn_kernel(h_ref, wg_ref, wu_ref, wd_ref, o_ref, acc_sc, *, dn, rows):
    j = pl.program_id(1)
    tm, d = acc_sc.shape

    def hidden_tile(first):
        h = h_ref[...]
        g = jnp.dot(h, wg_ref[...], preferred_element_type=F32)
        u = jnp.dot(h, wu_ref[...], preferred_element_type=F32)
        a = (g * jax.nn.sigmoid(g) * u).astype(BF16)
        for c in range(d // dn):
            cols = slice(c * dn, (c + 1) * dn)
            y = jnp.dot(a, wd_ref[:, cols], preferred_element_type=F32)
            acc_sc[:, cols] = y if first else acc_sc[:, cols] + y

    @pl.when(j == 0)
    def _():
        hidden_tile(True)

    @pl.when(j > 0)
    def _():
        hidden_tile(False)

    @pl.when(j == pl.num_programs(1) - 1)
    def _():
        def body(r, carry):
            rs = pl.ds(pl.multiple_of(r * rows, rows), rows)
            o_ref[rs, :] = acc_sc[rs, :].astype(o_ref.dtype)
            return carry

        lax.fori_loop(0, tm // rows, body, 0)


def ffn(h, wg, wu, wd, tm=1024, tf=FFN_TILE):
    m, d = h.shape
    f = wg.shape[1]
    return pl.pallas_call(
        functools.partial(_ffn_kernel, dn=1024, rows=64),
        grid=(m // tm, f // tf),
        in_specs=[pl.BlockSpec((tm, d), lambda i, j: (i, 0)),
                  pl.BlockSpec((d, tf), lambda i, j: (0, j)),
                  pl.BlockSpec((d, tf), lambda i, j: (0, j)),
                  pl.BlockSpec((tf, d), lambda i, j: (j, 0))],
        out_specs=pl.BlockSpec((tm, d), lambda i, j: (i, 0), pipeline_mode=pl.Buffered(1)),
        out_shape=jax.ShapeDtypeStruct((m, d), BF16),
        scratch_shapes=[pltpu.VMEM((tm, d), F32)],
        compiler_params=_cparams(("parallel", "arbitrary")),
        name="ffn",
    )(h, wg, wu, wd)


def _ple_kernel(x_ref, f_ref, gf_ref, wg_ref, p_ref, wp_ref, o_ref, *, chunks):
    o_ref[...] = x_ref[...] + _rms(f_ref[...].astype(F32), gf_ref[...])
    xb = o_ref[...].astype(BF16)
    pb = p_ref[...].astype(BF16)
    for lo, hi in chunks:
        gate = jax.nn.sigmoid(jnp.dot(xb, wg_ref[:, lo:hi], preferred_element_type=F32))
        e = jnp.dot(pb, wp_ref[:, lo:hi], preferred_element_type=F32)
        o_ref[:, lo:hi] += gate * e


def ple(x, f, g_f, wg, p, wp, tm=ROW_TILE):
    m, d = x.shape
    n = wg.shape[1]
    pd = p.shape[1]
    once = dict(pipeline_mode=pl.Buffered(1))
    return pl.pallas_call(
        functools.partial(_ple_kernel, chunks=_col_chunks(n, 4 * MXU_N)),
        grid=(m // tm,),
        in_specs=[pl.BlockSpec((tm, d), lambda i: (i, 0)),
                  pl.BlockSpec((tm, d), lambda i: (i, 0)),
                  pl.BlockSpec((1, d), lambda i: (0, 0)),
                  pl.BlockSpec((d, n), lambda i: (0, 0), **once),
                  pl.BlockSpec((tm, pd), lambda i: (i, 0)),
                  pl.BlockSpec((pd, n), lambda i: (0, 0), **once)],
        out_specs=pl.BlockSpec((tm, n), lambda i: (i, 0)),
        out_shape=jax.ShapeDtypeStruct((m, n), F32),
        compiler_params=_cparams(("parallel",)),
        name="ple",
    )(x, f, g_f, wg, p, wp)


_KR_BLOCK = (Q_LORA + KV_LORA) // MLA_ROPE
_QSWA_BLOCKS = (COL_QSWA // LANES, COL_KSWA // LANES)


def _w_in_prep_kernel(wa_ref, wb_ref, o_ref):
    ob = pl.program_id(0)
    a, b = wa_ref[0], wb_ref[0]
    half = MLA_ROPE // 2
    partner = jnp.concatenate([-b[half:], b[:half]], axis=0)
    b = jnp.where(ob == pl.num_programs(0) - 1, partner, b)
    scale = jnp.where((ob >= _QSWA_BLOCKS[0]) & (ob < _QSWA_BLOCKS[1]), LOG2E / math.sqrt(SWA_HEAD_DIM), 1.0)
    blk = jnp.concatenate([a, b], axis=0) * scale
    o_ref[...] = blk.T.astype(BF16)


def _prep_w_in(w3, layer):
    wt = jnp.swapaxes(w3, 1, 2)
    _, n, k = wt.shape
    n_out = D_IN2 // LANES
    shifted = COL_QSWA // LANES

    def src(ob, part):
        blk = 2 * ob + part + jnp.where(ob >= shifted, 1, 0)
        return jnp.where(ob == n_out - 1, _KR_BLOCK, blk)

    return pl.pallas_call(
        _w_in_prep_kernel,
        grid=(n_out,),
        in_specs=[pl.BlockSpec((1, MLA_ROPE, k), lambda ob: (layer, src(ob, 0), 0)),
                  pl.BlockSpec((1, MLA_ROPE, k), lambda ob: (layer, src(ob, 1), 0))],
        out_specs=pl.BlockSpec((k, LANES), lambda ob: (0, ob)),
        out_shape=jax.ShapeDtypeStruct((k, D_IN2), BF16),
        compiler_params=_cparams(("parallel",)),
        name="w_in_prep",
    )(wt, wt)


def _prep_w_q_up(w):
    w = (w * (LOG2E / math.sqrt(MLA_NOPE + MLA_ROPE))).reshape(Q_LORA, MLA_HEADS, MLA_NOPE + MLA_ROPE)
    nope = w[:, :, :MLA_NOPE].reshape(Q_LORA, MLA_WIDTH)
    rope = w[:, :, MLA_NOPE:].reshape(Q_LORA, MLA_HEADS * MLA_ROPE)
    return jnp.concatenate([rope, nope], axis=1).astype(BF16)


def _prep_w_kv_up(w):
    w = w.reshape(KV_LORA, MLA_HEADS, MLA_NOPE + MLA_V)
    wk = w[:, :, :MLA_NOPE].reshape(KV_LORA, MLA_WIDTH).astype(BF16)
    wvt = w[:, :, MLA_NOPE:].reshape(KV_LORA, MLA_WIDTH).T.astype(BF16)
    return wk, wvt


def _layer(x, p, pos_col, pos_row, invf, batch, seq, attn_pre_norm, w_in, q_a_norm, w_q_up,
           kv_a_norm, w_kv_up, sinks, mla_out_norm, swa_out_norm, w_o, attn_post_norm,
           ffn_pre_norm, w_gate, w_up, w_down, ffn_post_norm, w_ple_gate, w_ple_proj):
    row = lambda g: g.reshape(1, -1)
    proj = in_proj(x, row(attn_pre_norm), _prep_w_in(*w_in))
    qn, qr = q_up(proj, row(q_a_norm), _prep_w_q_up(w_q_up), pos_col, invf)
    wk, wvt = _prep_w_kv_up(w_kv_up)
    kn, vt, kr = kv_up(proj, row(kv_a_norm), wk, wvt, pos_col, invf, batch, seq)
    o_mla, (wo16, wg16, wu16, wd16, wpg16) = mla_attention(
        qn, qr, kn, kr, vt, [w_o, w_gate, w_up, w_down, w_ple_gate], batch, seq)
    o_swa = swa_attention(proj, sinks, pos_col, pos_row, batch, seq)
    x1, h2 = out_proj_resid(o_mla, o_swa, row(mla_out_norm), row(swa_out_norm), wo16,
                            x, row(attn_post_norm), row(ffn_pre_norm))
    f = ffn(h2, wg16, wu16, wd16)
    return ple(x1, f, row(ffn_post_norm), wpg16, p, w_ple_proj.astype(BF16))


def kernel(x, p, positions, attn_pre_norm, w_in, q_a_norm, w_q_up, kv_a_norm, w_kv_up, sinks,
           mla_out_norm, swa_out_norm, w_o, attn_post_norm, ffn_pre_norm, w_gate, w_up, w_down,
           ffn_post_norm, w_ple_gate, w_ple_proj):
    batch, seq, d = x.shape
    depth = w_in.shape[0]
    t = batch * seq
    pos_col = positions.reshape(t, 1)
    pos_row = positions.reshape(t // BLOCK, 1, BLOCK)
    half = MLA_ROPE // 2
    invf = np.asarray(ROPE_THETA ** (-(np.arange(LANES) % half) * 2.0 / MLA_ROPE), np.float32).reshape(1, LANES)
    invf = jnp.asarray(invf)
    xf = x.reshape(t, d)
    for i in range(depth):
        xf = _layer(xf, p[i].reshape(t, PLE_DIM), pos_col, pos_row, invf, batch, seq,
                    attn_pre_norm[i], (w_in, i), q_a_norm[i], w_q_up[i], kv_a_norm[i], w_kv_up[i],
                    sinks[i], mla_out_norm[i], swa_out_norm[i], w_o[i], attn_post_norm[i],
                    ffn_pre_norm[i], w_gate[i], w_up[i], w_down[i], ffn_post_norm[i],
                    w_ple_gate[i], w_ple_proj[i])
    return xf.reshape(batch, seq, d)
```

```python
import functools
import math

import numpy as np
import jax
import jax.numpy as jnp
from jax import lax
from jax.experimental import pallas as pl
from jax.experimental.pallas import tpu as pltpu

PLE_DIM = 256
MLA_HEADS = 16
MLA_NOPE = 128
MLA_ROPE = 64
MLA_V = 128
Q_LORA = 1024
KV_LORA = 512
MLA_WIDTH = MLA_HEADS * MLA_V
SWA_HEADS = 32
SWA_KV_HEADS = 4
SWA_HEAD_DIM = 64
SWA_GROUP = SWA_HEADS // SWA_KV_HEADS
SWA_WIDTH = SWA_HEADS * SWA_HEAD_DIM
WINDOW = 128
BLOCK = 128
ROPE_THETA = 10000.0
NORM_EPS = 1e-6
LOG2E = math.log2(math.e)

LANES = 128
MXU_N = 256
BF16_ROWS = 16
MLA_TILE = 512
ROW_TILE = 256
FFN_TILE = MXU_N
VT_ROWS = MLA_V + BF16_ROWS
VMEM_LIMIT = 60 * 1024 * 1024

COL_CQ = 0
COL_CKV = COL_CQ + Q_LORA
COL_QSWA = COL_CKV + KV_LORA
COL_KSWA = COL_QSWA + SWA_WIDTH
COL_VSWA = COL_KSWA + SWA_KV_HEADS * SWA_HEAD_DIM
COL_KR = COL_VSWA + SWA_KV_HEADS * SWA_HEAD_DIM
D_IN2 = COL_KR + 2 * MLA_ROPE

F32 = jnp.float32
BF16 = jnp.bfloat16


def _cparams(sem):
    return pltpu.CompilerParams(dimension_semantics=sem, vmem_limit_bytes=VMEM_LIMIT)


def _rms(xf, g):
    ms = jnp.mean(xf * xf, axis=-1, keepdims=True)
    return xf * lax.rsqrt(ms + NORM_EPS) * g


def _rope_tables(pos_col, invf_row):
    ang = pos_col.astype(F32) * invf_row
    return jnp.cos(ang), jnp.sin(ang)


def _col_chunks(n, width):
    return [(c, min(c + width, n)) for c in range(0, n, width)]


def _in_proj_kernel(x_ref, g_ref, w_ref, o_ref, *, chunks):
    h = _rms(x_ref[...], g_ref[...]).astype(BF16)
    for lo, hi in chunks:
        o_ref[:, lo:hi] = jnp.dot(h, w_ref[:, lo:hi], preferred_element_type=F32).astype(o_ref.dtype)


def in_proj(x, g, w, tm=ROW_TILE):
    m, k = x.shape
    n = w.shape[1]
    return pl.pallas_call(
        functools.partial(_in_proj_kernel, chunks=_col_chunks(n, 6 * MXU_N)),
        grid=(m // tm,),
        in_specs=[pl.BlockSpec((tm, k), lambda i: (i, 0)),
                  pl.BlockSpec((1, k), lambda i: (0, 0)),
                  pl.BlockSpec((k, n), lambda i: (0, 0), pipeline_mode=pl.Buffered(1))],
        out_specs=pl.BlockSpec((tm, n), lambda i: (i, 0)),
        out_shape=jax.ShapeDtypeStruct((m, n), BF16),
        compiler_params=_cparams(("parallel",)),
        name="in_proj",
    )(x, g, w)


def _q_up_kernel(cq_ref, g_ref, w_ref, pos_ref, invf_ref, qn_ref, qr_ref):
    h = _rms(cq_ref[...].astype(F32), g_ref[...]).astype(BF16)
    n_rope = MLA_HEADS * MLA_ROPE
    rot = jnp.dot(h, w_ref[:, :n_rope], preferred_element_type=F32)
    qn_ref[...] = jnp.dot(h, w_ref[:, n_rope:], preferred_element_type=F32).astype(BF16)
    cos, sin = _rope_tables(pos_ref[...], invf_ref[...])
    half = MLA_ROPE // 2
    lane = lax.broadcasted_iota(jnp.int32, cos.shape, 1)
    first_half = (lane % MLA_ROPE) < half
    for c in range(n_rope // LANES):
        a = rot[:, c * LANES:(c + 1) * LANES]
        b = jnp.where(first_half, -pltpu.roll(a, LANES - half, 1), pltpu.roll(a, half, 1))
        qr_ref[:, c * LANES:(c + 1) * LANES] = (a * cos + b * sin).astype(BF16)


def q_up(proj, g, w, pos_col, invf, tm=512):
    m = proj.shape[0]
    n_rope = MLA_HEADS * MLA_ROPE
    return pl.pallas_call(
        _q_up_kernel,
        grid=(m // tm,),
        in_specs=[pl.BlockSpec((tm, Q_LORA), lambda i: (i, COL_CQ // Q_LORA)),
                  pl.BlockSpec((1, Q_LORA), lambda i: (0, 0)),
                  pl.BlockSpec(w.shape, lambda i: (0, 0)),
                  pl.BlockSpec((tm, 1), lambda i: (i, 0)),
                  pl.BlockSpec((1, LANES), lambda i: (0, 0))],
        out_specs=[pl.BlockSpec((tm, MLA_WIDTH), lambda i: (i, 0)),
                   pl.BlockSpec((tm, n_rope), lambda i: (i, 0))],
        out_shape=[jax.ShapeDtypeStruct((m, MLA_WIDTH), BF16),
                   jax.ShapeDtypeStruct((m, n_rope), BF16)],
        compiler_params=_cparams(("parallel",)),
        name="q_up",
    )(proj, g, w, pos_col, invf)


def _kv_up_kernel(ckv_ref, g_ref, wk_ref, wvt_ref, krab_ref, pos_ref, invf_ref, kn_ref, vt_ref, kr_ref):
    h = _rms(ckv_ref[...].astype(F32), g_ref[...]).astype(BF16)
    kn_ref[...] = jnp.dot(h, wk_ref[...], preferred_element_type=F32).astype(BF16)
    vt = lax.dot_general(wvt_ref[...], h, (((1,), (1,)), ((), ())), preferred_element_type=F32)
    ones = jnp.ones((VT_ROWS - MLA_V, vt.shape[1]), BF16)
    for hh in range(MLA_HEADS):
        vt_ref[0, hh, 0, :MLA_V, :] = vt[hh * MLA_V:(hh + 1) * MLA_V, :].astype(BF16)
        vt_ref[0, hh, 0, MLA_V:, :] = ones
    cos, sin = _rope_tables(pos_ref[...], invf_ref[...])
    lane = lax.broadcasted_iota(jnp.int32, cos.shape, 1)
    lo = lane < MLA_ROPE
    prod = krab_ref[...].astype(F32) * jnp.where(lo, cos, sin)
    kk = prod + pltpu.roll(prod, MLA_ROPE, 1)
    zero = jnp.zeros_like(kk)
    kr_ref[:, :LANES] = jnp.where(lo, kk, zero).astype(BF16)
    kr_ref[:, LANES:] = jnp.where(lo, zero, kk).astype(BF16)


def kv_up(proj, g, wk, wvt, pos_col, invf, batch, seq, tm=MLA_TILE):
    m = proj.shape[0]
    nk = seq // tm
    return pl.pallas_call(
        _kv_up_kernel,
        grid=(m // tm,),
        in_specs=[pl.BlockSpec((tm, KV_LORA), lambda i: (i, COL_CKV // KV_LORA)),
                  pl.BlockSpec((1, KV_LORA), lambda i: (0, 0)),
                  pl.BlockSpec(wk.shape, lambda i: (0, 0)),
                  pl.BlockSpec(wvt.shape, lambda i: (0, 0)),
                  pl.BlockSpec((tm, LANES), lambda i: (i, COL_KR // LANES)),
                  pl.BlockSpec((tm, 1), lambda i: (i, 0)),
                  pl.BlockSpec((1, LANES), lambda i: (0, 0))],
        out_specs=[pl.BlockSpec((tm, MLA_WIDTH), lambda i: (i, 0)),
                   pl.BlockSpec((1, MLA_HEADS, 1, VT_ROWS, tm), lambda i: (i // nk, 0, i % nk, 0, 0)),
                   pl.BlockSpec((tm, 2 * LANES), lambda i: (i, 0))],
        out_shape=[jax.ShapeDtypeStruct((m, MLA_WIDTH), BF16),
                   jax.ShapeDtypeStruct((batch, MLA_HEADS, nk, VT_ROWS, tm), BF16),
                   jax.ShapeDtypeStruct((m, 2 * LANES), BF16)],
        compiler_params=_cparams(("parallel",)),
        name="kv_up",
    )(proj, g, wk, wvt, proj, pos_col, invf)


def _mla_kernel(*refs, tq, tk, n_cast):
    qn_ref, qr_ref, kn_ref, kr_ref, vt_ref = refs[:5]
    w_refs = refs[5:5 + n_cast]
    o_ref = refs[5 + n_cast]
    w16_refs = refs[6 + n_cast:6 + 2 * n_cast]
    s_a, s_b, mx_a, mx_b, m_sc, acc_sc = refs[6 + 2 * n_cast:]

    qi = pl.program_id(2)
    q = jnp.concatenate([qn_ref[...], qr_ref[...]], axis=-1)
    m_sc[...] = jnp.full(m_sc.shape, -jnp.inf, F32)
    acc_sc[...] = jnp.zeros(acc_sc.shape, F32)

    n_chunk = tq // MXU_N
    blk_chunks = tk // MXU_N
    n_diag = tq // tk
    bufs = ((s_a, mx_a), (s_b, mx_b))

    def live_chunks(diag):
        return range(0 if diag is None else diag * blk_chunks, n_chunk)

    def scores(kb, buf, diag=None):
        s_ref, mx_ref = buf
        off = pl.multiple_of(kb * tk, tk)
        k = jnp.concatenate([kn_ref[pl.ds(off, tk), :], kr_ref[pl.ds(off, tk), :]], axis=-1)
        for c in live_chunks(diag):
            s = lax.dot_general(k, q[c * MXU_N:(c + 1) * MXU_N, :], (((1,), (1,)), ((), ())),
                                preferred_element_type=F32)
            if diag is not None and c < (diag + 1) * blk_chunks:
                key = lax.broadcasted_iota(jnp.int32, s.shape, 0) + diag * tk
                qry = lax.broadcasted_iota(jnp.int32, s.shape, 1) + c * MXU_N
                s = jnp.where(qry >= key, s, -jnp.inf)
            s_ref[c] = s
            mx_ref[:, c * MXU_N:(c + 1) * MXU_N] = jnp.max(s, axis=0, keepdims=True)

    def softmax_pv(kb, buf, diag=None, mask_here=False):
        s_ref, mx_ref = buf
        vt = vt_ref[0, 0, kb]
        for c in live_chunks(diag):
            sl = slice(c * MXU_N, (c + 1) * MXU_N)
            remask = mask_here and c < blk_chunks

            def s_chunk():
                s = s_ref[c]
                if remask:
                    key = lax.broadcasted_iota(jnp.int32, s.shape, 0)
                    qry = lax.broadcasted_iota(jnp.int32, s.shape, 1) + c * MXU_N
                    s = jnp.where(qry >= key, s, -jnp.inf)
                return s

            mx = jnp.max(s_chunk(), axis=0, keepdims=True) if remask else mx_ref[:, sl]
            m_old = m_sc[:, sl]
            m_new = jnp.maximum(m_old, mx)
            alpha = jnp.exp2(m_old - m_new)
            p = jnp.exp2(s_chunk() - m_new).astype(BF16)
            acc_sc[:, sl] = alpha * acc_sc[:, sl] + jnp.dot(vt, p, preferred_element_type=F32)
            m_sc[:, sl] = m_new

    n_full = n_diag * qi
    scores(0, bufs[0])

    def body(j, carry):
        for t in range(n_diag):
            kb = n_diag * j + t
            scores(kb + 1, bufs[(t + 1) % 2])
            softmax_pv(kb, bufs[t % 2])
        return carry

    lax.fori_loop(0, qi, body, 0)

    for w_ref, w16_ref in zip(w_refs, w16_refs):
        w16_ref[...] = w_ref[...].astype(BF16)

    for d in range(n_diag):
        if d + 1 < n_diag:
            scores(n_full + d + 1, bufs[(d + 1) % 2], diag=d + 1)
        softmax_pv(n_full + d, bufs[d % 2], diag=d, mask_here=(d == 0))

    o_ref[...] = (acc_sc[:MLA_V, :] / acc_sc[MLA_V:MLA_V + 1, :]).T.astype(o_ref.dtype)


def mla_attention(qn, qr, kn, kr, vt, weights, batch, seq, tq=4 * MLA_TILE, tk=MLA_TILE):
    assert tq % (2 * tk) == 0
    nq = seq // tq
    steps = batch * MLA_HEADS * nq

    def slab(w):
        tiles, rem = divmod(w.shape[0], BF16_ROWS)
        assert rem == 0, w.shape
        n_slab = max(n for n in range(1, steps + 1) if tiles % n == 0)
        rows = w.shape[0] // n_slab
        return pl.BlockSpec((rows, w.shape[1]),
                            lambda b, h, i: (jnp.minimum((b * MLA_HEADS + h) * nq + i, n_slab - 1), 0))

    slabs = [slab(w) for w in weights]
    outs = pl.pallas_call(
        functools.partial(_mla_kernel, tq=tq, tk=tk, n_cast=len(weights)),
        grid=(batch, MLA_HEADS, nq),
        in_specs=[pl.BlockSpec((tq, MLA_NOPE), lambda b, h, i: (b * nq + i, h)),
                  pl.BlockSpec((tq, LANES), lambda b, h, i: (b * nq + i, h // 2)),
                  pl.BlockSpec((seq, MLA_NOPE), lambda b, h, i: (b, h)),
                  pl.BlockSpec((seq, LANES), lambda b, h, i: (b, h % 2)),
                  pl.BlockSpec((1, 1, seq // tk, VT_ROWS, tk), lambda b, h, i: (b, h, 0, 0, 0))] + slabs,
        out_specs=[pl.BlockSpec((tq, MLA_V), lambda b, h, i: (b * nq + i, h))] + slabs,
        out_shape=[jax.ShapeDtypeStruct((batch * seq, MLA_WIDTH), BF16)]
        + [jax.ShapeDtypeStruct(w.shape, BF16) for w in weights],
        scratch_shapes=[pltpu.VMEM((tq // MXU_N, tk, MXU_N), F32), pltpu.VMEM((tq // MXU_N, tk, MXU_N), F32),
                        pltpu.VMEM((1, tq), F32), pltpu.VMEM((1, tq), F32),
                        pltpu.VMEM((1, tq), F32), pltpu.VMEM((VT_ROWS, tq), F32)],
        compiler_params=_cparams(("arbitrary", "arbitrary", "arbitrary")),
        name="mla_attention",
    )(qn, qr, kn, kr, vt, *weights)
    return outs[0], outs[1:]


def _swa_kernel(sinks_ref, q0_ref, q1_ref, q2_ref, q3_ref, kc_ref, kp_ref, vc_ref, vp_ref,
                pkc_ref, pkp_ref, pq_ref, o_ref, ot_sc):
    blk = pl.program_id(1)
    q_refs = (q0_ref, q1_ref, q2_ref, q3_ref)
    k_all = jnp.concatenate([kp_ref[...], kc_ref[...]], axis=0).astype(F32)
    v_all = jnp.concatenate([vp_ref[...], vc_ref[...]], axis=0).astype(F32)
    pk = jnp.concatenate([pkp_ref[...], pkc_ref[...]], axis=0).astype(F32)
    dist = jnp.abs(pk - pq_ref[0].astype(F32))
    key = lax.broadcasted_iota(jnp.int32, dist.shape, 0)
    qry = lax.broadcasted_iota(jnp.int32, dist.shape, 1)
    rel = BLOCK + qry - key
    first_key = jnp.where(blk > 0, 0, BLOCK)
    valid = (rel >= 0) & (rel < WINDOW) & (key >= first_key)
    dist_w = jnp.where(valid, dist, jnp.inf)
    lane = lax.broadcasted_iota(jnp.int32, (2 * BLOCK, LANES), 1)
    lo = lane < SWA_HEAD_DIM

    def dup(x_all, g):
        pair = x_all[:, (g // 2) * LANES:(g // 2 + 1) * LANES]
        rolled = pltpu.roll(pair, SWA_HEAD_DIM, 1)
        return jnp.where(lo, pair, rolled) if g % 2 == 0 else jnp.where(lo, rolled, pair)

    for g in range(SWA_KV_HEADS):
        kk = dup(k_all, g)
        zero = jnp.zeros_like(kk)
        k_half = (jnp.where(lo, kk, zero).astype(BF16), jnp.where(lo, zero, kk).astype(BF16))
        vt = jnp.where(lo, dup(v_all, g), 1.0).T.astype(BF16)
        for i in range(SWA_GROUP // 2):
            q_pair = q_refs[g][:, i * LANES:(i + 1) * LANES]
            for j in range(2):
                head = g * SWA_GROUP + 2 * i + j
                slope2 = LOG2E * 2.0 ** (-8.0 * (head + 1) / SWA_HEADS)
                s = lax.dot_general(k_half[j], q_pair, (((1,), (1,)), ((), ())),
                                    preferred_element_type=F32)
                s = s - slope2 * dist_w
                sink2 = LOG2E * sinks_ref[head]
                m = jnp.maximum(jnp.max(s, axis=0, keepdims=True), sink2)
                e = jnp.exp2(s - m).astype(BF16)
                pv = jnp.dot(vt, e, preferred_element_type=F32)
                denom = pv[SWA_HEAD_DIM:SWA_HEAD_DIM + 1, :] + jnp.exp2(sink2 - m)
                ot_sc[head * SWA_HEAD_DIM:(head + 1) * SWA_HEAD_DIM, :] = pv[:SWA_HEAD_DIM, :] / denom
    o_ref[...] = ot_sc[...].T.astype(o_ref.dtype)


def swa_attention(proj, sinks, pos_col, pos_row, batch, seq):
    nblk = seq // BLOCK
    kvw = SWA_KV_HEADS * SWA_HEAD_DIM
    gw = SWA_GROUP * SWA_HEAD_DIM
    cur = lambda b, n: b * nblk + n
    prev = lambda b, n: b * nblk + jnp.maximum(n - 1, 0)
    q_specs = [pl.BlockSpec((BLOCK, gw), functools.partial(lambda b, n, g: (cur(b, n), COL_QSWA // gw + g), g=g))
               for g in range(SWA_KV_HEADS)]
    return pl.pallas_call(
        _swa_kernel,
        grid=(batch, nblk),
        in_specs=[pl.BlockSpec(memory_space=pltpu.SMEM)] + q_specs + [
            pl.BlockSpec((BLOCK, kvw), lambda b, n: (cur(b, n), COL_KSWA // kvw)),
            pl.BlockSpec((BLOCK, kvw), lambda b, n: (prev(b, n), COL_KSWA // kvw)),
            pl.BlockSpec((BLOCK, kvw), lambda b, n: (cur(b, n), COL_VSWA // kvw)),
            pl.BlockSpec((BLOCK, kvw), lambda b, n: (prev(b, n), COL_VSWA // kvw)),
            pl.BlockSpec((BLOCK, 1), lambda b, n: (cur(b, n), 0)),
            pl.BlockSpec((BLOCK, 1), lambda b, n: (prev(b, n), 0)),
            pl.BlockSpec((1, 1, BLOCK), lambda b, n: (cur(b, n), 0, 0))],
        out_specs=pl.BlockSpec((BLOCK, SWA_WIDTH), lambda b, n: (cur(b, n), 0)),
        out_shape=jax.ShapeDtypeStruct((batch * seq, SWA_WIDTH), BF16),
        scratch_shapes=[pltpu.VMEM((SWA_WIDTH, BLOCK), F32)],
        compiler_params=_cparams(("parallel", "parallel")),
        name="swa_attention",
    )(sinks, proj, proj, proj, proj, proj, proj, proj, proj, pos_col, pos_col, pos_row)


def _wo_kernel(oa_ref, ob_ref, ga_ref, gb_ref, w_ref, x_ref, gy_ref, gn_ref, x1_ref, h_ref, *, chunks):
    ka = oa_ref.shape[1]
    na = _rms(oa_ref[...].astype(F32), ga_ref[...]).astype(BF16)
    nb = _rms(ob_ref[...].astype(F32), gb_ref[...]).astype(BF16)
    for lo, hi in chunks:
        x1_ref[:, lo:hi] = (jnp.dot(na, w_ref[:ka, lo:hi], preferred_element_type=F32)
                            + jnp.dot(nb, w_ref[ka:, lo:hi], preferred_element_type=F32))
    rows = 32
    for r in range(x1_ref.shape[0] // rows):
        rs = slice(r * rows, (r + 1) * rows)
        x1 = x_ref[rs, :] + _rms(x1_ref[rs, :], gy_ref[...])
        x1_ref[rs, :] = x1
        h_ref[rs, :] = _rms(x1, gn_ref[...]).astype(h_ref.dtype)


def out_proj_resid(oa, ob, ga, gb, w, x, gy, gn, tm=ROW_TILE // 2):
    m, ka = oa.shape
    kb = ob.shape[1]
    n = w.shape[1]
    row = pl.BlockSpec((tm, n), lambda i: (i, 0))
    vec = pl.BlockSpec((1, n), lambda i: (0, 0))
    return pl.pallas_call(
        functools.partial(_wo_kernel, chunks=_col_chunks(n, 4 * MXU_N)),
        grid=(m // tm,),
        in_specs=[pl.BlockSpec((tm, ka), lambda i: (i, 0)),
                  pl.BlockSpec((tm, kb), lambda i: (i, 0)),
                  pl.BlockSpec((1, ka), lambda i: (0, 0)),
                  pl.BlockSpec((1, kb), lambda i: (0, 0)),
                  pl.BlockSpec((ka + kb, n), lambda i: (0, 0), pipeline_mode=pl.Buffered(1)),
                  row, vec, vec],
        out_specs=[row, row],
        out_shape=[jax.ShapeDtypeStruct((m, n), F32), jax.ShapeDtypeStruct((m, n), BF16)],
        compiler_params=_cparams(("parallel",)),
        name="out_proj",
    )(oa, ob, ga, gb, w, x, gy, gn)


def _ffn_kernel(h_ref, wg_ref, wu_ref, wd_ref, o_ref, acc_sc, *, dn, rows):
    j = pl.program_id(1)
    tm, d = acc_sc.shape

    def hidden_tile(first):
        h = h_ref[...]
        g = jnp.dot(h, wg_ref[...], preferred_element_type=F32)
        u = jnp.dot(h, wu_ref[...], preferred_element_type=F32)
        a = (g * jax.nn.sigmoid(g) * u).astype(BF16)
        for c in range(d // dn):
            cols = slice(c * dn, (c + 1) * dn)
            y = jnp.dot(a, wd_ref[:, cols], preferred_element_type=F32)
            acc_sc[:, cols] = y if first else acc_sc[:, cols] + y

    @pl.when(j == 0)
    def _():
        hidden_tile(True)

    @pl.when(j > 0)
    def _():
        hidden_tile(False)

    @pl.when(j == pl.num_programs(1) - 1)
    def _():
        def body(r, carry):
            rs = pl.ds(pl.multiple_of(r * rows, rows), rows)
            o_ref[rs, :] = acc_sc[rs, :].astype(o_ref.dtype)
            return carry

        lax.fori_loop(0, tm // rows, body, 0)


def ffn(h, wg, wu, wd, tm=1024, tf=FFN_TILE):
    m, d = h.shape
    f = wg.shape[1]
    return pl.pallas_call(
        functools.partial(_ffn_kernel, dn=1024, rows=64),
        grid=(m // tm, f // tf),
        in_specs=[pl.BlockSpec((tm, d), lambda i, j: (i, 0)),
                  pl.BlockSpec((d, tf), lambda i, j: (0, j)),
                  pl.BlockSpec((d, tf), lambda i, j: (0, j)),
                  pl.BlockSpec((tf, d), lambda i, j: (j, 0))],
        out_specs=pl.BlockSpec((tm, d), lambda i, j: (i, 0), pipeline_mode=pl.Buffered(1)),
        out_shape=jax.ShapeDtypeStruct((m, d), BF16),
        scratch_shapes=[pltpu.VMEM((tm, d), F32)],
        compiler_params=_cparams(("parallel", "arbitrary")),
        name="ffn",
    )(h, wg, wu, wd)


def _ple_kernel(x_ref, f_ref, gf_ref, wg_ref, p_ref, wp_ref, o_ref, *, chunks):
    o_ref[...] = x_ref[...] + _rms(f_ref[...].astype(F32), gf_ref[...])
    xb = o_ref[...].astype(BF16)
    pb = p_ref[...].astype(BF16)
    for lo, hi in chunks:
        gate = jax.nn.sigmoid(jnp.dot(xb, wg_ref[:, lo:hi], preferred_element_type=F32))
        e = jnp.dot(pb, wp_ref[:, lo:hi], preferred_element_type=F32)
        o_ref[:, lo:hi] += gate * e


def ple(x, f, g_f, wg, p, wp, tm=ROW_TILE):
    m, d = x.shape
    n = wg.shape[1]
    pd = p.shape[1]
    once = dict(pipeline_mode=pl.Buffered(1))
    return pl.pallas_call(
        functools.partial(_ple_kernel, chunks=_col_chunks(n, 4 * MXU_N)),
        grid=(m // tm,),
        in_specs=[pl.BlockSpec((tm, d), lambda i: (i, 0)),
                  pl.BlockSpec((tm, d), lambda i: (i, 0)),
                  pl.BlockSpec((1, d), lambda i: (0, 0)),
                  pl.BlockSpec((d, n), lambda i: (0, 0), **once),
                  pl.BlockSpec((tm, pd), lambda i: (i, 0)),
                  pl.BlockSpec((pd, n), lambda i: (0, 0), **once)],
        out_specs=pl.BlockSpec((tm, n), lambda i: (i, 0)),
        out_shape=jax.ShapeDtypeStruct((m, n), F32),
        compiler_params=_cparams(("parallel",)),
        name="ple",
    )(x, f, g_f, wg, p, wp)


_KR_BLOCK = (Q_LORA + KV_LORA) // MLA_ROPE
_QSWA_BLOCKS = (COL_QSWA // LANES, COL_KSWA // LANES)


def _w_in_prep_kernel(wa_ref, wb_ref, o_ref):
    ob = pl.program_id(0)
    a, b = wa_ref[0], wb_ref[0]
    half = MLA_ROPE // 2
    partner = jnp.concatenate([-b[half:], b[:half]], axis=0)
    b = jnp.where(ob == pl.num_programs(0) - 1, partner, b)
    scale = jnp.where((ob >= _QSWA_BLOCKS[0]) & (ob < _QSWA_BLOCKS[1]), LOG2E / math.sqrt(SWA_HEAD_DIM), 1.0)
    blk = jnp.concatenate([a, b], axis=0) * scale
    o_ref[...] = blk.T.astype(BF16)


def _prep_w_in(w3, layer):
    wt = jnp.swapaxes(w3, 1, 2)
    _, n, k = wt.shape
    n_out = D_IN2 // LANES
    shifted = COL_QSWA // LANES

    def src(ob, part):
        blk = 2 * ob + part + jnp.where(ob >= shifted, 1, 0)
        return jnp.where(ob == n_out - 1, _KR_BLOCK, blk)

    return pl.pallas_call(
        _w_in_prep_kernel,
        grid=(n_out,),
        in_specs=[pl.BlockSpec((1, MLA_ROPE, k), lambda ob: (layer, src(ob, 0), 0)),
                  pl.BlockSpec((1, MLA_ROPE, k), lambda ob: (layer, src(ob, 1), 0))],
        out_specs=pl.BlockSpec((k, LANES), lambda ob: (0, ob)),
        out_shape=jax.ShapeDtypeStruct((k, D_IN2), BF16),
        compiler_params=_cparams(("parallel",)),
        name="w_in_prep",
    )(wt, wt)


def _prep_w_q_up(w):
    w = (w * (LOG2E / math.sqrt(MLA_NOPE + MLA_ROPE))).reshape(Q_LORA, MLA_HEADS, MLA_NOPE + MLA_ROPE)
    nope = w[:, :, :MLA_NOPE].reshape(Q_LORA, MLA_WIDTH)
    rope = w[:, :, MLA_NOPE:].reshape(Q_LORA, MLA_HEADS * MLA_ROPE)
    return jnp.concatenate([rope, nope], axis=1).astype(BF16)


def _prep_w_kv_up(w):
    w = w.reshape(KV_LORA, MLA_HEADS, MLA_NOPE + MLA_V)
    wk = w[:, :, :MLA_NOPE].reshape(KV_LORA, MLA_WIDTH).astype(BF16)
    wvt = w[:, :, MLA_NOPE:].reshape(KV_LORA, MLA_WIDTH).T.astype(BF16)
    return wk, wvt


def _layer(x, p, pos_col, pos_row, invf, batch, seq, attn_pre_norm, w_in, q_a_norm, w_q_up,
           kv_a_norm, w_kv_up, sinks, mla_out_norm, swa_out_norm, w_o, attn_post_norm,
           ffn_pre_norm, w_gate, w_up, w_down, ffn_post_norm, w_ple_gate, w_ple_proj):
    row = lambda g: g.reshape(1, -1)
    proj = in_proj(x, row(attn_pre_norm), _prep_w_in(*w_in))
    qn, qr = q_up(proj, row(q_a_norm), _prep_w_q_up(w_q_up), pos_col, invf)
    wk, wvt = _prep_w_kv_up(w_kv_up)
    kn, vt, kr = kv_up(proj, row(kv_a_norm), wk, wvt, pos_col, invf, batch, seq)
    o_mla, (wo16, wg16, wu16, wd16, wpg16) = mla_attention(
        qn, qr, kn, kr, vt, [w_o, w_gate, w_up, w_down, w_ple_gate], batch, seq)
    o_swa = swa_attention(proj, sinks, pos_col, pos_row, batch, seq)
    x1, h2 = out_proj_resid(o_mla, o_swa, row(mla_out_norm), row(swa_out_norm), wo16,
                            x, row(attn_post_norm), row(ffn_pre_norm))
    f = ffn(h2, wg16, wu16, wd16)
    return ple(x1, f, row(ffn_post_norm), wpg16, p, w_ple_proj.astype(BF16))


def kernel(x, p, positions, attn_pre_norm, w_in, q_a_norm, w_q_up, kv_a_norm, w_kv_up, sinks,
           mla_out_norm, swa_out_norm, w_o, attn_post_norm, ffn_pre_norm, w_gate, w_up, w_down,
           ffn_post_norm, w_ple_gate, w_ple_proj):
    batch, seq, d = x.shape
    depth = w_in.shape[0]
    t = batch * seq
    pos_col = positions.reshape(t, 1)
    pos_row = positions.reshape(t // BLOCK, 1, BLOCK)
    half = MLA_ROPE // 2
    invf = np.asarray(ROPE_THETA ** (-(np.arange(LANES) % half) * 2.0 / MLA_ROPE), np.float32).reshape(1, LANES)
    invf = jnp.asarray(invf)
    xf = x.reshape(t, d)
    for i in range(depth):
        xf = _layer(xf, p[i].reshape(t, PLE_DIM), pos_col, pos_row, invf, batch, seq,
                    attn_pre_norm[i], (w_in, i), q_a_norm[i], w_q_up[i], kv_a_norm[i], w_kv_up[i],
                    sinks[i], mla_out_norm[i], swa_out_norm[i], w_o[i], attn_post_norm[i],
                    ffn_pre_norm[i], w_gate[i], w_up[i], w_down[i], ffn_post_norm[i],
                    w_ple_gate[i], w_ple_proj[i])
    return xf.reshape(batch, seq, d)
```

```python
import functools
import math

import numpy as np
import jax
import jax.numpy as jnp
from jax import lax
from jax.experimental import pallas as pl
from jax.experimental.pallas import tpu as pltpu

PLE_DIM = 256
MLA_HEADS = 16
MLA_NOPE = 128
MLA_ROPE = 64
MLA_V = 128
Q_LORA = 1024
KV_LORA = 512
MLA_WIDTH = MLA_HEADS * MLA_V
SWA_HEADS = 32
SWA_KV_HEADS = 4
SWA_HEAD_DIM = 64
SWA_GROUP = SWA_HEADS // SWA_KV_HEADS
SWA_WIDTH = SWA_HEADS * SWA_HEAD_DIM
WINDOW = 128
BLOCK = 128
ROPE_THETA = 10000.0
NORM_EPS = 1e-6
LOG2E = math.log2(math.e)

LANES = 128
MXU_N = 256
BF16_ROWS = 16
MLA_TILE = 512
ROW_TILE = 256
FFN_TILE = MXU_N
VT_ROWS = MLA_V + BF16_ROWS
VMEM_LIMIT = 60 * 1024 * 1024

COL_CQ = 0
COL_CKV = COL_CQ + Q_LORA
COL_QSWA = COL_CKV + KV_LORA
COL_KSWA = COL_QSWA + SWA_WIDTH
COL_VSWA = COL_KSWA + SWA_KV_HEADS * SWA_HEAD_DIM
COL_KR = COL_VSWA + SWA_KV_HEADS * SWA_HEAD_DIM
D_IN2 = COL_KR + 2 * MLA_ROPE

F32 = jnp.float32
BF16 = jnp.bfloat16


def _cparams(sem):
    return pltpu.CompilerParams(dimension_semantics=sem, vmem_limit_bytes=VMEM_LIMIT)


def _rms(xf, g):
    ms = jnp.mean(xf * xf, axis=-1, keepdims=True)
    return xf * lax.rsqrt(ms + NORM_EPS) * g


def _rope_tables(pos_col, invf_row):
    ang = pos_col.astype(F32) * invf_row
    return jnp.cos(ang), jnp.sin(ang)


def _col_chunks(n, width):
    return [(c, min(c + width, n)) for c in range(0, n, width)]


def _in_proj_kernel(x_ref, g_ref, w_ref, o_ref, *, chunks):
    h = _rms(x_ref[...], g_ref[...]).astype(BF16)
    for lo, hi in chunks:
        o_ref[:, lo:hi] = jnp.dot(h, w_ref[:, lo:hi], preferred_element_type=F32).astype(o_ref.dtype)


def in_proj(x, g, w, tm=ROW_TILE):
    m, k = x.shape
    n = w.shape[1]
    return pl.pallas_call(
        functools.partial(_in_proj_kernel, chunks=_col_chunks(n, 6 * MXU_N)),
        grid=(m // tm,),
        in_specs=[pl.BlockSpec((tm, k), lambda i: (i, 0)),
                  pl.BlockSpec((1, k), lambda i: (0, 0)),
                  pl.BlockSpec((k, n), lambda i: (0, 0), pipeline_mode=pl.Buffered(1))],
        out_specs=pl.BlockSpec((tm, n), lambda i: (i, 0)),
        out_shape=jax.ShapeDtypeStruct((m, n), BF16),
        compiler_params=_cparams(("parallel",)),
        name="in_proj",
    )(x, g, w)


def _q_up_kernel(cq_ref, g_ref, w_ref, cos_ref, sin_ref, qn_ref, qr_ref):
    h = _rms(cq_ref[...].astype(F32), g_ref[...]).astype(BF16)
    n_rope = MLA_HEADS * MLA_ROPE
    rot = jnp.dot(h, w_ref[:, :n_rope], preferred_element_type=F32)
    qn_ref[...] = jnp.dot(h, w_ref[:, n_rope:], preferred_element_type=F32).astype(BF16)
    cos, sin = cos_ref[...], sin_ref[...]
    half = MLA_ROPE // 2
    lane = lax.broadcasted_iota(jnp.int32, cos.shape, 1)
    first_half = (lane % MLA_ROPE) < half
    for c in range(n_rope // LANES):
        a = rot[:, c * LANES:(c + 1) * LANES]
        b = jnp.where(first_half, -pltpu.roll(a, LANES - half, 1), pltpu.roll(a, half, 1))
        qr_ref[:, c * LANES:(c + 1) * LANES] = (a * cos + b * sin).astype(BF16)


def q_up(proj, g, w, cos, sin, tm=512):
    m = proj.shape[0]
    n_rope = MLA_HEADS * MLA_ROPE
    return pl.pallas_call(
        _q_up_kernel,
        grid=(m // tm,),
        in_specs=[pl.BlockSpec((tm, Q_LORA), lambda i: (i, COL_CQ // Q_LORA)),
                  pl.BlockSpec((1, Q_LORA), lambda i: (0, 0)),
                  pl.BlockSpec(w.shape, lambda i: (0, 0)),
                  pl.BlockSpec((tm, LANES), lambda i: (i, 0)),
                  pl.BlockSpec((tm, LANES), lambda i: (i, 0))],
        out_specs=[pl.BlockSpec((tm, MLA_WIDTH), lambda i: (i, 0)),
                   pl.BlockSpec((tm, n_rope), lambda i: (i, 0))],
        out_shape=[jax.ShapeDtypeStruct((m, MLA_WIDTH), BF16),
                   jax.ShapeDtypeStruct((m, n_rope), BF16)],
        compiler_params=_cparams(("parallel",)),
        name="q_up",
    )(proj, g, w, cos, sin)


def _kv_up_kernel(ckv_ref, g_ref, wk_ref, wvt_ref, krab_ref, pos_ref, invf_ref, kn_ref, vt_ref, kr_ref,
                  cos_ref, sin_ref):
    h = _rms(ckv_ref[...].astype(F32), g_ref[...]).astype(BF16)
    kn_ref[...] = jnp.dot(h, wk_ref[...], preferred_element_type=F32).astype(BF16)
    vt = lax.dot_general(wvt_ref[...], h, (((1,), (1,)), ((), ())), preferred_element_type=F32)
    ones = jnp.ones((VT_ROWS - MLA_V, vt.shape[1]), BF16)
    for hh in range(MLA_HEADS):
        vt_ref[0, hh, 0, :MLA_V, :] = vt[hh * MLA_V:(hh + 1) * MLA_V, :].astype(BF16)
        vt_ref[0, hh, 0, MLA_V:, :] = ones
    cos, sin = _rope_tables(pos_ref[...], invf_ref[...])
    cos_ref[...] = cos
    sin_ref[...] = sin
    lane = lax.broadcasted_iota(jnp.int32, cos.shape, 1)
    lo = lane < MLA_ROPE
    prod = krab_ref[...].astype(F32) * jnp.where(lo, cos, sin)
    kk = prod + pltpu.roll(prod, MLA_ROPE, 1)
    zero = jnp.zeros_like(kk)
    kr_ref[:, :LANES] = jnp.where(lo, kk, zero).astype(BF16)
    kr_ref[:, LANES:] = jnp.where(lo, zero, kk).astype(BF16)


def kv_up(proj, g, wk, wvt, pos_col, invf, batch, seq, tm=MLA_TILE):
    m = proj.shape[0]
    nk = seq // tm
    return pl.pallas_call(
        _kv_up_kernel,
        grid=(m // tm,),
        in_specs=[pl.BlockSpec((tm, KV_LORA), lambda i: (i, COL_CKV // KV_LORA)),
                  pl.BlockSpec((1, KV_LORA), lambda i: (0, 0)),
                  pl.BlockSpec(wk.shape, lambda i: (0, 0)),
                  pl.BlockSpec(wvt.shape, lambda i: (0, 0)),
                  pl.BlockSpec((tm, LANES), lambda i: (i, COL_KR // LANES)),
                  pl.BlockSpec((tm, 1), lambda i: (i, 0)),
                  pl.BlockSpec((1, LANES), lambda i: (0, 0))],
        out_specs=[pl.BlockSpec((tm, MLA_WIDTH), lambda i: (i, 0)),
                   pl.BlockSpec((1, MLA_HEADS, 1, VT_ROWS, tm), lambda i: (i // nk, 0, i % nk, 0, 0)),
                   pl.BlockSpec((tm, 2 * LANES), lambda i: (i, 0)),
                   pl.BlockSpec((tm, LANES), lambda i: (i, 0)),
                   pl.BlockSpec((tm, LANES), lambda i: (i, 0))],
        out_shape=[jax.ShapeDtypeStruct((m, MLA_WIDTH), BF16),
                   jax.ShapeDtypeStruct((batch, MLA_HEADS, nk, VT_ROWS, tm), BF16),
                   jax.ShapeDtypeStruct((m, 2 * LANES), BF16),
                   jax.ShapeDtypeStruct((m, LANES), F32),
                   jax.ShapeDtypeStruct((m, LANES), F32)],
        compiler_params=_cparams(("parallel",)),
        name="kv_up",
    )(proj, g, wk, wvt, proj, pos_col, invf)


def _mla_kernel(*refs, tq, tk, n_cast):
    qn_ref, qr_ref, kn_ref, kr_ref, vt_ref = refs[:5]
    w_refs = refs[5:5 + n_cast]
    o_ref = refs[5 + n_cast]
    w16_refs = refs[6 + n_cast:6 + 2 * n_cast]
    s_a, s_b, mx_a, mx_b, m_sc, acc_sc = refs[6 + 2 * n_cast:]

    qi = pl.program_id(2)
    q = jnp.concatenate([qn_ref[...], qr_ref[...]], axis=-1)
    m_sc[...] = jnp.full(m_sc.shape, -jnp.inf, F32)
    acc_sc[...] = jnp.zeros(acc_sc.shape, F32)

    n_chunk = tq // MXU_N
    blk_chunks = tk // MXU_N
    n_diag = tq // tk
    bufs = ((s_a, mx_a), (s_b, mx_b))

    def live_chunks(diag):
        return range(0 if diag is None else diag * blk_chunks, n_chunk)

    def scores(kb, buf, diag=None):
        s_ref, mx_ref = buf
        off = pl.multiple_of(kb * tk, tk)
        k = jnp.concatenate([kn_ref[pl.ds(off, tk), :], kr_ref[pl.ds(off, tk), :]], axis=-1)
        for c in live_chunks(diag):
            s = lax.dot_general(k, q[c * MXU_N:(c + 1) * MXU_N, :], (((1,), (1,)), ((), ())),
                                preferred_element_type=F32)
            if diag is not None and c < (diag + 1) * blk_chunks:
                key = lax.broadcasted_iota(jnp.int32, s.shape, 0) + diag * tk
                qry = lax.broadcasted_iota(jnp.int32, s.shape, 1) + c * MXU_N
                s = jnp.where(qry >= key, s, -jnp.inf)
            s_ref[c] = s
            mx_ref[:, c * MXU_N:(c + 1) * MXU_N] = jnp.max(s, axis=0, keepdims=True)

    def softmax_pv(kb, buf, diag=None, mask_here=False):
        s_ref, mx_ref = buf
        vt = vt_ref[0, 0, kb]
        for c in live_chunks(diag):
            sl = slice(c * MXU_N, (c + 1) * MXU_N)
            remask = mask_here and c < blk_chunks

            def s_chunk():
                s = s_ref[c]
                if remask:
                    key = lax.broadcasted_iota(jnp.int32, s.shape, 0)
                    qry = lax.broadcasted_iota(jnp.int32, s.shape, 1) + c * MXU_N
                    s = jnp.where(qry >= key, s, -jnp.inf)
                return s

            mx = jnp.max(s_chunk(), axis=0, keepdims=True) if remask else mx_ref[:, sl]
            m_old = m_sc[:, sl]
            m_new = jnp.maximum(m_old, mx)
            alpha = jnp.exp2(m_old - m_new)
            p = jnp.exp2(s_chunk() - m_new).astype(BF16)
            acc_sc[:, sl] = alpha * acc_sc[:, sl] + jnp.dot(vt, p, preferred_element_type=F32)
            m_sc[:, sl] = m_new

    n_full = n_diag * qi
    scores(0, bufs[0])

    def body(j, carry):
        for t in range(n_diag):
            kb = n_diag * j + t
            scores(kb + 1, bufs[(t + 1) % 2])
            softmax_pv(kb, bufs[t % 2])
        return carry

    lax.fori_loop(0, qi, body, 0)

    for w_ref, w16_ref in zip(w_refs, w16_refs):
        w16_ref[...] = w_ref[...].astype(BF16)

    for d in range(n_diag):
        if d + 1 < n_diag:
            scores(n_full + d + 1, bufs[(d + 1) % 2], diag=d + 1)
        softmax_pv(n_full + d, bufs[d % 2], diag=d, mask_here=(d == 0))

    o_ref[...] = (acc_sc[:MLA_V, :] / acc_sc[MLA_V:MLA_V + 1, :]).T.astype(o_ref.dtype)


def mla_attention(qn, qr, kn, kr, vt, weights, batch, seq, tq=4 * MLA_TILE, tk=MLA_TILE):
    assert tq % (2 * tk) == 0
    nq = seq // tq
    steps = batch * MLA_HEADS * nq

    def slab(w):
        tiles, rem = divmod(w.shape[0], BF16_ROWS)
        assert rem == 0, w.shape
        n_slab = max(n for n in range(1, steps + 1) if tiles % n == 0)
        rows = w.shape[0] // n_slab
        return pl.BlockSpec((rows, w.shape[1]),
                            lambda b, h, i: (jnp.minimum((b * MLA_HEADS + h) * nq + i, n_slab - 1), 0))

    slabs = [slab(w) for w in weights]
    outs = pl.pallas_call(
        functools.partial(_mla_kernel, tq=tq, tk=tk, n_cast=len(weights)),
        grid=(batch, MLA_HEADS, nq),
        in_specs=[pl.BlockSpec((tq, MLA_NOPE), lambda b, h, i: (b * nq + i, h)),
                  pl.BlockSpec((tq, LANES), lambda b, h, i: (b * nq + i, h // 2)),
                  pl.BlockSpec((seq, MLA_NOPE), lambda b, h, i: (b, h)),
                  pl.BlockSpec((seq, LANES), lambda b, h, i: (b, h % 2)),
                  pl.BlockSpec((1, 1, seq // tk, VT_ROWS, tk), lambda b, h, i: (b, h, 0, 0, 0))] + slabs,
        out_specs=[pl.BlockSpec((tq, MLA_V), lambda b, h, i: (b * nq + i, h))] + slabs,
        out_shape=[jax.ShapeDtypeStruct((batch * seq, MLA_WIDTH), BF16)]
        + [jax.ShapeDtypeStruct(w.shape, BF16) for w in weights],
        scratch_shapes=[pltpu.VMEM((tq // MXU_N, tk, MXU_N), F32), pltpu.VMEM((tq // MXU_N, tk, MXU_N), F32),
                        pltpu.VMEM((1, tq), F32), pltpu.VMEM((1, tq), F32),
                        pltpu.VMEM((1, tq), F32), pltpu.VMEM((VT_ROWS, tq), F32)],
        compiler_params=_cparams(("arbitrary", "arbitrary", "arbitrary")),
        name="mla_attention",
    )(qn, qr, kn, kr, vt, *weights)
    return outs[0], outs[1:]


def _swa_kernel(sinks_ref, q0_ref, q1_ref, q2_ref, q3_ref, kc_ref, kp_ref, vc_ref, vp_ref,
                pkc_ref, pkp_ref, pq_ref, o_ref, ot_sc):
    blk = pl.program_id(1)
    q_refs = (q0_ref, q1_ref, q2_ref, q3_ref)
    k_all = jnp.concatenate([kp_ref[...], kc_ref[...]], axis=0).astype(F32)
    v_all = jnp.concatenate([vp_ref[...], vc_ref[...]], axis=0).astype(F32)
    pk = jnp.concatenate([pkp_ref[...], pkc_ref[...]], axis=0).astype(F32)
    dist = jnp.abs(pk - pq_ref[0].astype(F32))
    key = lax.broadcasted_iota(jnp.int32, dist.shape, 0)
    qry = lax.broadcasted_iota(jnp.int32, dist.shape, 1)
    rel = BLOCK + qry - key
    first_key = jnp.where(blk > 0, 0, BLOCK)
    valid = (rel >= 0) & (rel < WINDOW) & (key >= first_key)
    dist_w = jnp.where(valid, dist, jnp.inf)
    lane = lax.broadcasted_iota(jnp.int32, (2 * BLOCK, LANES), 1)
    lo = lane < SWA_HEAD_DIM

    def dup(x_all, g):
        pair = x_all[:, (g // 2) * LANES:(g // 2 + 1) * LANES]
        rolled = pltpu.roll(pair, SWA_HEAD_DIM, 1)
        return jnp.where(lo, pair, rolled) if g % 2 == 0 else jnp.where(lo, rolled, pair)

    for g in range(SWA_KV_HEADS):
        kk = dup(k_all, g)
        zero = jnp.zeros_like(kk)
        k_half = (jnp.where(lo, kk, zero).astype(BF16), jnp.where(lo, zero, kk).astype(BF16))
        vt = jnp.where(lo, dup(v_all, g), 1.0).T.astype(BF16)
        for i in range(SWA_GROUP // 2):
            q_pair = q_refs[g][:, i * LANES:(i + 1) * LANES]
            for j in range(2):
                head = g * SWA_GROUP + 2 * i + j
                slope2 = LOG2E * 2.0 ** (-8.0 * (head + 1) / SWA_HEADS)
                s = lax.dot_general(k_half[j], q_pair, (((1,), (1,)), ((), ())),
                                    preferred_element_type=F32)
                s = s - slope2 * dist_w
                sink2 = LOG2E * sinks_ref[head]
                m = jnp.maximum(jnp.max(s, axis=0, keepdims=True), sink2)
                e = jnp.exp2(s - m).astype(BF16)
                pv = jnp.dot(vt, e, preferred_element_type=F32)
                denom = pv[SWA_HEAD_DIM:SWA_HEAD_DIM + 1, :] + jnp.exp2(sink2 - m)
                ot_sc[head * SWA_HEAD_DIM:(head + 1) * SWA_HEAD_DIM, :] = pv[:SWA_HEAD_DIM, :] / denom
    o_ref[...] = ot_sc[...].T.astype(o_ref.dtype)


def swa_attention(proj, sinks, pos_col, pos_row, batch, seq):
    nblk = seq // BLOCK
    kvw = SWA_KV_HEADS * SWA_HEAD_DIM
    gw = SWA_GROUP * SWA_HEAD_DIM
    cur = lambda b, n: b * nblk + n
    prev = lambda b, n: b * nblk + jnp.maximum(n - 1, 0)
    q_specs = [pl.BlockSpec((BLOCK, gw), functools.partial(lambda b, n, g: (cur(b, n), COL_QSWA // gw + g), g=g))
               for g in range(SWA_KV_HEADS)]
    return pl.pallas_call(
        _swa_kernel,
        grid=(batch, nblk),
        in_specs=[pl.BlockSpec(memory_space=pltpu.SMEM)] + q_specs + [
            pl.BlockSpec((BLOCK, kvw), lambda b, n: (cur(b, n), COL_KSWA // kvw)),
            pl.BlockSpec((BLOCK, kvw), lambda b, n: (prev(b, n), COL_KSWA // kvw)),
            pl.BlockSpec((BLOCK, kvw), lambda b, n: (cur(b, n), COL_VSWA // kvw)),
            pl.BlockSpec((BLOCK, kvw), lambda b, n: (prev(b, n), COL_VSWA // kvw)),
            pl.BlockSpec((BLOCK, 1), lambda b, n: (cur(b, n), 0)),
            pl.BlockSpec((BLOCK, 1), lambda b, n: (prev(b, n), 0)),
            pl.BlockSpec((1, 1, BLOCK), lambda b, n: (cur(b, n), 0, 0))],
        out_specs=pl.BlockSpec((BLOCK, SWA_WIDTH), lambda b, n: (cur(b, n), 0)),
        out_shape=jax.ShapeDtypeStruct((batch * seq, SWA_WIDTH), BF16),
        scratch_shapes=[pltpu.VMEM((SWA_WIDTH, BLOCK), F32)],
        compiler_params=_cparams(("parallel", "parallel")),
        name="swa_attention",
    )(sinks, proj, proj, proj, proj, proj, proj, proj, proj, pos_col, pos_col, pos_row)


def _wo_kernel(oa_ref, ob_ref, ga_ref, gb_ref, w_ref, x_ref, gy_ref, gn_ref, x1_ref, h_ref, *, chunks):
    ka = oa_ref.shape[1]
    na = _rms(oa_ref[...].astype(F32), ga_ref[...]).astype(BF16)
    nb = _rms(ob_ref[...].astype(F32), gb_ref[...]).astype(BF16)
    for lo, hi in chunks:
        x1_ref[:, lo:hi] = (jnp.dot(na, w_ref[:ka, lo:hi], preferred_element_type=F32)
                            + jnp.dot(nb, w_ref[ka:, lo:hi], preferred_element_type=F32))
    rows = 32
    for r in range(x1_ref.shape[0] // rows):
        rs = slice(r * rows, (r + 1) * rows)
        x1 = x_ref[rs, :] + _rms(x1_ref[rs, :], gy_ref[...])
        x1_ref[rs, :] = x1
        h_ref[rs, :] = _rms(x1, gn_ref[...]).astype(h_ref.dtype)


def out_proj_resid(oa, ob, ga, gb, w, x, gy, gn, tm=ROW_TILE // 2):
    m, ka = oa.shape
    kb = ob.shape[1]
    n = w.shape[1]
    row = pl.BlockSpec((tm, n), lambda i: (i, 0))
    vec = pl.BlockSpec((1, n), lambda i: (0, 0))
    return pl.pallas_call(
        functools.partial(_wo_kernel, chunks=_col_chunks(n, 4 * MXU_N)),
        grid=(m // tm,),
        in_specs=[pl.BlockSpec((tm, ka), lambda i: (i, 0)),
                  pl.BlockSpec((tm, kb), lambda i: (i, 0)),
                  pl.BlockSpec((1, ka), lambda i: (0, 0)),
                  pl.BlockSpec((1, kb), lambda i: (0, 0)),
                  pl.BlockSpec((ka + kb, n), lambda i: (0, 0), pipeline_mode=pl.Buffered(1)),
                  row, vec, vec],
        out_specs=[row, row],
        out_shape=[jax.ShapeDtypeStruct((m, n), F32), jax.ShapeDtypeStruct((m, n), BF16)],
        compiler_params=_cparams(("parallel",)),
        name="out_proj",
    )(oa, ob, ga, gb, w, x, gy, gn)


def _ffn_kernel(h_ref, wg_ref, wu_ref, wd_ref, o_ref, acc_sc, *, dn, rows):
    j = pl.program_id(1)
    tm, d = acc_sc.shape

    def hidden_tile(first):
        h = h_ref[...]
        g = jnp.dot(h, wg_ref[...], preferred_element_type=F32)
        u = jnp.dot(h, wu_ref[...], preferred_element_type=F32)
        a = (g * jax.nn.sigmoid(g) * u).astype(BF16)
        for c in range(d // dn):
            cols = slice(c * dn, (c + 1) * dn)
            y = jnp.dot(a, wd_ref[:, cols], preferred_element_type=F32)
            acc_sc[:, cols] = y if first else acc_sc[:, cols] + y

    @pl.when(j == 0)
    def _():
        hidden_tile(True)

    @pl.when(j > 0)
    def _():
        hidden_tile(False)

    @pl.when(j == pl.num_programs(1) - 1)
    def _():
        def body(r, carry):
            rs = pl.ds(pl.multiple_of(r * rows, rows), rows)
            o_ref[rs, :] = acc_sc[rs, :].astype(o_ref.dtype)
            return carry

        lax.fori_loop(0, tm // rows, body, 0)


def ffn(h, wg, wu, wd, tm=1024, tf=FFN_TILE):
    m, d = h.shape
    f = wg.shape[1]
    return pl.pallas_call(
        functools.partial(_ffn_kernel, dn=1024, rows=64),
        grid=(m // tm, f // tf),
        in_specs=[pl.BlockSpec((tm, d), lambda i, j: (i, 0)),
                  pl.BlockSpec((d, tf), lambda i, j: (0, j)),
                  pl.BlockSpec((d, tf), lambda i, j: (0, j)),
                  pl.BlockSpec((tf, d), lambda i, j: (j, 0))],
        out_specs=pl.BlockSpec((tm, d), lambda i, j: (i, 0), pipeline_mode=pl.Buffered(1)),
        out_shape=jax.ShapeDtypeStruct((m, d), BF16),
        scratch_shapes=[pltpu.VMEM((tm, d), F32)],
        compiler_params=_cparams(("parallel", "arbitrary")),
        name="ffn",
    )(h, wg, wu, wd)


def _ple_kernel(x_ref, f_ref, gf_ref, wg_ref, p_ref, wp_ref, o_ref, *, chunks):
    o_ref[...] = x_ref[...] + _rms(f_ref[...].astype(F32), gf_ref[...])
    xb = o_ref[...].astype(BF16)
    pb = p_ref[...].astype(BF16)
    for lo, hi in chunks:
        gate = jax.nn.sigmoid(jnp.dot(xb, wg_ref[:, lo:hi], preferred_element_type=F32))
        e = jnp.dot(pb, wp_ref[:, lo:hi], preferred_element_type=F32)
        o_ref[:, lo:hi] += gate * e


def ple(x, f, g_f, wg, p, wp, tm=ROW_TILE):
    m, d = x.shape
    n = wg.shape[1]
    pd = p.shape[1]
    once = dict(pipeline_mode=pl.Buffered(1))
    return pl.pallas_call(
        functools.partial(_ple_kernel, chunks=_col_chunks(n, 4 * MXU_N)),
        grid=(m // tm,),
        in_specs=[pl.BlockSpec((tm, d), lambda i: (i, 0)),
                  pl.BlockSpec((tm, d), lambda i: (i, 0)),
                  pl.BlockSpec((1, d), lambda i: (0, 0)),
                  pl.BlockSpec((d, n), lambda i: (0, 0), **once),
                  pl.BlockSpec((tm, pd), lambda i: (i, 0)),
                  pl.BlockSpec((pd, n), lambda i: (0, 0), **once)],
        out_specs=pl.BlockSpec((tm, n), lambda i: (i, 0)),
        out_shape=jax.ShapeDtypeStruct((m, n), F32),
        compiler_params=_cparams(("parallel",)),
        name="ple",
    )(x, f, g_f, wg, p, wp)


_KR_BLOCK = (Q_LORA + KV_LORA) // MLA_ROPE
_QSWA_BLOCKS = (COL_QSWA // LANES, COL_KSWA // LANES)


def _w_in_prep_kernel(wa_ref, wb_ref, o_ref):
    ob = pl.program_id(0)
    a, b = wa_ref[0], wb_ref[0]
    half = MLA_ROPE // 2
    partner = jnp.concatenate([-b[half:], b[:half]], axis=0)
    b = jnp.where(ob == pl.num_programs(0) - 1, partner, b)
    scale = jnp.where((ob >= _QSWA_BLOCKS[0]) & (ob < _QSWA_BLOCKS[1]), LOG2E / math.sqrt(SWA_HEAD_DIM), 1.0)
    blk = jnp.concatenate([a, b], axis=0) * scale
    o_ref[...] = blk.T.astype(BF16)


def _prep_w_in(w3, layer):
    wt = jnp.swapaxes(w3, 1, 2)
    _, n, k = wt.shape
    n_out = D_IN2 // LANES
    shifted = COL_QSWA // LANES

    def src(ob, part):
        blk = 2 * ob + part + jnp.where(ob >= shifted, 1, 0)
        return jnp.where(ob == n_out - 1, _KR_BLOCK, blk)

    return pl.pallas_call(
        _w_in_prep_kernel,
        grid=(n_out,),
        in_specs=[pl.BlockSpec((1, MLA_ROPE, k), lambda ob: (layer, src(ob, 0), 0)),
                  pl.BlockSpec((1, MLA_ROPE, k), lambda ob: (layer, src(ob, 1), 0))],
        out_specs=pl.BlockSpec((k, LANES), lambda ob: (0, ob)),
        out_shape=jax.ShapeDtypeStruct((k, D_IN2), BF16),
        compiler_params=_cparams(("parallel",)),
        name="w_in_prep",
    )(wt, wt)


def _prep_w_q_up(w):
    w = (w * (LOG2E / math.sqrt(MLA_NOPE + MLA_ROPE))).reshape(Q_LORA, MLA_HEADS, MLA_NOPE + MLA_ROPE)
    nope = w[:, :, :MLA_NOPE].reshape(Q_LORA, MLA_WIDTH)
    rope = w[:, :, MLA_NOPE:].reshape(Q_LORA, MLA_HEADS * MLA_ROPE)
    return jnp.concatenate([rope, nope], axis=1).astype(BF16)


def _prep_w_kv_up(w):
    w = w.reshape(KV_LORA, MLA_HEADS, MLA_NOPE + MLA_V)
    wk = w[:, :, :MLA_NOPE].reshape(KV_LORA, MLA_WIDTH).astype(BF16)
    wvt = w[:, :, MLA_NOPE:].reshape(KV_LORA, MLA_WIDTH).T.astype(BF16)
    return wk, wvt


def _layer(x, p, pos_col, pos_row, invf, batch, seq, attn_pre_norm, w_in, q_a_norm, w_q_up,
           kv_a_norm, w_kv_up, sinks, mla_out_norm, swa_out_norm, w_o, attn_post_norm,
           ffn_pre_norm, w_gate, w_up, w_down, ffn_post_norm, w_ple_gate, w_ple_proj):
    row = lambda g: g.reshape(1, -1)
    proj = in_proj(x, row(attn_pre_norm), _prep_w_in(*w_in))
    wk, wvt = _prep_w_kv_up(w_kv_up)
    kn, vt, kr, cos, sin = kv_up(proj, row(kv_a_norm), wk, wvt, pos_col, invf, batch, seq)
    qn, qr = q_up(proj, row(q_a_norm), _prep_w_q_up(w_q_up), cos, sin)
    o_mla, (wo16, wg16, wu16, wd16, wpg16) = mla_attention(
        qn, qr, kn, kr, vt, [w_o, w_gate, w_up, w_down, w_ple_gate], batch, seq)
    o_swa = swa_attention(proj, sinks, pos_col, pos_row, batch, seq)
    x1, h2 = out_proj_resid(o_mla, o_swa, row(mla_out_norm), row(swa_out_norm), wo16,
                            x, row(attn_post_norm), row(ffn_pre_norm))
    f = ffn(h2, wg16, wu16, wd16)
    return ple(x1, f, row(ffn_post_norm), wpg16, p, w_ple_proj.astype(BF16))


def kernel(x, p, positions, attn_pre_norm, w_in, q_a_norm, w_q_up, kv_a_norm, w_kv_up, sinks,
           mla_out_norm, swa_out_norm, w_o, attn_post_norm, ffn_pre_norm, w_gate, w_up, w_down,
           ffn_post_norm, w_ple_gate, w_ple_proj):
    batch, seq, d = x.shape
    depth = w_in.shape[0]
    t = batch * seq
    pos_col = positions.reshape(t, 1)
    pos_row = positions.reshape(t // BLOCK, 1, BLOCK)
    half = MLA_ROPE // 2
    invf = np.asarray(ROPE_THETA ** (-(np.arange(LANES) % half) * 2.0 / MLA_ROPE), np.float32).reshape(1, LANES)
    invf = jnp.asarray(invf)
    xf = x.reshape(t, d)
    for i in range(depth):
        xf = _layer(xf, p[i].reshape(t, PLE_DIM), pos_col, pos_row, invf, batch, seq,
                    attn_pre_norm[i], (w_in, i), q_a_norm[i], w_q_up[i], kv_a_norm[i], w_kv_up[i],
                    sinks[i], mla_out_norm[i], swa_out_norm[i], w_o[i], attn_post_norm[i],
                    ffn_pre_norm[i], w_gate[i], w_up[i], w_down[i], ffn_post_norm[i],
                    w_ple_gate[i], w_ple_proj[i])
    return xf.reshape(batch, seq, d)
```
